```python
import jax, jax.numpy as jnp
from jax import lax
import numpy as np

D_MODEL = 1024
BATCH = 1
SEQ = 16384
DEPTH = 1
DEC_BATCH = 128
DEC_SEQ = 1
PAST_LEN = 16384
PAGE_SIZE = 128

HEAD_DIM = 64
RWKV_WIDTH = D_MODEL // 2
RWKV_HEADS = RWKV_WIDTH // HEAD_DIM
ATT_WIDTH = D_MODEL - RWKV_WIDTH
ATT_Q_HEADS = ATT_WIDTH // HEAD_DIM
ATT_KV_HEADS = 2
ATT_GROUP = ATT_Q_HEADS // ATT_KV_HEADS
MIX_WIDTH = RWKV_WIDTH + ATT_WIDTH
DECAY_LORA = 64
AAA_LORA = 64
GATE_LORA = 128
RWKV_PROJ = 3 * RWKV_WIDTH + DECAY_LORA + AAA_LORA + GATE_LORA
ATT_PROJ = ATT_WIDTH + 2 * ATT_KV_HEADS * HEAD_DIM
IN_COLS = RWKV_PROJ + ATT_PROJ
WINDOW = 128
ATT_BLOCK = 128
ROT_DIM = HEAD_DIM // 4
ROPE_THETA = 500000.0
N_EXPERTS = 32
TOP_K = 4
D_FF = D_MODEL
SWIGLU_ALPHA = 1.702
SWIGLU_LIMIT = 7.0
MOE_BLOCK = 128
NORM_EPS = 1e-5
LNX_EPS = HEAD_DIM * 1e-5

kernel_name = "hymba_rwkv7_swa_sink_moe_step"


def _rmsnorm(x, g):
    x32 = x.astype(jnp.float32)
    y = x32 * lax.rsqrt(jnp.mean(x32 * x32, axis=-1, keepdims=True) + NORM_EPS)
    return (y * g.astype(jnp.float32)).astype(x.dtype)


def _rope(x, pos):
    half = ROT_DIM // 2
    inv = ROPE_THETA ** (-jnp.arange(0, ROT_DIM, 2, dtype=jnp.float32) / ROT_DIM)
    ang = pos.astype(jnp.float32)[:, None] * inv[None, :]
    ang = ang.reshape((ang.shape[0],) + (1,) * (x.ndim - 3) + (half,))
    cos, sin = jnp.cos(ang), jnp.sin(ang)
    xr = x[..., :ROT_DIM].astype(jnp.float32)
    x1, x2 = xr[..., :half], xr[..., half:]
    rot = jnp.concatenate([x1 * cos - x2 * sin, x2 * cos + x1 * sin], axis=-1)
    return jnp.concatenate([rot.astype(x.dtype), x[..., ROT_DIM:]], axis=-1)


def _rwkv_step(S, inp):
    r, w, k, v, kk, b = inp
    sa = jnp.einsum('bhij,bhj->bhi', S, -kk)
    S = S * w[:, :, None, :] + sa[..., None] * b[:, :, None, :] + v[..., None] * k[:, :, None, :]
    y = jnp.einsum('bhij,bhj->bhi', S, r)
    return S, y


def _rwkv_mixer(p, shift0, S0, lp):
    B, T, _ = p.shape
    p32 = p.astype(jnp.float32)
    prev = jnp.concatenate([shift0.astype(jnp.float32)[:, None], p32[:, :-1]], axis=1)
    pm = p32 + (prev - p32) * lp['mu_shift'].astype(jnp.float32)
    W = RWKV_WIDTH
    r, k, v = pm[..., :W], pm[..., W:2 * W], pm[..., 2 * W:3 * W]
    o = 3 * W
    wd = pm[..., o:o + DECAY_LORA]
    ad = pm[..., o + DECAY_LORA:o + DECAY_LORA + AAA_LORA]
    gd = pm[..., o + DECAY_LORA + AAA_LORA:]
    w_log = -jax.nn.softplus(-(lp['decay_w0'] + jnp.tanh(wd) @ lp['decay_w2'])) - 0.5
    decay = jnp.exp(-jnp.exp(w_log.astype(jnp.float32)))
    a = jax.nn.sigmoid(lp['aaa_a0'] + ad @ lp['aaa_w2']).astype(jnp.float32)
    g = (jax.nn.sigmoid(gd) @ lp['gate_w2']).astype(jnp.float32)

    def heads(t):
        return t.astype(jnp.float32).reshape(B, T, RWKV_HEADS, HEAD_DIM)

    kk = heads(k * lp['k_k'])
    kk = kk / jnp.maximum(jnp.sqrt(jnp.sum(kk * kk, axis=-1, keepdims=True)), 1e-12)
    k = k * (1.0 + (a - 1.0) * lp['k_a'])
    r_h, k_h, v_h, w_h, a_h = heads(r), heads(k), heads(v), heads(decay), heads(a)
    b_h = kk * a_h

    def tm(t):
        return jnp.moveaxis(t, 1, 0)

    S_fin, y = lax.scan(_rwkv_step, S0.astype(jnp.float32),
                        (tm(r_h), tm(w_h), tm(k_h), tm(v_h), tm(kk), tm(b_h)))
    y = jnp.moveaxis(y, 0, 1)
    mu = jnp.mean(y, axis=-1, keepdims=True)
    var = jnp.mean(jnp.square(y - mu), axis=-1, keepdims=True)
    yn = ((y - mu) * lax.rsqrt(var + LNX_EPS)).reshape(B, T, W)
    yn = yn * lp['lnx_g'].astype(jnp.float32) + lp['lnx_b'].astype(jnp.float32)
    bonus = (jnp.sum(r_h * k_h * lp['r_k'].astype(jnp.float32), axis=-1, keepdims=True) * v_h).reshape(B, T, W)
    out = (yn + bonus) * g
    return out, S_fin, p[:, -1]


def _sink_attention(q, k, v, q_pos, k_pos, sinks):
    s = jnp.einsum('bnqhgd,bnkhd->bnhgqk', q.astype(jnp.float32), k.astype(jnp.float32)) * (HEAD_DIM ** -0.5)
    dist = q_pos[:, :, None] - k_pos[:, None, :]
    mask = (dist >= 0) & (dist < WINDOW) & (k_pos[:, None, :] >= 0)
    s = jnp.where(mask[None, :, None, None], s, -jnp.inf)
    sink = sinks.astype(jnp.float32).reshape(1, 1, ATT_KV_HEADS, ATT_GROUP, 1, 1)
    m = jnp.maximum(jnp.max(s, axis=-1, keepdims=True), sink)
    pr = jnp.exp(s - m)
    denom = jnp.sum(pr, axis=-1, keepdims=True) + jnp.exp(sink - m)
    return jnp.einsum('bnhgqk,bnkhd->bnqhgd', pr / denom, v.astype(jnp.float32))


def _swa_mixer(p, pos, kv_past, sinks):
    B, T, _ = p.shape
    kvw = ATT_KV_HEADS * HEAD_DIM
    q = p[..., :ATT_WIDTH].reshape(B, T, ATT_KV_HEADS, ATT_GROUP, HEAD_DIM)
    k = p[..., ATT_WIDTH:ATT_WIDTH + kvw].reshape(B, T, ATT_KV_HEADS, HEAD_DIM)
    v = p[..., ATT_WIDTH + kvw:].reshape(B, T, ATT_KV_HEADS, HEAD_DIM)
    q = _rope(q, pos)
    k = _rope(k, pos)
    if kv_past is None:
        nb = T // ATT_BLOCK
        qb = q.reshape(B, nb, ATT_BLOCK, ATT_KV_HEADS, ATT_GROUP, HEAD_DIM)
        kb = k.reshape(B, nb, ATT_BLOCK, ATT_KV_HEADS, HEAD_DIM)
        vb = v.reshape(B, nb, ATT_BLOCK, ATT_KV_HEADS, HEAD_DIM)
        pad = ((0, 0), (1, 0), (0, 0), (0, 0), (0, 0))
        k_band = jnp.concatenate([jnp.pad(kb, pad)[:, :-1], kb], axis=2)
        v_band = jnp.concatenate([jnp.pad(vb, pad)[:, :-1], vb], axis=2)
        q_pos = pos.reshape(nb, ATT_BLOCK)
        k_pos = q_pos[:, :1] - ATT_BLOCK + jnp.arange(2 * ATT_BLOCK)[None, :]
        out = _sink_attention(qb, k_band, v_band, q_pos, k_pos, sinks)
        wlen = min(WINDOW, T)
        k_new, v_new = k[:, T - wlen:], v[:, T - wlen:]
    else:
        k_buf, v_buf = kv_past
        wlen = k_buf.shape[1]
        kc = jnp.concatenate([k_buf.astype(k.dtype), k], axis=1)
        vc = jnp.concatenate([v_buf.astype(v.dtype), v], axis=1)
        k_pos = jnp.concatenate([pos[0] - wlen + jnp.arange(wlen), pos])
        out = _sink_attention(q[:, None], kc[:, None], vc[:, None], pos[None], k_pos[None], sinks)
        k_new, v_new = kc[:, -wlen:], vc[:, -wlen:]
    return out.reshape(B, T, ATT_WIDTH), k_new, v_new


def _moe(x, w_router, b_router, w1, b1, w2, b2):
    T = x.shape[0]
    logits = x.astype(jnp.float32) @ w_router.astype(jnp.float32) + b_router.astype(jnp.float32)
    top_val, top_idx = lax.top_k(logits, TOP_K)
    gate = jax.nn.softmax(top_val, axis=-1)
    A = T * TOP_K
    flat_e = top_idx.reshape(A)
    flat_tok = jnp.arange(A, dtype=jnp.int32) // TOP_K
    order = jnp.argsort(flat_e)
    sorted_e = flat_e[order]
    counts = jnp.bincount(flat_e, length=N_EXPERTS)
    padded = (counts + MOE_BLOCK - 1) // MOE_BLOCK * MOE_BLOCK
    start = jnp.cumsum(counts) - counts
    pend = jnp.cumsum(padded)
    pstart = pend - padded
    dest_sorted = (pstart[sorted_e] + jnp.arange(A) - start[sorted_e]).astype(jnp.int32)
    n_blocks = -(-A // MOE_BLOCK) + N_EXPERTS
    R = n_blocks * MOE_BLOCK
    row_tok = jnp.zeros((R,), jnp.int32).at[dest_sorted].set(flat_tok[order])
    block_e = jnp.minimum(jnp.searchsorted(pend, jnp.arange(n_blocks) * MOE_BLOCK, side='right'),
                          N_EXPERTS - 1)
    xb = x[row_tok].reshape(n_blocks, MOE_BLOCK, x.shape[-1])

    def expert_block(args):
        xe, e = args
        h = xe @ w1[e] + b1[e]
        hg = jnp.minimum(h[..., :D_FF], SWIGLU_LIMIT)
        hu = jnp.clip(h[..., D_FF:], -SWIGLU_LIMIT, SWIGLU_LIMIT)
        glu = hg * jax.nn.sigmoid(SWIGLU_ALPHA * hg)
        return (glu * (hu + 1.0)) @ w2[e] + b2[e]

    yb = lax.map(expert_block, (xb, block_e)).reshape(R, -1)
    dest = jnp.zeros((A,), jnp.int32).at[order].set(dest_sorted)
    y = jnp.einsum('tk,tkd->td', gate, yb[dest].reshape(T, TOP_K, -1).astype(jnp.float32))
    return y.astype(x.dtype)


def _layer(x, pos0, shift0, S0, kv_past, lp):
    B, T, _ = x.shape
    h = _rmsnorm(x, lp['norm1_g'])
    proj = jnp.einsum('btd,dc->btc', h, lp['w_in'])
    y_a, S_new, shift_new = _rwkv_mixer(proj[..., :RWKV_PROJ], shift0, S0, lp)
    pos = pos0 + jnp.arange(T)
    y_b, k_new, v_new = _swa_mixer(proj[..., RWKV_PROJ:], pos, kv_past, lp['attn_sinks'])
    mix = jnp.concatenate([y_a, y_b], axis=-1).astype(x.dtype)
    x = x + jnp.einsum('btc,cd->btd', mix, lp['w_out'])
    h2 = _rmsnorm(x, lp['norm2_g'])
    ff = _moe(h2.reshape(B * T, D_MODEL), lp['w_router'], lp['b_router'],
              lp['w_mlp1'], lp['b_mlp1'], lp['w_mlp2'], lp['b_mlp2'])
    x = x + ff.reshape(B, T, D_MODEL)
    return x, S_new, shift_new, k_new, v_new


def setup_inputs(seed: int = 0) -> dict:
    key = jax.random.key(seed)
    ks = jax.random.split(key, 32)

    def nrm(i, shape, scale):
        return jax.random.normal(ks[i], shape, jnp.float32) * scale

    L = DEPTH
    wbuf = min(WINDOW, PAST_LEN)
    return {
        "x_prompt": nrm(0, (BATCH, SEQ, D_MODEL), 1.0),
        "x_sample": nrm(1, (DEC_BATCH, DEC_SEQ, D_MODEL), 1.0),
        "state_rwkv_wkv": nrm(2, (L, DEC_BATCH, RWKV_HEADS, HEAD_DIM, HEAD_DIM), 0.1),
        "state_rwkv_shift": nrm(3, (L, DEC_BATCH, RWKV_PROJ), 1.0),
        "cache_swa_k": nrm(4, (L, DEC_BATCH, wbuf, ATT_KV_HEADS, HEAD_DIM), 1.0),
        "cache_swa_v": nrm(5, (L, DEC_BATCH, wbuf, ATT_KV_HEADS, HEAD_DIM), 1.0),
        "norm1_g": 1.0 + nrm(6, (L, D_MODEL), 0.05),
        "w_in": nrm(7, (L, D_MODEL, IN_COLS), D_MODEL ** -0.5),
        "mu_shift": jax.random.uniform(ks[8], (L, RWKV_PROJ), jnp.float32),
        "decay_w0": jax.random.uniform(ks[9], (L, RWKV_WIDTH), jnp.float32, -5.0, -1.0),
        "decay_w2": nrm(10, (L, DECAY_LORA, RWKV_WIDTH), 0.1 * DECAY_LORA ** -0.5),
        "aaa_a0": nrm(11, (L, RWKV_WIDTH), 0.1),
        "aaa_w2": nrm(12, (L, AAA_LORA, RWKV_WIDTH), 0.1 * AAA_LORA ** -0.5),
        "gate_w2": nrm(13, (L, GATE_LORA, RWKV_WIDTH), GATE_LORA ** -0.5),
        "k_k": 0.85 + nrm(14, (L, RWKV_WIDTH), 0.05),
        "k_a": 1.0 + nrm(15, (L, RWKV_WIDTH), 0.05),
        "r_k": nrm(16, (L, RWKV_HEADS, HEAD_DIM), 0.1),
        "lnx_g": 1.0 + nrm(17, (L, RWKV_WIDTH), 0.05),
        "lnx_b": nrm(18, (L, RWKV_WIDTH), 0.01),
        "attn_sinks": nrm(19, (L, ATT_Q_HEADS), 1.0),
        "w_out": nrm(20, (L, MIX_WIDTH, D_MODEL), MIX_WIDTH ** -0.5),
        "norm2_g": 1.0 + nrm(21, (L, D_MODEL), 0.05),
        "w_router": nrm(22, (L, D_MODEL, N_EXPERTS), D_MODEL ** -0.5),
        "b_router": nrm(23, (L, N_EXPERTS), 0.01),
        "w_mlp1": nrm(24, (L, N_EXPERTS, D_MODEL, 2 * D_FF), D_MODEL ** -0.5),
        "b_mlp1": nrm(25, (L, N_EXPERTS, 2 * D_FF), 0.01),
        "w_mlp2": nrm(26, (L, N_EXPERTS, D_FF, D_MODEL), D_FF ** -0.5),
        "b_mlp2": nrm(27, (L, N_EXPERTS, D_MODEL), 0.01),
        "norm_f_g": 1.0 + nrm(28, (D_MODEL,), 0.05),
    }


def reference(x_prompt, x_sample, state_rwkv_wkv, state_rwkv_shift, cache_swa_k, cache_swa_v,
              norm1_g, w_in, mu_shift, decay_w0, decay_w2, aaa_a0, aaa_w2, gate_w2, k_k, k_a, r_k,
              lnx_g, lnx_b, attn_sinks, w_out, norm2_g, w_router, b_router, w_mlp1, b_mlp1,
              w_mlp2, b_mlp2, norm_f_g):
    b_p = x_prompt.shape[0]
    hp, hs = x_prompt, x_sample
    shift_zero = jnp.zeros((b_p, RWKV_PROJ), x_prompt.dtype)
    S_zero = jnp.zeros((b_p, RWKV_HEADS, HEAD_DIM, HEAD_DIM), jnp.float32)
    wkv_p, sh_p, k_p, v_p = [], [], [], []
    wkv_s, sh_s, k_s, v_s = [], [], [], []
    for l in range(DEPTH):
        lp = dict(norm1_g=norm1_g[l], w_in=w_in[l], mu_shift=mu_shift[l], decay_w0=decay_w0[l],
                  decay_w2=decay_w2[l], aaa_a0=aaa_a0[l], aaa_w2=aaa_w2[l], gate_w2=gate_w2[l],
                  k_k=k_k[l], k_a=k_a[l], r_k=r_k[l], lnx_g=lnx_g[l], lnx_b=lnx_b[l],
                  attn_sinks=attn_sinks[l], w_out=w_out[l], norm2_g=norm2_g[l],
                  w_router=w_router[l], b_router=b_router[l], w_mlp1=w_mlp1[l], b_mlp1=b_mlp1[l],
                  w_mlp2=w_mlp2[l], b_mlp2=b_mlp2[l])
        hp, S1, s1, k1, v1 = _layer(hp, 0, shift_zero, S_zero, None, lp)
        hs, S2, s2, k2, v2 = _layer(hs, PAST_LEN, state_rwkv_shift[l], state_rwkv_wkv[l],
                                    (cache_swa_k[l], cache_swa_v[l]), lp)
        wkv_p.append(S1.astype(state_rwkv_wkv.dtype)); sh_p.append(s1.astype(state_rwkv_shift.dtype))
        k_p.append(k1.astype(cache_swa_k.dtype)); v_p.append(v1.astype(cache_swa_v.dtype))
        wkv_s.append(S2.astype(state_rwkv_wkv.dtype)); sh_s.append(s2.astype(state_rwkv_shift.dtype))
        k_s.append(k2.astype(cache_swa_k.dtype)); v_s.append(v2.astype(cache_swa_v.dtype))
    y_prompt = _rmsnorm(hp, norm_f_g)
    y_sample = _rmsnorm(hs, norm_f_g)
    return (y_prompt, y_sample,
            jnp.stack(wkv_p), jnp.stack(sh_p), jnp.stack(k_p), jnp.stack(v_p),
            jnp.stack(wkv_s), jnp.stack(sh_s), jnp.stack(k_s), jnp.stack(v_s))
```

```python
import functools

import jax
import jax.numpy as jnp
from jax import lax
from jax.experimental import pallas as pl
from jax.experimental.pallas import tpu as pltpu

F32 = jnp.float32
BF16 = jnp.bfloat16

LANES = 128
HEAD_DIM = 64
PAIR = 2 * HEAD_DIM
CHUNK = 64
RW_TILE = 256
ROT_DIM = 16
ROPE_THETA = 500000.0
WINDOW = 128
PAST_LEN = 16384
ATT_BLOCK = 128
N_EXPERTS = 32
TOP_K = 4
SWIGLU_ALPHA = 1.702
SWIGLU_LIMIT = 7.0
NORM_EPS = 1e-5
LNX_EPS = HEAD_DIM * 1e-5
MOE_BM = 256
ROW_TILE = 128
NEG_BIG = -1e30
VMEM_LIMIT = 52 * 1024 * 1024

NN = (((1,), (0,)), ((), ()))
NT = (((1,), (1,)), ((), ()))


def _mm(a, b, dn=NN):
    return lax.dot_general(a, b, dn, preferred_element_type=F32)


def _split2(a):
    hi = a.astype(BF16)
    lo = (a - hi.astype(F32)).astype(BF16)
    return hi, lo


def _split3(a):
    hi = a.astype(BF16)
    r1 = a - hi.astype(F32)
    mid = r1.astype(BF16)
    lo = (r1 - mid.astype(F32)).astype(BF16)
    return hi, mid, lo


def _dot3(a, b, dn=NN):
    ah, al = _split2(a)
    bh, bl = _split2(b)
    return _mm(ah, bh, dn) + (_mm(ah, bl, dn) + _mm(al, bh, dn))


def _dot_sel_l(sel, b, dn=NN):
    b0, b1, b2 = _split3(b)
    return _mm(sel, b0, dn) + (_mm(sel, b1, dn) + _mm(sel, b2, dn))


def _dot_sel_r(a, sel, dn=NN):
    a0, a1, a2 = _split3(a)
    return _mm(a0, sel, dn) + (_mm(a1, sel, dn) + _mm(a2, sel, dn))


def _iota(shape, dim):
    return lax.broadcasted_iota(jnp.int32, shape, dim)


def _seg_matrix():
    return ((_iota((PAIR, PAIR), 0) // HEAD_DIM) == (_iota((PAIR, PAIR), 1) // HEAD_DIM)).astype(BF16)


def _sigmoid(x):
    return 1.0 / (1.0 + jnp.exp(-x))


def _cparams(sem, vmem=VMEM_LIMIT):
    return pltpu.CompilerParams(dimension_semantics=sem, vmem_limit_bytes=vmem)


def _rope_slab(x, cos, sin_signed):
    lane = _iota(x.shape, 1) % HEAD_DIM
    up = pltpu.roll(x, LANES - ROT_DIM // 2, axis=1)
    down = pltpu.roll(x, ROT_DIM // 2, axis=1)
    partner = jnp.where(lane < ROT_DIM // 2, up, down)
    return x * cos + partner * sin_signed


def _inproj_kernel(rw_cols, q_cols, kv_cols, x_ref, g_ref, whi_ref, wlo_ref, cos_ref, sin_ref,
                   prw_ref, q_ref, k_ref, v_ref):
    x = x_ref[...]
    h = x * lax.rsqrt(jnp.mean(x * x, axis=-1, keepdims=True) + NORM_EPS) * g_ref[...]
    hh, hl = _split2(h)
    whi = whi_ref[...]
    proj = _mm(hh, whi) + (_mm(hl, whi) + _mm(hh, wlo_ref[...]))
    prw_ref[...] = proj[:, :rw_cols]
    cos = cos_ref[...]
    sin = sin_ref[...]
    for c in range(q_cols // LANES):
        lo = rw_cols + c * LANES
        q_ref[:, c * LANES:(c + 1) * LANES] = _rope_slab(proj[:, lo:lo + LANES], cos, sin)
    ko = rw_cols + q_cols
    for c in range(kv_cols // LANES):
        k_ref[:, c * LANES:(c + 1) * LANES] = _rope_slab(proj[:, ko + c * LANES:ko + (c + 1) * LANES], cos, sin)
    v_ref[...] = proj[:, ko + kv_cols:ko + 2 * kv_cols]


def _inproj(x, g, whi, wlo, cos_t, sin_t, tm, rw_cols, q_cols, kv_cols):
    rows, d = x.shape
    cols = whi.shape[1]
    full = lambda i: (0, 0)
    row = lambda i: (i, 0)
    return pl.pallas_call(
        functools.partial(_inproj_kernel, rw_cols, q_cols, kv_cols),
        grid=(rows // tm,),
        in_specs=[pl.BlockSpec((tm, d), row), pl.BlockSpec((1, d), full),
                  pl.BlockSpec((d, cols), full), pl.BlockSpec((d, cols), full),
                  pl.BlockSpec((tm, LANES), row), pl.BlockSpec((tm, LANES), row)],
        out_specs=[pl.BlockSpec((tm, rw_cols), row), pl.BlockSpec((tm, q_cols), row),
                   pl.BlockSpec((tm, kv_cols), row), pl.BlockSpec((tm, kv_cols), row)],
        out_shape=[jax.ShapeDtypeStruct((rows, rw_cols), F32), jax.ShapeDtypeStruct((rows, q_cols), F32),
                   jax.ShapeDtypeStruct((rows, kv_cols), F32), jax.ShapeDtypeStruct((rows, kv_cols), F32)],
        compiler_params=_cparams(("parallel",)),
        name="inproj",
    )(x, g, whi, wlo, cos_t, sin_t)


def _rwkv_tokenwise(pr, pk, pv, plo, pg, prev_r, prev_k, prev_v, prev_lo, prev_g,
                    mu_r, mu_k, mu_v, mu_lo, mu_g, w0, dw2, a0, aw2, gw2, kkp, kap, rkp, seg):
    r = pr + (prev_r - pr) * mu_r
    k = pk + (prev_k - pk) * mu_k
    v = pv + (prev_v - pv) * mu_v
    lo = plo + (prev_lo - plo) * mu_lo
    gd = pg + (prev_g - pg) * mu_g
    z = -(w0 + _dot3(jnp.tanh(lo), dw2))
    softplus = jnp.maximum(z, 0.0) + jnp.log(1.0 + jnp.exp(-jnp.abs(z)))
    logw = -jnp.exp(-softplus - 0.5)
    a = _sigmoid(a0 + _dot3(lo, aw2))
    g = _dot3(_sigmoid(gd), gw2)
    kk = k * kkp
    nrm = jnp.sqrt(_dot_sel_r(kk * kk, seg))
    kk = kk / jnp.maximum(nrm, 1e-12)
    k2 = k * (1.0 + (a - 1.0) * kap)
    bonus = _dot_sel_r(r * k2 * rkp, seg) * v
    return r, k2, v, logw, -kk, kk * a, g, bonus


def _rwkv_finish(y, bonus, g, lng, lnb, seg):
    mu = _dot_sel_r(y, seg) * (1.0 / HEAD_DIM)
    d = y - mu
    var = _dot_sel_r(d * d, seg) * (1.0 / HEAD_DIM)
    yn = d * lax.rsqrt(var + LNX_EPS) * lng + lnb
    return (yn + bonus) * g


def _rwkv_prompt_kernel(pr_ref, pk_ref, pv_ref, plo_ref, pg_ref,
                        hr_ref, hk_ref, hv_ref, hlo_ref, hg_ref,
                        s0r_ref, s0k_ref, s0v_ref, s0lo_ref, s0g_ref,
                        mur_ref, muk_ref, muv_ref, mulo_ref, mug_ref,
                        w0_ref, dw2_ref, a0_ref, aw2_ref, gw2_ref, kk_ref, ka_ref, rk_ref,
                        lng_ref, lnb_ref, sin_ref,
                        y_ref, sout_ref, st_ref):
    i = pl.program_id(1)
    n_i = pl.num_programs(1)
    tt = pr_ref.shape[0]

    @pl.when(i == 0)
    def _():
        st_ref[...] = sin_ref[0]

    row = _iota((tt, PAIR), 0)

    def prev_of(cur_ref, halo_ref, s0_ref):
        cur = cur_ref[...]
        first = jnp.where(i == 0, s0_ref[...], halo_ref[7:8, :])
        return cur, jnp.where(row == 0, first, pltpu.roll(cur, 1, axis=0))

    pr, prev_r = prev_of(pr_ref, hr_ref, s0r_ref)
    pk, prev_k = prev_of(pk_ref, hk_ref, s0k_ref)
    pv, prev_v = prev_of(pv_ref, hv_ref, s0v_ref)
    plo, prev_lo = prev_of(plo_ref, hlo_ref, s0lo_ref)
    pg, prev_g = prev_of(pg_ref, hg_ref, s0g_ref)
    seg = _seg_matrix()
    r, k2, v, logw, nkk, b, g, bonus = _rwkv_tokenwise(
        pr, pk, pv, plo, pg, prev_r, prev_k, prev_v, prev_lo, prev_g,
        mur_ref[...], muk_ref[...], muv_ref[...], mulo_ref[...], mug_ref[...],
        w0_ref[...], dw2_ref[...], a0_ref[...], aw2_ref[...], gw2_ref[...],
        kk_ref[...], ka_ref[...], rk_ref[...], seg)

    ti = _iota((tt, tt), 0)
    tj = _iota((tt, tt), 1)
    same_chunk = (ti // CHUNK) == (tj // CHUNK)
    incl = same_chunk & (tj <= ti)
    strict = same_chunk & (tj < ti)
    cs = _dot_sel_l(incl.astype(BF16), logw)
    gam = jnp.exp(cs)
    a_t = jnp.exp(cs - logw) * nkk
    r_t = gam * r
    inv = jnp.exp(-cs)
    b_t = b * inv
    k_t = k2 * inv
    bt_T = b_t.T
    kt_T = k_t.T
    gam_T = gam.T
    lane = _iota((tt, PAIR), 1)
    eye = (ti == tj).astype(F32)

    ta = jnp.zeros((tt, PAIR), F32)
    tp = jnp.zeros((tt, PAIR), F32)
    ry = jnp.zeros((tt, PAIR), F32)
    yc = jnp.zeros((tt, PAIR), F32)
    for hh in range(2):
        hm = (lane // HEAD_DIM) == hh
        a_h = jnp.where(hm, a_t, 0.0)
        r_h = jnp.where(hm, r_t, 0.0)
        v_h = jnp.where(hm, v, 0.0)
        l_ab = jnp.where(strict, _dot3(a_h, bt_T), 0.0)
        l_ak = jnp.where(strict, _dot3(a_h, kt_T), 0.0)
        m_rb = jnp.where(incl, _dot3(r_h, bt_T), 0.0)
        m_rk = jnp.where(incl, _dot3(r_h, kt_T), 0.0)
        tm = eye + l_ab
        lp = l_ab
        for _ in range(5):
            lp = _dot3(lp, lp)
            tm = tm + _dot3(tm, lp)
        p_h = _dot3(l_ak, v_h)
        q_h = _dot3(m_rk, v_h)
        ta_h = _dot3(tm, a_h)
        tp_h = _dot3(tm, p_h)
        ta = ta + ta_h
        tp = tp + tp_h
        ry = ry + (r_h + _dot3(m_rb, ta_h))
        yc = yc + (_dot3(m_rb, tp_h) + q_h)

    bd = _seg_matrix().astype(F32)
    eye_p = (_iota((PAIR, PAIR), 0) == _iota((PAIR, PAIR), 1)).astype(F32)
    col_t = _iota((PAIR, tt), 1)
    s = st_ref[...]
    for c in range(tt // CHUNK):
        sl = slice(c * CHUNK, (c + 1) * CHUNK)
        cm = (col_t // CHUNK) == c
        bt_c = jnp.where(cm, bt_T, 0.0)
        kt_c = jnp.where(cm, kt_T, 0.0)
        dcol = gam_T[:, (c + 1) * CHUNK - 1:(c + 1) * CHUNK]
        mc = dcol * (eye_p + bd * _dot3(bt_c, ta))
        nc = dcol * (bd * (_dot3(bt_c, tp) + _dot3(kt_c, v)))
        y_c = _dot3(ry[sl], s) + yc[sl]
        y_ref[sl, :] = _rwkv_finish(y_c, bonus[sl], g[sl], lng_ref[...], lnb_ref[...], seg)
        s = _dot3(mc, s) + nc
    st_ref[...] = s

    @pl.when(i == n_i - 1)
    def _():
        sout_ref[0] = s


def _rwkv_prompt(prw, shift0, s0_pairs, pp, tt):
    T = prw.shape[0]
    n_pairs = s0_pairs.shape[0]
    wcols = n_pairs * PAIR
    lo_blk = 3 * n_pairs
    g_blk = lo_blk + 1
    hb = tt // 8

    def cur(off):
        return pl.BlockSpec((tt, PAIR), lambda p, i: (i, off + p))

    def cur_fixed(blk):
        return pl.BlockSpec((tt, PAIR), lambda p, i: (i, blk))

    def halo(off):
        return pl.BlockSpec((8, PAIR), lambda p, i: (jnp.maximum(i * hb - 1, 0), off + p))

    def halo_fixed(blk):
        return pl.BlockSpec((8, PAIR), lambda p, i: (jnp.maximum(i * hb - 1, 0), blk))

    def vec(off):
        return pl.BlockSpec((1, PAIR), lambda p, i: (0, off + p))

    def vec_fixed(blk):
        return pl.BlockSpec((1, PAIR), lambda p, i: (0, blk))

    def wmat(rows):
        return pl.BlockSpec((rows, PAIR), lambda p, i: (0, p))

    in_specs = ([cur(0), cur(n_pairs), cur(2 * n_pairs), cur_fixed(lo_blk), cur_fixed(g_blk)]
                + [halo(0), halo(n_pairs), halo(2 * n_pairs), halo_fixed(lo_blk), halo_fixed(g_blk)]
                + [vec(0), vec(n_pairs), vec(2 * n_pairs), vec_fixed(lo_blk), vec_fixed(g_blk)]
                + [vec(0), vec(n_pairs), vec(2 * n_pairs), vec_fixed(lo_blk), vec_fixed(g_blk)]
                + [vec(0), wmat(PAIR), vec(0), wmat(PAIR), wmat(PAIR), vec(0), vec(0), vec(0), vec(0), vec(0)]
                + [pl.BlockSpec((1, PAIR, PAIR), lambda p, i: (p, 0, 0))])
    args = ([prw] * 5 + [prw] * 5 + [shift0] * 5 + [pp["mu"]] * 5
            + [pp["w0"], pp["dw2"], pp["a0"], pp["aw2"], pp["gw2"], pp["kk"], pp["ka"], pp["rk"],
               pp["lng"], pp["lnb"], s0_pairs])
    return pl.pallas_call(
        _rwkv_prompt_kernel,
        grid=(n_pairs, T // tt),
        in_specs=in_specs,
        out_specs=[pl.BlockSpec((tt, PAIR), lambda p, i: (i, p)),
                   pl.BlockSpec((1, PAIR, PAIR), lambda p, i: (p, 0, 0))],
        out_shape=[jax.ShapeDtypeStruct((T, wcols), F32),
                   jax.ShapeDtypeStruct((n_pairs, PAIR, PAIR), F32)],
        scratch_shapes=[pltpu.VMEM((PAIR, PAIR), F32)],
        compiler_params=_cparams(("parallel", "arbitrary")),
        name="rwkv_prompt",
    )(*args)


def _rwkv_step_kernel(slabs_per_step, pr_ref, pk_ref, pv_ref, plo_ref, pg_ref,
                      sr_ref, sk_ref, sv_ref, slo_ref, sg_ref,
                      mur_ref, muk_ref, muv_ref, mulo_ref, mug_ref,
                      w0_ref, dw2_ref, a0_ref, aw2_ref, gw2_ref, kk_ref, ka_ref, rk_ref,
                      lng_ref, lnb_ref, s_ref,
                      y_ref, snew_ref, yacc_ref):
    j = pl.program_id(1)
    n_j = pl.num_programs(1)
    seg = _seg_matrix()
    r, k2, v, logw, nkk, b, g, bonus = _rwkv_tokenwise(
        pr_ref[...], pk_ref[...], pv_ref[...], plo_ref[...], pg_ref[...],
        sr_ref[...], sk_ref[...], sv_ref[...], slo_ref[...], sg_ref[...],
        mur_ref[...], muk_ref[...], muv_ref[...], mulo_ref[...], mug_ref[...],
        w0_ref[...], dw2_ref[...], a0_ref[...], aw2_ref[...], gw2_ref[...],
        kk_ref[...], ka_ref[...], rk_ref[...], seg)
    w = jnp.exp(logw)

    @pl.when(j == 0)
    def _():
        yacc_ref[...] = jnp.zeros_like(yacc_ref)

    slabs_per_head = HEAD_DIM // 2
    ci = _iota((PAIR, PAIR), 0)
    li = _iota((PAIR, PAIR), 1)
    yacc = yacc_ref[...]
    for t in range(slabs_per_step):
        slab = j * slabs_per_step + t
        hh = slab // slabs_per_head
        i0 = 2 * (slab % slabs_per_head)
        dup = ((ci == hh * HEAD_DIM + li % HEAD_DIM)).astype(BF16)
        selv = (ci == hh * HEAD_DIM + i0 + li // HEAD_DIM).astype(BF16)
        sely = ((ci % HEAD_DIM == 0) & (li == hh * HEAD_DIM + i0 + ci // HEAD_DIM)).astype(BF16)
        s = s_ref[:, t * PAIR:(t + 1) * PAIR]
        sa = _dot_sel_r(s * _dot_sel_r(nkk, dup), seg)
        s_new = (s * _dot_sel_r(w, dup) + sa * _dot_sel_r(b, dup)
                 + _dot_sel_r(v, selv) * _dot_sel_r(k2, dup))
        snew_ref[:, t * PAIR:(t + 1) * PAIR] = s_new
        yred = _dot_sel_r(s_new * _dot_sel_r(r, dup), seg)
        yacc = yacc + _dot_sel_r(yred, sely)
    yacc_ref[...] = yacc

    @pl.when(j == n_j - 1)
    def _():
        y_ref[...] = _rwkv_finish(yacc, bonus, g, lng_ref[...], lnb_ref[...], seg)


def _rwkv_step(prw, shift, s_flat, pp, n_pairs):
    B = prw.shape[0]
    lanes_per_pair = 2 * HEAD_DIM * HEAD_DIM
    blk = 1024
    slabs_per_step = blk // PAIR
    steps = lanes_per_pair // blk
    lo_blk = 3 * n_pairs
    g_blk = lo_blk + 1

    def cur(off):
        return pl.BlockSpec((B, PAIR), lambda p, j: (0, off + p))

    def cur_fixed(b_):
        return pl.BlockSpec((B, PAIR), lambda p, j: (0, b_))

    def vec(off):
        return pl.BlockSpec((1, PAIR), lambda p, j: (0, off + p))

    def vec_fixed(b_):
        return pl.BlockSpec((1, PAIR), lambda p, j: (0, b_))

    def wmat(rows):
        return pl.BlockSpec((rows, PAIR), lambda p, j: (0, p))

    sspec = pl.BlockSpec((B, blk), lambda p, j: (0, p * steps + j))
    in_specs = ([cur(0), cur(n_pairs), cur(2 * n_pairs), cur_fixed(lo_blk), cur_fixed(g_blk)] * 2
                + [vec(0), vec(n_pairs), vec(2 * n_pairs), vec_fixed(lo_blk), vec_fixed(g_blk)]
                + [vec(0), wmat(PAIR), vec(0), wmat(PAIR), wmat(PAIR), vec(0), vec(0), vec(0), vec(0), vec(0)]
                + [sspec])
    args = ([prw] * 5 + [shift] * 5 + [pp["mu"]] * 5
            + [pp["w0"], pp["dw2"], pp["a0"], pp["aw2"], pp["gw2"], pp["kk"], pp["ka"], pp["rk"],
               pp["lng"], pp["lnb"], s_flat])
    return pl.pallas_call(
        functools.partial(_rwkv_step_kernel, slabs_per_step),
        grid=(n_pairs, steps),
        in_specs=in_specs,
        out_specs=[pl.BlockSpec((B, PAIR), lambda p, j: (0, p)), sspec],
        out_shape=[jax.ShapeDtypeStruct((B, n_pairs * PAIR), F32),
                   jax.ShapeDtypeStruct(s_flat.shape, F32)],
        scratch_shapes=[pltpu.VMEM((B, PAIR), F32)],
        compiler_params=_cparams(("parallel", "arbitrary")),
        name="rwkv_step",
    )(*args)


def _attn_prompt_kernel(n_q, group, sink_ref, q_ref, kc_ref, kp_ref, vc_ref, vp_ref, o_ref):
    blk = q_ref.shape[0]
    q = q_ref[...] * (HEAD_DIM ** -0.5)
    kband = jnp.concatenate([kp_ref[...], kc_ref[...]], axis=0)
    vband = jnp.concatenate([vp_ref[...], vc_ref[...]], axis=0)
    i = pl.program_id(0)
    rq = _iota((blk, 2 * blk), 0)
    ck = _iota((blk, 2 * blk), 1)
    dist = rq - ck + blk
    kpos = i * blk - blk + ck
    valid = (dist >= 0) & (dist < WINDOW) & (kpos >= 0)
    for h in range(n_q):
        gk = h // group
        qh = q[:, h * HEAD_DIM:(h + 1) * HEAD_DIM]
        kh = kband[:, gk * HEAD_DIM:(gk + 1) * HEAD_DIM]
        vh = vband[:, gk * HEAD_DIM:(gk + 1) * HEAD_DIM]
        s = jnp.where(valid, _dot3(qh, kh, NT), NEG_BIG)
        sink = sink_ref[h]
        m = jnp.maximum(jnp.max(s, axis=-1, keepdims=True), sink)
        p = jnp.exp(s - m)
        denom = jnp.sum(p, axis=-1, keepdims=True) + jnp.exp(sink - m)
        o_ref[:, h * HEAD_DIM:(h + 1) * HEAD_DIM] = _dot3(p, vh) / denom


def _attn_prompt(q, k, v, sinks, n_q, n_kv):
    T, qw = q.shape
    kvw = k.shape[1]
    blk = ATT_BLOCK
    curm = lambda i: (i, 0)
    prevm = lambda i: (jnp.maximum(i - 1, 0), 0)
    return pl.pallas_call(
        functools.partial(_attn_prompt_kernel, n_q, n_q // n_kv),
        grid=(T // blk,),
        in_specs=[pl.BlockSpec(memory_space=pltpu.SMEM),
                  pl.BlockSpec((blk, qw), curm),
                  pl.BlockSpec((blk, kvw), curm), pl.BlockSpec((blk, kvw), prevm),
                  pl.BlockSpec((blk, kvw), curm), pl.BlockSpec((blk, kvw), prevm)],
        out_specs=pl.BlockSpec((blk, qw), curm),
        out_shape=jax.ShapeDtypeStruct((T, qw), F32),
        compiler_params=_cparams(("parallel",)),
        name="attn_prompt",
    )(sinks, q, k, k, v, v)


def _attn_step_kernel(n_q, group, pos0, sink_ref, q_ref, kn_ref, vn_ref, kc_ref, vc_ref,
                      o_ref, ko_ref, vo_ref):
    bb, wlen, kvw = kc_ref.shape
    lane = _iota((n_q, kvw), 1)
    rowh = _iota((n_q, kvw), 0)
    mine = (lane // HEAD_DIM) == (rowh // group)
    dupm = (_iota((HEAD_DIM, kvw), 0) == _iota((HEAD_DIM, kvw), 1) % HEAD_DIM).astype(BF16)
    fold = (_iota((kvw, HEAD_DIM), 0) % HEAD_DIM == _iota((kvw, HEAD_DIM), 1)).astype(BF16)
    kidx = _iota((n_q, wlen), 1)
    dist = wlen - kidx
    valid = (dist < WINDOW) & (pos0 - dist >= 0)
    rk = _iota((wlen, kvw), 0)
    sink = sink_ref[...]
    for t in range(bb):
        qh = q_ref[t * n_q:(t + 1) * n_q, :] * (HEAD_DIM ** -0.5)
        qm = jnp.where(mine, _dot_sel_r(qh, dupm), 0.0)
        kc = kc_ref[t]
        vc = vc_ref[t]
        kn = kn_ref[t:t + 1, :]
        vn = vn_ref[t:t + 1, :]
        s = jnp.where(valid, _dot3(qm, kc, NT), NEG_BIG)
        s_new = jnp.sum(qm * kn, axis=-1, keepdims=True)
        m = jnp.maximum(jnp.maximum(jnp.max(s, axis=-1, keepdims=True), s_new), sink)
        p = jnp.exp(s - m)
        p_new = jnp.exp(s_new - m)
        denom = jnp.sum(p, axis=-1, keepdims=True) + p_new + jnp.exp(sink - m)
        res = (_dot3(p, vc) + p_new * vn) / denom
        o_ref[t * n_q:(t + 1) * n_q, :] = _dot_sel_r(jnp.where(mine, res, 0.0), fold)
        ko_ref[t] = jnp.where(rk == wlen - 1, kn, pltpu.roll(kc, wlen - 1, axis=0))
        vo_ref[t] = jnp.where(rk == wlen - 1, vn, pltpu.roll(vc, wlen - 1, axis=0))


def _attn_step(q2, k_new, v_new, k_cache, v_cache, sinks_col, n_q, n_kv, pos0):
    B, wlen, kvw = k_cache.shape
    bb = 8
    return pl.pallas_call(
        functools.partial(_attn_step_kernel, n_q, n_q // n_kv, pos0),
        grid=(B // bb,),
        in_specs=[pl.BlockSpec((n_q, 1), lambda i: (0, 0)),
                  pl.BlockSpec((bb * n_q, HEAD_DIM), lambda i: (i, 0)),
                  pl.BlockSpec((bb, kvw), lambda i: (i, 0)), pl.BlockSpec((bb, kvw), lambda i: (i, 0)),
                  pl.BlockSpec((bb, wlen, kvw), lambda i: (i, 0, 0)),
                  pl.BlockSpec((bb, wlen, kvw), lambda i: (i, 0, 0))],
        out_specs=[pl.BlockSpec((bb * n_q, HEAD_DIM), lambda i: (i, 0)),
                   pl.BlockSpec((bb, wlen, kvw), lambda i: (i, 0, 0)),
                   pl.BlockSpec((bb, wlen, kvw), lambda i: (i, 0, 0))],
        out_shape=[jax.ShapeDtypeStruct((B * n_q, HEAD_DIM), F32),
                   jax.ShapeDtypeStruct((B, wlen, kvw), F32),
                   jax.ShapeDtypeStruct((B, wlen, kvw), F32)],
        compiler_params=_cparams(("parallel",)),
        name="attn_step",
    )(sinks_col, q2, k_new, v_new, k_cache, v_cache)


def _post_kernel(x_ref, ya_ref, yb_ref, wahi_ref, walo_ref, wbhi_ref, wblo_ref, g2_ref, wr_ref, br_ref,
                 _x1_alias, _h2_alias, _lg_alias, x1_ref, h2_ref, lg_ref):
    yah, yal = _split2(ya_ref[...])
    ybh, ybl = _split2(yb_ref[...])
    wahi = wahi_ref[...]
    wbhi = wbhi_ref[...]
    mix = (_mm(yah, wahi) + (_mm(yal, wahi) + _mm(yah, walo_ref[...]))
           + _mm(ybh, wbhi) + (_mm(ybl, wbhi) + _mm(ybh, wblo_ref[...])))
    x1 = x_ref[...] + mix
    h2 = x1 * lax.rsqrt(jnp.mean(x1 * x1, axis=-1, keepdims=True) + NORM_EPS) * g2_ref[...]
    x1_ref[...] = x1
    h2_ref[...] = h2
    lg_ref[...] = _dot3(h2, wr_ref[...]) + br_ref[...]


def _post(x, ya, yb, wp, bufs, row_off, tm):
    rows, d = x.shape
    half = ya.shape[1]
    ob = row_off // tm
    full = lambda i: (0, 0)
    row = lambda i: (i, 0)
    orow = lambda i: (ob + i, 0)
    anyspec = pl.BlockSpec(memory_space=pl.ANY)
    return pl.pallas_call(
        _post_kernel,
        grid=(rows // tm,),
        in_specs=[pl.BlockSpec((tm, d), row), pl.BlockSpec((tm, half), row), pl.BlockSpec((tm, half), row),
                  pl.BlockSpec((half, d), full), pl.BlockSpec((half, d), full),
                  pl.BlockSpec((half, d), full), pl.BlockSpec((half, d), full),
                  pl.BlockSpec((1, d), full), pl.BlockSpec((d, LANES), full), pl.BlockSpec((1, LANES), full),
                  anyspec, anyspec, anyspec],
        out_specs=[pl.BlockSpec((tm, d), orow), pl.BlockSpec((tm, d), orow), pl.BlockSpec((tm, LANES), orow)],
        out_shape=[jax.ShapeDtypeStruct(bufs[0].shape, F32), jax.ShapeDtypeStruct(bufs[1].shape, F32),
                   jax.ShapeDtypeStruct(bufs[2].shape, F32)],
        input_output_aliases={10: 0, 11: 1, 12: 2},
        compiler_params=_cparams(("parallel",)),
        name="post",
    )(x, ya, yb, wp["wa_hi"], wp["wa_lo"], wp["wb_hi"], wp["wb_lo"], wp["g2"], wp["wr"], wp["br"], *bufs)


def _route_kernel(lg_ref, idx_ref, gate_ref, rank_ref, cnt_ref, carry_ref):
    i = pl.program_id(0)

    @pl.when(i == 0)
    def _():
        carry_ref[...] = jnp.zeros_like(carry_ref)

    l = lg_ref[...]
    tm = l.shape[0]
    lane = _iota(l.shape, 1)
    vals, idxs = [], []
    for _ in range(TOP_K):
        m = jnp.max(l, axis=-1, keepdims=True)
        sel = jnp.min(jnp.where(l == m, lane, LANES), axis=-1, keepdims=True)
        vals.append(m)
        idxs.append(sel)
        l = jnp.where(lane == sel, -jnp.inf, l)
    es = [jnp.exp(v - vals[0]) for v in vals]
    tot = es[0] + es[1] + es[2] + es[3]
    onehot = jnp.zeros(l.shape, F32)
    for sel in idxs:
        onehot = onehot + (lane == sel).astype(F32)
    strict = (_iota((tm, tm), 1) < _iota((tm, tm), 0)).astype(BF16)
    before = _mm(strict, onehot.astype(BF16)) + carry_ref[...]
    for k in range(TOP_K):
        idx_ref[:, k:k + 1] = idxs[k]
        gate_ref[:, k:k + 1] = es[k] / tot
        rank_ref[:, k:k + 1] = jnp.sum(jnp.where(lane == idxs[k], before, 0.0),
                                       axis=-1, keepdims=True).astype(jnp.int32)
    carry_ref[...] = carry_ref[...] + jnp.sum(onehot, axis=0, keepdims=True)
    cnt_ref[...] = carry_ref[...].astype(jnp.int32)


def _route(logits, tm):
    rows = logits.shape[0]
    row = lambda i: (i, 0)
    return pl.pallas_call(
        _route_kernel,
        grid=(rows // tm,),
        in_specs=[pl.BlockSpec((tm, LANES), row)],
        out_specs=[pl.BlockSpec((tm, TOP_K), row), pl.BlockSpec((tm, TOP_K), row),
                   pl.BlockSpec((tm, TOP_K), row), pl.BlockSpec((1, LANES), lambda i: (0, 0))],
        out_shape=[jax.ShapeDtypeStruct((rows, TOP_K), jnp.int32), jax.ShapeDtypeStruct((rows, TOP_K), F32),
                   jax.ShapeDtypeStruct((rows, TOP_K), jnp.int32), jax.ShapeDtypeStruct((1, LANES), jnp.int32)],
        scratch_shapes=[pltpu.VMEM((1, LANES), F32)],
        compiler_params=_cparams(("arbitrary",)),
        name="route",
    )(logits)


def _scatter_kernel(dest_ref, h_ref, _xs_alias, xs_ref, sem):
    tm = h_ref.shape[0]

    def copy(t, k):
        return pltpu.make_async_copy(h_ref.at[pl.ds(t, 1)], xs_ref.at[pl.ds(dest_ref[t * TOP_K + k], 1)], sem)

    def start(t, c):
        for k in range(TOP_K):
            copy(t, k).start()
        return c

    def wait(t, c):
        for k in range(TOP_K):
            copy(t, k).wait()
        return c

    lax.fori_loop(0, tm, start, 0)
    lax.fori_loop(0, tm, wait, 0)


def _scatter(dest_flat, h2, xs_init, tm):
    rows, d = h2.shape
    return pl.pallas_call(
        _scatter_kernel,
        grid=(rows // tm,),
        in_specs=[pl.BlockSpec((tm * TOP_K,), lambda i: (i,), memory_space=pltpu.SMEM),
                  pl.BlockSpec((tm, d), lambda i: (i, 0)),
                  pl.BlockSpec(memory_space=pl.ANY)],
        out_specs=pl.BlockSpec(memory_space=pl.ANY),
        out_shape=jax.ShapeDtypeStruct(xs_init.shape, F32),
        input_output_aliases={2: 0},
        scratch_shapes=[pltpu.SemaphoreType.DMA(())],
        compiler_params=_cparams(("arbitrary",)),
        name="moe_scatter",
    )(dest_flat, h2, xs_init)


def _expert_kernel(d_ff, be_ref, nused_ref, xs_ref, w1_ref, b1_ref, w2_ref, b2_ref, ys_ref, w1b_ref, w2b_ref):
    i = pl.program_id(0)
    new_expert = jnp.logical_or(i == 0, be_ref[i] != be_ref[jnp.maximum(i - 1, 0)])

    @pl.when(jnp.logical_and(i < nused_ref[0], new_expert))
    def _():
        w1b_ref[...] = w1_ref[0].astype(BF16)
        w2b_ref[...] = w2_ref[0].astype(BF16)

    @pl.when(i < nused_ref[0])
    def _():
        x = xs_ref[...].astype(BF16)
        h = _mm(x, w1b_ref[...]) + b1_ref[0]
        hg = jnp.minimum(h[:, :d_ff], SWIGLU_LIMIT)
        hu = jnp.clip(h[:, d_ff:], -SWIGLU_LIMIT, SWIGLU_LIMIT)
        act = hg * _sigmoid(SWIGLU_ALPHA * hg) * (hu + 1.0)
        ys_ref[...] = _mm(act.astype(BF16), w2b_ref[...]) + b2_ref[0]

    @pl.when(i >= nused_ref[0])
    def _():
        ys_ref[...] = jnp.zeros_like(ys_ref)


def _experts(block_e, n_used, xs, w1, b1, w2, b2, bm):
    R, d = xs.shape
    d_ff = w2.shape[1]
    nb = R // bm

    def rows(i, be, nu):
        return (jnp.minimum(i, nu[0] - 1), 0)

    def wsel(i, be, nu):
        return (be[i], 0, 0)

    return pl.pallas_call(
        functools.partial(_expert_kernel, d_ff),
        grid_spec=pltpu.PrefetchScalarGridSpec(
            num_scalar_prefetch=2,
            grid=(nb,),
            in_specs=[pl.BlockSpec((bm, d), rows),
                      pl.BlockSpec((1, d, 2 * d_ff), wsel), pl.BlockSpec((1, 1, 2 * d_ff), wsel),
                      pl.BlockSpec((1, d_ff, d), wsel), pl.BlockSpec((1, 1, d), wsel)],
            out_specs=pl.BlockSpec((bm, d), lambda i, be, nu: (i, 0)),
            scratch_shapes=[pltpu.VMEM((d, 2 * d_ff), BF16), pltpu.VMEM((d_ff, d), BF16)]),
        out_shape=jax.ShapeDtypeStruct((R, d), F32),
        compiler_params=_cparams(("arbitrary",)),
        name="moe_experts",
    )(block_e, n_used, xs, w1, b1, w2, b2)


def _combine_kernel(dest_ref, gate_ref, x1_ref, gf_ref, ys_ref, o_ref, buf_ref, sem):
    tm = x1_ref.shape[0]

    def copy(t, k):
        return pltpu.make_async_copy(ys_ref.at[pl.ds(dest_ref[t * TOP_K + k], 1)],
                                     buf_ref.at[k, pl.ds(t, 1)], sem)

    def start(t, c):
        for k in range(TOP_K):
            copy(t, k).start()
        return c

    def wait(t, c):
        for k in range(TOP_K):
            copy(t, k).wait()
        return c

    lax.fori_loop(0, tm, start, 0)
    lax.fori_loop(0, tm, wait, 0)
    gate = gate_ref[...]
    y = x1_ref[...]
    for k in range(TOP_K):
        y = y + gate[:, k:k + 1] * buf_ref[k]
    o_ref[...] = y * lax.rsqrt(jnp.mean(y * y, axis=-1, keepdims=True) + NORM_EPS) * gf_ref[...]


def _combine(dest_flat, gate, x1, gf, ys, tm):
    rows, d = x1.shape
    return pl.pallas_call(
        _combine_kernel,
        grid=(rows // tm,),
        in_specs=[pl.BlockSpec((tm * TOP_K,), lambda i: (i,), memory_space=pltpu.SMEM),
                  pl.BlockSpec((tm, TOP_K), lambda i: (i, 0)),
                  pl.BlockSpec((tm, d), lambda i: (i, 0)),
                  pl.BlockSpec((1, d), lambda i: (0, 0)),
                  pl.BlockSpec(memory_space=pl.ANY)],
        out_specs=pl.BlockSpec((tm, d), lambda i: (i, 0)),
        out_shape=jax.ShapeDtypeStruct((rows, d), F32),
        scratch_shapes=[pltpu.VMEM((TOP_K, tm, d), F32), pltpu.SemaphoreType.DMA(())],
        compiler_params=_cparams(("arbitrary",)),
        name="moe_combine",
    )(dest_flat, gate, x1, gf, ys)


def _hi_lo(w):
    hi = w.astype(BF16)
    return hi, (w - hi.astype(F32)).astype(BF16)


def _rope_tables(pos):
    half = ROT_DIM // 2
    inv = ROPE_THETA ** (-jnp.arange(0, ROT_DIM, 2, dtype=F32) / ROT_DIM)
    ang = pos.astype(F32)[:, None] * inv[None, :]
    cos, sin = jnp.cos(ang), jnp.sin(ang)
    n = pos.shape[0]
    pad1 = jnp.ones((n, HEAD_DIM - ROT_DIM), F32)
    pad0 = jnp.zeros((n, HEAD_DIM - ROT_DIM), F32)
    cos_h = jnp.concatenate([cos, cos, pad1], axis=1)
    sin_h = jnp.concatenate([-sin, sin, pad0], axis=1)
    return jnp.tile(cos_h, (1, LANES // HEAD_DIM)), jnp.tile(sin_h, (1, LANES // HEAD_DIM))


def _pairs_from_state(S):
    H = S.shape[0]
    St = jnp.swapaxes(S, 1, 2).reshape(H // 2, 2, HEAD_DIM, HEAD_DIM)
    z = jnp.zeros_like(St[:, 0])
    top = jnp.concatenate([St[:, 0], z], axis=2)
    bot = jnp.concatenate([z, St[:, 1]], axis=2)
    return jnp.concatenate([top, bot], axis=1)


def _state_from_pairs(Sp):
    a = Sp[:, :HEAD_DIM, :HEAD_DIM]
    b = Sp[:, HEAD_DIM:, HEAD_DIM:]
    St = jnp.stack([a, b], axis=1).reshape(-1, HEAD_DIM, HEAD_DIM)
    return jnp.swapaxes(St, 1, 2)


def kernel(x_prompt, x_sample, state_rwkv_wkv, state_rwkv_shift, cache_swa_k, cache_swa_v, norm1_g, w_in, mu_shift, decay_w0, decay_w2, aaa_a0, aaa_w2, gate_w2, k_k, k_a, r_k, lnx_g, lnx_b, attn_sinks, w_out, norm2_g, w_router, b_router, w_mlp1, b_mlp1, w_mlp2, b_mlp2, norm_f_g):
    depth = w_in.shape[0]
    assert depth == 1 and x_prompt.shape[0] == 1 and x_sample.shape[1] == 1
    T, d = x_prompt.shape[1], x_prompt.shape[2]
    B = x_sample.shape[0]
    past_len = PAST_LEN
    H = state_rwkv_wkv.shape[2]
    rw_w = H * HEAD_DIM
    n_pairs = H // 2
    n_q = attn_sinks.shape[1]
    n_kv = cache_swa_k.shape[3]
    q_cols = n_q * HEAD_DIM
    kv_cols = n_kv * HEAD_DIM
    rw_cols = state_rwkv_shift.shape[2]
    assert rw_cols == 3 * rw_w + 2 * HEAD_DIM + PAIR and kv_cols == LANES
    assert T % RW_TILE == 0 and B % ROW_TILE == 0 and B % 8 == 0
    wlen = cache_swa_k.shape[2]
    l = 0

    whi, wlo = _hi_lo(w_in[l])
    zero_half = jnp.zeros((HEAD_DIM, rw_w), F32)
    pp = dict(mu=mu_shift[l][None], w0=decay_w0[l][None],
              dw2=jnp.concatenate([decay_w2[l], zero_half], axis=0),
              a0=aaa_a0[l][None], aw2=jnp.concatenate([zero_half, aaa_w2[l]], axis=0),
              gw2=gate_w2[l], kk=k_k[l][None], ka=k_a[l][None], rk=r_k[l].reshape(1, rw_w),
              lng=lnx_g[l][None], lnb=lnx_b[l][None])
    wa_hi, wa_lo = _hi_lo(w_out[l][:rw_w])
    wb_hi, wb_lo = _hi_lo(w_out[l][rw_w:])
    n_e = w_router.shape[2]
    wr = jnp.pad(w_router[l], ((0, 0), (0, LANES - n_e)))
    br = jnp.concatenate([b_router[l], jnp.full((LANES - n_e,), NEG_BIG, F32)])[None]
    wp = dict(wa_hi=wa_hi, wa_lo=wa_lo, wb_hi=wb_hi, wb_lo=wb_lo, g2=norm2_g[l][None], wr=wr, br=br)
    g1 = norm1_g[l][None]

    xp = x_prompt[0]
    cos_p, sin_p = _rope_tables(jnp.arange(T))
    prw_p, q_p, k_p, v_p = _inproj(xp, g1, whi, wlo, cos_p, sin_p, 256, rw_cols, q_cols, kv_cols)
    s0_p = jnp.zeros((n_pairs, PAIR, PAIR), F32)
    shift0_p = jnp.zeros((1, rw_cols), F32)
    ya_p, sfin_p = _rwkv_prompt(prw_p, shift0_p, s0_p, pp, RW_TILE)
    yb_p = _attn_prompt(q_p, k_p, v_p, attn_sinks[l], n_q, n_kv)

    xs_ = x_sample[:, 0]
    cos_s, sin_s = _rope_tables(jnp.full((B,), past_len))
    prw_s, q_s, k_s, v_s = _inproj(xs_, g1, whi, wlo, cos_s, sin_s, ROW_TILE, rw_cols, q_cols, kv_cols)
    s_flat = state_rwkv_wkv[l].reshape(B, H * HEAD_DIM * HEAD_DIM)
    ya_s, snew_flat = _rwkv_step(prw_s, state_rwkv_shift[l], s_flat, pp, n_pairs)
    o2, kc_new, vc_new = _attn_step(q_s.reshape(B * n_q, HEAD_DIM), k_s, v_s,
                                    cache_swa_k[l].reshape(B, wlen, kv_cols),
                                    cache_swa_v[l].reshape(B, wlen, kv_cols),
                                    attn_sinks[l][:, None], n_q, n_kv, past_len)
    yb_s = o2.reshape(B, q_cols)

    rows = T + B
    bufs = (jnp.zeros((rows, d), F32), jnp.zeros((rows, d), F32), jnp.zeros((rows, LANES), F32))
    bufs = _post(xp, ya_p, yb_p, wp, bufs, 0, 256)
    x1, h2, logits = _post(xs_, ya_s, yb_s, wp, bufs, T, ROW_TILE)

    idx, gate, rank, counts = _route(logits, ROW_TILE)
    counts = counts[0, :n_e]
    padded = (counts + MOE_BM - 1) // MOE_BM * MOE_BM
    pend = jnp.cumsum(padded)
    pstart = pend - padded
    n_blocks = -(-(rows * TOP_K) // MOE_BM) + n_e
    block_e = jnp.minimum(jnp.searchsorted(pend, jnp.arange(n_blocks) * MOE_BM, side="right"),
                          n_e - 1).astype(jnp.int32)
    n_used = (pend[-1] // MOE_BM).astype(jnp.int32)[None]
    dest_flat = (pstart[idx] + rank).astype(jnp.int32).reshape(rows * TOP_K)

    xs_sorted = _scatter(dest_flat, h2, jnp.zeros((n_blocks * MOE_BM, d), F32), ROW_TILE)
    ys_sorted = _experts(block_e, n_used, xs_sorted, w_mlp1[l], b_mlp1[l][:, None], w_mlp2[l],
                         b_mlp2[l][:, None], MOE_BM)
    y_all = _combine(dest_flat, gate, x1, norm_f_g[None], ys_sorted, ROW_TILE)

    sdt = state_rwkv_wkv.dtype
    return (y_all[:T][None], y_all[T:][:, None],
            _state_from_pairs(sfin_p)[None, None].astype(sdt), prw_p[T - 1][None, None],
            k_p[T - min(WINDOW, T):].reshape(1, 1, -1, n_kv, HEAD_DIM),
            v_p[T - min(WINDOW, T):].reshape(1, 1, -1, n_kv, HEAD_DIM),
            snew_flat.reshape(1, B, H, HEAD_DIM, HEAD_DIM).astype(sdt), prw_s[None],
            kc_new.reshape(1, B, wlen, n_kv, HEAD_DIM), vc_new.reshape(1, B, wlen, n_kv, HEAD_DIM))
```

```python
import functools

import jax
import jax.numpy as jnp
from jax import lax
from jax.experimental import pallas as pl
from jax.experimental.pallas import tpu as pltpu

F32 = jnp.float32
BF16 = jnp.bfloat16

LANES = 128
HEAD_DIM = 64
PAIR = 2 * HEAD_DIM
CHUNK = 64
RW_TILE = 256
RW_PAIRS_PER_STEP = 4
ROT_DIM = 16
ROPE_THETA = 500000.0
WINDOW = 128
PAST_LEN = 16384
ATT_BLOCK = 128
N_EXPERTS = 32
TOP_K = 4
SWIGLU_ALPHA = 1.702
SWIGLU_LIMIT = 7.0
NORM_EPS = 1e-5
LNX_EPS = HEAD_DIM * 1e-5
MOE_BM = 512
ROW_TILE = 128
NEG_BIG = -1e30
VMEM_LIMIT = 52 * 1024 * 1024

NN = (((1,), (0,)), ((), ()))
NT = (((1,), (1,)), ((), ()))


def _mm(a, b, dn=NN):
    return lax.dot_general(a, b, dn, preferred_element_type=F32)


def _split2(a):
    hi = a.astype(BF16)
    lo = (a - hi.astype(F32)).astype(BF16)
    return hi, lo


def _split3(a):
    hi = a.astype(BF16)
    r1 = a - hi.astype(F32)
    mid = r1.astype(BF16)
    lo = (r1 - mid.astype(F32)).astype(BF16)
    return hi, mid, lo


def _dot3(a, b, dn=NN):
    ah, al = _split2(a)
    bh, bl = _split2(b)
    return _mm(ah, bh, dn) + (_mm(ah, bl, dn) + _mm(al, bh, dn))


def _pdot(a, b, dn=NN):
    return _mm(a[0], b[0], dn) + (_mm(a[0], b[1], dn) + _mm(a[1], b[0], dn))


def _stack_rows(a, b):
    return (jnp.concatenate([a[0], b[0]], axis=0), jnp.concatenate([a[1], b[1]], axis=0))


def _dot_sel_l(sel, b, dn=NN):
    b0, b1, b2 = _split3(b)
    return _mm(sel, b0, dn) + (_mm(sel, b1, dn) + _mm(sel, b2, dn))


def _dot_sel_r(a, sel, dn=NN):
    a0, a1, a2 = _split3(a)
    return _mm(a0, sel, dn) + (_mm(a1, sel, dn) + _mm(a2, sel, dn))


def _iota(shape, dim):
    return lax.broadcasted_iota(jnp.int32, shape, dim)


def _seg_matrix():
    return ((_iota((PAIR, PAIR), 0) // HEAD_DIM) == (_iota((PAIR, PAIR), 1) // HEAD_DIM)).astype(BF16)


def _sigmoid(x):
    return 1.0 / (1.0 + jnp.exp(-x))


def _cparams(sem, vmem=VMEM_LIMIT):
    return pltpu.CompilerParams(dimension_semantics=sem, vmem_limit_bytes=vmem)


def _rope_slab(x, cos, sin_signed):
    lane = _iota(x.shape, 1) % HEAD_DIM
    up = pltpu.roll(x, LANES - ROT_DIM // 2, axis=1)
    down = pltpu.roll(x, ROT_DIM // 2, axis=1)
    partner = jnp.where(lane < ROT_DIM // 2, up, down)
    return x * cos + partner * sin_signed


def _inproj_kernel(rw_cols, q_cols, kv_cols, x_ref, g_ref, whi_ref, wlo_ref, cos_ref, sin_ref,
                   prw_ref, q_ref, k_ref, v_ref):
    x = x_ref[...]
    h = x * lax.rsqrt(jnp.mean(x * x, axis=-1, keepdims=True) + NORM_EPS) * g_ref[...]
    hh, hl = _split2(h)
    whi = whi_ref[...]
    proj = _mm(hh, whi) + (_mm(hl, whi) + _mm(hh, wlo_ref[...]))
    prw_ref[...] = proj[:, :rw_cols]
    cos = cos_ref[...]
    sin = sin_ref[...]
    for c in range(q_cols // LANES):
        lo = rw_cols + c * LANES
        q_ref[:, c * LANES:(c + 1) * LANES] = _rope_slab(proj[:, lo:lo + LANES], cos, sin)
    ko = rw_cols + q_cols
    for c in range(kv_cols // LANES):
        k_ref[:, c * LANES:(c + 1) * LANES] = _rope_slab(proj[:, ko + c * LANES:ko + (c + 1) * LANES], cos, sin)
    v_ref[...] = proj[:, ko + kv_cols:ko + 2 * kv_cols]


def _inproj(x, g, whi, wlo, cos_t, sin_t, tm, rw_cols, q_cols, kv_cols):
    rows, d = x.shape
    cols = whi.shape[1]
    full = lambda i: (0, 0)
    row = lambda i: (i, 0)
    return pl.pallas_call(
        functools.partial(_inproj_kernel, rw_cols, q_cols, kv_cols),
        grid=(rows // tm,),
        in_specs=[pl.BlockSpec((tm, d), row), pl.BlockSpec((1, d), full),
                  pl.BlockSpec((d, cols), full), pl.BlockSpec((d, cols), full),
                  pl.BlockSpec((tm, LANES), row), pl.BlockSpec((tm, LANES), row)],
        out_specs=[pl.BlockSpec((tm, rw_cols), row), pl.BlockSpec((tm, q_cols), row),
                   pl.BlockSpec((tm, kv_cols), row), pl.BlockSpec((tm, kv_cols), row)],
        out_shape=[jax.ShapeDtypeStruct((rows, rw_cols), F32), jax.ShapeDtypeStruct((rows, q_cols), F32),
                   jax.ShapeDtypeStruct((rows, kv_cols), F32), jax.ShapeDtypeStruct((rows, kv_cols), F32)],
        compiler_params=_cparams(("parallel",)),
        name="inproj",
    )(x, g, whi, wlo, cos_t, sin_t)


def _rwkv_tokenwise(pr, pk, pv, plo, pg, prev_r, prev_k, prev_v, prev_lo, prev_g,
                    mu_r, mu_k, mu_v, mu_lo, mu_g, w0, dw2, a0, aw2, gw2, kkp, kap, rkp, seg):
    r = pr + (prev_r - pr) * mu_r
    k = pk + (prev_k - pk) * mu_k
    v = pv + (prev_v - pv) * mu_v
    lo = plo + (prev_lo - plo) * mu_lo
    gd = pg + (prev_g - pg) * mu_g
    z = -(w0 + _dot3(jnp.tanh(lo), dw2))
    softplus = jnp.maximum(z, 0.0) + jnp.log(1.0 + jnp.exp(-jnp.abs(z)))
    logw = -jnp.exp(-softplus - 0.5)
    a = _sigmoid(a0 + _dot3(lo, aw2))
    g = _dot3(_sigmoid(gd), gw2)
    kk = k * kkp
    nrm = jnp.sqrt(_dot_sel_r(kk * kk, seg))
    kk = kk / jnp.maximum(nrm, 1e-12)
    k2 = k * (1.0 + (a - 1.0) * kap)
    bonus = _dot_sel_r(r * k2 * rkp, seg) * v
    return r, k2, v, logw, -kk, kk * a, g, bonus


def _rwkv_finish(y, bonus, g, lng, lnb, seg):
    mu = _dot_sel_r(y, seg) * (1.0 / HEAD_DIM)
    d = y - mu
    var = _dot_sel_r(d * d, seg) * (1.0 / HEAD_DIM)
    yn = d * lax.rsqrt(var + LNX_EPS) * lng + lnb
    return (yn + bonus) * g


def _rwkv_prompt_kernel(pps, pr_ref, pk_ref, pv_ref, plo_ref, pg_ref,
                        hr_ref, hk_ref, hv_ref, hlo_ref, hg_ref,
                        s0r_ref, s0k_ref, s0v_ref, s0lo_ref, s0g_ref,
                        mur_ref, muk_ref, muv_ref, mulo_ref, mug_ref,
                        w0_ref, dw2_ref, a0_ref, aw2_ref, gw2_ref, kk_ref, ka_ref, rk_ref,
                        lng_ref, lnb_ref, sin_ref,
                        y_ref, sout_ref, st_ref):
    i = pl.program_id(1)
    n_i = pl.num_programs(1)
    tt = pr_ref.shape[0]

    @pl.when(i == 0)
    def _():
        st_ref[...] = sin_ref[...]

    row = _iota((tt, PAIR), 0)

    def prev_of(cur, halo_row, s0_row):
        first = jnp.where(i == 0, s0_row, halo_row)
        return jnp.where(row == 0, first, pltpu.roll(cur, 1, axis=0))

    plo = plo_ref[...]
    pg = pg_ref[...]
    prev_lo = prev_of(plo, hlo_ref[7:8, :], s0lo_ref[...])
    prev_g = prev_of(pg, hg_ref[7:8, :], s0g_ref[...])
    ti = _iota((tt, tt), 0)
    tj = _iota((tt, tt), 1)
    same_chunk = (ti // CHUNK) == (tj // CHUNK)
    incl = same_chunk & (tj <= ti)
    strict = same_chunk & (tj < ti)
    seg = _seg_matrix()
    lane = _iota((tt, PAIR), 1)
    eye = (ti == tj).astype(F32)
    pairs = []
    for p in range(pps):
        ls = slice(p * PAIR, (p + 1) * PAIR)
        pr, pk, pv = pr_ref[:, ls], pk_ref[:, ls], pv_ref[:, ls]
        r, k2, v, logw, nkk, b, g, bonus = _rwkv_tokenwise(
            pr, pk, pv, plo, pg,
            prev_of(pr, hr_ref[7:8, ls], s0r_ref[:, ls]), prev_of(pk, hk_ref[7:8, ls], s0k_ref[:, ls]),
            prev_of(pv, hv_ref[7:8, ls], s0v_ref[:, ls]), prev_lo, prev_g,
            mur_ref[:, ls], muk_ref[:, ls], muv_ref[:, ls], mulo_ref[...], mug_ref[...],
            w0_ref[:, ls], dw2_ref[:, ls], a0_ref[:, ls], aw2_ref[:, ls], gw2_ref[:, ls],
            kk_ref[:, ls], ka_ref[:, ls], rk_ref[:, ls], seg)
        pairs.append(dict(ls=ls, r=r, k2=k2, v=v, logw=logw, nkk=nkk, b=b, g=g, bonus=bonus))

    incl_b = incl.astype(BF16)
    for q in pairs:
        q["cs"] = _dot_sel_l(incl_b, q["logw"])
    for q in pairs:
        cs = q["cs"]
        gam = jnp.exp(cs)
        inv = jnp.exp(-cs)
        q["a_t"] = jnp.exp(cs - q["logw"]) * q["nkk"]
        q["r_t"] = gam * q["r"]
        q["bt_T"] = (q["b"] * inv).T
        q["kt_T"] = (q["k2"] * inv).T
        q["gam_T"] = gam.T
        q["bk_T"] = _split2(jnp.concatenate([q["bt_T"], q["kt_T"]], axis=1))

    heads = []
    for q in pairs:
        for hh in range(2):
            hm = (lane // HEAD_DIM) == hh
            heads.append(dict(q=q, a=jnp.where(hm, q["a_t"], 0.0), r=jnp.where(hm, q["r_t"], 0.0),
                              v=jnp.where(hm, q["v"], 0.0)))
    for h in heads:
        h["g"] = _pdot(_split2(jnp.concatenate([h["a"], h["r"]], axis=0)), h["q"]["bk_T"])
    for h in heads:
        gmat = h["g"]
        l_ab = jnp.where(strict, gmat[:tt, :tt], 0.0)
        h["l_ak_m_rk"] = _split2(jnp.concatenate([jnp.where(strict, gmat[:tt, tt:], 0.0),
                                                  jnp.where(incl, gmat[tt:, tt:], 0.0)], axis=0))
        h["m_rb"] = _split2(jnp.where(incl, gmat[tt:, :tt], 0.0))
        h["tm"] = eye + l_ab
        h["lps"] = _split2(l_ab)
    for h in heads:
        h["lps"] = _split2(_pdot(h["lps"], h["lps"]))
    for _ in range(4):
        for h in heads:
            h["both"] = _pdot(_stack_rows(_split2(h["tm"]), h["lps"]), h["lps"])
        for h in heads:
            h["tm"] = h["tm"] + h["both"][:tt]
            h["lps"] = _split2(h["both"][tt:])
    for h in heads:
        h["pq"] = _pdot(h["l_ak_m_rk"], _split2(h["v"]))
        h["tm"] = h["tm"] + _pdot(_split2(h["tm"]), h["lps"])
    for h in heads:
        h["tx"] = _pdot(_split2(h["tm"]), _split2(jnp.concatenate([h["a"], h["pq"][:tt]], axis=1)))
    for h in heads:
        h["rx"] = _pdot(h["m_rb"], _split2(h["tx"]))
    for n, q in enumerate(pairs):
        h0, h1 = heads[2 * n], heads[2 * n + 1]
        q["tatp"] = _split2(h0["tx"] + h1["tx"])
        ryc = (h0["rx"] + h1["rx"]) + jnp.concatenate([h0["r"] + h1["r"], h0["pq"][tt:] + h1["pq"][tt:]], axis=1)
        q["ry"] = ryc[:, :PAIR]
        q["yc"] = ryc[:, PAIR:]
        q["v_s"] = _split2(q["v"])
        q["bts"] = _split2(q["bt_T"])
        q["kts"] = _split2(q["kt_T"])
        q["s"] = st_ref[n]

    bd = seg.astype(F32)
    eye_p = (_iota((PAIR, PAIR), 0) == _iota((PAIR, PAIR), 1)).astype(F32)
    col_t = _iota((PAIR, tt), 1)
    zb = jnp.zeros((PAIR, tt), BF16)
    n_chunks = tt // CHUNK
    for c in range(n_chunks):
        cm = (col_t // CHUNK) == c
        for q in pairs:
            bt_c = (jnp.where(cm, q["bts"][0], zb), jnp.where(cm, q["bts"][1], zb))
            kt_c = (jnp.where(cm, q["kts"][0], zb), jnp.where(cm, q["kts"][1], zb))
            dcol = q["gam_T"][:, (c + 1) * CHUNK - 1:(c + 1) * CHUNK]
            bx = _pdot(bt_c, q["tatp"])
            q["mc", c] = _split2(dcol * (eye_p + bd * bx[:, :PAIR]))
            q["nc", c] = dcol * (bd * (bx[:, PAIR:] + _pdot(kt_c, q["v_s"])))
    for c in range(n_chunks):
        sl = slice(c * CHUNK, (c + 1) * CHUNK)
        for q in pairs:
            ss = _split2(q["s"])
            q["y", c] = _pdot(_split2(q["ry"][sl]), ss) + q["yc"][sl]
            q["s"] = _pdot(q["mc", c], ss) + q["nc", c]
    for q in pairs:
        y = jnp.concatenate([q["y", c] for c in range(n_chunks)], axis=0)
        y_ref[:, q["ls"]] = _rwkv_finish(y, q["bonus"], q["g"], lng_ref[:, q["ls"]], lnb_ref[:, q["ls"]], seg)
    for n, q in enumerate(pairs):
        st_ref[n] = q["s"]

    @pl.when(i == n_i - 1)
    def _():
        sout_ref[...] = st_ref[...]


def _rwkv_prompt(prw, shift0, s0_pairs, pp, tt):
    T = prw.shape[0]
    n_pairs = s0_pairs.shape[0]
    pps = RW_PAIRS_PER_STEP
    n_grp = n_pairs // pps
    gw = pps * PAIR
    wcols = n_pairs * PAIR
    lo_col = 3 * wcols
    g_col = lo_col + PAIR
    hb = tt // 8

    def cur(off):
        return pl.BlockSpec((tt, gw), lambda p, i: (i, off // gw + p))

    def cur_fixed(col):
        return pl.BlockSpec((tt, PAIR), lambda p, i: (i, col // PAIR))

    def halo(off):
        return pl.BlockSpec((8, gw), lambda p, i: (jnp.maximum(i * hb - 1, 0), off // gw + p))

    def halo_fixed(col):
        return pl.BlockSpec((8, PAIR), lambda p, i: (jnp.maximum(i * hb - 1, 0), col // PAIR))

    def vec(off):
        return pl.BlockSpec((1, gw), lambda p, i: (0, off // gw + p))

    def vec_fixed(col):
        return pl.BlockSpec((1, PAIR), lambda p, i: (0, col // PAIR))

    def wmat(rows):
        return pl.BlockSpec((rows, gw), lambda p, i: (0, p))

    in_specs = ([cur(0), cur(wcols), cur(2 * wcols), cur_fixed(lo_col), cur_fixed(g_col)]
                + [halo(0), halo(wcols), halo(2 * wcols), halo_fixed(lo_col), halo_fixed(g_col)]
                + [vec(0), vec(wcols), vec(2 * wcols), vec_fixed(lo_col), vec_fixed(g_col)]
                + [vec(0), vec(wcols), vec(2 * wcols), vec_fixed(lo_col), vec_fixed(g_col)]
                + [vec(0), wmat(PAIR), vec(0), wmat(PAIR), wmat(PAIR), vec(0), vec(0), vec(0), vec(0), vec(0)]
                + [pl.BlockSpec((pps, PAIR, PAIR), lambda p, i: (p, 0, 0))])
    args = ([prw] * 5 + [prw] * 5 + [shift0] * 5 + [pp["mu"]] * 5
            + [pp["w0"], pp["dw2"], pp["a0"], pp["aw2"], pp["gw2"], pp["kk"], pp["ka"], pp["rk"],
               pp["lng"], pp["lnb"], s0_pairs])
    return pl.pallas_call(
        functools.partial(_rwkv_prompt_kernel, pps),
        grid=(n_grp, T // tt),
        in_specs=in_specs,
        out_specs=[pl.BlockSpec((tt, gw), lambda p, i: (i, p)),
                   pl.BlockSpec((pps, PAIR, PAIR), lambda p, i: (p, 0, 0))],
        out_shape=[jax.ShapeDtypeStruct((T, wcols), F32),
                   jax.ShapeDtypeStruct((n_pairs, PAIR, PAIR), F32)],
        scratch_shapes=[pltpu.VMEM((pps, PAIR, PAIR), F32)],
        compiler_params=_cparams(("parallel", "arbitrary")),
        name="rwkv_prompt",
    )(*args)


def _rwkv_step_kernel(slabs_per_step, pr_ref, pk_ref, pv_ref, plo_ref, pg_ref,
                      sr_ref, sk_ref, sv_ref, slo_ref, sg_ref,
                      mur_ref, muk_ref, muv_ref, mulo_ref, mug_ref,
                      w0_ref, dw2_ref, a0_ref, aw2_ref, gw2_ref, kk_ref, ka_ref, rk_ref,
                      lng_ref, lnb_ref, s_ref,
                      y_ref, snew_ref, yacc_ref):
    j = pl.program_id(1)
    n_j = pl.num_programs(1)
    seg = _seg_matrix()
    r, k2, v, logw, nkk, b, g, bonus = _rwkv_tokenwise(
        pr_ref[...], pk_ref[...], pv_ref[...], plo_ref[...], pg_ref[...],
        sr_ref[...], sk_ref[...], sv_ref[...], slo_ref[...], sg_ref[...],
        mur_ref[...], muk_ref[...], muv_ref[...], mulo_ref[...], mug_ref[...],
        w0_ref[...], dw2_ref[...], a0_ref[...], aw2_ref[...], gw2_ref[...],
        kk_ref[...], ka_ref[...], rk_ref[...], seg)
    w = jnp.exp(logw)

    @pl.when(j == 0)
    def _():
        yacc_ref[...] = jnp.zeros_like(yacc_ref)

    slabs_per_head = HEAD_DIM // 2
    ci = _iota((PAIR, PAIR), 0)
    li = _iota((PAIR, PAIR), 1)
    yacc = yacc_ref[...]
    for t in range(slabs_per_step):
        slab = j * slabs_per_step + t
        hh = slab // slabs_per_head
        i0 = 2 * (slab % slabs_per_head)
        dup = ((ci == hh * HEAD_DIM + li % HEAD_DIM)).astype(BF16)
        selv = (ci == hh * HEAD_DIM + i0 + li // HEAD_DIM).astype(BF16)
        sely = ((ci % HEAD_DIM == 0) & (li == hh * HEAD_DIM + i0 + ci // HEAD_DIM)).astype(BF16)
        s = s_ref[:, t * PAIR:(t + 1) * PAIR]
        sa = _dot_sel_r(s * _dot_sel_r(nkk, dup), seg)
        s_new = (s * _dot_sel_r(w, dup) + sa * _dot_sel_r(b, dup)
                 + _dot_sel_r(v, selv) * _dot_sel_r(k2, dup))
        snew_ref[:, t * PAIR:(t + 1) * PAIR] = s_new
        yred = _dot_sel_r(s_new * _dot_sel_r(r, dup), seg)
        yacc = yacc + _dot_sel_r(yred, sely)
    yacc_ref[...] = yacc

    @pl.when(j == n_j - 1)
    def _():
        y_ref[...] = _rwkv_finish(yacc, bonus, g, lng_ref[...], lnb_ref[...], seg)


def _rwkv_step(prw, shift, s_flat, pp, n_pairs):
    B = prw.shape[0]
    lanes_per_pair = 2 * HEAD_DIM * HEAD_DIM
    blk = 1024
    slabs_per_step = blk // PAIR
    steps = lanes_per_pair // blk
    lo_blk = 3 * n_pairs
    g_blk = lo_blk + 1

    def cur(off):
        return pl.BlockSpec((B, PAIR), lambda p, j: (0, off + p))

    def cur_fixed(b_):
        return pl.BlockSpec((B, PAIR), lambda p, j: (0, b_))

    def vec(off):
        return pl.BlockSpec((1, PAIR), lambda p, j: (0, off + p))

    def vec_fixed(b_):
        return pl.BlockSpec((1, PAIR), lambda p, j: (0, b_))

    def wmat(rows):
        return pl.BlockSpec((rows, PAIR), lambda p, j: (0, p))

    sspec = pl.BlockSpec((B, blk), lambda p, j: (0, p * steps + j))
    in_specs = ([cur(0), cur(n_pairs), cur(2 * n_pairs), cur_fixed(lo_blk), cur_fixed(g_blk)] * 2
                + [vec(0), vec(n_pairs), vec(2 * n_pairs), vec_fixed(lo_blk), vec_fixed(g_blk)]
                + [vec(0), wmat(PAIR), vec(0), wmat(PAIR), wmat(PAIR), vec(0), vec(0), vec(0), vec(0), vec(0)]
                + [sspec])
    args = ([prw] * 5 + [shift] * 5 + [pp["mu"]] * 5
            + [pp["w0"], pp["dw2"], pp["a0"], pp["aw2"], pp["gw2"], pp["kk"], pp["ka"], pp["rk"],
               pp["lng"], pp["lnb"], s_flat])
    return pl.pallas_call(
        functools.partial(_rwkv_step_kernel, slabs_per_step),
        grid=(n_pairs, steps),
        in_specs=in_specs,
        out_specs=[pl.BlockSpec((B, PAIR), lambda p, j: (0, p)), sspec],
        out_shape=[jax.ShapeDtypeStruct((B, n_pairs * PAIR), F32),
                   jax.ShapeDtypeStruct(s_flat.shape, F32)],
        scratch_shapes=[pltpu.VMEM((B, PAIR), F32)],
        compiler_params=_cparams(("parallel", "arbitrary")),
        name="rwkv_step",
    )(*args)


def _attn_prompt_kernel(n_q, group, sink_ref, q_ref, kc_ref, kp_ref, vc_ref, vp_ref, o_ref):
    blk = q_ref.shape[0]
    q = q_ref[...] * (HEAD_DIM ** -0.5)
    kband = jnp.concatenate([kp_ref[...], kc_ref[...]], axis=0)
    vband = jnp.concatenate([vp_ref[...], vc_ref[...]], axis=0)
    i = pl.program_id(0)
    rq = _iota((blk, 2 * blk), 0)
    ck = _iota((blk, 2 * blk), 1)
    dist = rq - ck + blk
    kpos = i * blk - blk + ck
    valid = (dist >= 0) & (dist < WINDOW) & (kpos >= 0)
    for h in range(n_q):
        gk = h // group
        qh = q[:, h * HEAD_DIM:(h + 1) * HEAD_DIM]
        kh = kband[:, gk * HEAD_DIM:(gk + 1) * HEAD_DIM]
        vh = vband[:, gk * HEAD_DIM:(gk + 1) * HEAD_DIM]
        s = jnp.where(valid, _dot3(qh, kh, NT), NEG_BIG)
        sink = sink_ref[h]
        m = jnp.maximum(jnp.max(s, axis=-1, keepdims=True), sink)
        p = jnp.exp(s - m)
        denom = jnp.sum(p, axis=-1, keepdims=True) + jnp.exp(sink - m)
        o_ref[:, h * HEAD_DIM:(h + 1) * HEAD_DIM] = _dot3(p, vh) / denom


def _attn_prompt(q, k, v, sinks, n_q, n_kv):
    T, qw = q.shape
    kvw = k.shape[1]
    blk = ATT_BLOCK
    curm = lambda i: (i, 0)
    prevm = lambda i: (jnp.maximum(i - 1, 0), 0)
    return pl.pallas_call(
        functools.partial(_attn_prompt_kernel, n_q, n_q // n_kv),
        grid=(T // blk,),
        in_specs=[pl.BlockSpec(memory_space=pltpu.SMEM),
                  pl.BlockSpec((blk, qw), curm),
                  pl.BlockSpec((blk, kvw), curm), pl.BlockSpec((blk, kvw), prevm),
                  pl.BlockSpec((blk, kvw), curm), pl.BlockSpec((blk, kvw), prevm)],
        out_specs=pl.BlockSpec((blk, qw), curm),
        out_shape=jax.ShapeDtypeStruct((T, qw), F32),
        compiler_params=_cparams(("parallel",)),
        name="attn_prompt",
    )(sinks, q, k, k, v, v)


def _attn_step_kernel(n_q, group, pos0, sink_ref, q_ref, kn_ref, vn_ref, kc_ref, vc_ref,
                      o_ref, ko_ref, vo_ref):
    bb, wlen, kvw = kc_ref.shape
    lane = _iota((n_q, kvw), 1)
    rowh = _iota((n_q, kvw), 0)
    mine = (lane // HEAD_DIM) == (rowh // group)
    dupm = (_iota((HEAD_DIM, kvw), 0) == _iota((HEAD_DIM, kvw), 1) % HEAD_DIM).astype(BF16)
    fold = (_iota((kvw, HEAD_DIM), 0) % HEAD_DIM == _iota((kvw, HEAD_DIM), 1)).astype(BF16)
    kidx = _iota((n_q, wlen), 1)
    dist = wlen - kidx
    valid = (dist < WINDOW) & (pos0 - dist >= 0)
    rk = _iota((wlen, kvw), 0)
    sink = sink_ref[...]
    for t in range(bb):
        qh = q_ref[t * n_q:(t + 1) * n_q, :] * (HEAD_DIM ** -0.5)
        qm = jnp.where(mine, _dot_sel_r(qh, dupm), 0.0)
        kc = kc_ref[t]
        vc = vc_ref[t]
        kn = kn_ref[t:t + 1, :]
        vn = vn_ref[t:t + 1, :]
        s = jnp.where(valid, _dot3(qm, kc, NT), NEG_BIG)
        s_new = jnp.sum(qm * kn, axis=-1, keepdims=True)
        m = jnp.maximum(jnp.maximum(jnp.max(s, axis=-1, keepdims=True), s_new), sink)
        p = jnp.exp(s - m)
        p_new = jnp.exp(s_new - m)
        denom = jnp.sum(p, axis=-1, keepdims=True) + p_new + jnp.exp(sink - m)
        res = (_dot3(p, vc) + p_new * vn) / denom
        o_ref[t * n_q:(t + 1) * n_q, :] = _dot_sel_r(jnp.where(mine, res, 0.0), fold)
        ko_ref[t] = jnp.where(rk == wlen - 1, kn, pltpu.roll(kc, wlen - 1, axis=0))
        vo_ref[t] = jnp.where(rk == wlen - 1, vn, pltpu.roll(vc, wlen - 1, axis=0))


def _attn_step(q2, k_new, v_new, k_cache, v_cache, sinks_col, n_q, n_kv, pos0):
    B, wlen, kvw = k_cache.shape
    bb = 8
    return pl.pallas_call(
        functools.partial(_attn_step_kernel, n_q, n_q // n_kv, pos0),
        grid=(B // bb,),
        in_specs=[pl.BlockSpec((n_q, 1), lambda i: (0, 0)),
                  pl.BlockSpec((bb * n_q, HEAD_DIM), lambda i: (i, 0)),
                  pl.BlockSpec((bb, kvw), lambda i: (i, 0)), pl.BlockSpec((bb, kvw), lambda i: (i, 0)),
                  pl.BlockSpec((bb, wlen, kvw), lambda i: (i, 0, 0)),
                  pl.BlockSpec((bb, wlen, kvw), lambda i: (i, 0, 0))],
        out_specs=[pl.BlockSpec((bb * n_q, HEAD_DIM), lambda i: (i, 0)),
                   pl.BlockSpec((bb, wlen, kvw), lambda i: (i, 0, 0)),
                   pl.BlockSpec((bb, wlen, kvw), lambda i: (i, 0, 0))],
        out_shape=[jax.ShapeDtypeStruct((B * n_q, HEAD_DIM), F32),
                   jax.ShapeDtypeStruct((B, wlen, kvw), F32),
                   jax.ShapeDtypeStruct((B, wlen, kvw), F32)],
        compiler_params=_cparams(("parallel",)),
        name="attn_step",
    )(sinks_col, q2, k_new, v_new, k_cache, v_cache)


def _post_kernel(x_ref, ya_ref, yb_ref, wahi_ref, walo_ref, wbhi_ref, wblo_ref, g2_ref, wr_ref, br_ref, cnt0_ref,
                 _x1_alias, _h2_alias, _gate_alias, _meta_alias,
                 x1_ref, h2_ref, gate_ref, meta_ref, cnt_ref, carry_ref):
    i = pl.program_id(0)

    @pl.when(i == 0)
    def _():
        carry_ref[...] = cnt0_ref[...]

    yah, yal = _split2(ya_ref[...])
    ybh, ybl = _split2(yb_ref[...])
    wahi = wahi_ref[...]
    wbhi = wbhi_ref[...]
    mix = (_mm(yah, wahi) + (_mm(yal, wahi) + _mm(yah, walo_ref[...]))
           + _mm(ybh, wbhi) + (_mm(ybl, wbhi) + _mm(ybh, wblo_ref[...])))
    x1 = x_ref[...] + mix
    h2 = x1 * lax.rsqrt(jnp.mean(x1 * x1, axis=-1, keepdims=True) + NORM_EPS) * g2_ref[...]
    x1_ref[...] = x1
    h2_ref[...] = h2

    l = _dot3(h2, wr_ref[...]) + br_ref[...]
    tm = l.shape[0]
    lane = _iota(l.shape, 1)
    vals, idxs = [], []
    for _ in range(TOP_K):
        m = jnp.max(l, axis=-1, keepdims=True)
        sel = jnp.min(jnp.where(l == m, lane, LANES), axis=-1, keepdims=True)
        vals.append(m)
        idxs.append(sel)
        l = jnp.where(lane == sel, -jnp.inf, l)
    es = [jnp.exp(v - vals[0]) for v in vals]
    tot = es[0] + es[1] + es[2] + es[3]
    onehot = jnp.zeros(l.shape, F32)
    for sel in idxs:
        onehot = onehot + (lane == sel).astype(F32)
    strict = (_iota((tm, tm), 1) < _iota((tm, tm), 0)).astype(BF16)
    before = _mm(strict, onehot.astype(BF16)) + carry_ref[...]
    for k in range(TOP_K):
        gate_ref[:, k:k + 1] = es[k] / tot
        meta_ref[:, k:k + 1] = idxs[k]
        meta_ref[:, TOP_K + k:TOP_K + k + 1] = jnp.sum(
            jnp.where(lane == idxs[k], before, 0.0), axis=-1, keepdims=True).astype(jnp.int32)
    carry_ref[...] = carry_ref[...] + jnp.sum(onehot, axis=0, keepdims=True)
    cnt_ref[...] = carry_ref[...]


def _post(x, ya, yb, wp, cnt0, bufs, row_off, tm):
    rows, d = x.shape
    half = ya.shape[1]
    ob = row_off // tm
    full = lambda i: (0, 0)
    row = lambda i: (i, 0)
    orow = lambda i: (ob + i, 0)
    anyspec = pl.BlockSpec(memory_space=pl.ANY)
    return pl.pallas_call(
        _post_kernel,
        grid=(rows // tm,),
        in_specs=[pl.BlockSpec((tm, d), row), pl.BlockSpec((tm, half), row), pl.BlockSpec((tm, half), row),
                  pl.BlockSpec((half, d), full), pl.BlockSpec((half, d), full),
                  pl.BlockSpec((half, d), full), pl.BlockSpec((half, d), full),
                  pl.BlockSpec((1, d), full), pl.BlockSpec((d, LANES), full), pl.BlockSpec((1, LANES), full),
                  pl.BlockSpec((1, LANES), full),
                  anyspec, anyspec, anyspec, anyspec],
        out_specs=[pl.BlockSpec((tm, d), orow), pl.BlockSpec((tm, d), orow),
                   pl.BlockSpec((tm, TOP_K), orow), pl.BlockSpec((tm, 2 * TOP_K), orow),
                   pl.BlockSpec((1, LANES), full)],
        out_shape=[jax.ShapeDtypeStruct(bufs[0].shape, F32), jax.ShapeDtypeStruct(bufs[1].shape, F32),
                   jax.ShapeDtypeStruct(bufs[2].shape, F32), jax.ShapeDtypeStruct(bufs[3].shape, jnp.int32),
                   jax.ShapeDtypeStruct((1, LANES), F32)],
        input_output_aliases={11: 0, 12: 1, 13: 2, 14: 3},
        scratch_shapes=[pltpu.VMEM((1, LANES), F32)],
        compiler_params=_cparams(("arbitrary",)),
        name="post",
    )(x, ya, yb, wp["wa_hi"], wp["wa_lo"], wp["wb_hi"], wp["wb_lo"], wp["g2"], wp["wr"], wp["br"], cnt0, *bufs)


def _scatter_kernel(n_e, pstart_ref, tail_ref, meta_ref, h_ref, xs_ref, zero_ref, sem, zsem):
    tm = h_ref.shape[0]
    bm = zero_ref.shape[0]

    @pl.when(pl.program_id(0) == 0)
    def _():
        zero_ref[...] = jnp.zeros_like(zero_ref)

        def zcopy(e):
            return pltpu.make_async_copy(zero_ref, xs_ref.at[pl.ds(pl.multiple_of(tail_ref[e], 8), bm)], zsem)

        for e in range(n_e):
            @pl.when(tail_ref[e] >= 0)
            def _():
                zcopy(e).start()
        for e in range(n_e):
            @pl.when(tail_ref[e] >= 0)
            def _():
                zcopy(e).wait()

    def copy(t, k):
        j = t * (2 * TOP_K) + k
        dest = pstart_ref[meta_ref[j]] + meta_ref[j + TOP_K]
        return pltpu.make_async_copy(h_ref.at[pl.ds(t, 1)], xs_ref.at[pl.ds(dest, 1)], sem)

    def start(t, c):
        for k in range(TOP_K):
            copy(t, k).start(priority=k % 2)
        return c

    def wait(t, c):
        for k in range(TOP_K):
            copy(t, k).wait()
        return c

    lax.fori_loop(0, tm, start, 0)
    lax.fori_loop(0, tm, wait, 0)


def _scatter(pstart, tail_row, meta_flat, h2, n_rows_sorted, bm, tm):
    rows, d = h2.shape
    n_e = pstart.shape[0]
    smem = pl.BlockSpec(memory_space=pltpu.SMEM)
    return pl.pallas_call(
        functools.partial(_scatter_kernel, n_e),
        grid=(rows // tm,),
        in_specs=[smem, smem,
                  pl.BlockSpec((tm * 2 * TOP_K,), lambda i: (i,), memory_space=pltpu.SMEM),
                  pl.BlockSpec((tm, d), lambda i: (i, 0))],
        out_specs=pl.BlockSpec(memory_space=pl.ANY),
        out_shape=jax.ShapeDtypeStruct((n_rows_sorted, d), F32),
        scratch_shapes=[pltpu.VMEM((bm, d), F32), pltpu.SemaphoreType.DMA(()), pltpu.SemaphoreType.DMA(())],
        compiler_params=_cparams(("arbitrary",)),
        name="moe_scatter",
    )(pstart, tail_row, meta_flat, h2)


def _expert_kernel(d_ff, be_ref, nused_ref, xs_ref, w1_ref, b1_ref, w2_ref, b2_ref, ys_ref, w1b_ref, w2b_ref):
    i = pl.program_id(0)
    new_expert = jnp.logical_or(i == 0, be_ref[i] != be_ref[jnp.maximum(i - 1, 0)])

    @pl.when(jnp.logical_and(i < nused_ref[0], new_expert))
    def _():
        w1b_ref[...] = w1_ref[0].astype(BF16)
        w2b_ref[...] = w2_ref[0].astype(BF16)

    @pl.when(i < nused_ref[0])
    def _():
        x = xs_ref[...].astype(BF16)
        h = _mm(x, w1b_ref[...]) + b1_ref[0]
        hg = jnp.minimum(h[:, :d_ff], SWIGLU_LIMIT)
        hu = jnp.clip(h[:, d_ff:], -SWIGLU_LIMIT, SWIGLU_LIMIT)
        act = hg * _sigmoid(SWIGLU_ALPHA * hg) * (hu + 1.0)
        ys_ref[...] = _mm(act.astype(BF16), w2b_ref[...]) + b2_ref[0]

    @pl.when(i >= nused_ref[0])
    def _():
        ys_ref[...] = jnp.zeros_like(ys_ref)


def _experts(block_e, n_used, xs, w1, b1, w2, b2, bm):
    R, d = xs.shape
    d_ff = w2.shape[1]
    nb = R // bm

    def rows(i, be, nu):
        return (jnp.minimum(i, nu[0] - 1), 0)

    def wsel(i, be, nu):
        return (be[i], 0, 0)

    return pl.pallas_call(
        functools.partial(_expert_kernel, d_ff),
        grid_spec=pltpu.PrefetchScalarGridSpec(
            num_scalar_prefetch=2,
            grid=(nb,),
            in_specs=[pl.BlockSpec((bm, d), rows),
                      pl.BlockSpec((1, d, 2 * d_ff), wsel), pl.BlockSpec((1, 1, 2 * d_ff), wsel),
                      pl.BlockSpec((1, d_ff, d), wsel), pl.BlockSpec((1, 1, d), wsel)],
            out_specs=pl.BlockSpec((bm, d), lambda i, be, nu: (i, 0)),
            scratch_shapes=[pltpu.VMEM((d, 2 * d_ff), BF16), pltpu.VMEM((d_ff, d), BF16)]),
        out_shape=jax.ShapeDtypeStruct((R, d), F32),
        compiler_params=_cparams(("arbitrary",)),
        name="moe_experts",
    )(block_e, n_used, xs, w1, b1, w2, b2)


def _combine_kernel(n_p, pstart_ref, meta_ref, metan_ref, gate_ref, x1_ref, gf_ref, ys_ref,
                    op_ref, os_ref, buf_ref, sem):
    i = pl.program_id(0)
    n = pl.num_programs(0)
    tm = x1_ref.shape[0]
    slot = i % 2

    def copy(m_ref, s, t, k):
        j = t * (2 * TOP_K) + k
        src = pstart_ref[m_ref[j]] + m_ref[j + TOP_K]
        return pltpu.make_async_copy(ys_ref.at[pl.ds(src, 1)], buf_ref.at[s, k, pl.ds(t, 1)], sem.at[s])

    def start_tile(m_ref, s):
        def body(t, c):
            for k in range(TOP_K):
                copy(m_ref, s, t, k).start(priority=k % 2)
            return c
        lax.fori_loop(0, tm, body, 0)

    @pl.when(i == 0)
    def _():
        start_tile(meta_ref, slot)

    @pl.when(i + 1 < n)
    def _():
        start_tile(metan_ref, 1 - slot)

    def wait_body(t, c):
        for k in range(TOP_K):
            copy(meta_ref, slot, t, k).wait()
        return c

    lax.fori_loop(0, tm, wait_body, 0)
    gate = gate_ref[...]
    y = x1_ref[...]
    for k in range(TOP_K):
        y = y + gate[:, k:k + 1] * buf_ref[slot, k]
    out = y * lax.rsqrt(jnp.mean(y * y, axis=-1, keepdims=True) + NORM_EPS) * gf_ref[...]

    @pl.when(i < n_p)
    def _():
        op_ref[...] = out

    @pl.when(i >= n_p)
    def _():
        os_ref[...] = out


def _combine(pstart, meta_flat, gate, x1, gf, ys, n_prompt_rows, tm):
    rows, d = x1.shape
    n = rows // tm
    n_p = n_prompt_rows // tm
    msz = tm * 2 * TOP_K
    return pl.pallas_call(
        functools.partial(_combine_kernel, n_p),
        grid=(n,),
        in_specs=[pl.BlockSpec(memory_space=pltpu.SMEM),
                  pl.BlockSpec((msz,), lambda i: (i,), memory_space=pltpu.SMEM),
                  pl.BlockSpec((msz,), lambda i: (jnp.minimum(i + 1, n - 1),), memory_space=pltpu.SMEM),
                  pl.BlockSpec((tm, TOP_K), lambda i: (i, 0)),
                  pl.BlockSpec((tm, d), lambda i: (i, 0)),
                  pl.BlockSpec((1, d), lambda i: (0, 0)),
                  pl.BlockSpec(memory_space=pl.ANY)],
        out_specs=[pl.BlockSpec((tm, d), lambda i: (jnp.minimum(i, n_p - 1), 0)),
                   pl.BlockSpec((tm, d), lambda i: (jnp.maximum(i - n_p, 0), 0))],
        out_shape=[jax.ShapeDtypeStruct((n_prompt_rows, d), F32),
                   jax.ShapeDtypeStruct((rows - n_prompt_rows, d), F32)],
        scratch_shapes=[pltpu.VMEM((2, TOP_K, tm, d), F32), pltpu.SemaphoreType.DMA((2,))],
        compiler_params=_cparams(("arbitrary",)),
        name="moe_combine",
    )(pstart, meta_flat, meta_flat, gate, x1, gf, ys)


def _hi_lo(w):
    hi = w.astype(BF16)
    return hi, (w - hi.astype(F32)).astype(BF16)


def _rope_tables(pos):
    half = ROT_DIM // 2
    inv = ROPE_THETA ** (-jnp.arange(0, ROT_DIM, 2, dtype=F32) / ROT_DIM)
    ang = inv[:, None] * pos.astype(F32)[None, :]
    cos, sin = jnp.cos(ang), jnp.sin(ang)
    n = pos.shape[0]
    pad1 = jnp.ones((HEAD_DIM - ROT_DIM, n), F32)
    pad0 = jnp.zeros((HEAD_DIM - ROT_DIM, n), F32)
    cos_h = jnp.concatenate([cos, cos, pad1], axis=0)
    sin_h = jnp.concatenate([-sin, sin, pad0], axis=0)
    reps = (LANES // HEAD_DIM, 1)
    return jnp.tile(cos_h, reps).T, jnp.tile(sin_h, reps).T


def _pairs_from_state(S):
    H = S.shape[0]
    St = jnp.swapaxes(S, 1, 2).reshape(H // 2, 2, HEAD_DIM, HEAD_DIM)
    z = jnp.zeros_like(St[:, 0])
    top = jnp.concatenate([St[:, 0], z], axis=2)
    bot = jnp.concatenate([z, St[:, 1]], axis=2)
    return jnp.concatenate([top, bot], axis=1)


def _state_from_pairs(Sp):
    a = Sp[:, :HEAD_DIM, :HEAD_DIM]
    b = Sp[:, HEAD_DIM:, HEAD_DIM:]
    St = jnp.stack([a, b], axis=1).reshape(-1, HEAD_DIM, HEAD_DIM)
    return jnp.swapaxes(St, 1, 2)


def kernel(x_prompt, x_sample, state_rwkv_wkv, state_rwkv_shift, cache_swa_k, cache_swa_v, norm1_g, w_in, mu_shift, decay_w0, decay_w2, aaa_a0, aaa_w2, gate_w2, k_k, k_a, r_k, lnx_g, lnx_b, attn_sinks, w_out, norm2_g, w_router, b_router, w_mlp1, b_mlp1, w_mlp2, b_mlp2, norm_f_g):
    depth = w_in.shape[0]
    assert depth == 1 and x_prompt.shape[0] == 1 and x_sample.shape[1] == 1
    T, d = x_prompt.shape[1], x_prompt.shape[2]
    B = x_sample.shape[0]
    past_len = PAST_LEN
    H = state_rwkv_wkv.shape[2]
    rw_w = H * HEAD_DIM
    n_pairs = H // 2
    n_q = attn_sinks.shape[1]
    n_kv = cache_swa_k.shape[3]
    q_cols = n_q * HEAD_DIM
    kv_cols = n_kv * HEAD_DIM
    rw_cols = state_rwkv_shift.shape[2]
    assert rw_cols == 3 * rw_w + 2 * HEAD_DIM + PAIR and kv_cols == LANES
    assert T % RW_TILE == 0 and B % ROW_TILE == 0 and B % 8 == 0
    wlen = cache_swa_k.shape[2]
    l = 0

    whi, wlo = _hi_lo(w_in[l])
    zero_half = jnp.zeros((HEAD_DIM, rw_w), F32)
    pp = dict(mu=mu_shift[l][None], w0=decay_w0[l][None],
              dw2=jnp.concatenate([decay_w2[l], zero_half], axis=0),
              a0=aaa_a0[l][None], aw2=jnp.concatenate([zero_half, aaa_w2[l]], axis=0),
              gw2=gate_w2[l], kk=k_k[l][None], ka=k_a[l][None], rk=r_k[l].reshape(1, rw_w),
              lng=lnx_g[l][None], lnb=lnx_b[l][None])
    wa_hi, wa_lo = _hi_lo(w_out[l][:rw_w])
    wb_hi, wb_lo = _hi_lo(w_out[l][rw_w:])
    n_e = w_router.shape[2]
    wr = jnp.pad(w_router[l], ((0, 0), (0, LANES - n_e)))
    br = jnp.concatenate([b_router[l], jnp.full((LANES - n_e,), NEG_BIG, F32)])[None]
    wp = dict(wa_hi=wa_hi, wa_lo=wa_lo, wb_hi=wb_hi, wb_lo=wb_lo, g2=norm2_g[l][None], wr=wr, br=br)
    g1 = norm1_g[l][None]

    xp = x_prompt[0]
    cos_p, sin_p = _rope_tables(jnp.arange(T))
    prw_p, q_p, k_p, v_p = _inproj(xp, g1, whi, wlo, cos_p, sin_p, 256, rw_cols, q_cols, kv_cols)
    s0_p = jnp.zeros((n_pairs, PAIR, PAIR), F32)
    shift0_p = jnp.zeros((1, rw_cols), F32)
    ya_p, sfin_p = _rwkv_prompt(prw_p, shift0_p, s0_p, pp, RW_TILE)
    yb_p = _attn_prompt(q_p, k_p, v_p, attn_sinks[l], n_q, n_kv)

    xs_ = x_sample[:, 0]
    cos_s, sin_s = _rope_tables(jnp.full((B,), past_len))
    prw_s, q_s, k_s, v_s = _inproj(xs_, g1, whi, wlo, cos_s, sin_s, ROW_TILE, rw_cols, q_cols, kv_cols)
    s_flat = state_rwkv_wkv[l].reshape(B, H * HEAD_DIM * HEAD_DIM)
    ya_s, snew_flat = _rwkv_step(prw_s, state_rwkv_shift[l], s_flat, pp, n_pairs)
    o2, kc_new, vc_new = _attn_step(q_s.reshape(B * n_q, HEAD_DIM), k_s, v_s,
                                    cache_swa_k[l].reshape(B, wlen, kv_cols),
                                    cache_swa_v[l].reshape(B, wlen, kv_cols),
                                    attn_sinks[l][:, None], n_q, n_kv, past_len)
    yb_s = o2.reshape(B, q_cols)

    rows = T + B
    bufs = (jnp.zeros((rows, d), F32), jnp.zeros((rows, d), F32),
            jnp.zeros((rows, TOP_K), F32), jnp.zeros((rows, 2 * TOP_K), jnp.int32))
    *bufs, cnt = _post(xp, ya_p, yb_p, wp, jnp.zeros((1, LANES), F32), bufs, 0, 256)
    x1, h2, gate, meta, cnt = _post(xs_, ya_s, yb_s, wp, cnt, bufs, T, ROW_TILE)

    counts = cnt[0, :n_e].astype(jnp.int32)
    padded = (counts + MOE_BM - 1) // MOE_BM * MOE_BM
    pend = jnp.cumsum(padded)
    pstart = (pend - padded).astype(jnp.int32)
    n_blocks = -(-(rows * TOP_K) // MOE_BM) + n_e
    block_start = jnp.arange(n_blocks, dtype=jnp.int32) * MOE_BM
    block_e = jnp.minimum(jnp.sum((pend[None, :] <= block_start[:, None]).astype(jnp.int32), axis=1),
                          n_e - 1).astype(jnp.int32)
    n_used = (pend[-1] // MOE_BM).astype(jnp.int32)[None]
    tail_row = jnp.where(counts > 0, pend - MOE_BM, -1).astype(jnp.int32)
    meta_flat = meta.reshape(rows * 2 * TOP_K)

    xs_sorted = _scatter(pstart, tail_row, meta_flat, h2, n_blocks * MOE_BM, MOE_BM, ROW_TILE)
    ys_sorted = _experts(block_e, n_used, xs_sorted, w_mlp1[l], b_mlp1[l][:, None], w_mlp2[l],
                         b_mlp2[l][:, None], MOE_BM)
    y_p, y_s = _combine(pstart, meta_flat, gate, x1, norm_f_g[None], ys_sorted, T, ROW_TILE)

    sdt = state_rwkv_wkv.dtype
    return (y_p[None], y_s[:, None],
            _state_from_pairs(sfin_p)[None, None].astype(sdt), prw_p[T - 1][None, None],
            k_p[T - min(WINDOW, T):].reshape(1, 1, -1, n_kv, HEAD_DIM),
            v_p[T - min(WINDOW, T):].reshape(1, 1, -1, n_kv, HEAD_DIM),
            snew_flat.reshape(1, B, H, HEAD_DIM, HEAD_DIM).astype(sdt), prw_s[None],
            kc_new.reshape(1, B, wlen, n_kv, HEAD_DIM), vc_new.reshape(1, B, wlen, n_kv, HEAD_DIM))
```

```python
import functools

import jax
import jax.numpy as jnp
from jax import lax
from jax.experimental import pallas as pl
from jax.experimental.pallas import tpu as pltpu

F32 = jnp.float32
BF16 = jnp.bfloat16

LANES = 128
HEAD_DIM = 64
PAIR = 2 * HEAD_DIM
CHUNK = 64
RW_TILE = 256
RW_PAIRS_PER_STEP = 4
ROT_DIM = 16
ROPE_THETA = 500000.0
WINDOW = 128
PAST_LEN = 16384
ATT_BLOCK = 128
N_EXPERTS = 32
TOP_K = 4
SWIGLU_ALPHA = 1.702
SWIGLU_LIMIT = 7.0
NORM_EPS = 1e-5
LNX_EPS = HEAD_DIM * 1e-5
MOE_BM = 512
ROW_TILE = 128
NEG_BIG = -1e30
VMEM_LIMIT = 52 * 1024 * 1024

NN = (((1,), (0,)), ((), ()))
NT = (((1,), (1,)), ((), ()))


def _mm(a, b, dn=NN):
    return lax.dot_general(a, b, dn, preferred_element_type=F32)


def _split2(a):
    hi = a.astype(BF16)
    lo = (a - hi.astype(F32)).astype(BF16)
    return hi, lo


def _split3(a):
    hi = a.astype(BF16)
    r1 = a - hi.astype(F32)
    mid = r1.astype(BF16)
    lo = (r1 - mid.astype(F32)).astype(BF16)
    return hi, mid, lo


def _dot3(a, b, dn=NN):
    ah, al = _split2(a)
    bh, bl = _split2(b)
    return _mm(ah, bh, dn) + (_mm(ah, bl, dn) + _mm(al, bh, dn))


def _pdot(a, b, dn=NN):
    return _mm(a[0], b[0], dn) + (_mm(a[0], b[1], dn) + _mm(a[1], b[0], dn))


def _stack_rows(a, b):
    return (jnp.concatenate([a[0], b[0]], axis=0), jnp.concatenate([a[1], b[1]], axis=0))


def _dot1(a, b, dn=NN):
    return _mm(a.astype(BF16), b.astype(BF16), dn)


def _dot_sel_l(sel, b, dn=NN):
    b0, b1, b2 = _split3(b)
    return _mm(sel, b0, dn) + (_mm(sel, b1, dn) + _mm(sel, b2, dn))


def _dot_sel_r(a, sel, dn=NN):
    a0, a1, a2 = _split3(a)
    return _mm(a0, sel, dn) + (_mm(a1, sel, dn) + _mm(a2, sel, dn))


def _iota(shape, dim):
    return lax.broadcasted_iota(jnp.int32, shape, dim)


def _seg_matrix():
    return ((_iota((PAIR, PAIR), 0) // HEAD_DIM) == (_iota((PAIR, PAIR), 1) // HEAD_DIM)).astype(BF16)


def _sigmoid(x):
    return 1.0 / (1.0 + jnp.exp(-x))


def _cparams(sem, vmem=VMEM_LIMIT):
    return pltpu.CompilerParams(dimension_semantics=sem, vmem_limit_bytes=vmem)


def _rope_slab(x, cos, sin_signed):
    lane = _iota(x.shape, 1) % HEAD_DIM
    up = pltpu.roll(x, LANES - ROT_DIM // 2, axis=1)
    down = pltpu.roll(x, ROT_DIM // 2, axis=1)
    partner = jnp.where(lane < ROT_DIM // 2, up, down)
    return x * cos + partner * sin_signed


def _inproj_kernel(rw_cols, q_cols, kv_cols, x_ref, g_ref, w_ref, cos_ref, sin_ref,
                   prw_ref, q_ref, k_ref, v_ref):
    x = x_ref[...]
    h = x * lax.rsqrt(jnp.mean(x * x, axis=-1, keepdims=True) + NORM_EPS) * g_ref[...]
    proj = _mm(h.astype(BF16), w_ref[...])
    prw_ref[...] = proj[:, :rw_cols]
    cos = cos_ref[...]
    sin = sin_ref[...]
    for c in range(q_cols // LANES):
        lo = rw_cols + c * LANES
        q_ref[:, c * LANES:(c + 1) * LANES] = _rope_slab(proj[:, lo:lo + LANES], cos, sin)
    ko = rw_cols + q_cols
    for c in range(kv_cols // LANES):
        k_ref[:, c * LANES:(c + 1) * LANES] = _rope_slab(proj[:, ko + c * LANES:ko + (c + 1) * LANES], cos, sin)
    v_ref[...] = proj[:, ko + kv_cols:ko + 2 * kv_cols]


def _inproj(x, g, w_bf, cos_t, sin_t, tm, rw_cols, q_cols, kv_cols):
    rows, d = x.shape
    cols = w_bf.shape[1]
    full = lambda i: (0, 0)
    row = lambda i: (i, 0)
    return pl.pallas_call(
        functools.partial(_inproj_kernel, rw_cols, q_cols, kv_cols),
        grid=(rows // tm,),
        in_specs=[pl.BlockSpec((tm, d), row), pl.BlockSpec((1, d), full),
                  pl.BlockSpec((d, cols), full),
                  pl.BlockSpec((tm, LANES), row), pl.BlockSpec((tm, LANES), row)],
        out_specs=[pl.BlockSpec((tm, rw_cols), row), pl.BlockSpec((tm, q_cols), row),
                   pl.BlockSpec((tm, kv_cols), row), pl.BlockSpec((tm, kv_cols), row)],
        out_shape=[jax.ShapeDtypeStruct((rows, rw_cols), F32), jax.ShapeDtypeStruct((rows, q_cols), F32),
                   jax.ShapeDtypeStruct((rows, kv_cols), F32), jax.ShapeDtypeStruct((rows, kv_cols), F32)],
        compiler_params=_cparams(("parallel",)),
        name="inproj",
    )(x, g, w_bf, cos_t, sin_t)


def _rwkv_tokenwise(pr, pk, pv, plo, pg, prev_r, prev_k, prev_v, prev_lo, prev_g,
                    mu_r, mu_k, mu_v, mu_lo, mu_g, w0, dw2, a0, aw2, gw2, kkp, kap, rkp, seg):
    r = pr + (prev_r - pr) * mu_r
    k = pk + (prev_k - pk) * mu_k
    v = pv + (prev_v - pv) * mu_v
    lo = plo + (prev_lo - plo) * mu_lo
    gd = pg + (prev_g - pg) * mu_g
    z = -(w0 + _dot1(jnp.tanh(lo), dw2))
    softplus = jnp.maximum(z, 0.0) + jnp.log(1.0 + jnp.exp(-jnp.abs(z)))
    logw = -jnp.exp(-softplus - 0.5)
    a = _sigmoid(a0 + _dot1(lo, aw2))
    g = _dot1(_sigmoid(gd), gw2)
    kk = k * kkp
    nrm = jnp.sqrt(_seg_sum(kk * kk, seg))
    kk = kk / jnp.maximum(nrm, 1e-12)
    k2 = k * (1.0 + (a - 1.0) * kap)
    bonus = _seg_sum(r * k2 * rkp, seg) * v
    return r, k2, v, logw, -kk, kk * a, g, bonus


def _seg_sum(x, seg):
    xh, xl = _split2(x)
    return _mm(xh, seg) + _mm(xl, seg)


def _rwkv_finish(y, bonus, g, lng, lnb, seg):
    mu = _seg_sum(y, seg) * (1.0 / HEAD_DIM)
    d = y - mu
    var = _seg_sum(d * d, seg) * (1.0 / HEAD_DIM)
    yn = d * lax.rsqrt(var + LNX_EPS) * lng + lnb
    return (yn + bonus) * g


def _rwkv_prompt_kernel(pps, pr_ref, pk_ref, pv_ref, plo_ref, pg_ref,
                        hr_ref, hk_ref, hv_ref, hlo_ref, hg_ref,
                        s0r_ref, s0k_ref, s0v_ref, s0lo_ref, s0g_ref,
                        mur_ref, muk_ref, muv_ref, mulo_ref, mug_ref,
                        w0_ref, dw2_ref, a0_ref, aw2_ref, gw2_ref, kk_ref, ka_ref, rk_ref,
                        lng_ref, lnb_ref, sin_ref,
                        y_ref, sout_ref, st_ref):
    i = pl.program_id(1)
    n_i = pl.num_programs(1)
    tt = pr_ref.shape[0]

    @pl.when(i == 0)
    def _():
        st_ref[...] = sin_ref[...]

    row = _iota((tt, PAIR), 0)

    def prev_of(cur, halo_row, s0_row):
        first = jnp.where(i == 0, s0_row, halo_row)
        return jnp.where(row == 0, first, pltpu.roll(cur, 1, axis=0))

    plo = plo_ref[...]
    pg = pg_ref[...]
    prev_lo = prev_of(plo, hlo_ref[7:8, :], s0lo_ref[...])
    prev_g = prev_of(pg, hg_ref[7:8, :], s0g_ref[...])
    ti = _iota((tt, tt), 0)
    tj = _iota((tt, tt), 1)
    same_chunk = (ti // CHUNK) == (tj // CHUNK)
    incl = same_chunk & (tj <= ti)
    strict = same_chunk & (tj < ti)
    seg = _seg_matrix()
    lane = _iota((tt, PAIR), 1)
    eye = (ti == tj).astype(F32)
    pairs = []
    for p in range(pps):
        ls = slice(p * PAIR, (p + 1) * PAIR)
        pr, pk, pv = pr_ref[:, ls], pk_ref[:, ls], pv_ref[:, ls]
        r, k2, v, logw, nkk, b, g, bonus = _rwkv_tokenwise(
            pr, pk, pv, plo, pg,
            prev_of(pr, hr_ref[7:8, ls], s0r_ref[:, ls]), prev_of(pk, hk_ref[7:8, ls], s0k_ref[:, ls]),
            prev_of(pv, hv_ref[7:8, ls], s0v_ref[:, ls]), prev_lo, prev_g,
            mur_ref[:, ls], muk_ref[:, ls], muv_ref[:, ls], mulo_ref[...], mug_ref[...],
            w0_ref[:, ls], dw2_ref[:, ls], a0_ref[:, ls], aw2_ref[:, ls], gw2_ref[:, ls],
            kk_ref[:, ls], ka_ref[:, ls], rk_ref[:, ls], seg)
        pairs.append(dict(ls=ls, r=r, k2=k2, v=v, logw=logw, nkk=nkk, b=b, g=g, bonus=bonus))

    incl_b = incl.astype(BF16)
    for q in pairs:
        q["cs"] = _dot_sel_l(incl_b, q["logw"])
    for q in pairs:
        cs = q["cs"]
        gam = jnp.exp(cs)
        inv = jnp.exp(-cs)
        q["a_t"] = jnp.exp(cs - q["logw"]) * q["nkk"]
        q["r_t"] = gam * q["r"]
        q["bt_T"] = (q["b"] * inv).T
        q["kt_T"] = (q["k2"] * inv).T
        q["gam_T"] = gam.T
        q["bk_T"] = _split2(jnp.concatenate([q["bt_T"], q["kt_T"]], axis=1))

    heads = []
    for q in pairs:
        for hh in range(2):
            hm = (lane // HEAD_DIM) == hh
            heads.append(dict(q=q, a=jnp.where(hm, q["a_t"], 0.0), r=jnp.where(hm, q["r_t"], 0.0),
                              v=jnp.where(hm, q["v"], 0.0)))
    for h in heads:
        h["g"] = _pdot(_split2(jnp.concatenate([h["a"], h["r"]], axis=0)), h["q"]["bk_T"])
    for h in heads:
        gmat = h["g"]
        l_ab = jnp.where(strict, gmat[:tt, :tt], 0.0)
        h["l_ak_m_rk"] = _split2(jnp.concatenate([jnp.where(strict, gmat[:tt, tt:], 0.0),
                                                  jnp.where(incl, gmat[tt:, tt:], 0.0)], axis=0))
        h["m_rb"] = _split2(jnp.where(incl, gmat[tt:, :tt], 0.0))
        h["tm"] = eye + l_ab
        h["lps"] = _split2(l_ab)
    for h in heads:
        h["lps"] = _split2(_pdot(h["lps"], h["lps"]))
    for _ in range(4):
        for h in heads:
            h["both"] = _pdot(_stack_rows(_split2(h["tm"]), h["lps"]), h["lps"])
        for h in heads:
            h["tm"] = h["tm"] + h["both"][:tt]
            h["lps"] = _split2(h["both"][tt:])
    for h in heads:
        h["pq"] = _pdot(h["l_ak_m_rk"], _split2(h["v"]))
        h["tm"] = h["tm"] + _pdot(_split2(h["tm"]), h["lps"])
    for h in heads:
        h["tx"] = _pdot(_split2(h["tm"]), _split2(jnp.concatenate([h["a"], h["pq"][:tt]], axis=1)))
    for h in heads:
        h["rx"] = _pdot(h["m_rb"], _split2(h["tx"]))
    for n, q in enumerate(pairs):
        h0, h1 = heads[2 * n], heads[2 * n + 1]
        q["tatp"] = _split2(h0["tx"] + h1["tx"])
        ryc = (h0["rx"] + h1["rx"]) + jnp.concatenate([h0["r"] + h1["r"], h0["pq"][tt:] + h1["pq"][tt:]], axis=1)
        q["ry"] = ryc[:, :PAIR]
        q["yc"] = ryc[:, PAIR:]
        q["v_s"] = _split2(q["v"])
        q["bts"] = _split2(q["bt_T"])
        q["kts"] = _split2(q["kt_T"])
        q["s"] = st_ref[n]

    bd = seg.astype(F32)
    eye_p = (_iota((PAIR, PAIR), 0) == _iota((PAIR, PAIR), 1)).astype(F32)
    col_t = _iota((PAIR, tt), 1)
    zb = jnp.zeros((PAIR, tt), BF16)
    n_chunks = tt // CHUNK
    for c in range(n_chunks):
        cm = (col_t // CHUNK) == c
        for q in pairs:
            bt_c = (jnp.where(cm, q["bts"][0], zb), jnp.where(cm, q["bts"][1], zb))
            kt_c = (jnp.where(cm, q["kts"][0], zb), jnp.where(cm, q["kts"][1], zb))
            dcol = q["gam_T"][:, (c + 1) * CHUNK - 1:(c + 1) * CHUNK]
            bx = _pdot(bt_c, q["tatp"])
            q["mc", c] = _split2(dcol * (eye_p + bd * bx[:, :PAIR]))
            q["nc", c] = dcol * (bd * (bx[:, PAIR:] + _pdot(kt_c, q["v_s"])))
    for c in range(n_chunks):
        sl = slice(c * CHUNK, (c + 1) * CHUNK)
        for q in pairs:
            ss = _split2(q["s"])
            q["y", c] = _pdot(_split2(q["ry"][sl]), ss) + q["yc"][sl]
            q["s"] = _pdot(q["mc", c], ss) + q["nc", c]
    for q in pairs:
        y = jnp.concatenate([q["y", c] for c in range(n_chunks)], axis=0)
        y_ref[:, q["ls"]] = _rwkv_finish(y, q["bonus"], q["g"], lng_ref[:, q["ls"]], lnb_ref[:, q["ls"]], seg)
    for n, q in enumerate(pairs):
        st_ref[n] = q["s"]

    @pl.when(i == n_i - 1)
    def _():
        sout_ref[...] = st_ref[...]


def _rwkv_prompt(prw, shift0, s0_pairs, pp, tt):
    T = prw.shape[0]
    n_pairs = s0_pairs.shape[0]
    pps = RW_PAIRS_PER_STEP
    n_grp = n_pairs // pps
    gw = pps * PAIR
    wcols = n_pairs * PAIR
    lo_col = 3 * wcols
    g_col = lo_col + PAIR
    hb = tt // 8

    def cur(off):
        return pl.BlockSpec((tt, gw), lambda p, i: (i, off // gw + p))

    def cur_fixed(col):
        return pl.BlockSpec((tt, PAIR), lambda p, i: (i, col // PAIR))

    def halo(off):
        return pl.BlockSpec((8, gw), lambda p, i: (jnp.maximum(i * hb - 1, 0), off // gw + p))

    def halo_fixed(col):
        return pl.BlockSpec((8, PAIR), lambda p, i: (jnp.maximum(i * hb - 1, 0), col // PAIR))

    def vec(off):
        return pl.BlockSpec((1, gw), lambda p, i: (0, off // gw + p))

    def vec_fixed(col):
        return pl.BlockSpec((1, PAIR), lambda p, i: (0, col // PAIR))

    def wmat(rows):
        return pl.BlockSpec((rows, gw), lambda p, i: (0, p))

    in_specs = ([cur(0), cur(wcols), cur(2 * wcols), cur_fixed(lo_col), cur_fixed(g_col)]
                + [halo(0), halo(wcols), halo(2 * wcols), halo_fixed(lo_col), halo_fixed(g_col)]
                + [vec(0), vec(wcols), vec(2 * wcols), vec_fixed(lo_col), vec_fixed(g_col)]
                + [vec(0), vec(wcols), vec(2 * wcols), vec_fixed(lo_col), vec_fixed(g_col)]
                + [vec(0), wmat(PAIR), vec(0), wmat(PAIR), wmat(PAIR), vec(0), vec(0), vec(0), vec(0), vec(0)]
                + [pl.BlockSpec((pps, PAIR, PAIR), lambda p, i: (p, 0, 0))])
    args = ([prw] * 5 + [prw] * 5 + [shift0] * 5 + [pp["mu"]] * 5
            + [pp["w0"], pp["dw2"], pp["a0"], pp["aw2"], pp["gw2"], pp["kk"], pp["ka"], pp["rk"],
               pp["lng"], pp["lnb"], s0_pairs])
    return pl.pallas_call(
        functools.partial(_rwkv_prompt_kernel, pps),
        grid=(n_grp, T // tt),
        in_specs=in_specs,
        out_specs=[pl.BlockSpec((tt, gw), lambda p, i: (i, p)),
                   pl.BlockSpec((pps, PAIR, PAIR), lambda p, i: (p, 0, 0))],
        out_shape=[jax.ShapeDtypeStruct((T, wcols), F32),
                   jax.ShapeDtypeStruct((n_pairs, PAIR, PAIR), F32)],
        scratch_shapes=[pltpu.VMEM((pps, PAIR, PAIR), F32)],
        compiler_params=_cparams(("parallel", "arbitrary")),
        name="rwkv_prompt",
    )(*args)


def _rwkv_step_kernel(slabs_per_step, pr_ref, pk_ref, pv_ref, plo_ref, pg_ref,
                      sr_ref, sk_ref, sv_ref, slo_ref, sg_ref,
                      mur_ref, muk_ref, muv_ref, mulo_ref, mug_ref,
                      w0_ref, dw2_ref, a0_ref, aw2_ref, gw2_ref, kk_ref, ka_ref, rk_ref,
                      lng_ref, lnb_ref, s_ref,
                      y_ref, snew_ref, yacc_ref):
    j = pl.program_id(1)
    n_j = pl.num_programs(1)
    seg = _seg_matrix()
    r, k2, v, logw, nkk, b, g, bonus = _rwkv_tokenwise(
        pr_ref[...], pk_ref[...], pv_ref[...], plo_ref[...], pg_ref[...],
        sr_ref[...], sk_ref[...], sv_ref[...], slo_ref[...], sg_ref[...],
        mur_ref[...], muk_ref[...], muv_ref[...], mulo_ref[...], mug_ref[...],
        w0_ref[...], dw2_ref[...], a0_ref[...], aw2_ref[...], gw2_ref[...],
        kk_ref[...], ka_ref[...], rk_ref[...], seg)
    w = jnp.exp(logw)

    @pl.when(j == 0)
    def _():
        yacc_ref[...] = jnp.zeros_like(yacc_ref)

    slabs_per_head = HEAD_DIM // 2
    ci = _iota((PAIR, PAIR), 0)
    li = _iota((PAIR, PAIR), 1)
    assert slabs_per_head % slabs_per_step == 0
    yacc = yacc_ref[...]
    hh = (j * slabs_per_step) // slabs_per_head
    dup = ((ci == hh * HEAD_DIM + li % HEAD_DIM)).astype(BF16)
    nkk_d, w_d, b_d, k_d, r_d = [_dot_sel_r(x, dup) for x in (nkk, w, b, k2, r)]
    for t in range(slabs_per_step):
        slab = j * slabs_per_step + t
        i0 = 2 * (slab % slabs_per_head)
        selv = (ci == hh * HEAD_DIM + i0 + li // HEAD_DIM).astype(BF16)
        sely = ((ci % HEAD_DIM == 0) & (li == hh * HEAD_DIM + i0 + ci // HEAD_DIM)).astype(BF16)
        s = s_ref[:, t * PAIR:(t + 1) * PAIR]
        sa = _dot_sel_r(s * nkk_d, seg)
        s_new = s * w_d + sa * b_d + _dot_sel_r(v, selv) * k_d
        snew_ref[:, t * PAIR:(t + 1) * PAIR] = s_new
        yred = _dot_sel_r(s_new * r_d, seg)
        yacc = yacc + _dot_sel_r(yred, sely)
    yacc_ref[...] = yacc

    @pl.when(j == n_j - 1)
    def _():
        y_ref[...] = _rwkv_finish(yacc, bonus, g, lng_ref[...], lnb_ref[...], seg)


def _rwkv_step(prw, shift, s_flat, pp, n_pairs):
    B = prw.shape[0]
    lanes_per_pair = 2 * HEAD_DIM * HEAD_DIM
    blk = 1024
    slabs_per_step = blk // PAIR
    steps = lanes_per_pair // blk
    lo_blk = 3 * n_pairs
    g_blk = lo_blk + 1

    def cur(off):
        return pl.BlockSpec((B, PAIR), lambda p, j: (0, off + p))

    def cur_fixed(b_):
        return pl.BlockSpec((B, PAIR), lambda p, j: (0, b_))

    def vec(off):
        return pl.BlockSpec((1, PAIR), lambda p, j: (0, off + p))

    def vec_fixed(b_):
        return pl.BlockSpec((1, PAIR), lambda p, j: (0, b_))

    def wmat(rows):
        return pl.BlockSpec((rows, PAIR), lambda p, j: (0, p))

    sspec = pl.BlockSpec((B, blk), lambda p, j: (0, p * steps + j))
    in_specs = ([cur(0), cur(n_pairs), cur(2 * n_pairs), cur_fixed(lo_blk), cur_fixed(g_blk)] * 2
                + [vec(0), vec(n_pairs), vec(2 * n_pairs), vec_fixed(lo_blk), vec_fixed(g_blk)]
                + [vec(0), wmat(PAIR), vec(0), wmat(PAIR), wmat(PAIR), vec(0), vec(0), vec(0), vec(0), vec(0)]
                + [sspec])
    args = ([prw] * 5 + [shift] * 5 + [pp["mu"]] * 5
            + [pp["w0"], pp["dw2"], pp["a0"], pp["aw2"], pp["gw2"], pp["kk"], pp["ka"], pp["rk"],
               pp["lng"], pp["lnb"], s_flat])
    return pl.pallas_call(
        functools.partial(_rwkv_step_kernel, slabs_per_step),
        grid=(n_pairs, steps),
        in_specs=in_specs,
        out_specs=[pl.BlockSpec((B, PAIR), lambda p, j: (0, p)), sspec],
        out_shape=[jax.ShapeDtypeStruct((B, n_pairs * PAIR), F32),
                   jax.ShapeDtypeStruct(s_flat.shape, F32)],
        scratch_shapes=[pltpu.VMEM((B, PAIR), F32)],
        compiler_params=_cparams(("parallel", "arbitrary")),
        name="rwkv_step",
    )(*args)


def _attn_prompt_kernel(n_q, group, sink_ref, q_ref, kc_ref, kp_ref, vc_ref, vp_ref, o_ref):
    blk = q_ref.shape[0]
    q = q_ref[...] * (HEAD_DIM ** -0.5)
    kband = jnp.concatenate([kp_ref[...], kc_ref[...]], axis=0)
    vband = jnp.concatenate([vp_ref[...], vc_ref[...]], axis=0)
    i = pl.program_id(0)
    rq = _iota((blk, 2 * blk), 0)
    ck = _iota((blk, 2 * blk), 1)
    dist = rq - ck + blk
    kpos = i * blk - blk + ck
    valid = (dist >= 0) & (dist < WINDOW) & (kpos >= 0)
    for h in range(n_q):
        gk = h // group
        qh = q[:, h * HEAD_DIM:(h + 1) * HEAD_DIM]
        kh = kband[:, gk * HEAD_DIM:(gk + 1) * HEAD_DIM]
        vh = vband[:, gk * HEAD_DIM:(gk + 1) * HEAD_DIM]
        s = jnp.where(valid, _dot1(qh, kh, NT), NEG_BIG)
        sink = sink_ref[h]
        m = jnp.maximum(jnp.max(s, axis=-1, keepdims=True), sink)
        p = jnp.exp(s - m)
        denom = jnp.sum(p, axis=-1, keepdims=True) + jnp.exp(sink - m)
        o_ref[:, h * HEAD_DIM:(h + 1) * HEAD_DIM] = _dot1(p, vh) / denom


def _attn_prompt(q, k, v, sinks, n_q, n_kv):
    T, qw = q.shape
    kvw = k.shape[1]
    blk = ATT_BLOCK
    curm = lambda i: (i, 0)
    prevm = lambda i: (jnp.maximum(i - 1, 0), 0)
    return pl.pallas_call(
        functools.partial(_attn_prompt_kernel, n_q, n_q // n_kv),
        grid=(T // blk,),
        in_specs=[pl.BlockSpec(memory_space=pltpu.SMEM),
                  pl.BlockSpec((blk, qw), curm),
                  pl.BlockSpec((blk, kvw), curm), pl.BlockSpec((blk, kvw), prevm),
                  pl.BlockSpec((blk, kvw), curm), pl.BlockSpec((blk, kvw), prevm)],
        out_specs=pl.BlockSpec((blk, qw), curm),
        out_shape=jax.ShapeDtypeStruct((T, qw), F32),
        compiler_params=_cparams(("parallel",)),
        name="attn_prompt",
    )(sinks, q, k, k, v, v)


def _attn_step_kernel(n_q, group, pos0, sink_ref, q_ref, kn_ref, vn_ref, kc_ref, vc_ref,
                      o_ref, ko_ref, vo_ref):
    bb, wlen, kvw = kc_ref.shape
    lane = _iota((n_q, kvw), 1)
    rowh = _iota((n_q, kvw), 0)
    mine = (lane // HEAD_DIM) == (rowh // group)
    dupm = (_iota((HEAD_DIM, kvw), 0) == _iota((HEAD_DIM, kvw), 1) % HEAD_DIM).astype(BF16)
    fold = (_iota((kvw, HEAD_DIM), 0) % HEAD_DIM == _iota((kvw, HEAD_DIM), 1)).astype(BF16)
    kidx = _iota((n_q, wlen), 1)
    dist = wlen - kidx
    valid = (dist < WINDOW) & (pos0 - dist >= 0)
    rk = _iota((wlen, kvw), 0)
    sink = sink_ref[...]
    for t in range(bb):
        qh = q_ref[t * n_q:(t + 1) * n_q, :] * (HEAD_DIM ** -0.5)
        qm = jnp.where(mine, _dot_sel_r(qh, dupm), 0.0)
        kc = kc_ref[t]
        vc = vc_ref[t]
        kn = kn_ref[t:t + 1, :]
        vn = vn_ref[t:t + 1, :]
        s = jnp.where(valid, _dot1(qm, kc, NT), NEG_BIG)
        s_new = jnp.sum(qm * kn, axis=-1, keepdims=True)
        m = jnp.maximum(jnp.maximum(jnp.max(s, axis=-1, keepdims=True), s_new), sink)
        p = jnp.exp(s - m)
        p_new = jnp.exp(s_new - m)
        denom = jnp.sum(p, axis=-1, keepdims=True) + p_new + jnp.exp(sink - m)
        res = (_dot1(p, vc) + p_new * vn) / denom
        o_ref[t * n_q:(t + 1) * n_q, :] = _dot_sel_r(jnp.where(mine, res, 0.0), fold)
        ko_ref[t] = jnp.where(rk == wlen - 1, kn, pltpu.roll(kc, wlen - 1, axis=0))
        vo_ref[t] = jnp.where(rk == wlen - 1, vn, pltpu.roll(vc, wlen - 1, axis=0))


def _attn_step(q2, k_new, v_new, k_cache, v_cache, sinks_col, n_q, n_kv, pos0):
    B, wlen, kvw = k_cache.shape
    bb = 8
    return pl.pallas_call(
        functools.partial(_attn_step_kernel, n_q, n_q // n_kv, pos0),
        grid=(B // bb,),
        in_specs=[pl.BlockSpec((n_q, 1), lambda i: (0, 0)),
                  pl.BlockSpec((bb * n_q, HEAD_DIM), lambda i: (i, 0)),
                  pl.BlockSpec((bb, kvw), lambda i: (i, 0)), pl.BlockSpec((bb, kvw), lambda i: (i, 0)),
                  pl.BlockSpec((bb, wlen, kvw), lambda i: (i, 0, 0)),
                  pl.BlockSpec((bb, wlen, kvw), lambda i: (i, 0, 0))],
        out_specs=[pl.BlockSpec((bb * n_q, HEAD_DIM), lambda i: (i, 0)),
                   pl.BlockSpec((bb, wlen, kvw), lambda i: (i, 0, 0)),
                   pl.BlockSpec((bb, wlen, kvw), lambda i: (i, 0, 0))],
        out_shape=[jax.ShapeDtypeStruct((B * n_q, HEAD_DIM), F32),
                   jax.ShapeDtypeStruct((B, wlen, kvw), F32),
                   jax.ShapeDtypeStruct((B, wlen, kvw), F32)],
        compiler_params=_cparams(("parallel",)),
        name="attn_step",
    )(sinks_col, q2, k_new, v_new, k_cache, v_cache)


def _post_kernel(x_ref, ya_ref, yb_ref, wa_ref, wb_ref, g2_ref, wr_ref, br_ref, cnt0_ref,
                 _x1_alias, _h2_alias, _gate_alias, _meta_alias,
                 x1_ref, h2_ref, gate_ref, meta_ref, cnt_ref, carry_ref):
    i = pl.program_id(0)

    @pl.when(i == 0)
    def _():
        carry_ref[...] = cnt0_ref[...]

    mix = _mm(ya_ref[...].astype(BF16), wa_ref[...]) + _mm(yb_ref[...].astype(BF16), wb_ref[...])
    x1 = x_ref[...] + mix
    h2 = x1 * lax.rsqrt(jnp.mean(x1 * x1, axis=-1, keepdims=True) + NORM_EPS) * g2_ref[...]
    x1_ref[...] = x1
    h2_ref[...] = h2

    l = _dot1(h2, wr_ref[...]) + br_ref[...]
    tm = l.shape[0]
    lane = _iota(l.shape, 1)
    vals, idxs = [], []
    for _ in range(TOP_K):
        m = jnp.max(l, axis=-1, keepdims=True)
        sel = jnp.min(jnp.where(l == m, lane, LANES), axis=-1, keepdims=True)
        vals.append(m)
        idxs.append(sel)
        l = jnp.where(lane == sel, -jnp.inf, l)
    es = [jnp.exp(v - vals[0]) for v in vals]
    tot = es[0] + es[1] + es[2] + es[3]
    onehot = jnp.zeros(l.shape, F32)
    for sel in idxs:
        onehot = onehot + (lane == sel).astype(F32)
    strict = (_iota((tm, tm), 1) < _iota((tm, tm), 0)).astype(BF16)
    before = _mm(strict, onehot.astype(BF16)) + carry_ref[...]
    for k in range(TOP_K):
        gate_ref[:, k:k + 1] = es[k] / tot
        meta_ref[:, k:k + 1] = idxs[k]
        meta_ref[:, TOP_K + k:TOP_K + k + 1] = jnp.sum(
            jnp.where(lane == idxs[k], before, 0.0), axis=-1, keepdims=True).astype(jnp.int32)
    carry_ref[...] = carry_ref[...] + jnp.sum(onehot, axis=0, keepdims=True)
    cnt_ref[...] = carry_ref[...]


def _post(x, ya, yb, wp, cnt0, bufs, row_off, tm):
    rows, d = x.shape
    half = ya.shape[1]
    ob = row_off // tm
    full = lambda i: (0, 0)
    row = lambda i: (i, 0)
    orow = lambda i: (ob + i, 0)
    anyspec = pl.BlockSpec(memory_space=pl.ANY)
    return pl.pallas_call(
        _post_kernel,
        grid=(rows // tm,),
        in_specs=[pl.BlockSpec((tm, d), row), pl.BlockSpec((tm, half), row), pl.BlockSpec((tm, half), row),
                  pl.BlockSpec((half, d), full), pl.BlockSpec((half, d), full),
                  pl.BlockSpec((1, d), full), pl.BlockSpec((d, LANES), full), pl.BlockSpec((1, LANES), full),
                  pl.BlockSpec((1, LANES), full),
                  anyspec, anyspec, anyspec, anyspec],
        out_specs=[pl.BlockSpec((tm, d), orow), pl.BlockSpec((tm, d), orow),
                   pl.BlockSpec((tm, TOP_K), orow), pl.BlockSpec((tm, 2 * TOP_K), orow),
                   pl.BlockSpec((1, LANES), full)],
        out_shape=[jax.ShapeDtypeStruct(bufs[0].shape, F32), jax.ShapeDtypeStruct(bufs[1].shape, F32),
                   jax.ShapeDtypeStruct(bufs[2].shape, F32), jax.ShapeDtypeStruct(bufs[3].shape, jnp.int32),
                   jax.ShapeDtypeStruct((1, LANES), F32)],
        input_output_aliases={9: 0, 10: 1, 11: 2, 12: 3},
        scratch_shapes=[pltpu.VMEM((1, LANES), F32)],
        compiler_params=_cparams(("arbitrary",)),
        name="post",
    )(x, ya, yb, wp["wa"], wp["wb"], wp["g2"], wp["wr"], wp["br"], cnt0, *bufs)


def _scatter_kernel(n_e, tail_ref, dest_ref, h_ref, xs_ref, zero_ref, sem, zsem):
    tm = h_ref.shape[0]
    bm = zero_ref.shape[0]

    @pl.when(pl.program_id(0) == 0)
    def _():
        zero_ref[...] = jnp.zeros_like(zero_ref)

        def zcopy(e):
            return pltpu.make_async_copy(zero_ref, xs_ref.at[pl.ds(pl.multiple_of(tail_ref[e], 8), bm)], zsem)

        for e in range(n_e):
            @pl.when(tail_ref[e] >= 0)
            def _():
                zcopy(e).start()
        for e in range(n_e):
            @pl.when(tail_ref[e] >= 0)
            def _():
                zcopy(e).wait()

    def copy(t, k):
        return pltpu.make_async_copy(h_ref.at[pl.ds(t, 1)], xs_ref.at[pl.ds(dest_ref[t * TOP_K + k], 1)], sem)

    def start(t, c):
        for k in range(TOP_K):
            copy(t, k).start(priority=k % 2)
        return c

    def wait(t, c):
        for k in range(TOP_K):
            copy(t, k).wait()
        return c

    lax.fori_loop(0, tm, start, 0)
    lax.fori_loop(0, tm, wait, 0)


def _scatter(tail_row, dest_flat, h2, n_rows_sorted, bm, tm):
    rows, d = h2.shape
    n_e = tail_row.shape[0]
    return pl.pallas_call(
        functools.partial(_scatter_kernel, n_e),
        grid=(rows // tm,),
        in_specs=[pl.BlockSpec(memory_space=pltpu.SMEM),
                  pl.BlockSpec((tm * TOP_K,), lambda i: (i,), memory_space=pltpu.SMEM),
                  pl.BlockSpec((tm, d), lambda i: (i, 0))],
        out_specs=pl.BlockSpec(memory_space=pl.ANY),
        out_shape=jax.ShapeDtypeStruct((n_rows_sorted, d), F32),
        scratch_shapes=[pltpu.VMEM((bm, d), F32), pltpu.SemaphoreType.DMA(()), pltpu.SemaphoreType.DMA(())],
        compiler_params=_cparams(("arbitrary",)),
        name="moe_scatter",
    )(tail_row, dest_flat, h2)


def _expert_kernel(d_ff, be_ref, nused_ref, xs_ref, w1_ref, b1_ref, w2_ref, b2_ref, ys_ref, w1b_ref, w2b_ref):
    i = pl.program_id(0)
    new_expert = jnp.logical_or(i == 0, be_ref[i] != be_ref[jnp.maximum(i - 1, 0)])

    @pl.when(jnp.logical_and(i < nused_ref[0], new_expert))
    def _():
        w1b_ref[...] = w1_ref[0].astype(BF16)
        w2b_ref[...] = w2_ref[0].astype(BF16)

    @pl.when(i < nused_ref[0])
    def _():
        x = xs_ref[...].astype(BF16)
        h = _mm(x, w1b_ref[...]) + b1_ref[0]
        hg = jnp.minimum(h[:, :d_ff], SWIGLU_LIMIT)
        hu = jnp.clip(h[:, d_ff:], -SWIGLU_LIMIT, SWIGLU_LIMIT)
        act = hg * _sigmoid(SWIGLU_ALPHA * hg) * (hu + 1.0)
        ys_ref[...] = _mm(act.astype(BF16), w2b_ref[...]) + b2_ref[0]

    @pl.when(i >= nused_ref[0])
    def _():
        ys_ref[...] = jnp.zeros_like(ys_ref)


def _experts(block_e, n_used, xs, w1, b1, w2, b2, bm):
    R, d = xs.shape
    d_ff = w2.shape[1]
    nb = R // bm

    def rows(i, be, nu):
        return (jnp.minimum(i, nu[0] - 1), 0)

    def wsel(i, be, nu):
        return (be[i], 0, 0)

    return pl.pallas_call(
        functools.partial(_expert_kernel, d_ff),
        grid_spec=pltpu.PrefetchScalarGridSpec(
            num_scalar_prefetch=2,
            grid=(nb,),
            in_specs=[pl.BlockSpec((bm, d), rows),
                      pl.BlockSpec((1, d, 2 * d_ff), wsel), pl.BlockSpec((1, 1, 2 * d_ff), wsel),
                      pl.BlockSpec((1, d_ff, d), wsel), pl.BlockSpec((1, 1, d), wsel)],
            out_specs=pl.BlockSpec((bm, d), lambda i, be, nu: (i, 0)),
            scratch_shapes=[pltpu.VMEM((d, 2 * d_ff), BF16), pltpu.VMEM((d_ff, d), BF16)]),
        out_shape=jax.ShapeDtypeStruct((R, d), F32),
        compiler_params=_cparams(("arbitrary",)),
        name="moe_experts",
    )(block_e, n_used, xs, w1, b1, w2, b2)


def _combine_kernel(n_p, meta_ref, metan_ref, gate_ref, x1_ref, gf_ref, ys_ref,
                    op_ref, os_ref, buf_ref, sem):
    i = pl.program_id(0)
    n = pl.num_programs(0)
    tm = x1_ref.shape[0]
    slot = i % 2

    def copy(m_ref, s, t, k):
        return pltpu.make_async_copy(ys_ref.at[pl.ds(m_ref[t * TOP_K + k], 1)],
                                     buf_ref.at[s, k, pl.ds(t, 1)], sem.at[s])

    def start_tile(m_ref, s):
        def body(t, c):
            for k in range(TOP_K):
                copy(m_ref, s, t, k).start(priority=k % 2)
            return c
        lax.fori_loop(0, tm, body, 0)

    @pl.when(i == 0)
    def _():
        start_tile(meta_ref, slot)

    @pl.when(i + 1 < n)
    def _():
        start_tile(metan_ref, 1 - slot)

    def wait_body(t, c):
        for k in range(TOP_K):
            copy(meta_ref, slot, t, k).wait()
        return c

    lax.fori_loop(0, tm, wait_body, 0)
    gate = gate_ref[...]
    y = x1_ref[...]
    for k in range(TOP_K):
        y = y + gate[:, k:k + 1] * buf_ref[slot, k]
    out = y * lax.rsqrt(jnp.mean(y * y, axis=-1, keepdims=True) + NORM_EPS) * gf_ref[...]

    @pl.when(i < n_p)
    def _():
        op_ref[...] = out

    @pl.when(i >= n_p)
    def _():
        os_ref[...] = out


def _combine(dest_flat, gate, x1, gf, ys, n_prompt_rows, tm):
    rows, d = x1.shape
    n = rows // tm
    n_p = n_prompt_rows // tm
    msz = tm * TOP_K
    return pl.pallas_call(
        functools.partial(_combine_kernel, n_p),
        grid=(n,),
        in_specs=[pl.BlockSpec((msz,), lambda i: (i,), memory_space=pltpu.SMEM),
                  pl.BlockSpec((msz,), lambda i: (jnp.minimum(i + 1, n - 1),), memory_space=pltpu.SMEM),
                  pl.BlockSpec((tm, TOP_K), lambda i: (i, 0)),
                  pl.BlockSpec((tm, d), lambda i: (i, 0)),
                  pl.BlockSpec((1, d), lambda i: (0, 0)),
                  pl.BlockSpec(memory_space=pl.ANY)],
        out_specs=[pl.BlockSpec((tm, d), lambda i: (jnp.minimum(i, n_p - 1), 0)),
                   pl.BlockSpec((tm, d), lambda i: (jnp.maximum(i - n_p, 0), 0))],
        out_shape=[jax.ShapeDtypeStruct((n_prompt_rows, d), F32),
                   jax.ShapeDtypeStruct((rows - n_prompt_rows, d), F32)],
        scratch_shapes=[pltpu.VMEM((2, TOP_K, tm, d), F32), pltpu.SemaphoreType.DMA((2,))],
        compiler_params=_cparams(("arbitrary",)),
        name="moe_combine",
    )(dest_flat, dest_flat, gate, x1, gf, ys)


def _rope_tables(pos):
    half = ROT_DIM // 2
    inv = ROPE_THETA ** (-jnp.arange(0, ROT_DIM, 2, dtype=F32) / ROT_DIM)
    ang = inv[:, None] * pos.astype(F32)[None, :]
    cos, sin = jnp.cos(ang), jnp.sin(ang)
    n = pos.shape[0]
    pad1 = jnp.ones((HEAD_DIM - ROT_DIM, n), F32)
    pad0 = jnp.zeros((HEAD_DIM - ROT_DIM, n), F32)
    cos_h = jnp.concatenate([cos, cos, pad1], axis=0)
    sin_h = jnp.concatenate([-sin, sin, pad0], axis=0)
    reps = (LANES // HEAD_DIM, 1)
    return jnp.tile(cos_h, reps).T, jnp.tile(sin_h, reps).T


def _pairs_from_state(S):
    H = S.shape[0]
    St = jnp.swapaxes(S, 1, 2).reshape(H // 2, 2, HEAD_DIM, HEAD_DIM)
    z = jnp.zeros_like(St[:, 0])
    top = jnp.concatenate([St[:, 0], z], axis=2)
    bot = jnp.concatenate([z, St[:, 1]], axis=2)
    return jnp.concatenate([top, bot], axis=1)


def _state_from_pairs(Sp):
    a = Sp[:, :HEAD_DIM, :HEAD_DIM]
    b = Sp[:, HEAD_DIM:, HEAD_DIM:]
    St = jnp.stack([a, b], axis=1).reshape(-1, HEAD_DIM, HEAD_DIM)
    return jnp.swapaxes(St, 1, 2)


def kernel(x_prompt, x_sample, state_rwkv_wkv, state_rwkv_shift, cache_swa_k, cache_swa_v, norm1_g, w_in, mu_shift, decay_w0, decay_w2, aaa_a0, aaa_w2, gate_w2, k_k, k_a, r_k, lnx_g, lnx_b, attn_sinks, w_out, norm2_g, w_router, b_router, w_mlp1, b_mlp1, w_mlp2, b_mlp2, norm_f_g):
    depth = w_in.shape[0]
    assert depth == 1 and x_prompt.shape[0] == 1 and x_sample.shape[1] == 1
    T, d = x_prompt.shape[1], x_prompt.shape[2]
    B = x_sample.shape[0]
    past_len = PAST_LEN
    H = state_rwkv_wkv.shape[2]
    rw_w = H * HEAD_DIM
    n_pairs = H // 2
    n_q = attn_sinks.shape[1]
    n_kv = cache_swa_k.shape[3]
    q_cols = n_q * HEAD_DIM
    kv_cols = n_kv * HEAD_DIM
    rw_cols = state_rwkv_shift.shape[2]
    assert rw_cols == 3 * rw_w + 2 * HEAD_DIM + PAIR and kv_cols == LANES
    assert T % RW_TILE == 0 and B % ROW_TILE == 0 and B % 8 == 0
    wlen = cache_swa_k.shape[2]
    l = 0

    w_in_bf = w_in[l].astype(BF16)
    zero_half = jnp.zeros((HEAD_DIM, rw_w), F32)
    pp = dict(mu=mu_shift[l][None], w0=decay_w0[l][None],
              dw2=jnp.concatenate([decay_w2[l], zero_half], axis=0),
              a0=aaa_a0[l][None], aw2=jnp.concatenate([zero_half, aaa_w2[l]], axis=0),
              gw2=gate_w2[l], kk=k_k[l][None], ka=k_a[l][None], rk=r_k[l].reshape(1, rw_w),
              lng=lnx_g[l][None], lnb=lnx_b[l][None])
    w_out_bf = w_out[l].astype(BF16)
    n_e = w_router.shape[2]
    wr = jnp.pad(w_router[l], ((0, 0), (0, LANES - n_e)))
    br = jnp.concatenate([b_router[l], jnp.full((LANES - n_e,), NEG_BIG, F32)])[None]
    wp = dict(wa=w_out_bf[:rw_w], wb=w_out_bf[rw_w:], g2=norm2_g[l][None], wr=wr, br=br)
    g1 = norm1_g[l][None]

    xp = x_prompt[0]
    cos_p, sin_p = _rope_tables(jnp.arange(T))
    prw_p, q_p, k_p, v_p = _inproj(xp, g1, w_in_bf, cos_p, sin_p, 512, rw_cols, q_cols, kv_cols)
    s0_p = jnp.zeros((n_pairs, PAIR, PAIR), F32)
    shift0_p = jnp.zeros((1, rw_cols), F32)
    ya_p, sfin_p = _rwkv_prompt(prw_p, shift0_p, s0_p, pp, RW_TILE)
    yb_p = _attn_prompt(q_p, k_p, v_p, attn_sinks[l], n_q, n_kv)

    xs_ = x_sample[:, 0]
    cos_s, sin_s = _rope_tables(jnp.full((B,), past_len))
    prw_s, q_s, k_s, v_s = _inproj(xs_, g1, w_in_bf, cos_s, sin_s, ROW_TILE, rw_cols, q_cols, kv_cols)
    s_flat = state_rwkv_wkv[l].reshape(B, H * HEAD_DIM * HEAD_DIM)
    ya_s, snew_flat = _rwkv_step(prw_s, state_rwkv_shift[l], s_flat, pp, n_pairs)
    o2, kc_new, vc_new = _attn_step(q_s.reshape(B * n_q, HEAD_DIM), k_s, v_s,
                                    cache_swa_k[l].reshape(B, wlen, kv_cols),
                                    cache_swa_v[l].reshape(B, wlen, kv_cols),
                                    attn_sinks[l][:, None], n_q, n_kv, past_len)
    yb_s = o2.reshape(B, q_cols)

    rows = T + B
    bufs = (jnp.zeros((rows, d), F32), jnp.zeros((rows, d), F32),
            jnp.zeros((rows, TOP_K), F32), jnp.zeros((rows, 2 * TOP_K), jnp.int32))
    *bufs, cnt = _post(xp, ya_p, yb_p, wp, jnp.zeros((1, LANES), F32), bufs, 0, 256)
    x1, h2, gate, meta, cnt = _post(xs_, ya_s, yb_s, wp, cnt, bufs, T, ROW_TILE)

    counts = cnt[0, :n_e].astype(jnp.int32)
    padded = (counts + MOE_BM - 1) // MOE_BM * MOE_BM
    pend = jnp.cumsum(padded)
    pstart = (pend - padded).astype(jnp.int32)
    n_blocks = -(-(rows * TOP_K) // MOE_BM) + n_e
    block_start = jnp.arange(n_blocks, dtype=jnp.int32) * MOE_BM
    block_e = jnp.minimum(jnp.sum((pend[None, :] <= block_start[:, None]).astype(jnp.int32), axis=1),
                          n_e - 1).astype(jnp.int32)
    n_used = (pend[-1] // MOE_BM).astype(jnp.int32)[None]
    tail_row = jnp.where(counts > 0, pend - MOE_BM, -1).astype(jnp.int32)
    dest_flat = (pstart[meta[:, :TOP_K]] + meta[:, TOP_K:]).reshape(rows * TOP_K)

    xs_sorted = _scatter(tail_row, dest_flat, h2, n_blocks * MOE_BM, MOE_BM, ROW_TILE)
    ys_sorted = _experts(block_e, n_used, xs_sorted, w_mlp1[l], b_mlp1[l][:, None], w_mlp2[l],
                         b_mlp2[l][:, None], MOE_BM)
    y_p, y_s = _combine(dest_flat, gate, x1, norm_f_g[None], ys_sorted, T, ROW_TILE)

    sdt = state_rwkv_wkv.dtype
    return (y_p[None], y_s[:, None],
            _state_from_pairs(sfin_p)[None, None].astype(sdt), prw_p[T - 1][None, None],
            k_p[T - min(WINDOW, T):].reshape(1, 1, -1, n_kv, HEAD_DIM),
            v_p[T - min(WINDOW, T):].reshape(1, 1, -1, n_kv, HEAD_DIM),
            snew_flat.reshape(1, B, H, HEAD_DIM, HEAD_DIM).astype(sdt), prw_s[None],
            kc_new.reshape(1, B, wlen, n_kv, HEAD_DIM), vc_new.reshape(1, B, wlen, n_kv, HEAD_DIM))
```

```python
import functools

import jax
import jax.numpy as jnp
from jax import lax
from jax.experimental import pallas as pl
from jax.experimental.pallas import tpu as pltpu

F32 = jnp.float32
BF16 = jnp.bfloat16

LANES = 128
HEAD_DIM = 64
PAIR = 2 * HEAD_DIM
CHUNK = 64
RW_TILE = 256
RW_PAIRS_PER_STEP = 4
ROT_DIM = 16
ROPE_THETA = 500000.0
WINDOW = 128
PAST_LEN = 16384
ATT_BLOCK = 128
N_EXPERTS = 32
TOP_K = 4
SWIGLU_ALPHA = 1.702
SWIGLU_LIMIT = 7.0
NORM_EPS = 1e-5
LNX_EPS = HEAD_DIM * 1e-5
MOE_BM = 512
ROW_TILE = 128
NEG_BIG = -1e30
VMEM_LIMIT = 52 * 1024 * 1024

NN = (((1,), (0,)), ((), ()))
NT = (((1,), (1,)), ((), ()))


def _mm(a, b, dn=NN):
    return lax.dot_general(a, b, dn, preferred_element_type=F32)


def _split2(a):
    hi = a.astype(BF16)
    lo = (a - hi.astype(F32)).astype(BF16)
    return hi, lo


def _split3(a):
    hi = a.astype(BF16)
    r1 = a - hi.astype(F32)
    mid = r1.astype(BF16)
    lo = (r1 - mid.astype(F32)).astype(BF16)
    return hi, mid, lo


def _pdot(a, b, dn=NN):
    return _mm(a[0], b[0], dn) + (_mm(a[0], b[1], dn) + _mm(a[1], b[0], dn))


def _dot1(a, b, dn=NN):
    return _mm(a.astype(BF16), b.astype(BF16), dn)


def _dot_sel_l(sel, b, dn=NN):
    b0, b1, b2 = _split3(b)
    return _mm(sel, b0, dn) + (_mm(sel, b1, dn) + _mm(sel, b2, dn))


def _dot_sel_r(a, sel, dn=NN):
    a0, a1, a2 = _split3(a)
    return _mm(a0, sel, dn) + (_mm(a1, sel, dn) + _mm(a2, sel, dn))


def _iota(shape, dim):
    return lax.broadcasted_iota(jnp.int32, shape, dim)


def _seg_matrix():
    return ((_iota((PAIR, PAIR), 0) // HEAD_DIM) == (_iota((PAIR, PAIR), 1) // HEAD_DIM)).astype(BF16)


def _sigmoid(x):
    return 1.0 / (1.0 + jnp.exp(-x))


def _cparams(sem, vmem=VMEM_LIMIT):
    return pltpu.CompilerParams(dimension_semantics=sem, vmem_limit_bytes=vmem)


def _rope_slab(x, cos, sin_signed):
    lane = _iota(x.shape, 1) % HEAD_DIM
    up = pltpu.roll(x, LANES - ROT_DIM // 2, axis=1)
    down = pltpu.roll(x, ROT_DIM // 2, axis=1)
    partner = jnp.where(lane < ROT_DIM // 2, up, down)
    return x * cos + partner * sin_signed


def _inproj_kernel(rw_cols, q_cols, kv_cols, x_ref, g_ref, w_ref, cos_ref, sin_ref,
                   prw_ref, q_ref, k_ref, v_ref):
    x = x_ref[...]
    h = x * lax.rsqrt(jnp.mean(x * x, axis=-1, keepdims=True) + NORM_EPS) * g_ref[...]
    proj = _mm(h.astype(BF16), w_ref[...])
    prw_ref[...] = proj[:, :rw_cols]
    cos = cos_ref[...]
    sin = sin_ref[...]
    for c in range(q_cols // LANES):
        lo = rw_cols + c * LANES
        q_ref[:, c * LANES:(c + 1) * LANES] = _rope_slab(proj[:, lo:lo + LANES], cos, sin)
    ko = rw_cols + q_cols
    for c in range(kv_cols // LANES):
        k_ref[:, c * LANES:(c + 1) * LANES] = _rope_slab(proj[:, ko + c * LANES:ko + (c + 1) * LANES], cos, sin)
    v_ref[...] = proj[:, ko + kv_cols:ko + 2 * kv_cols]


def _inproj(x, g, w_bf, cos_t, sin_t, tm, rw_cols, q_cols, kv_cols):
    rows, d = x.shape
    cols = w_bf.shape[1]
    full = lambda i: (0, 0)
    row = lambda i: (i, 0)
    return pl.pallas_call(
        functools.partial(_inproj_kernel, rw_cols, q_cols, kv_cols),
        grid=(rows // tm,),
        in_specs=[pl.BlockSpec((tm, d), row), pl.BlockSpec((1, d), full),
                  pl.BlockSpec((d, cols), full),
                  pl.BlockSpec((tm, LANES), row), pl.BlockSpec((tm, LANES), row)],
        out_specs=[pl.BlockSpec((tm, rw_cols), row), pl.BlockSpec((tm, q_cols), row),
                   pl.BlockSpec((tm, kv_cols), row), pl.BlockSpec((tm, kv_cols), row)],
        out_shape=[jax.ShapeDtypeStruct((rows, rw_cols), F32), jax.ShapeDtypeStruct((rows, q_cols), F32),
                   jax.ShapeDtypeStruct((rows, kv_cols), F32), jax.ShapeDtypeStruct((rows, kv_cols), F32)],
        compiler_params=_cparams(("parallel",)),
        name="inproj",
    )(x, g, w_bf, cos_t, sin_t)


def _rwkv_tokenwise(pr, pk, pv, plo, pg, prev_r, prev_k, prev_v, prev_lo, prev_g,
                    mu_r, mu_k, mu_v, mu_lo, mu_g, w0, dw2, a0, aw2, gw2, kkp, kap, rkp, seg):
    r = pr + (prev_r - pr) * mu_r
    k = pk + (prev_k - pk) * mu_k
    v = pv + (prev_v - pv) * mu_v
    lo = plo + (prev_lo - plo) * mu_lo
    gd = pg + (prev_g - pg) * mu_g
    z = -(w0 + _dot1(jnp.tanh(lo), dw2))
    softplus = jnp.maximum(z, 0.0) + jnp.log(1.0 + jnp.exp(-jnp.abs(z)))
    logw = -jnp.exp(-softplus - 0.5)
    a = _sigmoid(a0 + _dot1(lo, aw2))
    g = _dot1(_sigmoid(gd), gw2)
    kk = k * kkp
    nrm = jnp.sqrt(_seg_sum(kk * kk, seg))
    kk = kk / jnp.maximum(nrm, 1e-12)
    k2 = k * (1.0 + (a - 1.0) * kap)
    bonus = _seg_sum(r * k2 * rkp, seg) * v
    return r, k2, v, logw, -kk, kk * a, g, bonus


def _seg_sum(x, seg):
    xh, xl = _split2(x)
    return _mm(xh, seg) + _mm(xl, seg)


def _rwkv_finish(y, bonus, g, lng, lnb, seg):
    mu = _seg_sum(y, seg) * (1.0 / HEAD_DIM)
    d = y - mu
    var = _seg_sum(d * d, seg) * (1.0 / HEAD_DIM)
    yn = d * lax.rsqrt(var + LNX_EPS) * lng + lnb
    return (yn + bonus) * g


def _rwkv_prompt_kernel(pps, pr_ref, pk_ref, pv_ref, plo_ref, pg_ref,
                        hr_ref, hk_ref, hv_ref, hlo_ref, hg_ref,
                        s0r_ref, s0k_ref, s0v_ref, s0lo_ref, s0g_ref,
                        mur_ref, muk_ref, muv_ref, mulo_ref, mug_ref,
                        w0_ref, dw2_ref, a0_ref, aw2_ref, gw2_ref, kk_ref, ka_ref, rk_ref,
                        lng_ref, lnb_ref, sin_ref,
                        y_ref, sout_ref, st_ref):
    i = pl.program_id(1)
    n_i = pl.num_programs(1)
    tt = pr_ref.shape[0]

    @pl.when(i == 0)
    def _():
        st_ref[...] = sin_ref[...]

    row = _iota((tt, PAIR), 0)

    def prev_of(cur, halo_row, s0_row):
        first = jnp.where(i == 0, s0_row, halo_row)
        return jnp.where(row == 0, first, pltpu.roll(cur, 1, axis=0))

    plo = plo_ref[...]
    pg = pg_ref[...]
    prev_lo = prev_of(plo, hlo_ref[7:8, :], s0lo_ref[...])
    prev_g = prev_of(pg, hg_ref[7:8, :], s0g_ref[...])
    ti = _iota((tt, tt), 0)
    tj = _iota((tt, tt), 1)
    same_chunk = (ti // CHUNK) == (tj // CHUNK)
    incl = same_chunk & (tj <= ti)
    strict = same_chunk & (tj < ti)
    seg = _seg_matrix()
    lane = _iota((tt, PAIR), 1)
    eye = (ti == tj).astype(F32)
    pairs = []
    for p in range(pps):
        ls = slice(p * PAIR, (p + 1) * PAIR)
        pr, pk, pv = pr_ref[:, ls], pk_ref[:, ls], pv_ref[:, ls]
        r, k2, v, logw, nkk, b, g, bonus = _rwkv_tokenwise(
            pr, pk, pv, plo, pg,
            prev_of(pr, hr_ref[7:8, ls], s0r_ref[:, ls]), prev_of(pk, hk_ref[7:8, ls], s0k_ref[:, ls]),
            prev_of(pv, hv_ref[7:8, ls], s0v_ref[:, ls]), prev_lo, prev_g,
            mur_ref[:, ls], muk_ref[:, ls], muv_ref[:, ls], mulo_ref[...], mug_ref[...],
            w0_ref[:, ls], dw2_ref[:, ls], a0_ref[:, ls], aw2_ref[:, ls], gw2_ref[:, ls],
            kk_ref[:, ls], ka_ref[:, ls], rk_ref[:, ls], seg)
        pairs.append(dict(ls=ls, r=r, k2=k2, v=v, logw=logw, nkk=nkk, b=b, g=g, bonus=bonus))

    incl_b = incl.astype(BF16)
    for q in pairs:
        q["cs"] = _dot_sel_l(incl_b, q["logw"])
    for q in pairs:
        cs = q["cs"]
        gam = jnp.exp(cs)
        inv = jnp.exp(-cs)
        q["a_t"] = jnp.exp(cs - q["logw"]) * q["nkk"]
        q["r_t"] = gam * q["r"]
        q["bt_T"] = (q["b"] * inv).T
        q["kt_T"] = (q["k2"] * inv).T
        q["gam_T"] = gam.T
        q["bk_T"] = jnp.concatenate([q["bt_T"], q["kt_T"]], axis=1).astype(BF16)

    heads = []
    for q in pairs:
        for hh in range(2):
            hm = (lane // HEAD_DIM) == hh
            heads.append(dict(q=q, a=jnp.where(hm, q["a_t"], 0.0), r=jnp.where(hm, q["r_t"], 0.0),
                              v=jnp.where(hm, q["v"], 0.0)))
    for h in heads:
        h["g"] = _mm(jnp.concatenate([h["a"], h["r"]], axis=0).astype(BF16), h["q"]["bk_T"])
    for h in heads:
        gmat = h["g"]
        l_ab = jnp.where(strict, gmat[:tt, :tt], 0.0)
        h["l_ak_m_rk"] = jnp.concatenate([jnp.where(strict, gmat[:tt, tt:], 0.0),
                                          jnp.where(incl, gmat[tt:, tt:], 0.0)], axis=0).astype(BF16)
        h["m_rb"] = jnp.where(incl, gmat[tt:, :tt], 0.0).astype(BF16)
        h["tm"] = eye + l_ab
        h["lp"] = l_ab.astype(BF16)
    for h in heads:
        h["lp"] = _mm(h["lp"], h["lp"]).astype(BF16)
    for _ in range(4):
        for h in heads:
            h["both"] = _mm(jnp.concatenate([h["tm"].astype(BF16), h["lp"]], axis=0), h["lp"])
        for h in heads:
            h["tm"] = h["tm"] + h["both"][:tt]
            h["lp"] = h["both"][tt:].astype(BF16)
    for h in heads:
        h["pq"] = _mm(h["l_ak_m_rk"], h["v"].astype(BF16))
        h["tm"] = h["tm"] + _mm(h["tm"].astype(BF16), h["lp"])
    for h in heads:
        h["tx"] = _mm(h["tm"].astype(BF16),
                      jnp.concatenate([h["a"], h["pq"][:tt]], axis=1).astype(BF16))
    for h in heads:
        h["rx"] = _mm(h["m_rb"], h["tx"].astype(BF16))
    for n, q in enumerate(pairs):
        h0, h1 = heads[2 * n], heads[2 * n + 1]
        q["tatp"] = (h0["tx"] + h1["tx"]).astype(BF16)
        ryc = (h0["rx"] + h1["rx"]) + jnp.concatenate([h0["r"] + h1["r"], h0["pq"][tt:] + h1["pq"][tt:]], axis=1)
        q["ry"] = ryc[:, :PAIR]
        q["yc"] = ryc[:, PAIR:]
        q["v_b"] = q["v"].astype(BF16)
        q["bt_b"] = q["bt_T"].astype(BF16)
        q["kt_b"] = q["kt_T"].astype(BF16)
        q["s"] = st_ref[n]

    bd = seg.astype(F32)
    eye_p = (_iota((PAIR, PAIR), 0) == _iota((PAIR, PAIR), 1)).astype(F32)
    col_t = _iota((PAIR, tt), 1)
    zb = jnp.zeros((PAIR, tt), BF16)
    n_chunks = tt // CHUNK
    for c in range(n_chunks):
        cm = (col_t // CHUNK) == c
        for q in pairs:
            bt_c = jnp.where(cm, q["bt_b"], zb)
            kt_c = jnp.where(cm, q["kt_b"], zb)
            dcol = q["gam_T"][:, (c + 1) * CHUNK - 1:(c + 1) * CHUNK]
            bx = _mm(bt_c, q["tatp"])
            q["mc", c] = _split2(dcol * (eye_p + bd * bx[:, :PAIR]))
            q["nc", c] = dcol * (bd * (bx[:, PAIR:] + _mm(kt_c, q["v_b"])))
    for c in range(n_chunks):
        sl = slice(c * CHUNK, (c + 1) * CHUNK)
        for q in pairs:
            ss = _split2(q["s"])
            q["y", c] = _pdot(_split2(q["ry"][sl]), ss) + q["yc"][sl]
            q["s"] = _pdot(q["mc", c], ss) + q["nc", c]
    for q in pairs:
        y = jnp.concatenate([q["y", c] for c in range(n_chunks)], axis=0)
        y_ref[:, q["ls"]] = _rwkv_finish(y, q["bonus"], q["g"], lng_ref[:, q["ls"]], lnb_ref[:, q["ls"]], seg)
    for n, q in enumerate(pairs):
        st_ref[n] = q["s"]

    @pl.when(i == n_i - 1)
    def _():
        sout_ref[...] = st_ref[...]


def _rwkv_prompt(prw, shift0, s0_pairs, pp, tt):
    T = prw.shape[0]
    n_pairs = s0_pairs.shape[0]
    pps = RW_PAIRS_PER_STEP
    n_grp = n_pairs // pps
    gw = pps * PAIR
    wcols = n_pairs * PAIR
    lo_col = 3 * wcols
    g_col = lo_col + PAIR
    hb = tt // 8

    def cur(off):
        return pl.BlockSpec((tt, gw), lambda p, i: (i, off // gw + p))

    def cur_fixed(col):
        return pl.BlockSpec((tt, PAIR), lambda p, i: (i, col // PAIR))

    def halo(off):
        return pl.BlockSpec((8, gw), lambda p, i: (jnp.maximum(i * hb - 1, 0), off // gw + p))

    def halo_fixed(col):
        return pl.BlockSpec((8, PAIR), lambda p, i: (jnp.maximum(i * hb - 1, 0), col // PAIR))

    def vec(off):
        return pl.BlockSpec((1, gw), lambda p, i: (0, off // gw + p))

    def vec_fixed(col):
        return pl.BlockSpec((1, PAIR), lambda p, i: (0, col // PAIR))

    def wmat(rows):
        return pl.BlockSpec((rows, gw), lambda p, i: (0, p))

    in_specs = ([cur(0), cur(wcols), cur(2 * wcols), cur_fixed(lo_col), cur_fixed(g_col)]
                + [halo(0), halo(wcols), halo(2 * wcols), halo_fixed(lo_col), halo_fixed(g_col)]
                + [vec(0), vec(wcols), vec(2 * wcols), vec_fixed(lo_col), vec_fixed(g_col)]
                + [vec(0), vec(wcols), vec(2 * wcols), vec_fixed(lo_col), vec_fixed(g_col)]
                + [vec(0), wmat(PAIR), vec(0), wmat(PAIR), wmat(PAIR), vec(0), vec(0), vec(0), vec(0), vec(0)]
                + [pl.BlockSpec((pps, PAIR, PAIR), lambda p, i: (p, 0, 0))])
    args = ([prw] * 5 + [prw] * 5 + [shift0] * 5 + [pp["mu"]] * 5
            + [pp["w0"], pp["dw2"], pp["a0"], pp["aw2"], pp["gw2"], pp["kk"], pp["ka"], pp["rk"],
               pp["lng"], pp["lnb"], s0_pairs])
    return pl.pallas_call(
        functools.partial(_rwkv_prompt_kernel, pps),
        grid=(n_grp, T // tt),
        in_specs=in_specs,
        out_specs=[pl.BlockSpec((tt, gw), lambda p, i: (i, p)),
                   pl.BlockSpec((pps, PAIR, PAIR), lambda p, i: (p, 0, 0))],
        out_shape=[jax.ShapeDtypeStruct((T, wcols), F32),
                   jax.ShapeDtypeStruct((n_pairs, PAIR, PAIR), F32)],
        scratch_shapes=[pltpu.VMEM((pps, PAIR, PAIR), F32)],
        compiler_params=_cparams(("parallel", "arbitrary")),
        name="rwkv_prompt",
    )(*args)


def _rwkv_step_kernel(slabs_per_step, pr_ref, pk_ref, pv_ref, plo_ref, pg_ref,
                      sr_ref, sk_ref, sv_ref, slo_ref, sg_ref,
                      mur_ref, muk_ref, muv_ref, mulo_ref, mug_ref,
                      w0_ref, dw2_ref, a0_ref, aw2_ref, gw2_ref, kk_ref, ka_ref, rk_ref,
                      lng_ref, lnb_ref, s_ref,
                      y_ref, snew_ref, yacc_ref):
    j = pl.program_id(1)
    n_j = pl.num_programs(1)
    seg = _seg_matrix()
    r, k2, v, logw, nkk, b, g, bonus = _rwkv_tokenwise(
        pr_ref[...], pk_ref[...], pv_ref[...], plo_ref[...], pg_ref[...],
        sr_ref[...], sk_ref[...], sv_ref[...], slo_ref[...], sg_ref[...],
        mur_ref[...], muk_ref[...], muv_ref[...], mulo_ref[...], mug_ref[...],
        w0_ref[...], dw2_ref[...], a0_ref[...], aw2_ref[...], gw2_ref[...],
        kk_ref[...], ka_ref[...], rk_ref[...], seg)
    w = jnp.exp(logw)

    @pl.when(j == 0)
    def _():
        yacc_ref[...] = jnp.zeros_like(yacc_ref)

    slabs_per_head = HEAD_DIM // 2
    ci = _iota((PAIR, PAIR), 0)
    li = _iota((PAIR, PAIR), 1)
    assert slabs_per_head % slabs_per_step == 0
    yacc = yacc_ref[...]
    hh = (j * slabs_per_step) // slabs_per_head
    dup = ((ci == hh * HEAD_DIM + li % HEAD_DIM)).astype(BF16)
    nkk_d, w_d, b_d, k_d, r_d = [_dot_sel_r(x, dup) for x in (nkk, w, b, k2, r)]
    for t in range(slabs_per_step):
        slab = j * slabs_per_step + t
        i0 = 2 * (slab % slabs_per_head)
        selv = (ci == hh * HEAD_DIM + i0 + li // HEAD_DIM).astype(BF16)
        sely = ((ci % HEAD_DIM == 0) & (li == hh * HEAD_DIM + i0 + ci // HEAD_DIM)).astype(BF16)
        s = s_ref[:, t * PAIR:(t + 1) * PAIR]
        sa = _dot_sel_r(s * nkk_d, seg)
        s_new = s * w_d + sa * b_d + _dot_sel_r(v, selv) * k_d
        snew_ref[:, t * PAIR:(t + 1) * PAIR] = s_new
        yred = _dot_sel_r(s_new * r_d, seg)
        yacc = yacc + _dot_sel_r(yred, sely)
    yacc_ref[...] = yacc

    @pl.when(j == n_j - 1)
    def _():
        y_ref[...] = _rwkv_finish(yacc, bonus, g, lng_ref[...], lnb_ref[...], seg)


def _rwkv_step(prw, shift, s_flat, pp, n_pairs):
    B = prw.shape[0]
    lanes_per_pair = 2 * HEAD_DIM * HEAD_DIM
    blk = 1024
    slabs_per_step = blk // PAIR
    steps = lanes_per_pair // blk
    lo_blk = 3 * n_pairs
    g_blk = lo_blk + 1

    def cur(off):
        return pl.BlockSpec((B, PAIR), lambda p, j: (0, off + p))

    def cur_fixed(b_):
        return pl.BlockSpec((B, PAIR), lambda p, j: (0, b_))

    def vec(off):
        return pl.BlockSpec((1, PAIR), lambda p, j: (0, off + p))

    def vec_fixed(b_):
        return pl.BlockSpec((1, PAIR), lambda p, j: (0, b_))

    def wmat(rows):
        return pl.BlockSpec((rows, PAIR), lambda p, j: (0, p))

    sspec = pl.BlockSpec((B, blk), lambda p, j: (0, p * steps + j))
    in_specs = ([cur(0), cur(n_pairs), cur(2 * n_pairs), cur_fixed(lo_blk), cur_fixed(g_blk)] * 2
                + [vec(0), vec(n_pairs), vec(2 * n_pairs), vec_fixed(lo_blk), vec_fixed(g_blk)]
                + [vec(0), wmat(PAIR), vec(0), wmat(PAIR), wmat(PAIR), vec(0), vec(0), vec(0), vec(0), vec(0)]
                + [sspec])
    args = ([prw] * 5 + [shift] * 5 + [pp["mu"]] * 5
            + [pp["w0"], pp["dw2"], pp["a0"], pp["aw2"], pp["gw2"], pp["kk"], pp["ka"], pp["rk"],
               pp["lng"], pp["lnb"], s_flat])
    return pl.pallas_call(
        functools.partial(_rwkv_step_kernel, slabs_per_step),
        grid=(n_pairs, steps),
        in_specs=in_specs,
        out_specs=[pl.BlockSpec((B, PAIR), lambda p, j: (0, p)), sspec],
        out_shape=[jax.ShapeDtypeStruct((B, n_pairs * PAIR), F32),
                   jax.ShapeDtypeStruct(s_flat.shape, F32)],
        scratch_shapes=[pltpu.VMEM((B, PAIR), F32)],
        compiler_params=_cparams(("parallel", "arbitrary")),
        name="rwkv_step",
    )(*args)


def _attn_prompt_kernel(n_q, group, sink_ref, q_ref, kc_ref, kp_ref, vc_ref, vp_ref, o_ref):
    blk = q_ref.shape[0]
    q = q_ref[...] * (HEAD_DIM ** -0.5)
    kband = jnp.concatenate([kp_ref[...], kc_ref[...]], axis=0)
    vband = jnp.concatenate([vp_ref[...], vc_ref[...]], axis=0)
    i = pl.program_id(0)
    rq = _iota((blk, 2 * blk), 0)
    ck = _iota((blk, 2 * blk), 1)
    dist = rq - ck + blk
    kpos = i * blk - blk + ck
    valid = (dist >= 0) & (dist < WINDOW) & (kpos >= 0)
    for h in range(n_q):
        gk = h // group
        qh = q[:, h * HEAD_DIM:(h + 1) * HEAD_DIM]
        kh = kband[:, gk * HEAD_DIM:(gk + 1) * HEAD_DIM]
        vh = vband[:, gk * HEAD_DIM:(gk + 1) * HEAD_DIM]
        s = jnp.where(valid, _dot1(qh, kh, NT), NEG_BIG)
        sink = sink_ref[h]
        m = jnp.maximum(jnp.max(s, axis=-1, keepdims=True), sink)
        p = jnp.exp(s - m)
        denom = jnp.sum(p, axis=-1, keepdims=True) + jnp.exp(sink - m)
        o_ref[:, h * HEAD_DIM:(h + 1) * HEAD_DIM] = _dot1(p, vh) / denom


def _attn_prompt(q, k, v, sinks, n_q, n_kv):
    T, qw = q.shape
    kvw = k.shape[1]
    blk = ATT_BLOCK
    curm = lambda i: (i, 0)
    prevm = lambda i: (jnp.maximum(i - 1, 0), 0)
    return pl.pallas_call(
        functools.partial(_attn_prompt_kernel, n_q, n_q // n_kv),
        grid=(T // blk,),
        in_specs=[pl.BlockSpec(memory_space=pltpu.SMEM),
                  pl.BlockSpec((blk, qw), curm),
                  pl.BlockSpec((blk, kvw), curm), pl.BlockSpec((blk, kvw), prevm),
                  pl.BlockSpec((blk, kvw), curm), pl.BlockSpec((blk, kvw), prevm)],
        out_specs=pl.BlockSpec((blk, qw), curm),
        out_shape=jax.ShapeDtypeStruct((T, qw), F32),
        compiler_params=_cparams(("parallel",)),
        name="attn_prompt",
    )(sinks, q, k, k, v, v)


def _attn_step_kernel(n_q, group, pos0, sink_ref, q_ref, kn_ref, vn_ref, kc_ref, vc_ref,
                      o_ref, ko_ref, vo_ref):
    bb, wlen, kvw = kc_ref.shape
    lane = _iota((n_q, kvw), 1)
    rowh = _iota((n_q, kvw), 0)
    mine = (lane // HEAD_DIM) == (rowh // group)
    dupm = (_iota((HEAD_DIM, kvw), 0) == _iota((HEAD_DIM, kvw), 1) % HEAD_DIM).astype(BF16)
    fold = (_iota((kvw, HEAD_DIM), 0) % HEAD_DIM == _iota((kvw, HEAD_DIM), 1)).astype(BF16)
    kidx = _iota((n_q, wlen), 1)
    dist = wlen - kidx
    valid = (dist < WINDOW) & (pos0 - dist >= 0)
    rk = _iota((wlen, kvw), 0)
    sink = sink_ref[...]
    for t in range(bb):
        qh = q_ref[t * n_q:(t + 1) * n_q, :] * (HEAD_DIM ** -0.5)
        qm = jnp.where(mine, _dot_sel_r(qh, dupm), 0.0)
        kc = kc_ref[t]
        vc = vc_ref[t]
        kn = kn_ref[t:t + 1, :]
        vn = vn_ref[t:t + 1, :]
        s = jnp.where(valid, _dot1(qm, kc, NT), NEG_BIG)
        s_new = jnp.sum(qm * kn, axis=-1, keepdims=True)
        m = jnp.maximum(jnp.maximum(jnp.max(s, axis=-1, keepdims=True), s_new), sink)
        p = jnp.exp(s - m)
        p_new = jnp.exp(s_new - m)
        denom = jnp.sum(p, axis=-1, keepdims=True) + p_new + jnp.exp(sink - m)
        res = (_dot1(p, vc) + p_new * vn) / denom
        o_ref[t * n_q:(t + 1) * n_q, :] = _dot_sel_r(jnp.where(mine, res, 0.0), fold)
        ko_ref[t] = jnp.where(rk == wlen - 1, kn, pltpu.roll(kc, wlen - 1, axis=0))
        vo_ref[t] = jnp.where(rk == wlen - 1, vn, pltpu.roll(vc, wlen - 1, axis=0))


def _attn_step(q2, k_new, v_new, k_cache, v_cache, sinks_col, n_q, n_kv, pos0):
    B, wlen, kvw = k_cache.shape
    bb = 8
    return pl.pallas_call(
        functools.partial(_attn_step_kernel, n_q, n_q // n_kv, pos0),
        grid=(B // bb,),
        in_specs=[pl.BlockSpec((n_q, 1), lambda i: (0, 0)),
                  pl.BlockSpec((bb * n_q, HEAD_DIM), lambda i: (i, 0)),
                  pl.BlockSpec((bb, kvw), lambda i: (i, 0)), pl.BlockSpec((bb, kvw), lambda i: (i, 0)),
                  pl.BlockSpec((bb, wlen, kvw), lambda i: (i, 0, 0)),
                  pl.BlockSpec((bb, wlen, kvw), lambda i: (i, 0, 0))],
        out_specs=[pl.BlockSpec((bb * n_q, HEAD_DIM), lambda i: (i, 0)),
                   pl.BlockSpec((bb, wlen, kvw), lambda i: (i, 0, 0)),
                   pl.BlockSpec((bb, wlen, kvw), lambda i: (i, 0, 0))],
        out_shape=[jax.ShapeDtypeStruct((B * n_q, HEAD_DIM), F32),
                   jax.ShapeDtypeStruct((B, wlen, kvw), F32),
                   jax.ShapeDtypeStruct((B, wlen, kvw), F32)],
        compiler_params=_cparams(("parallel",)),
        name="attn_step",
    )(sinks_col, q2, k_new, v_new, k_cache, v_cache)


def _post_kernel(n_alias, x_ref, ya_ref, yb_ref, wa_ref, wb_ref, g2_ref, wr_ref, br_ref, cnt0_ref, *rest):
    x1_ref, h2_ref, gate_ref, meta_ref, cnt_ref, carry_ref = rest[n_alias:]
    i = pl.program_id(0)

    @pl.when(i == 0)
    def _():
        carry_ref[...] = cnt0_ref[...]

    mix = _mm(ya_ref[...].astype(BF16), wa_ref[...]) + _mm(yb_ref[...].astype(BF16), wb_ref[...])
    x1 = x_ref[...] + mix
    h2 = x1 * lax.rsqrt(jnp.mean(x1 * x1, axis=-1, keepdims=True) + NORM_EPS) * g2_ref[...]
    x1_ref[...] = x1
    h2_ref[...] = h2

    l = _dot1(h2, wr_ref[...]) + br_ref[...]
    tm = l.shape[0]
    lane = _iota(l.shape, 1)
    vals, idxs = [], []
    for _ in range(TOP_K):
        m = jnp.max(l, axis=-1, keepdims=True)
        sel = jnp.min(jnp.where(l == m, lane, LANES), axis=-1, keepdims=True)
        vals.append(m)
        idxs.append(sel)
        l = jnp.where(lane == sel, -jnp.inf, l)
    es = [jnp.exp(v - vals[0]) for v in vals]
    tot = es[0] + es[1] + es[2] + es[3]
    onehot = jnp.zeros(l.shape, F32)
    for sel in idxs:
        onehot = onehot + (lane == sel).astype(F32)
    strict = (_iota((tm, tm), 1) < _iota((tm, tm), 0)).astype(BF16)
    before = _mm(strict, onehot.astype(BF16)) + carry_ref[...]
    for k in range(TOP_K):
        gate_ref[:, k:k + 1] = es[k] / tot
        meta_ref[:, k:k + 1] = idxs[k]
        meta_ref[:, TOP_K + k:TOP_K + k + 1] = jnp.sum(
            jnp.where(lane == idxs[k], before, 0.0), axis=-1, keepdims=True).astype(jnp.int32)
    carry_ref[...] = carry_ref[...] + jnp.sum(onehot, axis=0, keepdims=True)
    cnt_ref[...] = carry_ref[...]


def _post(x, ya, yb, wp, cnt0, bufs, total_rows, row_off, tm):
    rows, d = x.shape
    half = ya.shape[1]
    ob = row_off // tm
    full = lambda i: (0, 0)
    row = lambda i: (i, 0)
    orow = lambda i: (ob + i, 0)
    n_fixed = 9
    return pl.pallas_call(
        functools.partial(_post_kernel, len(bufs)),
        grid=(rows // tm,),
        in_specs=[pl.BlockSpec((tm, d), row), pl.BlockSpec((tm, half), row), pl.BlockSpec((tm, half), row),
                  pl.BlockSpec((half, d), full), pl.BlockSpec((half, d), full),
                  pl.BlockSpec((1, d), full), pl.BlockSpec((d, LANES), full), pl.BlockSpec((1, LANES), full),
                  pl.BlockSpec((1, LANES), full)] + [pl.BlockSpec(memory_space=pl.ANY)] * len(bufs),
        out_specs=[pl.BlockSpec((tm, d), orow), pl.BlockSpec((tm, d), orow),
                   pl.BlockSpec((tm, TOP_K), orow), pl.BlockSpec((tm, 2 * TOP_K), orow),
                   pl.BlockSpec((1, LANES), full)],
        out_shape=[jax.ShapeDtypeStruct((total_rows, d), F32), jax.ShapeDtypeStruct((total_rows, d), F32),
                   jax.ShapeDtypeStruct((total_rows, TOP_K), F32),
                   jax.ShapeDtypeStruct((total_rows, 2 * TOP_K), jnp.int32),
                   jax.ShapeDtypeStruct((1, LANES), F32)],
        input_output_aliases={n_fixed + j: j for j in range(len(bufs))},
        scratch_shapes=[pltpu.VMEM((1, LANES), F32)],
        compiler_params=_cparams(("arbitrary",)),
        name="post",
    )(x, ya, yb, wp["wa"], wp["wb"], wp["g2"], wp["wr"], wp["br"], cnt0, *bufs)


def _scatter_kernel(n_e, tail_ref, dest_ref, h_ref, xs_ref, zero_ref, sem, zsem):
    tm = h_ref.shape[0]
    bm = zero_ref.shape[0]

    @pl.when(pl.program_id(0) == 0)
    def _():
        zero_ref[...] = jnp.zeros_like(zero_ref)

        def zcopy(e):
            return pltpu.make_async_copy(zero_ref, xs_ref.at[pl.ds(pl.multiple_of(tail_ref[e], 8), bm)], zsem)

        for e in range(n_e):
            @pl.when(tail_ref[e] >= 0)
            def _():
                zcopy(e).start()
        for e in range(n_e):
            @pl.when(tail_ref[e] >= 0)
            def _():
                zcopy(e).wait()

    def copy(t, k):
        return pltpu.make_async_copy(h_ref.at[pl.ds(t, 1)], xs_ref.at[pl.ds(dest_ref[t * TOP_K + k], 1)], sem)

    def start(t, c):
        for k in range(TOP_K):
            copy(t, k).start(priority=k % 2)
        return c

    def wait(t, c):
        for k in range(TOP_K):
            copy(t, k).wait()
        return c

    lax.fori_loop(0, tm, start, 0)
    lax.fori_loop(0, tm, wait, 0)


def _scatter(tail_row, dest_flat, h2, n_rows_sorted, bm, tm):
    rows, d = h2.shape
    n_e = tail_row.shape[0]
    return pl.pallas_call(
        functools.partial(_scatter_kernel, n_e),
        grid=(rows // tm,),
        in_specs=[pl.BlockSpec(memory_space=pltpu.SMEM),
                  pl.BlockSpec((tm * TOP_K,), lambda i: (i,), memory_space=pltpu.SMEM),
                  pl.BlockSpec((tm, d), lambda i: (i, 0))],
        out_specs=pl.BlockSpec(memory_space=pl.ANY),
        out_shape=jax.ShapeDtypeStruct((n_rows_sorted, d), F32),
        scratch_shapes=[pltpu.VMEM((bm, d), F32), pltpu.SemaphoreType.DMA(()), pltpu.SemaphoreType.DMA(())],
        compiler_params=_cparams(("arbitrary",)),
        name="moe_scatter",
    )(tail_row, dest_flat, h2)


def _expert_kernel(d_ff, be_ref, nused_ref, xs_ref, w1_ref, b1_ref, w2_ref, b2_ref, ys_ref, w1b_ref, w2b_ref):
    i = pl.program_id(0)
    new_expert = jnp.logical_or(i == 0, be_ref[i] != be_ref[jnp.maximum(i - 1, 0)])

    @pl.when(jnp.logical_and(i < nused_ref[0], new_expert))
    def _():
        w1b_ref[...] = w1_ref[0].astype(BF16)
        w2b_ref[...] = w2_ref[0].astype(BF16)

    @pl.when(i < nused_ref[0])
    def _():
        x = xs_ref[...].astype(BF16)
        h = _mm(x, w1b_ref[...]) + b1_ref[0]
        hg = jnp.minimum(h[:, :d_ff], SWIGLU_LIMIT)
        hu = jnp.clip(h[:, d_ff:], -SWIGLU_LIMIT, SWIGLU_LIMIT)
        act = hg * _sigmoid(SWIGLU_ALPHA * hg) * (hu + 1.0)
        ys_ref[...] = _mm(act.astype(BF16), w2b_ref[...]) + b2_ref[0]

    @pl.when(i >= nused_ref[0])
    def _():
        ys_ref[...] = jnp.zeros_like(ys_ref)


def _experts(block_e, n_used, xs, w1, b1, w2, b2, bm):
    R, d = xs.shape
    d_ff = w2.shape[1]
    nb = R // bm

    def rows(i, be, nu):
        return (jnp.minimum(i, nu[0] - 1), 0)

    def wsel(i, be, nu):
        return (be[i], 0, 0)

    return pl.pallas_call(
        functools.partial(_expert_kernel, d_ff),
        grid_spec=pltpu.PrefetchScalarGridSpec(
            num_scalar_prefetch=2,
            grid=(nb,),
            in_specs=[pl.BlockSpec((bm, d), rows),
                      pl.BlockSpec((1, d, 2 * d_ff), wsel), pl.BlockSpec((1, 1, 2 * d_ff), wsel),
                      pl.BlockSpec((1, d_ff, d), wsel), pl.BlockSpec((1, 1, d), wsel)],
            out_specs=pl.BlockSpec((bm, d), lambda i, be, nu: (i, 0)),
            scratch_shapes=[pltpu.VMEM((d, 2 * d_ff), BF16), pltpu.VMEM((d_ff, d), BF16)]),
        out_shape=jax.ShapeDtypeStruct((R, d), F32),
        compiler_params=_cparams(("arbitrary",)),
        name="moe_experts",
    )(block_e, n_used, xs, w1, b1, w2, b2)


def _combine_kernel(n_p, meta_ref, metan_ref, gate_ref, x1_ref, gf_ref, ys_ref,
                    op_ref, os_ref, buf_ref, sem):
    i = pl.program_id(0)
    n = pl.num_programs(0)
    tm = x1_ref.shape[0]
    slot = i % 2

    def copy(m_ref, s, t, k):
        return pltpu.make_async_copy(ys_ref.at[pl.ds(m_ref[t * TOP_K + k], 1)],
                                     buf_ref.at[s, k, pl.ds(t, 1)], sem.at[s])

    def start_tile(m_ref, s):
        def body(t, c):
            for k in range(TOP_K):
                copy(m_ref, s, t, k).start(priority=k % 2)
            return c
        lax.fori_loop(0, tm, body, 0)

    @pl.when(i == 0)
    def _():
        start_tile(meta_ref, slot)

    @pl.when(i + 1 < n)
    def _():
        start_tile(metan_ref, 1 - slot)

    def wait_body(t, c):
        for k in range(TOP_K):
            copy(meta_ref, slot, t, k).wait()
        return c

    lax.fori_loop(0, tm, wait_body, 0)
    gate = gate_ref[...]
    y = x1_ref[...]
    for k in range(TOP_K):
        y = y + gate[:, k:k + 1] * buf_ref[slot, k]
    out = y * lax.rsqrt(jnp.mean(y * y, axis=-1, keepdims=True) + NORM_EPS) * gf_ref[...]

    @pl.when(i < n_p)
    def _():
        op_ref[...] = out

    @pl.when(i >= n_p)
    def _():
        os_ref[...] = out


def _combine(dest_flat, gate, x1, gf, ys, n_prompt_rows, tm):
    rows, d = x1.shape
    n = rows // tm
    n_p = n_prompt_rows // tm
    msz = tm * TOP_K
    return pl.pallas_call(
        functools.partial(_combine_kernel, n_p),
        grid=(n,),
        in_specs=[pl.BlockSpec((msz,), lambda i: (i,), memory_space=pltpu.SMEM),
                  pl.BlockSpec((msz,), lambda i: (jnp.minimum(i + 1, n - 1),), memory_space=pltpu.SMEM),
                  pl.BlockSpec((tm, TOP_K), lambda i: (i, 0)),
                  pl.BlockSpec((tm, d), lambda i: (i, 0)),
                  pl.BlockSpec((1, d), lambda i: (0, 0)),
                  pl.BlockSpec(memory_space=pl.ANY)],
        out_specs=[pl.BlockSpec((tm, d), lambda i: (jnp.minimum(i, n_p - 1), 0)),
                   pl.BlockSpec((tm, d), lambda i: (jnp.maximum(i - n_p, 0), 0))],
        out_shape=[jax.ShapeDtypeStruct((n_prompt_rows, d), F32),
                   jax.ShapeDtypeStruct((rows - n_prompt_rows, d), F32)],
        scratch_shapes=[pltpu.VMEM((2, TOP_K, tm, d), F32), pltpu.SemaphoreType.DMA((2,))],
        compiler_params=_cparams(("arbitrary",)),
        name="moe_combine",
    )(dest_flat, dest_flat, gate, x1, gf, ys)


def _rope_tables(pos):
    half = ROT_DIM // 2
    inv = ROPE_THETA ** (-jnp.arange(0, ROT_DIM, 2, dtype=F32) / ROT_DIM)
    ang = inv[:, None] * pos.astype(F32)[None, :]
    cos, sin = jnp.cos(ang), jnp.sin(ang)
    n = pos.shape[0]
    pad1 = jnp.ones((HEAD_DIM - ROT_DIM, n), F32)
    pad0 = jnp.zeros((HEAD_DIM - ROT_DIM, n), F32)
    cos_h = jnp.concatenate([cos, cos, pad1], axis=0)
    sin_h = jnp.concatenate([-sin, sin, pad0], axis=0)
    reps = (LANES // HEAD_DIM, 1)
    return jnp.tile(cos_h, reps).T, jnp.tile(sin_h, reps).T


def _pairs_from_state(S):
    H = S.shape[0]
    St = jnp.swapaxes(S, 1, 2).reshape(H // 2, 2, HEAD_DIM, HEAD_DIM)
    z = jnp.zeros_like(St[:, 0])
    top = jnp.concatenate([St[:, 0], z], axis=2)
    bot = jnp.concatenate([z, St[:, 1]], axis=2)
    return jnp.concatenate([top, bot], axis=1)


def _state_from_pairs(Sp):
    a = Sp[:, :HEAD_DIM, :HEAD_DIM]
    b = Sp[:, HEAD_DIM:, HEAD_DIM:]
    St = jnp.stack([a, b], axis=1).reshape(-1, HEAD_DIM, HEAD_DIM)
    return jnp.swapaxes(St, 1, 2)


def kernel(x_prompt, x_sample, state_rwkv_wkv, state_rwkv_shift, cache_swa_k, cache_swa_v, norm1_g, w_in, mu_shift, decay_w0, decay_w2, aaa_a0, aaa_w2, gate_w2, k_k, k_a, r_k, lnx_g, lnx_b, attn_sinks, w_out, norm2_g, w_router, b_router, w_mlp1, b_mlp1, w_mlp2, b_mlp2, norm_f_g):
    depth = w_in.shape[0]
    assert depth == 1 and x_prompt.shape[0] == 1 and x_sample.shape[1] == 1
    T, d = x_prompt.shape[1], x_prompt.shape[2]
    B = x_sample.shape[0]
    past_len = PAST_LEN
    H = state_rwkv_wkv.shape[2]
    rw_w = H * HEAD_DIM
    n_pairs = H // 2
    n_q = attn_sinks.shape[1]
    n_kv = cache_swa_k.shape[3]
    q_cols = n_q * HEAD_DIM
    kv_cols = n_kv * HEAD_DIM
    rw_cols = state_rwkv_shift.shape[2]
    assert rw_cols == 3 * rw_w + 2 * HEAD_DIM + PAIR and kv_cols == LANES
    assert T % RW_TILE == 0 and B % ROW_TILE == 0 and B % 8 == 0
    wlen = cache_swa_k.shape[2]
    l = 0

    w_in_bf = w_in[l].astype(BF16)
    zero_half = jnp.zeros((HEAD_DIM, rw_w), F32)
    pp = dict(mu=mu_shift[l][None], w0=decay_w0[l][None],
              dw2=jnp.concatenate([decay_w2[l], zero_half], axis=0),
              a0=aaa_a0[l][None], aw2=jnp.concatenate([zero_half, aaa_w2[l]], axis=0),
              gw2=gate_w2[l], kk=k_k[l][None], ka=k_a[l][None], rk=r_k[l].reshape(1, rw_w),
              lng=lnx_g[l][None], lnb=lnx_b[l][None])
    w_out_bf = w_out[l].astype(BF16)
    n_e = w_router.shape[2]
    wr = jnp.pad(w_router[l], ((0, 0), (0, LANES - n_e)))
    br = jnp.concatenate([b_router[l], jnp.full((LANES - n_e,), NEG_BIG, F32)])[None]
    wp = dict(wa=w_out_bf[:rw_w], wb=w_out_bf[rw_w:], g2=norm2_g[l][None], wr=wr, br=br)
    g1 = norm1_g[l][None]

    xp = x_prompt[0]
    cos_p, sin_p = _rope_tables(jnp.arange(T))
    prw_p, q_p, k_p, v_p = _inproj(xp, g1, w_in_bf, cos_p, sin_p, 512, rw_cols, q_cols, kv_cols)
    s0_p = jnp.zeros((n_pairs, PAIR, PAIR), F32)
    shift0_p = jnp.zeros((1, rw_cols), F32)
    ya_p, sfin_p = _rwkv_prompt(prw_p, shift0_p, s0_p, pp, RW_TILE)
    yb_p = _attn_prompt(q_p, k_p, v_p, attn_sinks[l], n_q, n_kv)

    xs_ = x_sample[:, 0]
    cos_s, sin_s = _rope_tables(jnp.full((B,), past_len))
    prw_s, q_s, k_s, v_s = _inproj(xs_, g1, w_in_bf, cos_s, sin_s, ROW_TILE, rw_cols, q_cols, kv_cols)
    s_flat = state_rwkv_wkv[l].reshape(B, H * HEAD_DIM * HEAD_DIM)
    ya_s, snew_flat = _rwkv_step(prw_s, state_rwkv_shift[l], s_flat, pp, n_pairs)
    o2, kc_new, vc_new = _attn_step(q_s.reshape(B * n_q, HEAD_DIM), k_s, v_s,
                                    cache_swa_k[l].reshape(B, wlen, kv_cols),
                                    cache_swa_v[l].reshape(B, wlen, kv_cols),
                                    attn_sinks[l][:, None], n_q, n_kv, past_len)
    yb_s = o2.reshape(B, q_cols)

    rows = T + B
    *bufs, cnt = _post(xp, ya_p, yb_p, wp, jnp.zeros((1, LANES), F32), (), rows, 0, 256)
    x1, h2, gate, meta, cnt = _post(xs_, ya_s, yb_s, wp, cnt, bufs, rows, T, ROW_TILE)

    counts = cnt[0, :n_e].astype(jnp.int32)
    padded = (counts + MOE_BM - 1) // MOE_BM * MOE_BM
    pend = jnp.cumsum(padded)
    pstart = (pend - padded).astype(jnp.int32)
    n_blocks = -(-(rows * TOP_K) // MOE_BM) + n_e
    block_start = jnp.arange(n_blocks, dtype=jnp.int32) * MOE_BM
    block_e = jnp.minimum(jnp.sum((pend[None, :] <= block_start[:, None]).astype(jnp.int32), axis=1),
                          n_e - 1).astype(jnp.int32)
    n_used = (pend[-1] // MOE_BM).astype(jnp.int32)[None]
    tail_row = jnp.where(counts > 0, pend - MOE_BM, -1).astype(jnp.int32)
    dest_flat = (pstart[meta[:, :TOP_K]] + meta[:, TOP_K:]).reshape(rows * TOP_K)

    xs_sorted = _scatter(tail_row, dest_flat, h2, n_blocks * MOE_BM, MOE_BM, ROW_TILE)
    ys_sorted = _experts(block_e, n_used, xs_sorted, w_mlp1[l], b_mlp1[l][:, None], w_mlp2[l],
                         b_mlp2[l][:, None], MOE_BM)
    y_p, y_s = _combine(dest_flat, gate, x1, norm_f_g[None], ys_sorted, T, ROW_TILE)

    sdt = state_rwkv_wkv.dtype
    return (y_p[None], y_s[:, None],
            _state_from_pairs(sfin_p)[None, None].astype(sdt), prw_p[T - 1][None, None],
            k_p[T - min(WINDOW, T):].reshape(1, 1, -1, n_kv, HEAD_DIM),
            v_p[T - min(WINDOW, T):].reshape(1, 1, -1, n_kv, HEAD_DIM),
            snew_flat.reshape(1, B, H, HEAD_DIM, HEAD_DIM).astype(sdt), prw_s[None],
            kc_new.reshape(1, B, wlen, n_kv, HEAD_DIM), vc_new.reshape(1, B, wlen, n_kv, HEAD_DIM))
```

```python
import functools

import jax
import jax.numpy as jnp
from jax import lax
from jax.experimental import pallas as pl
from jax.experimental.pallas import tpu as pltpu

F32 = jnp.float32
BF16 = jnp.bfloat16

LANES = 128
HEAD_DIM = 64
PAIR = 2 * HEAD_DIM
CHUNK = 64
RW_TILE = 256
RW_PAIRS_PER_STEP = 4
ROT_DIM = 16
ROPE_THETA = 500000.0
WINDOW = 128
PAST_LEN = 16384
ATT_BLOCK = 128
N_EXPERTS = 32
TOP_K = 4
SWIGLU_ALPHA = 1.702
SWIGLU_LIMIT = 7.0
NORM_EPS = 1e-5
LNX_EPS = HEAD_DIM * 1e-5
MOE_BM = 512
ROW_TILE = 128
MOE_WIN_SHIFT = 5
MOE_WIN = 1 << MOE_WIN_SHIFT
SUBLANES = 8
MOE_SLOTS = 56
assert MOE_SLOTS * MOE_WIN >= ROW_TILE * TOP_K + N_EXPERTS * (SUBLANES - 1 + MOE_WIN - 1)
NEG_BIG = -1e30
VMEM_LIMIT = 52 * 1024 * 1024

NN = (((1,), (0,)), ((), ()))
NT = (((1,), (1,)), ((), ()))


def _mm(a, b, dn=NN):
    return lax.dot_general(a, b, dn, preferred_element_type=F32)


def _split2(a):
    hi = a.astype(BF16)
    lo = (a - hi.astype(F32)).astype(BF16)
    return hi, lo


def _split3(a):
    hi = a.astype(BF16)
    r1 = a - hi.astype(F32)
    mid = r1.astype(BF16)
    lo = (r1 - mid.astype(F32)).astype(BF16)
    return hi, mid, lo


def _pdot(a, b, dn=NN):
    return _mm(a[0], b[0], dn) + (_mm(a[0], b[1], dn) + _mm(a[1], b[0], dn))


def _dot1(a, b, dn=NN):
    return _mm(a.astype(BF16), b.astype(BF16), dn)


def _dot_sel_l(sel, b, dn=NN):
    b0, b1, b2 = _split3(b)
    return _mm(sel, b0, dn) + (_mm(sel, b1, dn) + _mm(sel, b2, dn))


def _dot_sel_r(a, sel, dn=NN):
    a0, a1, a2 = _split3(a)
    return _mm(a0, sel, dn) + (_mm(a1, sel, dn) + _mm(a2, sel, dn))


def _iota(shape, dim):
    return lax.broadcasted_iota(jnp.int32, shape, dim)


def _seg_matrix():
    return ((_iota((PAIR, PAIR), 0) // HEAD_DIM) == (_iota((PAIR, PAIR), 1) // HEAD_DIM)).astype(BF16)


def _sigmoid(x):
    return 1.0 / (1.0 + jnp.exp(-x))


def _cparams(sem, vmem=VMEM_LIMIT):
    return pltpu.CompilerParams(dimension_semantics=sem, vmem_limit_bytes=vmem)


def _rope_slab(x, cos, sin_signed):
    lane = _iota(x.shape, 1) % HEAD_DIM
    up = pltpu.roll(x, LANES - ROT_DIM // 2, axis=1)
    down = pltpu.roll(x, ROT_DIM // 2, axis=1)
    partner = jnp.where(lane < ROT_DIM // 2, up, down)
    return x * cos + partner * sin_signed


def _inproj_kernel(rw_cols, q_cols, kv_cols, x_ref, g_ref, w_ref, cos_ref, sin_ref,
                   prw_ref, q_ref, k_ref, v_ref):
    x = x_ref[...]
    h = x * lax.rsqrt(jnp.mean(x * x, axis=-1, keepdims=True) + NORM_EPS) * g_ref[...]
    proj = _mm(h.astype(BF16), w_ref[...])
    prw_ref[...] = proj[:, :rw_cols]
    cos = cos_ref[...]
    sin = sin_ref[...]
    for c in range(q_cols // LANES):
        lo = rw_cols + c * LANES
        q_ref[:, c * LANES:(c + 1) * LANES] = _rope_slab(proj[:, lo:lo + LANES], cos, sin)
    ko = rw_cols + q_cols
    for c in range(kv_cols // LANES):
        k_ref[:, c * LANES:(c + 1) * LANES] = _rope_slab(proj[:, ko + c * LANES:ko + (c + 1) * LANES], cos, sin)
    v_ref[...] = proj[:, ko + kv_cols:ko + 2 * kv_cols]


def _inproj(x, g, w_bf, cos_t, sin_t, tm, rw_cols, q_cols, kv_cols):
    rows, d = x.shape
    cols = w_bf.shape[1]
    full = lambda i: (0, 0)
    row = lambda i: (i, 0)
    return pl.pallas_call(
        functools.partial(_inproj_kernel, rw_cols, q_cols, kv_cols),
        grid=(rows // tm,),
        in_specs=[pl.BlockSpec((tm, d), row), pl.BlockSpec((1, d), full),
                  pl.BlockSpec((d, cols), full),
                  pl.BlockSpec((tm, LANES), row), pl.BlockSpec((tm, LANES), row)],
        out_specs=[pl.BlockSpec((tm, rw_cols), row), pl.BlockSpec((tm, q_cols), row),
                   pl.BlockSpec((tm, kv_cols), row), pl.BlockSpec((tm, kv_cols), row)],
        out_shape=[jax.ShapeDtypeStruct((rows, rw_cols), F32), jax.ShapeDtypeStruct((rows, q_cols), F32),
                   jax.ShapeDtypeStruct((rows, kv_cols), F32), jax.ShapeDtypeStruct((rows, kv_cols), F32)],
        compiler_params=_cparams(("parallel",)),
        name="inproj",
    )(x, g, w_bf, cos_t, sin_t)


def _rwkv_tokenwise(pr, pk, pv, plo, pg, prev_r, prev_k, prev_v, prev_lo, prev_g,
                    mu_r, mu_k, mu_v, mu_lo, mu_g, w0, dw2, a0, aw2, gw2, kkp, kap, rkp, seg):
    r = pr + (prev_r - pr) * mu_r
    k = pk + (prev_k - pk) * mu_k
    v = pv + (prev_v - pv) * mu_v
    lo = plo + (prev_lo - plo) * mu_lo
    gd = pg + (prev_g - pg) * mu_g
    z = -(w0 + _dot1(jnp.tanh(lo), dw2))
    softplus = jnp.maximum(z, 0.0) + jnp.log(1.0 + jnp.exp(-jnp.abs(z)))
    logw = -jnp.exp(-softplus - 0.5)
    a = _sigmoid(a0 + _dot1(lo, aw2))
    g = _dot1(_sigmoid(gd), gw2)
    kk = k * kkp
    nrm = jnp.sqrt(_seg_sum(kk * kk, seg))
    kk = kk / jnp.maximum(nrm, 1e-12)
    k2 = k * (1.0 + (a - 1.0) * kap)
    bonus = _seg_sum(r * k2 * rkp, seg) * v
    return r, k2, v, logw, -kk, kk * a, g, bonus


def _seg_sum(x, seg):
    xh, xl = _split2(x)
    return _mm(xh, seg) + _mm(xl, seg)


def _rwkv_finish(y, bonus, g, lng, lnb, seg):
    mu = _seg_sum(y, seg) * (1.0 / HEAD_DIM)
    d = y - mu
    var = _seg_sum(d * d, seg) * (1.0 / HEAD_DIM)
    yn = d * lax.rsqrt(var + LNX_EPS) * lng + lnb
    return (yn + bonus) * g


def _rwkv_prompt_kernel(pps, pr_ref, pk_ref, pv_ref, plo_ref, pg_ref,
                        hr_ref, hk_ref, hv_ref, hlo_ref, hg_ref,
                        s0r_ref, s0k_ref, s0v_ref, s0lo_ref, s0g_ref,
                        mur_ref, muk_ref, muv_ref, mulo_ref, mug_ref,
                        w0_ref, dw2_ref, a0_ref, aw2_ref, gw2_ref, kk_ref, ka_ref, rk_ref,
                        lng_ref, lnb_ref, sin_ref,
                        y_ref, sout_ref, st_ref):
    i = pl.program_id(1)
    n_i = pl.num_programs(1)
    tt = pr_ref.shape[0]

    @pl.when(i == 0)
    def _():
        st_ref[...] = sin_ref[...]

    row = _iota((tt, PAIR), 0)

    def prev_of(cur, halo_row, s0_row):
        first = jnp.where(i == 0, s0_row, halo_row)
        return jnp.where(row == 0, first, pltpu.roll(cur, 1, axis=0))

    plo = plo_ref[...]
    pg = pg_ref[...]
    prev_lo = prev_of(plo, hlo_ref[7:8, :], s0lo_ref[...])
    prev_g = prev_of(pg, hg_ref[7:8, :], s0g_ref[...])
    ti = _iota((tt, tt), 0)
    tj = _iota((tt, tt), 1)
    same_chunk = (ti // CHUNK) == (tj // CHUNK)
    incl = same_chunk & (tj <= ti)
    strict = same_chunk & (tj < ti)
    seg = _seg_matrix()
    lane = _iota((tt, PAIR), 1)
    eye = (ti == tj).astype(F32)
    pairs = []
    for p in range(pps):
        ls = slice(p * PAIR, (p + 1) * PAIR)
        pr, pk, pv = pr_ref[:, ls], pk_ref[:, ls], pv_ref[:, ls]
        r, k2, v, logw, nkk, b, g, bonus = _rwkv_tokenwise(
            pr, pk, pv, plo, pg,
            prev_of(pr, hr_ref[7:8, ls], s0r_ref[:, ls]), prev_of(pk, hk_ref[7:8, ls], s0k_ref[:, ls]),
            prev_of(pv, hv_ref[7:8, ls], s0v_ref[:, ls]), prev_lo, prev_g,
            mur_ref[:, ls], muk_ref[:, ls], muv_ref[:, ls], mulo_ref[...], mug_ref[...],
            w0_ref[:, ls], dw2_ref[:, ls], a0_ref[:, ls], aw2_ref[:, ls], gw2_ref[:, ls],
            kk_ref[:, ls], ka_ref[:, ls], rk_ref[:, ls], seg)
        pairs.append(dict(ls=ls, r=r, k2=k2, v=v, logw=logw, nkk=nkk, b=b, g=g, bonus=bonus))

    incl_b = incl.astype(BF16)
    for q in pairs:
        q["cs"] = _dot_sel_l(incl_b, q["logw"])
    for q in pairs:
        cs = q["cs"]
        gam = jnp.exp(cs)
        inv = jnp.exp(-cs)
        q["a_t"] = jnp.exp(cs - q["logw"]) * q["nkk"]
        q["r_t"] = gam * q["r"]
        q["bt_T"] = (q["b"] * inv).T
        q["kt_T"] = (q["k2"] * inv).T
        q["gam_T"] = gam.T
        q["bk_T"] = jnp.concatenate([q["bt_T"], q["kt_T"]], axis=1).astype(BF16)

    heads = []
    for q in pairs:
        for hh in range(2):
            hm = (lane // HEAD_DIM) == hh
            heads.append(dict(q=q, a=jnp.where(hm, q["a_t"], 0.0), r=jnp.where(hm, q["r_t"], 0.0),
                              v=jnp.where(hm, q["v"], 0.0)))
    for h in heads:
        h["g"] = _mm(jnp.concatenate([h["a"], h["r"]], axis=0).astype(BF16), h["q"]["bk_T"])
    for h in heads:
        gmat = h["g"]
        l_ab = jnp.where(strict, gmat[:tt, :tt], 0.0)
        h["l_ak_m_rk"] = jnp.concatenate([jnp.where(strict, gmat[:tt, tt:], 0.0),
                                          jnp.where(incl, gmat[tt:, tt:], 0.0)], axis=0).astype(BF16)
        h["m_rb"] = jnp.where(incl, gmat[tt:, :tt], 0.0).astype(BF16)
        h["tm"] = eye + l_ab
        h["lp"] = l_ab.astype(BF16)
    for h in heads:
        h["lp"] = _mm(h["lp"], h["lp"]).astype(BF16)
    for _ in range(4):
        for h in heads:
            h["both"] = _mm(jnp.concatenate([h["tm"].astype(BF16), h["lp"]], axis=0), h["lp"])
        for h in heads:
            h["tm"] = h["tm"] + h["both"][:tt]
            h["lp"] = h["both"][tt:].astype(BF16)
    for h in heads:
        h["pq"] = _mm(h["l_ak_m_rk"], h["v"].astype(BF16))
        h["tm"] = h["tm"] + _mm(h["tm"].astype(BF16), h["lp"])
    for h in heads:
        h["tx"] = _mm(h["tm"].astype(BF16),
                      jnp.concatenate([h["a"], h["pq"][:tt]], axis=1).astype(BF16))
    for h in heads:
        h["rx"] = _mm(h["m_rb"], h["tx"].astype(BF16))
    for n, q in enumerate(pairs):
        h0, h1 = heads[2 * n], heads[2 * n + 1]
        q["tatp"] = (h0["tx"] + h1["tx"]).astype(BF16)
        ryc = (h0["rx"] + h1["rx"]) + jnp.concatenate([h0["r"] + h1["r"], h0["pq"][tt:] + h1["pq"][tt:]], axis=1)
        q["ry"] = ryc[:, :PAIR]
        q["yc"] = ryc[:, PAIR:]
        q["v_b"] = q["v"].astype(BF16)
        q["bt_b"] = q["bt_T"].astype(BF16)
        q["kt_b"] = q["kt_T"].astype(BF16)
        q["s"] = st_ref[n]

    bd = seg.astype(F32)
    eye_p = (_iota((PAIR, PAIR), 0) == _iota((PAIR, PAIR), 1)).astype(F32)
    col_t = _iota((PAIR, tt), 1)
    zb = jnp.zeros((PAIR, tt), BF16)
    n_chunks = tt // CHUNK
    for c in range(n_chunks):
        cm = (col_t // CHUNK) == c
        for q in pairs:
            bt_c = jnp.where(cm, q["bt_b"], zb)
            kt_c = jnp.where(cm, q["kt_b"], zb)
            dcol = q["gam_T"][:, (c + 1) * CHUNK - 1:(c + 1) * CHUNK]
            bx = _mm(bt_c, q["tatp"])
            q["mc", c] = _split2(dcol * (eye_p + bd * bx[:, :PAIR]))
            q["nc", c] = dcol * (bd * (bx[:, PAIR:] + _mm(kt_c, q["v_b"])))
    for c in range(n_chunks):
        sl = slice(c * CHUNK, (c + 1) * CHUNK)
        for q in pairs:
            ss = _split2(q["s"])
            q["y", c] = _pdot(_split2(q["ry"][sl]), ss) + q["yc"][sl]
            q["s"] = _pdot(q["mc", c], ss) + q["nc", c]
    for q in pairs:
        y = jnp.concatenate([q["y", c] for c in range(n_chunks)], axis=0)
        y_ref[:, q["ls"]] = _rwkv_finish(y, q["bonus"], q["g"], lng_ref[:, q["ls"]], lnb_ref[:, q["ls"]], seg)
    for n, q in enumerate(pairs):
        st_ref[n] = q["s"]

    @pl.when(i == n_i - 1)
    def _():
        sout_ref[...] = st_ref[...]


def _rwkv_prompt(prw, shift0, s0_pairs, pp, tt):
    T = prw.shape[0]
    n_pairs = s0_pairs.shape[0]
    pps = RW_PAIRS_PER_STEP
    n_grp = n_pairs // pps
    gw = pps * PAIR
    wcols = n_pairs * PAIR
    lo_col = 3 * wcols
    g_col = lo_col + PAIR
    hb = tt // 8

    def cur(off):
        return pl.BlockSpec((tt, gw), lambda p, i: (i, off // gw + p))

    def cur_fixed(col):
        return pl.BlockSpec((tt, PAIR), lambda p, i: (i, col // PAIR))

    def halo(off):
        return pl.BlockSpec((8, gw), lambda p, i: (jnp.maximum(i * hb - 1, 0), off // gw + p))

    def halo_fixed(col):
        return pl.BlockSpec((8, PAIR), lambda p, i: (jnp.maximum(i * hb - 1, 0), col // PAIR))

    def vec(off):
        return pl.BlockSpec((1, gw), lambda p, i: (0, off // gw + p))

    def vec_fixed(col):
        return pl.BlockSpec((1, PAIR), lambda p, i: (0, col // PAIR))

    def wmat(rows):
        return pl.BlockSpec((rows, gw), lambda p, i: (0, p))

    in_specs = ([cur(0), cur(wcols), cur(2 * wcols), cur_fixed(lo_col), cur_fixed(g_col)]
                + [halo(0), halo(wcols), halo(2 * wcols), halo_fixed(lo_col), halo_fixed(g_col)]
                + [vec(0), vec(wcols), vec(2 * wcols), vec_fixed(lo_col), vec_fixed(g_col)]
                + [vec(0), vec(wcols), vec(2 * wcols), vec_fixed(lo_col), vec_fixed(g_col)]
                + [vec(0), wmat(PAIR), vec(0), wmat(PAIR), wmat(PAIR), vec(0), vec(0), vec(0), vec(0), vec(0)]
                + [pl.BlockSpec((pps, PAIR, PAIR), lambda p, i: (p, 0, 0))])
    args = ([prw] * 5 + [prw] * 5 + [shift0] * 5 + [pp["mu"]] * 5
            + [pp["w0"], pp["dw2"], pp["a0"], pp["aw2"], pp["gw2"], pp["kk"], pp["ka"], pp["rk"],
               pp["lng"], pp["lnb"], s0_pairs])
    return pl.pallas_call(
        functools.partial(_rwkv_prompt_kernel, pps),
        grid=(n_grp, T // tt),
        in_specs=in_specs,
        out_specs=[pl.BlockSpec((tt, gw), lambda p, i: (i, p)),
                   pl.BlockSpec((pps, PAIR, PAIR), lambda p, i: (p, 0, 0))],
        out_shape=[jax.ShapeDtypeStruct((T, wcols), F32),
                   jax.ShapeDtypeStruct((n_pairs, PAIR, PAIR), F32)],
        scratch_shapes=[pltpu.VMEM((pps, PAIR, PAIR), F32)],
        compiler_params=_cparams(("parallel", "arbitrary")),
        name="rwkv_prompt",
    )(*args)


def _rwkv_step_kernel(slabs_per_step, pr_ref, pk_ref, pv_ref, plo_ref, pg_ref,
                      sr_ref, sk_ref, sv_ref, slo_ref, sg_ref,
                      mur_ref, muk_ref, muv_ref, mulo_ref, mug_ref,
                      w0_ref, dw2_ref, a0_ref, aw2_ref, gw2_ref, kk_ref, ka_ref, rk_ref,
                      lng_ref, lnb_ref, s_ref,
                      y_ref, snew_ref, yacc_ref):
    j = pl.program_id(1)
    n_j = pl.num_programs(1)
    seg = _seg_matrix()
    r, k2, v, logw, nkk, b, g, bonus = _rwkv_tokenwise(
        pr_ref[...], pk_ref[...], pv_ref[...], plo_ref[...], pg_ref[...],
        sr_ref[...], sk_ref[...], sv_ref[...], slo_ref[...], sg_ref[...],
        mur_ref[...], muk_ref[...], muv_ref[...], mulo_ref[...], mug_ref[...],
        w0_ref[...], dw2_ref[...], a0_ref[...], aw2_ref[...], gw2_ref[...],
        kk_ref[...], ka_ref[...], rk_ref[...], seg)
    w = jnp.exp(logw)

    @pl.when(j == 0)
    def _():
        yacc_ref[...] = jnp.zeros_like(yacc_ref)

    slabs_per_head = HEAD_DIM // 2
    ci = _iota((PAIR, PAIR), 0)
    li = _iota((PAIR, PAIR), 1)
    assert slabs_per_head % slabs_per_step == 0
    yacc = yacc_ref[...]
    hh = (j * slabs_per_step) // slabs_per_head
    dup = ((ci == hh * HEAD_DIM + li % HEAD_DIM)).astype(BF16)
    nkk_d, w_d, b_d, k_d, r_d = [_dot_sel_r(x, dup) for x in (nkk, w, b, k2, r)]
    for t in range(slabs_per_step):
        slab = j * slabs_per_step + t
        i0 = 2 * (slab % slabs_per_head)
        selv = (ci == hh * HEAD_DIM + i0 + li // HEAD_DIM).astype(BF16)
        sely = ((ci % HEAD_DIM == 0) & (li == hh * HEAD_DIM + i0 + ci // HEAD_DIM)).astype(BF16)
        s = s_ref[:, t * PAIR:(t + 1) * PAIR]
        sa = _dot_sel_r(s * nkk_d, seg)
        s_new = s * w_d + sa * b_d + _dot_sel_r(v, selv) * k_d
        snew_ref[:, t * PAIR:(t + 1) * PAIR] = s_new
        yred = _dot_sel_r(s_new * r_d, seg)
        yacc = yacc + _dot_sel_r(yred, sely)
    yacc_ref[...] = yacc

    @pl.when(j == n_j - 1)
    def _():
        y_ref[...] = _rwkv_finish(yacc, bonus, g, lng_ref[...], lnb_ref[...], seg)


def _rwkv_step(prw, shift, s_flat, pp, n_pairs):
    B = prw.shape[0]
    lanes_per_pair = 2 * HEAD_DIM * HEAD_DIM
    blk = 1024
    slabs_per_step = blk // PAIR
    steps = lanes_per_pair // blk
    lo_blk = 3 * n_pairs
    g_blk = lo_blk + 1

    def cur(off):
        return pl.BlockSpec((B, PAIR), lambda p, j: (0, off + p))

    def cur_fixed(b_):
        return pl.BlockSpec((B, PAIR), lambda p, j: (0, b_))

    def vec(off):
        return pl.BlockSpec((1, PAIR), lambda p, j: (0, off + p))

    def vec_fixed(b_):
        return pl.BlockSpec((1, PAIR), lambda p, j: (0, b_))

    def wmat(rows):
        return pl.BlockSpec((rows, PAIR), lambda p, j: (0, p))

    sspec = pl.BlockSpec((B, blk), lambda p, j: (0, p * steps + j))
    in_specs = ([cur(0), cur(n_pairs), cur(2 * n_pairs), cur_fixed(lo_blk), cur_fixed(g_blk)] * 2
                + [vec(0), vec(n_pairs), vec(2 * n_pairs), vec_fixed(lo_blk), vec_fixed(g_blk)]
                + [vec(0), wmat(PAIR), vec(0), wmat(PAIR), wmat(PAIR), vec(0), vec(0), vec(0), vec(0), vec(0)]
                + [sspec])
    args = ([prw] * 5 + [shift] * 5 + [pp["mu"]] * 5
            + [pp["w0"], pp["dw2"], pp["a0"], pp["aw2"], pp["gw2"], pp["kk"], pp["ka"], pp["rk"],
               pp["lng"], pp["lnb"], s_flat])
    return pl.pallas_call(
        functools.partial(_rwkv_step_kernel, slabs_per_step),
        grid=(n_pairs, steps),
        in_specs=in_specs,
        out_specs=[pl.BlockSpec((B, PAIR), lambda p, j: (0, p)), sspec],
        out_shape=[jax.ShapeDtypeStruct((B, n_pairs * PAIR), F32),
                   jax.ShapeDtypeStruct(s_flat.shape, F32)],
        scratch_shapes=[pltpu.VMEM((B, PAIR), F32)],
        compiler_params=_cparams(("parallel", "arbitrary")),
        name="rwkv_step",
    )(*args)


def _attn_prompt_kernel(n_q, group, sink_ref, q_ref, kc_ref, kp_ref, vc_ref, vp_ref, o_ref):
    blk = q_ref.shape[0]
    q = q_ref[...] * (HEAD_DIM ** -0.5)
    kband = jnp.concatenate([kp_ref[...], kc_ref[...]], axis=0)
    vband = jnp.concatenate([vp_ref[...], vc_ref[...]], axis=0)
    i = pl.program_id(0)
    rq = _iota((blk, 2 * blk), 0)
    ck = _iota((blk, 2 * blk), 1)
    dist = rq - ck + blk
    kpos = i * blk - blk + ck
    valid = (dist >= 0) & (dist < WINDOW) & (kpos >= 0)
    for h in range(n_q):
        gk = h // group
        qh = q[:, h * HEAD_DIM:(h + 1) * HEAD_DIM]
        kh = kband[:, gk * HEAD_DIM:(gk + 1) * HEAD_DIM]
        vh = vband[:, gk * HEAD_DIM:(gk + 1) * HEAD_DIM]
        s = jnp.where(valid, _dot1(qh, kh, NT), NEG_BIG)
        sink = sink_ref[h]
        m = jnp.maximum(jnp.max(s, axis=-1, keepdims=True), sink)
        p = jnp.exp(s - m)
        denom = jnp.sum(p, axis=-1, keepdims=True) + jnp.exp(sink - m)
        o_ref[:, h * HEAD_DIM:(h + 1) * HEAD_DIM] = _dot1(p, vh) / denom


def _attn_prompt(q, k, v, sinks, n_q, n_kv):
    T, qw = q.shape
    kvw = k.shape[1]
    blk = ATT_BLOCK
    curm = lambda i: (i, 0)
    prevm = lambda i: (jnp.maximum(i - 1, 0), 0)
    return pl.pallas_call(
        functools.partial(_attn_prompt_kernel, n_q, n_q // n_kv),
        grid=(T // blk,),
        in_specs=[pl.BlockSpec(memory_space=pltpu.SMEM),
                  pl.BlockSpec((blk, qw), curm),
                  pl.BlockSpec((blk, kvw), curm), pl.BlockSpec((blk, kvw), prevm),
                  pl.BlockSpec((blk, kvw), curm), pl.BlockSpec((blk, kvw), prevm)],
        out_specs=pl.BlockSpec((blk, qw), curm),
        out_shape=jax.ShapeDtypeStruct((T, qw), F32),
        compiler_params=_cparams(("parallel",)),
        name="attn_prompt",
    )(sinks, q, k, k, v, v)


def _attn_step_kernel(n_q, group, pos0, sink_ref, q_ref, kn_ref, vn_ref, kc_ref, vc_ref,
                      o_ref, ko_ref, vo_ref):
    bb, wlen, kvw = kc_ref.shape
    lane = _iota((n_q, kvw), 1)
    rowh = _iota((n_q, kvw), 0)
    mine = (lane // HEAD_DIM) == (rowh // group)
    dupm = (_iota((HEAD_DIM, kvw), 0) == _iota((HEAD_DIM, kvw), 1) % HEAD_DIM).astype(BF16)
    fold = (_iota((kvw, HEAD_DIM), 0) % HEAD_DIM == _iota((kvw, HEAD_DIM), 1)).astype(BF16)
    kidx = _iota((n_q, wlen), 1)
    dist = wlen - kidx
    valid = (dist < WINDOW) & (pos0 - dist >= 0)
    rk = _iota((wlen, kvw), 0)
    sink = sink_ref[...]
    for t in range(bb):
        qh = q_ref[t * n_q:(t + 1) * n_q, :] * (HEAD_DIM ** -0.5)
        qm = jnp.where(mine, _dot_sel_r(qh, dupm), 0.0)
        kc = kc_ref[t]
        vc = vc_ref[t]
        kn = kn_ref[t:t + 1, :]
        vn = vn_ref[t:t + 1, :]
        s = jnp.where(valid, _dot1(qm, kc, NT), NEG_BIG)
        s_new = jnp.sum(qm * kn, axis=-1, keepdims=True)
        m = jnp.maximum(jnp.maximum(jnp.max(s, axis=-1, keepdims=True), s_new), sink)
        p = jnp.exp(s - m)
        p_new = jnp.exp(s_new - m)
        denom = jnp.sum(p, axis=-1, keepdims=True) + p_new + jnp.exp(sink - m)
        res = (_dot1(p, vc) + p_new * vn) / denom
        o_ref[t * n_q:(t + 1) * n_q, :] = _dot_sel_r(jnp.where(mine, res, 0.0), fold)
        ko_ref[t] = jnp.where(rk == wlen - 1, kn, pltpu.roll(kc, wlen - 1, axis=0))
        vo_ref[t] = jnp.where(rk == wlen - 1, vn, pltpu.roll(vc, wlen - 1, axis=0))


def _attn_step(q2, k_new, v_new, k_cache, v_cache, sinks_col, n_q, n_kv, pos0):
    B, wlen, kvw = k_cache.shape
    bb = 8
    return pl.pallas_call(
        functools.partial(_attn_step_kernel, n_q, n_q // n_kv, pos0),
        grid=(B // bb,),
        in_specs=[pl.BlockSpec((n_q, 1), lambda i: (0, 0)),
                  pl.BlockSpec((bb * n_q, HEAD_DIM), lambda i: (i, 0)),
                  pl.BlockSpec((bb, kvw), lambda i: (i, 0)), pl.BlockSpec((bb, kvw), lambda i: (i, 0)),
                  pl.BlockSpec((bb, wlen, kvw), lambda i: (i, 0, 0)),
                  pl.BlockSpec((bb, wlen, kvw), lambda i: (i, 0, 0))],
        out_specs=[pl.BlockSpec((bb * n_q, HEAD_DIM), lambda i: (i, 0)),
                   pl.BlockSpec((bb, wlen, kvw), lambda i: (i, 0, 0)),
                   pl.BlockSpec((bb, wlen, kvw), lambda i: (i, 0, 0))],
        out_shape=[jax.ShapeDtypeStruct((B * n_q, HEAD_DIM), F32),
                   jax.ShapeDtypeStruct((B, wlen, kvw), F32),
                   jax.ShapeDtypeStruct((B, wlen, kvw), F32)],
        compiler_params=_cparams(("parallel",)),
        name="attn_step",
    )(sinks_col, q2, k_new, v_new, k_cache, v_cache)


def _post_kernel(n_alias, x_ref, ya_ref, yb_ref, wa_ref, wb_ref, g2_ref, wr_ref, br_ref, cnt0_ref, *rest):
    x1_ref, h2_ref, gate_ref, meta_ref, tb_ref, tl_ref, cnt_ref, carry_ref = rest[n_alias:]
    i = pl.program_id(0)

    @pl.when(i == 0)
    def _():
        carry_ref[...] = cnt0_ref[...]

    mix = _mm(ya_ref[...].astype(BF16), wa_ref[...]) + _mm(yb_ref[...].astype(BF16), wb_ref[...])
    x1 = x_ref[...] + mix
    h2 = x1 * lax.rsqrt(jnp.mean(x1 * x1, axis=-1, keepdims=True) + NORM_EPS) * g2_ref[...]
    x1_ref[...] = x1
    h2_ref[...] = h2

    l = _dot1(h2, wr_ref[...]) + br_ref[...]
    tm = l.shape[0]
    lane = _iota(l.shape, 1)
    vals, idxs = [], []
    for _ in range(TOP_K):
        m = jnp.max(l, axis=-1, keepdims=True)
        sel = jnp.min(jnp.where(l == m, lane, LANES), axis=-1, keepdims=True)
        vals.append(m)
        idxs.append(sel)
        l = jnp.where(lane == sel, -jnp.inf, l)
    es = [jnp.exp(v - vals[0]) for v in vals]
    tot = es[0] + es[1] + es[2] + es[3]
    onehot = jnp.zeros(l.shape, F32)
    for sel in idxs:
        onehot = onehot + (lane == sel).astype(F32)
    ri = _iota((tm, tm), 0)
    ci = _iota((tm, tm), 1)
    strict = ((ci < ri) & (ci // ROW_TILE == ri // ROW_TILE)).astype(BF16)
    before = _mm(strict, onehot.astype(BF16))
    for k in range(TOP_K):
        gate_ref[:, k:k + 1] = es[k] / tot
        meta_ref[:, k:k + 1] = idxs[k]
        meta_ref[:, TOP_K + k:TOP_K + k + 1] = jnp.sum(
            jnp.where(lane == idxs[k], before, 0.0), axis=-1, keepdims=True).astype(jnp.int32)
    carry = carry_ref[...]
    for s in range(tm // ROW_TILE):
        cnt_s = jnp.sum(onehot[s * ROW_TILE:(s + 1) * ROW_TILE], axis=0, keepdims=True)
        tb_ref[s] = carry.astype(jnp.int32)
        tl_ref[s] = cnt_s.astype(jnp.int32)
        carry = carry + cnt_s
    carry_ref[...] = carry
    cnt_ref[...] = carry


def _post(x, ya, yb, wp, cnt0, bufs, total_rows, row_off, tm):
    rows, d = x.shape
    half = ya.shape[1]
    ob = row_off // tm
    sub = tm // ROW_TILE
    n_sub = total_rows // ROW_TILE
    full = lambda i: (0, 0)
    row = lambda i: (i, 0)
    orow = lambda i: (ob + i, 0)
    srow = lambda i: (ob + i, 0, 0)
    n_fixed = 9
    return pl.pallas_call(
        functools.partial(_post_kernel, len(bufs)),
        grid=(rows // tm,),
        in_specs=[pl.BlockSpec((tm, d), row), pl.BlockSpec((tm, half), row), pl.BlockSpec((tm, half), row),
                  pl.BlockSpec((half, d), full), pl.BlockSpec((half, d), full),
                  pl.BlockSpec((1, d), full), pl.BlockSpec((d, LANES), full), pl.BlockSpec((1, LANES), full),
                  pl.BlockSpec((1, LANES), full)] + [pl.BlockSpec(memory_space=pl.ANY)] * len(bufs),
        out_specs=[pl.BlockSpec((tm, d), orow), pl.BlockSpec((tm, d), orow),
                   pl.BlockSpec((tm, TOP_K), orow), pl.BlockSpec((tm, 2 * TOP_K), orow),
                   pl.BlockSpec((sub, 1, LANES), srow), pl.BlockSpec((sub, 1, LANES), srow),
                   pl.BlockSpec((1, LANES), full)],
        out_shape=[jax.ShapeDtypeStruct((total_rows, d), F32), jax.ShapeDtypeStruct((total_rows, d), F32),
                   jax.ShapeDtypeStruct((total_rows, TOP_K), F32),
                   jax.ShapeDtypeStruct((total_rows, 2 * TOP_K), jnp.int32),
                   jax.ShapeDtypeStruct((n_sub, 1, LANES), jnp.int32),
                   jax.ShapeDtypeStruct((n_sub, 1, LANES), jnp.int32),
                   jax.ShapeDtypeStruct((1, LANES), F32)],
        input_output_aliases={n_fixed + j: j for j in range(len(bufs))},
        scratch_shapes=[pltpu.VMEM((1, LANES), F32)],
        compiler_params=_cparams(("arbitrary",)),
        name="post",
    )(x, ya, yb, wp["wa"], wp["wb"], wp["g2"], wp["wr"], wp["br"], cnt0, *bufs)


def _n_windows(base, length):
    off = base & (SUBLANES - 1)
    n = lax.shift_right_logical(off + length + (MOE_WIN - 1), MOE_WIN_SHIFT)
    return off, jnp.where(length > 0, n, 0)


def _window_targets(meta, tb_vec, tl_vec):
    tm = meta.shape[0]
    off, n_win = _n_windows(tb_vec, tl_vec)
    upper = (_iota((LANES, LANES), 0) < _iota((LANES, LANES), 1)).astype(BF16)
    slot_start = _mm(n_win.astype(F32).astype(BF16), upper)
    pos0 = slot_start * MOE_WIN + off.astype(F32)
    lane = _iota((tm, LANES), 1)
    tgts = []
    for k in range(TOP_K):
        p0 = jnp.sum(jnp.where(lane == meta[:, k:k + 1], pos0, 0.0), axis=-1, keepdims=True)
        tgts.append(p0.astype(jnp.int32) + meta[:, TOP_K + k:TOP_K + k + 1])
    return tgts


def _for_each_window(n_e, pstart_ref, tb_ref, tl_ref, fn, per_expert_fn=None):
    def per_expert(e, slot0):
        base = tb_ref[0, 0, e]
        length = tl_ref[0, 0, e]
        off, n = _n_windows(base, length)
        row0 = pstart_ref[e] + base - off
        if per_expert_fn is not None:
            per_expert_fn(e, slot0, off, length, n)

        def per_window(w, c):
            fn(slot0 + w, pl.multiple_of(row0 + w * MOE_WIN, SUBLANES))
            return c

        lax.fori_loop(0, n, per_window, 0)
        return slot0 + n

    lax.fori_loop(0, n_e, per_expert, 0)


def _scatter_kernel(n_e, pstart_ref, z1_ref, z2_ref, tb_ref, tl_ref, tbp_ref, tlp_ref, tbv_ref, tlv_ref,
                    meta_ref, h_ref, xs_ref, xw_ref, zero_ref, carry_ref, sem, zsem):
    i = pl.program_id(0)
    n_i = pl.num_programs(0)
    tm = h_ref.shape[0]
    bm = zero_ref.shape[0]
    buf = i % 2

    @pl.when(i == 0)
    def _():
        zero_ref[...] = jnp.zeros_like(zero_ref)

        def zcopy(row):
            return pltpu.make_async_copy(zero_ref, xs_ref.at[pl.ds(pl.multiple_of(row, 8), bm)], zsem)

        for e in range(n_e):
            zcopy(z1_ref[e]).start()

            @pl.when(z2_ref[e] != z1_ref[e])
            def _():
                zcopy(z2_ref[e]).start()
        for e in range(n_e):
            zcopy(z1_ref[e]).wait()

            @pl.when(z2_ref[e] != z1_ref[e])
            def _():
                zcopy(z2_ref[e]).wait()

        carry_ref[...] = jnp.zeros_like(carry_ref)

    tgts = _window_targets(meta_ref[...], tbv_ref[0], tlv_ref[0])
    lane_s = _iota((tm, MOE_SLOTS * MOE_WIN), 1)
    sel = jnp.zeros((tm, MOE_SLOTS * MOE_WIN), F32)
    for tgt in tgts:
        sel = sel + (lane_s == tgt).astype(F32)
    xw_ref[buf] = _mm(sel.T.astype(BF16), h_ref[...].astype(BF16))

    def splice_carry(e, slot0, off, length, n):
        @pl.when(n > 0)
        def _():
            g0 = pl.multiple_of(slot0 * MOE_WIN, MOE_WIN)
            xw_ref[buf, pl.ds(g0, SUBLANES), :] = xw_ref[buf, pl.ds(g0, SUBLANES), :] + carry_ref[e]
            filled = off + length
            gl = pl.multiple_of(g0 + lax.shift_right_logical(filled, 3) * SUBLANES, SUBLANES)
            last = xw_ref[buf, pl.ds(gl, SUBLANES), :]
            carry_ref[e] = jnp.where((filled & (SUBLANES - 1)) != 0, last, 0.0)

    def copy(b, slot, row):
        return pltpu.make_async_copy(xw_ref.at[b, pl.ds(pl.multiple_of(slot * MOE_WIN, MOE_WIN), MOE_WIN)],
                                     xs_ref.at[pl.ds(row, MOE_WIN)], sem.at[b])

    @pl.when(i > 0)
    def _():
        _for_each_window(n_e, pstart_ref, tbp_ref, tlp_ref, lambda slot, row: copy(1 - buf, slot, row).wait())

    _for_each_window(n_e, pstart_ref, tb_ref, tl_ref, lambda slot, row: copy(buf, slot, row).start(),
                     splice_carry)

    @pl.when(i == n_i - 1)
    def _():
        _for_each_window(n_e, pstart_ref, tb_ref, tl_ref, lambda slot, row: copy(buf, slot, row).wait())


def _scatter(pstart, z1, z2, tbase, tlen, meta, h2, n_rows_sorted, bm, tm):
    rows, d = h2.shape
    n_e = pstart.shape[0]
    smem = pl.BlockSpec(memory_space=pltpu.SMEM)
    tile3 = lambda i: (i, 0, 0)
    prev3 = lambda i: (jnp.maximum(i - 1, 0), 0, 0)
    tile_smem = lambda im: pl.BlockSpec((1, 1, LANES), im, memory_space=pltpu.SMEM)
    return pl.pallas_call(
        functools.partial(_scatter_kernel, n_e),
        grid=(rows // tm,),
        in_specs=[smem, smem, smem,
                  tile_smem(tile3), tile_smem(tile3), tile_smem(prev3), tile_smem(prev3),
                  pl.BlockSpec((1, 1, LANES), tile3), pl.BlockSpec((1, 1, LANES), tile3),
                  pl.BlockSpec((tm, 2 * TOP_K), lambda i: (i, 0)),
                  pl.BlockSpec((tm, d), lambda i: (i, 0))],
        out_specs=pl.BlockSpec(memory_space=pl.ANY),
        out_shape=jax.ShapeDtypeStruct((n_rows_sorted, d), F32),
        scratch_shapes=[pltpu.VMEM((2, MOE_SLOTS * MOE_WIN, d), F32), pltpu.VMEM((bm, d), F32),
                        pltpu.VMEM((n_e, SUBLANES, d), F32),
                        pltpu.SemaphoreType.DMA((2,)), pltpu.SemaphoreType.DMA(())],
        compiler_params=_cparams(("arbitrary",)),
        name="moe_scatter",
    )(pstart, z1, z2, tbase, tlen, tbase, tlen, tbase, tlen, meta, h2)


def _expert_kernel(d_ff, be_ref, nused_ref, xs_ref, w1_ref, b1_ref, w2_ref, b2_ref, ys_ref, w1b_ref, w2b_ref):
    i = pl.program_id(0)
    new_expert = jnp.logical_or(i == 0, be_ref[i] != be_ref[jnp.maximum(i - 1, 0)])

    @pl.when(jnp.logical_and(i < nused_ref[0], new_expert))
    def _():
        w1b_ref[...] = w1_ref[0].astype(BF16)
        w2b_ref[...] = w2_ref[0].astype(BF16)

    @pl.when(i < nused_ref[0])
    def _():
        x = xs_ref[...].astype(BF16)
        h = _mm(x, w1b_ref[...]) + b1_ref[0]
        hg = jnp.minimum(h[:, :d_ff], SWIGLU_LIMIT)
        hu = jnp.clip(h[:, d_ff:], -SWIGLU_LIMIT, SWIGLU_LIMIT)
        act = hg * _sigmoid(SWIGLU_ALPHA * hg) * (hu + 1.0)
        ys_ref[...] = _mm(act.astype(BF16), w2b_ref[...]) + b2_ref[0]

    @pl.when(i >= nused_ref[0])
    def _():
        ys_ref[...] = jnp.zeros_like(ys_ref)


def _experts(block_e, n_used, xs, w1, b1, w2, b2, bm):
    R, d = xs.shape
    d_ff = w2.shape[1]
    nb = R // bm

    def rows(i, be, nu):
        return (jnp.minimum(i, nu[0] - 1), 0)

    def wsel(i, be, nu):
        return (be[i], 0, 0)

    return pl.pallas_call(
        functools.partial(_expert_kernel, d_ff),
        grid_spec=pltpu.PrefetchScalarGridSpec(
            num_scalar_prefetch=2,
            grid=(nb,),
            in_specs=[pl.BlockSpec((bm, d), rows),
                      pl.BlockSpec((1, d, 2 * d_ff), wsel), pl.BlockSpec((1, 1, 2 * d_ff), wsel),
                      pl.BlockSpec((1, d_ff, d), wsel), pl.BlockSpec((1, 1, d), wsel)],
            out_specs=pl.BlockSpec((bm, d), lambda i, be, nu: (i, 0)),
            scratch_shapes=[pltpu.VMEM((d, 2 * d_ff), BF16), pltpu.VMEM((d_ff, d), BF16)]),
        out_shape=jax.ShapeDtypeStruct((R, d), F32),
        compiler_params=_cparams(("arbitrary",)),
        name="moe_experts",
    )(block_e, n_used, xs, w1, b1, w2, b2)


def _combine_kernel(n_p, n_e, pstart_ref, tb_ref, tl_ref, tbn_ref, tln_ref, tbv_ref, tlv_ref, meta_ref, gate_ref,
                    x1_ref, gf_ref, ys_ref, op_ref, os_ref, win_ref, sem):
    i = pl.program_id(0)
    n = pl.num_programs(0)
    tm = x1_ref.shape[0]
    buf = i % 2

    def copy(b, slot, row):
        return pltpu.make_async_copy(
            ys_ref.at[pl.ds(row, MOE_WIN)],
            win_ref.at[b, pl.ds(pl.multiple_of(slot * MOE_WIN, MOE_WIN), MOE_WIN)], sem.at[b])

    @pl.when(i == 0)
    def _():
        win_ref[...] = jnp.zeros_like(win_ref)
        _for_each_window(n_e, pstart_ref, tb_ref, tl_ref, lambda slot, row: copy(buf, slot, row).start())

    @pl.when(i + 1 < n)
    def _():
        _for_each_window(n_e, pstart_ref, tbn_ref, tln_ref, lambda slot, row: copy(1 - buf, slot, row).start())

    _for_each_window(n_e, pstart_ref, tb_ref, tl_ref, lambda slot, row: copy(buf, slot, row).wait())

    tgts = _window_targets(meta_ref[...], tbv_ref[0], tlv_ref[0])
    gate = gate_ref[...]
    lane_s = _iota((tm, MOE_SLOTS * MOE_WIN), 1)
    sel = jnp.zeros((tm, MOE_SLOTS * MOE_WIN), F32)
    for k, tgt in enumerate(tgts):
        sel = sel + jnp.where(lane_s == tgt, gate[:, k:k + 1], 0.0)
    sel_hi, sel_lo = _split2(sel)
    wb = win_ref[buf].astype(BF16)
    y = x1_ref[...] + (_mm(sel_hi, wb) + _mm(sel_lo, wb))
    out = y * lax.rsqrt(jnp.mean(y * y, axis=-1, keepdims=True) + NORM_EPS) * gf_ref[...]

    @pl.when(i < n_p)
    def _():
        op_ref[...] = out

    @pl.when(i >= n_p)
    def _():
        os_ref[...] = out


def _combine(pstart, tbase, tlen, meta, gate, x1, gf, ys, n_prompt_rows, tm):
    rows, d = x1.shape
    n = rows // tm
    n_p = n_prompt_rows // tm
    n_e = pstart.shape[0]
    cur3 = lambda i: (i, 0, 0)
    nxt3 = lambda i: (jnp.minimum(i + 1, n - 1), 0, 0)
    tile_smem = lambda im: pl.BlockSpec((1, 1, LANES), im, memory_space=pltpu.SMEM)
    return pl.pallas_call(
        functools.partial(_combine_kernel, n_p, n_e),
        grid=(n,),
        in_specs=[pl.BlockSpec(memory_space=pltpu.SMEM),
                  tile_smem(cur3), tile_smem(cur3), tile_smem(nxt3), tile_smem(nxt3),
                  pl.BlockSpec((1, 1, LANES), cur3), pl.BlockSpec((1, 1, LANES), cur3),
                  pl.BlockSpec((tm, 2 * TOP_K), lambda i: (i, 0)),
                  pl.BlockSpec((tm, TOP_K), lambda i: (i, 0)),
                  pl.BlockSpec((tm, d), lambda i: (i, 0)),
                  pl.BlockSpec((1, d), lambda i: (0, 0)),
                  pl.BlockSpec(memory_space=pl.ANY)],
        out_specs=[pl.BlockSpec((tm, d), lambda i: (jnp.minimum(i, n_p - 1), 0)),
                   pl.BlockSpec((tm, d), lambda i: (jnp.maximum(i - n_p, 0), 0))],
        out_shape=[jax.ShapeDtypeStruct((n_prompt_rows, d), F32),
                   jax.ShapeDtypeStruct((rows - n_prompt_rows, d), F32)],
        scratch_shapes=[pltpu.VMEM((2, MOE_SLOTS * MOE_WIN, d), F32), pltpu.SemaphoreType.DMA((2,))],
        compiler_params=_cparams(("arbitrary",)),
        name="moe_combine",
    )(pstart, tbase, tlen, tbase, tlen, tbase, tlen, meta, gate, x1, gf, ys)


def _rope_tables(pos):
    half = ROT_DIM // 2
    inv = ROPE_THETA ** (-jnp.arange(0, ROT_DIM, 2, dtype=F32) / ROT_DIM)
    ang = inv[:, None] * pos.astype(F32)[None, :]
    cos, sin = jnp.cos(ang), jnp.sin(ang)
    n = pos.shape[0]
    pad1 = jnp.ones((HEAD_DIM - ROT_DIM, n), F32)
    pad0 = jnp.zeros((HEAD_DIM - ROT_DIM, n), F32)
    cos_h = jnp.concatenate([cos, cos, pad1], axis=0)
    sin_h = jnp.concatenate([-sin, sin, pad0], axis=0)
    reps = (LANES // HEAD_DIM, 1)
    return jnp.tile(cos_h, reps).T, jnp.tile(sin_h, reps).T


def _pairs_from_state(S):
    H = S.shape[0]
    St = jnp.swapaxes(S, 1, 2).reshape(H // 2, 2, HEAD_DIM, HEAD_DIM)
    z = jnp.zeros_like(St[:, 0])
    top = jnp.concatenate([St[:, 0], z], axis=2)
    bot = jnp.concatenate([z, St[:, 1]], axis=2)
    return jnp.concatenate([top, bot], axis=1)


def _state_from_pairs(Sp):
    a = Sp[:, :HEAD_DIM, :HEAD_DIM]
    b = Sp[:, HEAD_DIM:, HEAD_DIM:]
    St = jnp.stack([a, b], axis=1).reshape(-1, HEAD_DIM, HEAD_DIM)
    return jnp.swapaxes(St, 1, 2)


def kernel(x_prompt, x_sample, state_rwkv_wkv, state_rwkv_shift, cache_swa_k, cache_swa_v, norm1_g, w_in, mu_shift, decay_w0, decay_w2, aaa_a0, aaa_w2, gate_w2, k_k, k_a, r_k, lnx_g, lnx_b, attn_sinks, w_out, norm2_g, w_router, b_router, w_mlp1, b_mlp1, w_mlp2, b_mlp2, norm_f_g):
    depth = w_in.shape[0]
    assert depth == 1 and x_prompt.shape[0] == 1 and x_sample.shape[1] == 1
    T, d = x_prompt.shape[1], x_prompt.shape[2]
    B = x_sample.shape[0]
    past_len = PAST_LEN
    H = state_rwkv_wkv.shape[2]
    rw_w = H * HEAD_DIM
    n_pairs = H // 2
    n_q = attn_sinks.shape[1]
    n_kv = cache_swa_k.shape[3]
    q_cols = n_q * HEAD_DIM
    kv_cols = n_kv * HEAD_DIM
    rw_cols = state_rwkv_shift.shape[2]
    assert rw_cols == 3 * rw_w + 2 * HEAD_DIM + PAIR and kv_cols == LANES
    assert T % RW_TILE == 0 and B % ROW_TILE == 0 and B % 8 == 0
    wlen = cache_swa_k.shape[2]
    l = 0

    w_in_bf = w_in[l].astype(BF16)
    zero_half = jnp.zeros((HEAD_DIM, rw_w), F32)
    pp = dict(mu=mu_shift[l][None], w0=decay_w0[l][None],
              dw2=jnp.concatenate([decay_w2[l], zero_half], axis=0),
              a0=aaa_a0[l][None], aw2=jnp.concatenate([zero_half, aaa_w2[l]], axis=0),
              gw2=gate_w2[l], kk=k_k[l][None], ka=k_a[l][None], rk=r_k[l].reshape(1, rw_w),
              lng=lnx_g[l][None], lnb=lnx_b[l][None])
    w_out_bf = w_out[l].astype(BF16)
    n_e = w_router.shape[2]
    wr = jnp.pad(w_router[l], ((0, 0), (0, LANES - n_e)))
    br = jnp.concatenate([b_router[l], jnp.full((LANES - n_e,), NEG_BIG, F32)])[None]
    wp = dict(wa=w_out_bf[:rw_w], wb=w_out_bf[rw_w:], g2=norm2_g[l][None], wr=wr, br=br)
    g1 = norm1_g[l][None]

    xp = x_prompt[0]
    cos_p, sin_p = _rope_tables(jnp.arange(T))
    prw_p, q_p, k_p, v_p = _inproj(xp, g1, w_in_bf, cos_p, sin_p, 512, rw_cols, q_cols, kv_cols)
    s0_p = jnp.zeros((n_pairs, PAIR, PAIR), F32)
    shift0_p = jnp.zeros((1, rw_cols), F32)
    ya_p, sfin_p = _rwkv_prompt(prw_p, shift0_p, s0_p, pp, RW_TILE)
    yb_p = _attn_prompt(q_p, k_p, v_p, attn_sinks[l], n_q, n_kv)

    xs_ = x_sample[:, 0]
    cos_s, sin_s = _rope_tables(jnp.full((B,), past_len))
    prw_s, q_s, k_s, v_s = _inproj(xs_, g1, w_in_bf, cos_s, sin_s, ROW_TILE, rw_cols, q_cols, kv_cols)
    s_flat = state_rwkv_wkv[l].reshape(B, H * HEAD_DIM * HEAD_DIM)
    ya_s, snew_flat = _rwkv_step(prw_s, state_rwkv_shift[l], s_flat, pp, n_pairs)
    o2, kc_new, vc_new = _attn_step(q_s.reshape(B * n_q, HEAD_DIM), k_s, v_s,
                                    cache_swa_k[l].reshape(B, wlen, kv_cols),
                                    cache_swa_v[l].reshape(B, wlen, kv_cols),
                                    attn_sinks[l][:, None], n_q, n_kv, past_len)
    yb_s = o2.reshape(B, q_cols)

    rows = T + B
    *bufs, cnt = _post(xp, ya_p, yb_p, wp, jnp.zeros((1, LANES), F32), (), rows, 0, 256)
    x1, h2, gate, meta, tbase, tlen, cnt = _post(xs_, ya_s, yb_s, wp, cnt, bufs, rows, T, ROW_TILE)

    counts = cnt[0, :n_e].astype(jnp.int32)
    padded = (counts + MOE_WIN + MOE_BM - 1) // MOE_BM * MOE_BM
    pend = jnp.cumsum(padded)
    pstart = (pend - padded).astype(jnp.int32)
    n_blocks = -(-(rows * TOP_K) // MOE_BM) + n_e + -(-(n_e * MOE_WIN) // MOE_BM)
    block_start = jnp.arange(n_blocks, dtype=jnp.int32) * MOE_BM
    block_e = jnp.minimum(jnp.sum((pend[None, :] <= block_start[:, None]).astype(jnp.int32), axis=1),
                          n_e - 1).astype(jnp.int32)
    n_used = (pend[-1] // MOE_BM).astype(jnp.int32)[None]
    z1 = (pstart + counts // MOE_BM * MOE_BM).astype(jnp.int32)
    z2 = (pend - MOE_BM).astype(jnp.int32)

    xs_sorted = _scatter(pstart, z1, z2, tbase, tlen, meta, h2, n_blocks * MOE_BM, MOE_BM, ROW_TILE)
    ys_sorted = _experts(block_e, n_used, xs_sorted, w_mlp1[l], b_mlp1[l][:, None], w_mlp2[l],
                         b_mlp2[l][:, None], MOE_BM)
    y_p, y_s = _combine(pstart, tbase, tlen, meta, gate, x1, norm_f_g[None], ys_sorted, T, ROW_TILE)

    sdt = state_rwkv_wkv.dtype
    return (y_p[None], y_s[:, None],
            _state_from_pairs(sfin_p)[None, None].astype(sdt), prw_p[T - 1][None, None],
            k_p[T - min(WINDOW, T):].reshape(1, 1, -1, n_kv, HEAD_DIM),
            v_p[T - min(WINDOW, T):].reshape(1, 1, -1, n_kv, HEAD_DIM),
            snew_flat.reshape(1, B, H, HEAD_DIM, HEAD_DIM).astype(sdt), prw_s[None],
            kc_new.reshape(1, B, wlen, n_kv, HEAD_DIM), vc_new.reshape(1, B, wlen, n_kv, HEAD_DIM))
```

```python
import functools

import jax
import jax.numpy as jnp
from jax import lax
from jax.experimental import pallas as pl
from jax.experimental.pallas import tpu as pltpu

F32 = jnp.float32
BF16 = jnp.bfloat16

LANES = 128
HEAD_DIM = 64
PAIR = 2 * HEAD_DIM
CHUNK = 64
RW_TILE = 256
RW_PAIRS_PER_STEP = 4
ROT_DIM = 16
ROPE_THETA = 500000.0
WINDOW = 128
PAST_LEN = 16384
ATT_BLOCK = 128
N_EXPERTS = 32
TOP_K = 4
SWIGLU_ALPHA = 1.702
SWIGLU_LIMIT = 7.0
NORM_EPS = 1e-5
LNX_EPS = HEAD_DIM * 1e-5
MOE_BM = 512
ROW_TILE = 128
MOE_WIN_SHIFT = 5
MOE_WIN = 1 << MOE_WIN_SHIFT
SORT_ALIGN_SHIFT = 4
SORT_ALIGN = 1 << SORT_ALIGN_SHIFT
MOE_TILE = 256
NEG_BIG = -1e30
VMEM_LIMIT = 52 * 1024 * 1024

NN = (((1,), (0,)), ((), ()))
NT = (((1,), (1,)), ((), ()))


def _mm(a, b, dn=NN):
    return lax.dot_general(a, b, dn, preferred_element_type=F32)


def _split2(a):
    hi = a.astype(BF16)
    lo = (a - hi.astype(F32)).astype(BF16)
    return hi, lo


def _split3(a):
    hi = a.astype(BF16)
    r1 = a - hi.astype(F32)
    mid = r1.astype(BF16)
    lo = (r1 - mid.astype(F32)).astype(BF16)
    return hi, mid, lo


def _dot1(a, b, dn=NN):
    return _mm(a.astype(BF16), b.astype(BF16), dn)


def _dot_sel_l(sel, b, dn=NN):
    b0, b1, b2 = _split3(b)
    return _mm(sel, b0, dn) + (_mm(sel, b1, dn) + _mm(sel, b2, dn))


def _dot_sel_r(a, sel, dn=NN):
    a0, a1, a2 = _split3(a)
    return _mm(a0, sel, dn) + (_mm(a1, sel, dn) + _mm(a2, sel, dn))


def _iota(shape, dim):
    return lax.broadcasted_iota(jnp.int32, shape, dim)


def _seg_matrix():
    return ((_iota((PAIR, PAIR), 0) // HEAD_DIM) == (_iota((PAIR, PAIR), 1) // HEAD_DIM)).astype(BF16)


def _sigmoid(x):
    return 1.0 / (1.0 + jnp.exp(-x))


def _cparams(sem, vmem=VMEM_LIMIT):
    return pltpu.CompilerParams(dimension_semantics=sem, vmem_limit_bytes=vmem)


def _rope_slab(x, cos, sin_signed):
    lane = _iota(x.shape, 1) % HEAD_DIM
    up = pltpu.roll(x, LANES - ROT_DIM // 2, axis=1)
    down = pltpu.roll(x, ROT_DIM // 2, axis=1)
    partner = jnp.where(lane < ROT_DIM // 2, up, down)
    return x * cos + partner * sin_signed


def _inproj_kernel(rw_cols, q_cols, kv_cols, x_ref, g_ref, w_ref, cos_ref, sin_ref,
                   prw_ref, q_ref, k_ref, v_ref):
    x = x_ref[...]
    h = x * lax.rsqrt(jnp.mean(x * x, axis=-1, keepdims=True) + NORM_EPS) * g_ref[...]
    proj = _mm(h.astype(BF16), w_ref[...])
    prw_ref[...] = proj[:, :rw_cols]
    cos = cos_ref[...]
    sin = sin_ref[...]
    for c in range(q_cols // LANES):
        lo = rw_cols + c * LANES
        q_ref[:, c * LANES:(c + 1) * LANES] = _rope_slab(proj[:, lo:lo + LANES], cos, sin)
    ko = rw_cols + q_cols
    for c in range(kv_cols // LANES):
        k_ref[:, c * LANES:(c + 1) * LANES] = _rope_slab(proj[:, ko + c * LANES:ko + (c + 1) * LANES], cos, sin)
    v_ref[...] = proj[:, ko + kv_cols:ko + 2 * kv_cols]


def _inproj(x, g, w_bf, cos_t, sin_t, tm, rw_cols, q_cols, kv_cols):
    rows, d = x.shape
    cols = w_bf.shape[1]
    full = lambda i: (0, 0)
    row = lambda i: (i, 0)
    return pl.pallas_call(
        functools.partial(_inproj_kernel, rw_cols, q_cols, kv_cols),
        grid=(rows // tm,),
        in_specs=[pl.BlockSpec((tm, d), row), pl.BlockSpec((1, d), full),
                  pl.BlockSpec((d, cols), full),
                  pl.BlockSpec((tm, LANES), row), pl.BlockSpec((tm, LANES), row)],
        out_specs=[pl.BlockSpec((tm, rw_cols), row), pl.BlockSpec((tm, q_cols), row),
                   pl.BlockSpec((tm, kv_cols), row), pl.BlockSpec((tm, kv_cols), row)],
        out_shape=[jax.ShapeDtypeStruct((rows, rw_cols), F32), jax.ShapeDtypeStruct((rows, q_cols), F32),
                   jax.ShapeDtypeStruct((rows, kv_cols), F32), jax.ShapeDtypeStruct((rows, kv_cols), F32)],
        compiler_params=_cparams(("parallel",)),
        name="inproj",
    )(x, g, w_bf, cos_t, sin_t)


def _rwkv_tokenwise(pr, pk, pv, plo, pg, prev_r, prev_k, prev_v, prev_lo, prev_g,
                    mu_r, mu_k, mu_v, mu_lo, mu_g, w0, dw2, a0, aw2, gw2, kkp, kap, rkp, seg):
    r = pr + (prev_r - pr) * mu_r
    k = pk + (prev_k - pk) * mu_k
    v = pv + (prev_v - pv) * mu_v
    lo = plo + (prev_lo - plo) * mu_lo
    gd = pg + (prev_g - pg) * mu_g
    z = -(w0 + _dot1(jnp.tanh(lo), dw2))
    softplus = jnp.maximum(z, 0.0) + jnp.log(1.0 + jnp.exp(-jnp.abs(z)))
    logw = -jnp.exp(-softplus - 0.5)
    a = _sigmoid(a0 + _dot1(lo, aw2))
    g = _dot1(_sigmoid(gd), gw2)
    kk = k * kkp
    nrm = jnp.sqrt(_seg_sum(kk * kk, seg))
    kk = kk / jnp.maximum(nrm, 1e-12)
    k2 = k * (1.0 + (a - 1.0) * kap)
    bonus = _seg_sum(r * k2 * rkp, seg) * v
    return r, k2, v, logw, -kk, kk * a, g, bonus


def _seg_sum(x, seg):
    xh, xl = _split2(x)
    return _mm(xh, seg) + _mm(xl, seg)


def _rwkv_finish(y, bonus, g, lng, lnb, seg):
    mu = _seg_sum(y, seg) * (1.0 / HEAD_DIM)
    d = y - mu
    var = _seg_sum(d * d, seg) * (1.0 / HEAD_DIM)
    yn = d * lax.rsqrt(var + LNX_EPS) * lng + lnb
    return (yn + bonus) * g


def _rwkv_prompt_kernel(pps, pr_ref, pk_ref, pv_ref, plo_ref, pg_ref,
                        hr_ref, hk_ref, hv_ref, hlo_ref, hg_ref,
                        s0r_ref, s0k_ref, s0v_ref, s0lo_ref, s0g_ref,
                        mur_ref, muk_ref, muv_ref, mulo_ref, mug_ref,
                        w0_ref, dw2_ref, a0_ref, aw2_ref, gw2_ref, kk_ref, ka_ref, rk_ref,
                        lng_ref, lnb_ref, sin_ref,
                        y_ref, sout_ref, st_ref):
    i = pl.program_id(1)
    n_i = pl.num_programs(1)
    tt = pr_ref.shape[0]

    @pl.when(i == 0)
    def _():
        st_ref[...] = sin_ref[...]

    row = _iota((tt, PAIR), 0)

    def prev_of(cur, halo_row, s0_row):
        first = jnp.where(i == 0, s0_row, halo_row)
        return jnp.where(row == 0, first, pltpu.roll(cur, 1, axis=0))

    plo = plo_ref[...]
    pg = pg_ref[...]
    prev_lo = prev_of(plo, hlo_ref[7:8, :], s0lo_ref[...])
    prev_g = prev_of(pg, hg_ref[7:8, :], s0g_ref[...])
    ti = _iota((tt, tt), 0)
    tj = _iota((tt, tt), 1)
    same_chunk = (ti // CHUNK) == (tj // CHUNK)
    incl = same_chunk & (tj <= ti)
    strict = same_chunk & (tj < ti)
    seg = _seg_matrix()
    lane = _iota((tt, PAIR), 1)
    eye = (ti == tj).astype(F32)
    pairs = []
    for p in range(pps):
        ls = slice(p * PAIR, (p + 1) * PAIR)
        pr, pk, pv = pr_ref[:, ls], pk_ref[:, ls], pv_ref[:, ls]
        r, k2, v, logw, nkk, b, g, bonus = _rwkv_tokenwise(
            pr, pk, pv, plo, pg,
            prev_of(pr, hr_ref[7:8, ls], s0r_ref[:, ls]), prev_of(pk, hk_ref[7:8, ls], s0k_ref[:, ls]),
            prev_of(pv, hv_ref[7:8, ls], s0v_ref[:, ls]), prev_lo, prev_g,
            mur_ref[:, ls], muk_ref[:, ls], muv_ref[:, ls], mulo_ref[...], mug_ref[...],
            w0_ref[:, ls], dw2_ref[:, ls], a0_ref[:, ls], aw2_ref[:, ls], gw2_ref[:, ls],
            kk_ref[:, ls], ka_ref[:, ls], rk_ref[:, ls], seg)
        pairs.append(dict(ls=ls, r=r, k2=k2, v=v, logw=logw, nkk=nkk, b=b, g=g, bonus=bonus))

    incl_b = incl.astype(BF16)
    for q in pairs:
        q["cs"] = _dot_sel_l(incl_b, q["logw"])
    for q in pairs:
        cs = q["cs"]
        gam = jnp.exp(cs)
        inv = jnp.exp(-cs)
        q["a_t"] = jnp.exp(cs - q["logw"]) * q["nkk"]
        q["r_t"] = gam * q["r"]
        q["bt_T"] = (q["b"] * inv).T
        q["kt_T"] = (q["k2"] * inv).T
        q["gam_T"] = gam.T
        q["bk_T"] = jnp.concatenate([q["bt_T"], q["kt_T"]], axis=1).astype(BF16)

    heads = []
    for q in pairs:
        for hh in range(2):
            hm = (lane // HEAD_DIM) == hh
            heads.append(dict(q=q, a=jnp.where(hm, q["a_t"], 0.0), r=jnp.where(hm, q["r_t"], 0.0),
                              v=jnp.where(hm, q["v"], 0.0)))
    for h in heads:
        h["g"] = _mm(jnp.concatenate([h["a"], h["r"]], axis=0).astype(BF16), h["q"]["bk_T"])
    for h in heads:
        gmat = h["g"]
        l_ab = jnp.where(strict, gmat[:tt, :tt], 0.0)
        h["l_ak_m_rk"] = jnp.concatenate([jnp.where(strict, gmat[:tt, tt:], 0.0),
                                          jnp.where(incl, gmat[tt:, tt:], 0.0)], axis=0).astype(BF16)
        h["m_rb"] = jnp.where(incl, gmat[tt:, :tt], 0.0).astype(BF16)
        h["tm"] = eye + l_ab
        h["lp"] = l_ab.astype(BF16)
    for h in heads:
        h["lp"] = _mm(h["lp"], h["lp"]).astype(BF16)
    for _ in range(4):
        for h in heads:
            h["both"] = _mm(jnp.concatenate([h["tm"].astype(BF16), h["lp"]], axis=0), h["lp"])
        for h in heads:
            h["tm"] = h["tm"] + h["both"][:tt]
            h["lp"] = h["both"][tt:].astype(BF16)
    for h in heads:
        h["pq"] = _mm(h["l_ak_m_rk"], h["v"].astype(BF16))
        h["tm"] = h["tm"] + _mm(h["tm"].astype(BF16), h["lp"])
    for h in heads:
        h["tx"] = _mm(h["tm"].astype(BF16),
                      jnp.concatenate([h["a"], h["pq"][:tt]], axis=1).astype(BF16))
    for h in heads:
        h["rx"] = _mm(h["m_rb"], h["tx"].astype(BF16))
    for n, q in enumerate(pairs):
        h0, h1 = heads[2 * n], heads[2 * n + 1]
        q["tatp"] = (h0["tx"] + h1["tx"]).astype(BF16)
        ryc = (h0["rx"] + h1["rx"]) + jnp.concatenate([h0["r"] + h1["r"], h0["pq"][tt:] + h1["pq"][tt:]], axis=1)
        q["ry"] = ryc[:, :PAIR]
        q["yc"] = ryc[:, PAIR:]
        q["v_b"] = q["v"].astype(BF16)
        q["bt_b"] = q["bt_T"].astype(BF16)
        q["kt_b"] = q["kt_T"].astype(BF16)
        q["s"] = st_ref[n]

    bd = seg.astype(F32)
    eye_p = (_iota((PAIR, PAIR), 0) == _iota((PAIR, PAIR), 1)).astype(F32)
    col_t = _iota((PAIR, tt), 1)
    zb = jnp.zeros((PAIR, tt), BF16)
    n_chunks = tt // CHUNK
    for c in range(n_chunks):
        cm = (col_t // CHUNK) == c
        for q in pairs:
            bt_c = jnp.where(cm, q["bt_b"], zb)
            kt_c = jnp.where(cm, q["kt_b"], zb)
            dcol = q["gam_T"][:, (c + 1) * CHUNK - 1:(c + 1) * CHUNK]
            bx = _mm(bt_c, q["tatp"])
            q["mc", c] = (dcol * (eye_p + bd * bx[:, :PAIR])).astype(BF16)
            q["nc", c] = dcol * (bd * (bx[:, PAIR:] + _mm(kt_c, q["v_b"])))
    for c in range(n_chunks):
        sl = slice(c * CHUNK, (c + 1) * CHUNK)
        for q in pairs:
            s_b = q["s"].astype(BF16)
            q["y", c] = _mm(q["ry"][sl].astype(BF16), s_b) + q["yc"][sl]
            q["s"] = _mm(q["mc", c], s_b) + q["nc", c]
    for q in pairs:
        y = jnp.concatenate([q["y", c] for c in range(n_chunks)], axis=0)
        y_ref[:, q["ls"]] = _rwkv_finish(y, q["bonus"], q["g"], lng_ref[:, q["ls"]], lnb_ref[:, q["ls"]], seg)
    for n, q in enumerate(pairs):
        st_ref[n] = q["s"]

    @pl.when(i == n_i - 1)
    def _():
        sout_ref[...] = st_ref[...]


def _rwkv_prompt(prw, shift0, s0_pairs, pp, tt):
    T = prw.shape[0]
    n_pairs = s0_pairs.shape[0]
    pps = RW_PAIRS_PER_STEP
    n_grp = n_pairs // pps
    gw = pps * PAIR
    wcols = n_pairs * PAIR
    lo_col = 3 * wcols
    g_col = lo_col + PAIR
    hb = tt // 8

    def cur(off):
        return pl.BlockSpec((tt, gw), lambda p, i: (i, off // gw + p))

    def cur_fixed(col):
        return pl.BlockSpec((tt, PAIR), lambda p, i: (i, col // PAIR))

    def halo(off):
        return pl.BlockSpec((8, gw), lambda p, i: (jnp.maximum(i * hb - 1, 0), off // gw + p))

    def halo_fixed(col):
        return pl.BlockSpec((8, PAIR), lambda p, i: (jnp.maximum(i * hb - 1, 0), col // PAIR))

    def vec(off):
        return pl.BlockSpec((1, gw), lambda p, i: (0, off // gw + p))

    def vec_fixed(col):
        return pl.BlockSpec((1, PAIR), lambda p, i: (0, col // PAIR))

    def wmat(rows):
        return pl.BlockSpec((rows, gw), lambda p, i: (0, p))

    in_specs = ([cur(0), cur(wcols), cur(2 * wcols), cur_fixed(lo_col), cur_fixed(g_col)]
                + [halo(0), halo(wcols), halo(2 * wcols), halo_fixed(lo_col), halo_fixed(g_col)]
                + [vec(0), vec(wcols), vec(2 * wcols), vec_fixed(lo_col), vec_fixed(g_col)]
                + [vec(0), vec(wcols), vec(2 * wcols), vec_fixed(lo_col), vec_fixed(g_col)]
                + [vec(0), wmat(PAIR), vec(0), wmat(PAIR), wmat(PAIR), vec(0), vec(0), vec(0), vec(0), vec(0)]
                + [pl.BlockSpec((pps, PAIR, PAIR), lambda p, i: (p, 0, 0))])
    args = ([prw] * 5 + [prw] * 5 + [shift0] * 5 + [pp["mu"]] * 5
            + [pp["w0"], pp["dw2"], pp["a0"], pp["aw2"], pp["gw2"], pp["kk"], pp["ka"], pp["rk"],
               pp["lng"], pp["lnb"], s0_pairs])
    return pl.pallas_call(
        functools.partial(_rwkv_prompt_kernel, pps),
        grid=(n_grp, T // tt),
        in_specs=in_specs,
        out_specs=[pl.BlockSpec((tt, gw), lambda p, i: (i, p)),
                   pl.BlockSpec((pps, PAIR, PAIR), lambda p, i: (p, 0, 0))],
        out_shape=[jax.ShapeDtypeStruct((T, wcols), F32),
                   jax.ShapeDtypeStruct((n_pairs, PAIR, PAIR), F32)],
        scratch_shapes=[pltpu.VMEM((pps, PAIR, PAIR), F32)],
        compiler_params=_cparams(("parallel", "arbitrary")),
        name="rwkv_prompt",
    )(*args)


def _rwkv_step_kernel(slabs_per_step, pr_ref, pk_ref, pv_ref, plo_ref, pg_ref,
                      sr_ref, sk_ref, sv_ref, slo_ref, sg_ref,
                      mur_ref, muk_ref, muv_ref, mulo_ref, mug_ref,
                      w0_ref, dw2_ref, a0_ref, aw2_ref, gw2_ref, kk_ref, ka_ref, rk_ref,
                      lng_ref, lnb_ref, s_ref,
                      y_ref, snew_ref, yacc_ref):
    j = pl.program_id(1)
    n_j = pl.num_programs(1)
    seg = _seg_matrix()
    r, k2, v, logw, nkk, b, g, bonus = _rwkv_tokenwise(
        pr_ref[...], pk_ref[...], pv_ref[...], plo_ref[...], pg_ref[...],
        sr_ref[...], sk_ref[...], sv_ref[...], slo_ref[...], sg_ref[...],
        mur_ref[...], muk_ref[...], muv_ref[...], mulo_ref[...], mug_ref[...],
        w0_ref[...], dw2_ref[...], a0_ref[...], aw2_ref[...], gw2_ref[...],
        kk_ref[...], ka_ref[...], rk_ref[...], seg)
    w = jnp.exp(logw)

    @pl.when(j == 0)
    def _():
        yacc_ref[...] = jnp.zeros_like(yacc_ref)

    slabs_per_head = HEAD_DIM // 2
    ci = _iota((PAIR, PAIR), 0)
    li = _iota((PAIR, PAIR), 1)
    assert slabs_per_head % slabs_per_step == 0
    yacc = yacc_ref[...]
    hh = (j * slabs_per_step) // slabs_per_head
    dup = ((ci == hh * HEAD_DIM + li % HEAD_DIM)).astype(BF16)
    nkk_d, w_d, b_d, k_d, r_d = [_dot_sel_r(x, dup) for x in (nkk, w, b, k2, r)]
    for t in range(slabs_per_step):
        slab = j * slabs_per_step + t
        i0 = 2 * (slab % slabs_per_head)
        selv = (ci == hh * HEAD_DIM + i0 + li // HEAD_DIM).astype(BF16)
        sely = ((ci % HEAD_DIM == 0) & (li == hh * HEAD_DIM + i0 + ci // HEAD_DIM)).astype(BF16)
        s = s_ref[:, t * PAIR:(t + 1) * PAIR]
        sa = _seg_sum(s * nkk_d, seg)
        s_new = s * w_d + sa * b_d + _seg_sum(v, selv) * k_d
        snew_ref[:, t * PAIR:(t + 1) * PAIR] = s_new
        yred = _seg_sum(s_new * r_d, seg)
        yacc = yacc + _seg_sum(yred, sely)
    yacc_ref[...] = yacc

    @pl.when(j == n_j - 1)
    def _():
        y_ref[...] = _rwkv_finish(yacc, bonus, g, lng_ref[...], lnb_ref[...], seg)


def _rwkv_step(prw, shift, s_flat, pp, n_pairs):
    B = prw.shape[0]
    lanes_per_pair = 2 * HEAD_DIM * HEAD_DIM
    blk = 1024
    slabs_per_step = blk // PAIR
    steps = lanes_per_pair // blk
    lo_blk = 3 * n_pairs
    g_blk = lo_blk + 1

    def cur(off):
        return pl.BlockSpec((B, PAIR), lambda p, j: (0, off + p))

    def cur_fixed(b_):
        return pl.BlockSpec((B, PAIR), lambda p, j: (0, b_))

    def vec(off):
        return pl.BlockSpec((1, PAIR), lambda p, j: (0, off + p))

    def vec_fixed(b_):
        return pl.BlockSpec((1, PAIR), lambda p, j: (0, b_))

    def wmat(rows):
        return pl.BlockSpec((rows, PAIR), lambda p, j: (0, p))

    sspec = pl.BlockSpec((B, blk), lambda p, j: (0, p * steps + j))
    in_specs = ([cur(0), cur(n_pairs), cur(2 * n_pairs), cur_fixed(lo_blk), cur_fixed(g_blk)] * 2
                + [vec(0), vec(n_pairs), vec(2 * n_pairs), vec_fixed(lo_blk), vec_fixed(g_blk)]
                + [vec(0), wmat(PAIR), vec(0), wmat(PAIR), wmat(PAIR), vec(0), vec(0), vec(0), vec(0), vec(0)]
                + [sspec])
    args = ([prw] * 5 + [shift] * 5 + [pp["mu"]] * 5
            + [pp["w0"], pp["dw2"], pp["a0"], pp["aw2"], pp["gw2"], pp["kk"], pp["ka"], pp["rk"],
               pp["lng"], pp["lnb"], s_flat])
    return pl.pallas_call(
        functools.partial(_rwkv_step_kernel, slabs_per_step),
        grid=(n_pairs, steps),
        in_specs=in_specs,
        out_specs=[pl.BlockSpec((B, PAIR), lambda p, j: (0, p)), sspec],
        out_shape=[jax.ShapeDtypeStruct((B, n_pairs * PAIR), F32),
                   jax.ShapeDtypeStruct(s_flat.shape, F32)],
        scratch_shapes=[pltpu.VMEM((B, PAIR), F32)],
        compiler_params=_cparams(("parallel", "arbitrary")),
        name="rwkv_step",
    )(*args)


def _attn_prompt_kernel(n_q, group, sink_ref, q_ref, kc_ref, kp_ref, vc_ref, vp_ref, o_ref):
    blk = q_ref.shape[0]
    q = q_ref[...] * (HEAD_DIM ** -0.5)
    kband = jnp.concatenate([kp_ref[...], kc_ref[...]], axis=0)
    vband = jnp.concatenate([vp_ref[...], vc_ref[...]], axis=0)
    i = pl.program_id(0)
    rq = _iota((blk, 2 * blk), 0)
    ck = _iota((blk, 2 * blk), 1)
    dist = rq - ck + blk
    kpos = i * blk - blk + ck
    valid = (dist >= 0) & (dist < WINDOW) & (kpos >= 0)
    for h in range(n_q):
        gk = h // group
        qh = q[:, h * HEAD_DIM:(h + 1) * HEAD_DIM]
        kh = kband[:, gk * HEAD_DIM:(gk + 1) * HEAD_DIM]
        vh = vband[:, gk * HEAD_DIM:(gk + 1) * HEAD_DIM]
        s = jnp.where(valid, _dot1(qh, kh, NT), NEG_BIG)
        sink = sink_ref[h]
        m = jnp.maximum(jnp.max(s, axis=-1, keepdims=True), sink)
        p = jnp.exp(s - m)
        denom = jnp.sum(p, axis=-1, keepdims=True) + jnp.exp(sink - m)
        o_ref[:, h * HEAD_DIM:(h + 1) * HEAD_DIM] = _dot1(p, vh) / denom


def _attn_prompt(q, k, v, sinks, n_q, n_kv):
    T, qw = q.shape
    kvw = k.shape[1]
    blk = ATT_BLOCK
    curm = lambda i: (i, 0)
    prevm = lambda i: (jnp.maximum(i - 1, 0), 0)
    return pl.pallas_call(
        functools.partial(_attn_prompt_kernel, n_q, n_q // n_kv),
        grid=(T // blk,),
        in_specs=[pl.BlockSpec(memory_space=pltpu.SMEM),
                  pl.BlockSpec((blk, qw), curm),
                  pl.BlockSpec((blk, kvw), curm), pl.BlockSpec((blk, kvw), prevm),
                  pl.BlockSpec((blk, kvw), curm), pl.BlockSpec((blk, kvw), prevm)],
        out_specs=pl.BlockSpec((blk, qw), curm),
        out_shape=jax.ShapeDtypeStruct((T, qw), F32),
        compiler_params=_cparams(("parallel",)),
        name="attn_prompt",
    )(sinks, q, k, k, v, v)


def _attn_step_kernel(n_q, group, pos0, sink_ref, q_ref, kn_ref, vn_ref, kc_ref, vc_ref,
                      o_ref, ko_ref, vo_ref):
    bb, wlen, kvw = kc_ref.shape
    lane = _iota((n_q, kvw), 1)
    rowh = _iota((n_q, kvw), 0)
    mine = (lane // HEAD_DIM) == (rowh // group)
    dupm = (_iota((HEAD_DIM, kvw), 0) == _iota((HEAD_DIM, kvw), 1) % HEAD_DIM).astype(BF16)
    fold = (_iota((kvw, HEAD_DIM), 0) % HEAD_DIM == _iota((kvw, HEAD_DIM), 1)).astype(BF16)
    kidx = _iota((n_q, wlen), 1)
    dist = wlen - kidx
    valid = (dist < WINDOW) & (pos0 - dist >= 0)
    rk = _iota((wlen, kvw), 0)
    sink = sink_ref[...]
    for t in range(bb):
        qh = q_ref[t * n_q:(t + 1) * n_q, :] * (HEAD_DIM ** -0.5)
        qm = jnp.where(mine, _dot_sel_r(qh, dupm), 0.0)
        kc = kc_ref[t]
        vc = vc_ref[t]
        kn = kn_ref[t:t + 1, :]
        vn = vn_ref[t:t + 1, :]
        s = jnp.where(valid, _dot1(qm, kc, NT), NEG_BIG)
        s_new = jnp.sum(qm * kn, axis=-1, keepdims=True)
        m = jnp.maximum(jnp.maximum(jnp.max(s, axis=-1, keepdims=True), s_new), sink)
        p = jnp.exp(s - m)
        p_new = jnp.exp(s_new - m)
        denom = jnp.sum(p, axis=-1, keepdims=True) + p_new + jnp.exp(sink - m)
        res = (_dot1(p, vc) + p_new * vn) / denom
        o_ref[t * n_q:(t + 1) * n_q, :] = _dot_sel_r(jnp.where(mine, res, 0.0), fold)
        ko_ref[t] = jnp.where(rk == wlen - 1, kn, pltpu.roll(kc, wlen - 1, axis=0))
        vo_ref[t] = jnp.where(rk == wlen - 1, vn, pltpu.roll(vc, wlen - 1, axis=0))


def _attn_step(q2, k_new, v_new, k_cache, v_cache, sinks_col, n_q, n_kv, pos0):
    B, wlen, kvw = k_cache.shape
    bb = 8
    return pl.pallas_call(
        functools.partial(_attn_step_kernel, n_q, n_q // n_kv, pos0),
        grid=(B // bb,),
        in_specs=[pl.BlockSpec((n_q, 1), lambda i: (0, 0)),
                  pl.BlockSpec((bb * n_q, HEAD_DIM), lambda i: (i, 0)),
                  pl.BlockSpec((bb, kvw), lambda i: (i, 0)), pl.BlockSpec((bb, kvw), lambda i: (i, 0)),
                  pl.BlockSpec((bb, wlen, kvw), lambda i: (i, 0, 0)),
                  pl.BlockSpec((bb, wlen, kvw), lambda i: (i, 0, 0))],
        out_specs=[pl.BlockSpec((bb * n_q, HEAD_DIM), lambda i: (i, 0)),
                   pl.BlockSpec((bb, wlen, kvw), lambda i: (i, 0, 0)),
                   pl.BlockSpec((bb, wlen, kvw), lambda i: (i, 0, 0))],
        out_shape=[jax.ShapeDtypeStruct((B * n_q, HEAD_DIM), F32),
                   jax.ShapeDtypeStruct((B, wlen, kvw), F32),
                   jax.ShapeDtypeStruct((B, wlen, kvw), F32)],
        compiler_params=_cparams(("parallel",)),
        name="attn_step",
    )(sinks_col, q2, k_new, v_new, k_cache, v_cache)


def _post_kernel(x_ref, ya_ref, yb_ref, wa_ref, wb_ref, g2_ref, wr_ref, br_ref, cnt0_ref,
                 x1_ref, h2_ref, gate_ref, meta_ref, tb_ref, tl_ref, cnt_ref, carry_ref):
    i = pl.program_id(0)

    @pl.when(i == 0)
    def _():
        carry_ref[...] = cnt0_ref[...]

    mix = _mm(ya_ref[...].astype(BF16), wa_ref[...]) + _mm(yb_ref[...].astype(BF16), wb_ref[...])
    x1 = x_ref[...] + mix
    h2 = x1 * lax.rsqrt(jnp.mean(x1 * x1, axis=-1, keepdims=True) + NORM_EPS) * g2_ref[...]
    x1_ref[...] = x1
    h2_ref[...] = h2

    l = _dot1(h2, wr_ref[...]) + br_ref[...]
    tm = l.shape[0]
    lane = _iota(l.shape, 1)
    vals, idxs = [], []
    for _ in range(TOP_K):
        m = jnp.max(l, axis=-1, keepdims=True)
        sel = jnp.min(jnp.where(l == m, lane, LANES), axis=-1, keepdims=True)
        vals.append(m)
        idxs.append(sel)
        l = jnp.where(lane == sel, -jnp.inf, l)
    es = [jnp.exp(v - vals[0]) for v in vals]
    tot = es[0] + es[1] + es[2] + es[3]
    onehot = jnp.zeros(l.shape, F32)
    for sel in idxs:
        onehot = onehot + (lane == sel).astype(F32)
    strict = (_iota((tm, tm), 1) < _iota((tm, tm), 0)).astype(BF16)
    before = _mm(strict, onehot.astype(BF16))
    for k in range(TOP_K):
        gate_ref[:, k:k + 1] = es[k] / tot
        meta_ref[:, k:k + 1] = idxs[k]
        meta_ref[:, TOP_K + k:TOP_K + k + 1] = jnp.sum(
            jnp.where(lane == idxs[k], before, 0.0), axis=-1, keepdims=True).astype(jnp.int32)
    carry = carry_ref[...]
    cnt_t = jnp.sum(onehot, axis=0, keepdims=True)
    tb_ref[0] = carry.astype(jnp.int32)
    tl_ref[0] = cnt_t.astype(jnp.int32)
    carry_ref[...] = carry + cnt_t
    cnt_ref[...] = carry + cnt_t


def _post(x, ya, yb, wp, cnt0, tm):
    rows, d = x.shape
    half = ya.shape[1]
    n_t = rows // tm
    full = lambda i: (0, 0)
    row = lambda i: (i, 0)
    trow = lambda i: (i, 0, 0)
    return pl.pallas_call(
        _post_kernel,
        grid=(n_t,),
        in_specs=[pl.BlockSpec((tm, d), row), pl.BlockSpec((tm, half), row), pl.BlockSpec((tm, half), row),
                  pl.BlockSpec((half, d), full), pl.BlockSpec((half, d), full),
                  pl.BlockSpec((1, d), full), pl.BlockSpec((d, LANES), full), pl.BlockSpec((1, LANES), full),
                  pl.BlockSpec((1, LANES), full)],
        out_specs=[pl.BlockSpec((tm, d), row), pl.BlockSpec((tm, d), row),
                   pl.BlockSpec((tm, TOP_K), row), pl.BlockSpec((tm, 2 * TOP_K), row),
                   pl.BlockSpec((1, 1, LANES), trow), pl.BlockSpec((1, 1, LANES), trow),
                   pl.BlockSpec((1, LANES), full)],
        out_shape=[jax.ShapeDtypeStruct((rows, d), F32), jax.ShapeDtypeStruct((rows, d), F32),
                   jax.ShapeDtypeStruct((rows, TOP_K), F32),
                   jax.ShapeDtypeStruct((rows, 2 * TOP_K), jnp.int32),
                   jax.ShapeDtypeStruct((n_t, 1, LANES), jnp.int32),
                   jax.ShapeDtypeStruct((n_t, 1, LANES), jnp.int32),
                   jax.ShapeDtypeStruct((1, LANES), F32)],
        scratch_shapes=[pltpu.VMEM((1, LANES), F32)],
        compiler_params=_cparams(("arbitrary",)),
        name="post",
    )(x, ya, yb, wp["wa"], wp["wb"], wp["g2"], wp["wr"], wp["br"], cnt0)


def _n_windows(base, length):
    off = base & (SORT_ALIGN - 1)
    n = lax.shift_right_logical(off + length + (MOE_WIN - 1), MOE_WIN_SHIFT)
    return off, jnp.where(length > 0, n, 0)


def _window_targets(meta, tb_vec, tl_vec):
    tm = meta.shape[0]
    off, n_win = _n_windows(tb_vec, tl_vec)
    upper = (_iota((LANES, LANES), 0) < _iota((LANES, LANES), 1)).astype(BF16)
    slot_start = _mm(n_win.astype(F32).astype(BF16), upper)
    pos0 = slot_start * MOE_WIN + off.astype(F32)
    lane = _iota((tm, LANES), 1)
    tgts = []
    for k in range(TOP_K):
        p0 = jnp.sum(jnp.where(lane == meta[:, k:k + 1], pos0, 0.0), axis=-1, keepdims=True)
        tgts.append(p0.astype(jnp.int32) + meta[:, TOP_K + k:TOP_K + k + 1])
    return tgts


def _for_each_window(n_e, pstart_ref, tb_ref, tl_ref, fn, per_expert_fn=None):
    def per_expert(e, slot0):
        base = tb_ref[0, 0, e]
        length = tl_ref[0, 0, e]
        off, n = _n_windows(base, length)
        row0 = pstart_ref[e] + base - off
        if per_expert_fn is not None:
            per_expert_fn(e, slot0, off, length, n)

        def per_window(w, c):
            fn(slot0 + w, pl.multiple_of(row0 + w * MOE_WIN, SORT_ALIGN))
            return c

        lax.fori_loop(0, n, per_window, 0)
        return slot0 + n

    lax.fori_loop(0, n_e, per_expert, 0)


def _moe_slots(tm):
    n = -(-(tm * TOP_K + N_EXPERTS * (SORT_ALIGN - 1 + MOE_WIN - 1)) // MOE_WIN)
    per_lane_tile = LANES // MOE_WIN
    return -(-n // per_lane_tile) * per_lane_tile


def _scatter_kernel(n_e, continues, pstart_ref, z1_ref, z2_ref, tb_ref, tl_ref, tbp_ref, tlp_ref, tbv_ref, tlv_ref,
                    meta_ref, h_ref, *rest):
    if continues:
        cin_ref, _xs_alias, xs_ref, cout_ref, xw_ref, zero_ref, carry_ref, sem, zsem = rest
    else:
        xs_ref, cout_ref, xw_ref, zero_ref, carry_ref, sem, zsem = rest
    i = pl.program_id(0)
    n_i = pl.num_programs(0)
    tm = h_ref.shape[0]
    bm = zero_ref.shape[0]
    buf = i % 2
    n_rows_w = xw_ref.shape[1]

    @pl.when(i == 0)
    def _():
        if continues:
            carry_ref[...] = cin_ref[...]
        else:
            zero_ref[...] = jnp.zeros_like(zero_ref)

            def zcopy(row):
                return pltpu.make_async_copy(zero_ref, xs_ref.at[pl.ds(pl.multiple_of(row, SORT_ALIGN), bm)], zsem)

            for e in range(n_e):
                zcopy(z1_ref[e]).start()

                @pl.when(z2_ref[e] != z1_ref[e])
                def _():
                    zcopy(z2_ref[e]).start()
            for e in range(n_e):
                zcopy(z1_ref[e]).wait()

                @pl.when(z2_ref[e] != z1_ref[e])
                def _():
                    zcopy(z2_ref[e]).wait()

            carry_ref[...] = jnp.zeros_like(carry_ref)

    tgts = _window_targets(meta_ref[...], tbv_ref[0], tlv_ref[0])
    lane_s = _iota((tm, n_rows_w), 1)
    sel = jnp.zeros((tm, n_rows_w), F32)
    for tgt in tgts:
        sel = sel + (lane_s == tgt).astype(F32)
    xw_ref[buf] = _mm(sel.T.astype(BF16), h_ref[...].astype(BF16)).astype(BF16)

    def splice_carry(e, slot0, off, length, n):
        @pl.when(n > 0)
        def _():
            g0 = pl.multiple_of(slot0 * MOE_WIN, MOE_WIN)
            xw_ref[buf, pl.ds(g0, SORT_ALIGN), :] = xw_ref[buf, pl.ds(g0, SORT_ALIGN), :] + carry_ref[e]
            filled = off + length
            gl = pl.multiple_of(g0 + lax.shift_right_logical(filled, SORT_ALIGN_SHIFT) * SORT_ALIGN, SORT_ALIGN)
            last = xw_ref[buf, pl.ds(gl, SORT_ALIGN), :]
            carry_ref[e] = jnp.where((filled & (SORT_ALIGN - 1)) != 0, last, jnp.zeros_like(last))

    def copy(b, slot, row):
        return pltpu.make_async_copy(xw_ref.at[b, pl.ds(pl.multiple_of(slot * MOE_WIN, MOE_WIN), MOE_WIN)],
                                     xs_ref.at[pl.ds(row, MOE_WIN)], sem.at[b])

    @pl.when(i > 0)
    def _():
        _for_each_window(n_e, pstart_ref, tbp_ref, tlp_ref, lambda slot, row: copy(1 - buf, slot, row).wait())

    _for_each_window(n_e, pstart_ref, tb_ref, tl_ref, lambda slot, row: copy(buf, slot, row).start(),
                     splice_carry)

    @pl.when(i == n_i - 1)
    def _():
        _for_each_window(n_e, pstart_ref, tb_ref, tl_ref, lambda slot, row: copy(buf, slot, row).wait())
        cout_ref[...] = carry_ref[...]


def _scatter(pstart, z1, z2, tbase, tlen, meta, h2, prior, n_rows_sorted, bm, tm):
    rows, d = h2.shape
    n_e = pstart.shape[0]
    smem = pl.BlockSpec(memory_space=pltpu.SMEM)
    tile3 = lambda i: (i, 0, 0)
    prev3 = lambda i: (jnp.maximum(i - 1, 0), 0, 0)
    tile_smem = lambda im: pl.BlockSpec((1, 1, LANES), im, memory_space=pltpu.SMEM)
    carry_spec = pl.BlockSpec((n_e, SORT_ALIGN, d), lambda i: (0, 0, 0))
    in_specs = [smem, smem, smem,
                tile_smem(tile3), tile_smem(tile3), tile_smem(prev3), tile_smem(prev3),
                pl.BlockSpec((1, 1, LANES), tile3), pl.BlockSpec((1, 1, LANES), tile3),
                pl.BlockSpec((tm, 2 * TOP_K), lambda i: (i, 0)),
                pl.BlockSpec((tm, d), lambda i: (i, 0))]
    args = [pstart, z1, z2, tbase, tlen, tbase, tlen, tbase, tlen, meta, h2]
    aliases = {}
    if prior is not None:
        in_specs += [carry_spec, pl.BlockSpec(memory_space=pl.ANY)]
        aliases = {len(args) + 1: 0}
        args += list(prior)
    return pl.pallas_call(
        functools.partial(_scatter_kernel, n_e, prior is not None),
        grid=(rows // tm,),
        in_specs=in_specs,
        out_specs=[pl.BlockSpec(memory_space=pl.ANY), carry_spec],
        out_shape=[jax.ShapeDtypeStruct((n_rows_sorted, d), BF16),
                   jax.ShapeDtypeStruct((n_e, SORT_ALIGN, d), BF16)],
        input_output_aliases=aliases,
        scratch_shapes=[pltpu.VMEM((2, _moe_slots(tm) * MOE_WIN, d), BF16), pltpu.VMEM((bm, d), BF16),
                        pltpu.VMEM((n_e, SORT_ALIGN, d), BF16),
                        pltpu.SemaphoreType.DMA((2,)), pltpu.SemaphoreType.DMA(())],
        compiler_params=_cparams(("arbitrary",)),
        name="moe_scatter",
    )(*args)


def _expert_kernel(d_ff, be_ref, nused_ref, xs_ref, w1_ref, b1_ref, w2_ref, b2_ref, ys_ref, w1b_ref, w2b_ref):
    i = pl.program_id(0)
    new_expert = jnp.logical_or(i == 0, be_ref[i] != be_ref[jnp.maximum(i - 1, 0)])

    @pl.when(jnp.logical_and(i < nused_ref[0], new_expert))
    def _():
        w1b_ref[...] = w1_ref[0].astype(BF16)
        w2b_ref[...] = w2_ref[0].astype(BF16)

    @pl.when(i < nused_ref[0])
    def _():
        h = _mm(xs_ref[...], w1b_ref[...]) + b1_ref[0]
        hg = jnp.minimum(h[:, :d_ff], SWIGLU_LIMIT)
        hu = jnp.clip(h[:, d_ff:], -SWIGLU_LIMIT, SWIGLU_LIMIT)
        act = hg * _sigmoid(SWIGLU_ALPHA * hg) * (hu + 1.0)
        ys_ref[...] = (_mm(act.astype(BF16), w2b_ref[...]) + b2_ref[0]).astype(ys_ref.dtype)

    @pl.when(i >= nused_ref[0])
    def _():
        ys_ref[...] = jnp.zeros_like(ys_ref)


def _experts(block_e, n_used, xs, w1, b1, w2, b2, bm):
    R, d = xs.shape
    d_ff = w2.shape[1]
    nb = R // bm

    def rows(i, be, nu):
        return (jnp.minimum(i, nu[0] - 1), 0)

    def wsel(i, be, nu):
        return (be[i], 0, 0)

    return pl.pallas_call(
        functools.partial(_expert_kernel, d_ff),
        grid_spec=pltpu.PrefetchScalarGridSpec(
            num_scalar_prefetch=2,
            grid=(nb,),
            in_specs=[pl.BlockSpec((bm, d), rows),
                      pl.BlockSpec((1, d, 2 * d_ff), wsel), pl.BlockSpec((1, 1, 2 * d_ff), wsel),
                      pl.BlockSpec((1, d_ff, d), wsel), pl.BlockSpec((1, 1, d), wsel)],
            out_specs=pl.BlockSpec((bm, d), lambda i, be, nu: (i, 0)),
            scratch_shapes=[pltpu.VMEM((d, 2 * d_ff), BF16), pltpu.VMEM((d_ff, d), BF16)]),
        out_shape=jax.ShapeDtypeStruct((R, d), xs.dtype),
        compiler_params=_cparams(("arbitrary",)),
        name="moe_experts",
    )(block_e, n_used, xs, w1, b1, w2, b2)


def _combine_kernel(n_e, pstart_ref, tb_ref, tl_ref, tbn_ref, tln_ref, tbv_ref, tlv_ref, meta_ref, gate_ref,
                    x1_ref, gf_ref, ys_ref, o_ref, win_ref, sem):
    i = pl.program_id(0)
    n = pl.num_programs(0)
    tm = x1_ref.shape[0]
    buf = i % 2
    n_rows_w = win_ref.shape[1]

    def copy(b, slot, row):
        return pltpu.make_async_copy(
            ys_ref.at[pl.ds(row, MOE_WIN)],
            win_ref.at[b, pl.ds(pl.multiple_of(slot * MOE_WIN, MOE_WIN), MOE_WIN)], sem.at[b])

    @pl.when(i == 0)
    def _():
        win_ref[...] = jnp.zeros_like(win_ref)
        _for_each_window(n_e, pstart_ref, tb_ref, tl_ref, lambda slot, row: copy(buf, slot, row).start())

    @pl.when(i + 1 < n)
    def _():
        _for_each_window(n_e, pstart_ref, tbn_ref, tln_ref, lambda slot, row: copy(1 - buf, slot, row).start())

    _for_each_window(n_e, pstart_ref, tb_ref, tl_ref, lambda slot, row: copy(buf, slot, row).wait())

    tgts = _window_targets(meta_ref[...], tbv_ref[0], tlv_ref[0])
    gate = gate_ref[...]
    lane_s = _iota((tm, n_rows_w), 1)
    sel = jnp.zeros((tm, n_rows_w), F32)
    for k, tgt in enumerate(tgts):
        sel = sel + jnp.where(lane_s == tgt, gate[:, k:k + 1], 0.0)
    sel_hi, sel_lo = _split2(sel)
    wb = win_ref[buf]
    y = x1_ref[...] + (_mm(sel_hi, wb) + _mm(sel_lo, wb))
    o_ref[...] = y * lax.rsqrt(jnp.mean(y * y, axis=-1, keepdims=True) + NORM_EPS) * gf_ref[...]


def _combine(pstart, tbase, tlen, meta, gate, x1, gf, ys, tm):
    rows, d = x1.shape
    n = rows // tm
    n_e = pstart.shape[0]
    cur3 = lambda i: (i, 0, 0)
    nxt3 = lambda i: (jnp.minimum(i + 1, n - 1), 0, 0)
    tile_smem = lambda im: pl.BlockSpec((1, 1, LANES), im, memory_space=pltpu.SMEM)
    return pl.pallas_call(
        functools.partial(_combine_kernel, n_e),
        grid=(n,),
        in_specs=[pl.BlockSpec(memory_space=pltpu.SMEM),
                  tile_smem(cur3), tile_smem(cur3), tile_smem(nxt3), tile_smem(nxt3),
                  pl.BlockSpec((1, 1, LANES), cur3), pl.BlockSpec((1, 1, LANES), cur3),
                  pl.BlockSpec((tm, 2 * TOP_K), lambda i: (i, 0)),
                  pl.BlockSpec((tm, TOP_K), lambda i: (i, 0)),
                  pl.BlockSpec((tm, d), lambda i: (i, 0)),
                  pl.BlockSpec((1, d), lambda i: (0, 0)),
                  pl.BlockSpec(memory_space=pl.ANY)],
        out_specs=pl.BlockSpec((tm, d), lambda i: (i, 0)),
        out_shape=jax.ShapeDtypeStruct((rows, d), F32),
        scratch_shapes=[pltpu.VMEM((2, _moe_slots(tm) * MOE_WIN, d), ys.dtype), pltpu.SemaphoreType.DMA((2,))],
        compiler_params=_cparams(("arbitrary",)),
        name="moe_combine",
    )(pstart, tbase, tlen, tbase, tlen, tbase, tlen, meta, gate, x1, gf, ys)


def _rope_tables(pos):
    half = ROT_DIM // 2
    inv = ROPE_THETA ** (-jnp.arange(0, ROT_DIM, 2, dtype=F32) / ROT_DIM)
    ang = inv[:, None] * pos.astype(F32)[None, :]
    cos, sin = jnp.cos(ang), jnp.sin(ang)
    n = pos.shape[0]
    pad1 = jnp.ones((HEAD_DIM - ROT_DIM, n), F32)
    pad0 = jnp.zeros((HEAD_DIM - ROT_DIM, n), F32)
    cos_h = jnp.concatenate([cos, cos, pad1], axis=0)
    sin_h = jnp.concatenate([-sin, sin, pad0], axis=0)
    reps = (LANES // HEAD_DIM, 1)
    return jnp.tile(cos_h, reps).T, jnp.tile(sin_h, reps).T


def _pairs_from_state(S):
    H = S.shape[0]
    St = jnp.swapaxes(S, 1, 2).reshape(H // 2, 2, HEAD_DIM, HEAD_DIM)
    z = jnp.zeros_like(St[:, 0])
    top = jnp.concatenate([St[:, 0], z], axis=2)
    bot = jnp.concatenate([z, St[:, 1]], axis=2)
    return jnp.concatenate([top, bot], axis=1)


def _state_from_pairs(Sp):
    a = Sp[:, :HEAD_DIM, :HEAD_DIM]
    b = Sp[:, HEAD_DIM:, HEAD_DIM:]
    St = jnp.stack([a, b], axis=1).reshape(-1, HEAD_DIM, HEAD_DIM)
    return jnp.swapaxes(St, 1, 2)


def kernel(x_prompt, x_sample, state_rwkv_wkv, state_rwkv_shift, cache_swa_k, cache_swa_v, norm1_g, w_in, mu_shift, decay_w0, decay_w2, aaa_a0, aaa_w2, gate_w2, k_k, k_a, r_k, lnx_g, lnx_b, attn_sinks, w_out, norm2_g, w_router, b_router, w_mlp1, b_mlp1, w_mlp2, b_mlp2, norm_f_g):
    depth = w_in.shape[0]
    assert depth == 1 and x_prompt.shape[0] == 1 and x_sample.shape[1] == 1
    T, d = x_prompt.shape[1], x_prompt.shape[2]
    B = x_sample.shape[0]
    past_len = PAST_LEN
    H = state_rwkv_wkv.shape[2]
    rw_w = H * HEAD_DIM
    n_pairs = H // 2
    n_q = attn_sinks.shape[1]
    n_kv = cache_swa_k.shape[3]
    q_cols = n_q * HEAD_DIM
    kv_cols = n_kv * HEAD_DIM
    rw_cols = state_rwkv_shift.shape[2]
    assert rw_cols == 3 * rw_w + 2 * HEAD_DIM + PAIR and kv_cols == LANES
    assert T % RW_TILE == 0 and B % ROW_TILE == 0 and B % 8 == 0
    wlen = cache_swa_k.shape[2]
    l = 0

    w_in_bf = w_in[l].astype(BF16)
    zero_half = jnp.zeros((HEAD_DIM, rw_w), F32)
    pp = dict(mu=mu_shift[l][None], w0=decay_w0[l][None],
              dw2=jnp.concatenate([decay_w2[l], zero_half], axis=0),
              a0=aaa_a0[l][None], aw2=jnp.concatenate([zero_half, aaa_w2[l]], axis=0),
              gw2=gate_w2[l], kk=k_k[l][None], ka=k_a[l][None], rk=r_k[l].reshape(1, rw_w),
              lng=lnx_g[l][None], lnb=lnx_b[l][None])
    w_out_bf = w_out[l].astype(BF16)
    n_e = w_router.shape[2]
    wr = jnp.pad(w_router[l], ((0, 0), (0, LANES - n_e)))
    br = jnp.concatenate([b_router[l], jnp.full((LANES - n_e,), NEG_BIG, F32)])[None]
    wp = dict(wa=w_out_bf[:rw_w], wb=w_out_bf[rw_w:], g2=norm2_g[l][None], wr=wr, br=br)
    g1 = norm1_g[l][None]

    xp = x_prompt[0]
    cos_p, sin_p = _rope_tables(jnp.arange(T))
    prw_p, q_p, k_p, v_p = _inproj(xp, g1, w_in_bf, cos_p, sin_p, 512, rw_cols, q_cols, kv_cols)
    s0_p = jnp.zeros((n_pairs, PAIR, PAIR), F32)
    shift0_p = jnp.zeros((1, rw_cols), F32)
    ya_p, sfin_p = _rwkv_prompt(prw_p, shift0_p, s0_p, pp, RW_TILE)
    yb_p = _attn_prompt(q_p, k_p, v_p, attn_sinks[l], n_q, n_kv)

    xs_ = x_sample[:, 0]
    cos_s, sin_s = _rope_tables(jnp.full((B,), past_len))
    prw_s, q_s, k_s, v_s = _inproj(xs_, g1, w_in_bf, cos_s, sin_s, ROW_TILE, rw_cols, q_cols, kv_cols)
    s_flat = state_rwkv_wkv[l].reshape(B, H * HEAD_DIM * HEAD_DIM)
    ya_s, snew_flat = _rwkv_step(prw_s, state_rwkv_shift[l], s_flat, pp, n_pairs)
    o2, kc_new, vc_new = _attn_step(q_s.reshape(B * n_q, HEAD_DIM), k_s, v_s,
                                    cache_swa_k[l].reshape(B, wlen, kv_cols),
                                    cache_swa_v[l].reshape(B, wlen, kv_cols),
                                    attn_sinks[l][:, None], n_q, n_kv, past_len)
    yb_s = o2.reshape(B, q_cols)

    rows = T + B
    x1_p, h2_p, gate_p, meta_p, tb_p, tl_p, cnt = _post(xp, ya_p, yb_p, wp, jnp.zeros((1, LANES), F32), MOE_TILE)
    x1_s, h2_s, gate_s, meta_s, tb_s, tl_s, cnt = _post(xs_, ya_s, yb_s, wp, cnt, ROW_TILE)

    counts = cnt[0, :n_e].astype(jnp.int32)
    padded = (counts + MOE_WIN + MOE_BM - 1) // MOE_BM * MOE_BM
    pend = jnp.cumsum(padded)
    pstart = (pend - padded).astype(jnp.int32)
    n_blocks = -(-(rows * TOP_K) // MOE_BM) + n_e + -(-(n_e * MOE_WIN) // MOE_BM)
    block_start = jnp.arange(n_blocks, dtype=jnp.int32) * MOE_BM
    block_e = jnp.minimum(jnp.sum((pend[None, :] <= block_start[:, None]).astype(jnp.int32), axis=1),
                          n_e - 1).astype(jnp.int32)
    n_used = (pend[-1] // MOE_BM).astype(jnp.int32)[None]
    z1 = (pstart + counts // MOE_BM * MOE_BM).astype(jnp.int32)
    z2 = (pend - MOE_BM).astype(jnp.int32)

    n_sorted = n_blocks * MOE_BM
    xs_sorted, carry = _scatter(pstart, z1, z2, tb_p, tl_p, meta_p, h2_p, None, n_sorted, MOE_BM, MOE_TILE)
    xs_sorted, _ = _scatter(pstart, z1, z2, tb_s, tl_s, meta_s, h2_s, (carry, xs_sorted), n_sorted, MOE_BM,
                            ROW_TILE)
    ys_sorted = _experts(block_e, n_used, xs_sorted, w_mlp1[l], b_mlp1[l][:, None], w_mlp2[l],
                         b_mlp2[l][:, None], MOE_BM)
    gf = norm_f_g[None]
    y_p = _combine(pstart, tb_p, tl_p, meta_p, gate_p, x1_p, gf, ys_sorted, MOE_TILE)
    y_s = _combine(pstart, tb_s, tl_s, meta_s, gate_s, x1_s, gf, ys_sorted, ROW_TILE)

    sdt = state_rwkv_wkv.dtype
    return (y_p[None], y_s[:, None],
            _state_from_pairs(sfin_p)[None, None].astype(sdt), prw_p[T - 1][None, None],
            k_p[T - min(WINDOW, T):].reshape(1, 1, -1, n_kv, HEAD_DIM),
            v_p[T - min(WINDOW, T):].reshape(1, 1, -1, n_kv, HEAD_DIM),
            snew_flat.reshape(1, B, H, HEAD_DIM, HEAD_DIM).astype(sdt), prw_s[None],
            kc_new.reshape(1, B, wlen, n_kv, HEAD_DIM), vc_new.reshape(1, B, wlen, n_kv, HEAD_DIM))
```

```python
import functools

import jax
import jax.numpy as jnp
from jax import lax
from jax.experimental import pallas as pl
from jax.experimental.pallas import tpu as pltpu

F32 = jnp.float32
BF16 = jnp.bfloat16

LANES = 128
HEAD_DIM = 64
PAIR = 2 * HEAD_DIM
CHUNK = 64
RW_TILE = 256
RW_PAIRS_PER_STEP = 4
ROT_DIM = 16
ROPE_THETA = 500000.0
WINDOW = 128
PAST_LEN = 16384
ATT_BLOCK = 128
N_EXPERTS = 32
TOP_K = 4
SWIGLU_ALPHA = 1.702
SWIGLU_LIMIT = 7.0
NORM_EPS = 1e-5
LNX_EPS = HEAD_DIM * 1e-5
MOE_BM = 512
ROW_TILE = 128
MOE_WIN_SHIFT = 5
MOE_WIN = 1 << MOE_WIN_SHIFT
SORT_ALIGN_SHIFT = 4
SORT_ALIGN = 1 << SORT_ALIGN_SHIFT
MOE_TILE = 256
NEG_BIG = -1e30
VMEM_LIMIT = 52 * 1024 * 1024

NN = (((1,), (0,)), ((), ()))
NT = (((1,), (1,)), ((), ()))


def _mm(a, b, dn=NN):
    return lax.dot_general(a, b, dn, preferred_element_type=F32)


def _split2(a):
    hi = a.astype(BF16)
    lo = (a - hi.astype(F32)).astype(BF16)
    return hi, lo


def _split3(a):
    hi = a.astype(BF16)
    r1 = a - hi.astype(F32)
    mid = r1.astype(BF16)
    lo = (r1 - mid.astype(F32)).astype(BF16)
    return hi, mid, lo


def _dot1(a, b, dn=NN):
    return _mm(a.astype(BF16), b.astype(BF16), dn)


def _dot_sel_l(sel, b, dn=NN):
    b0, b1, b2 = _split3(b)
    return _mm(sel, b0, dn) + (_mm(sel, b1, dn) + _mm(sel, b2, dn))


def _dot_sel_r(a, sel, dn=NN):
    a0, a1, a2 = _split3(a)
    return _mm(a0, sel, dn) + (_mm(a1, sel, dn) + _mm(a2, sel, dn))


def _iota(shape, dim):
    return lax.broadcasted_iota(jnp.int32, shape, dim)


def _seg_matrix():
    return ((_iota((PAIR, PAIR), 0) // HEAD_DIM) == (_iota((PAIR, PAIR), 1) // HEAD_DIM)).astype(BF16)


def _sigmoid(x):
    return 1.0 / (1.0 + jnp.exp(-x))


def _cparams(sem, vmem=VMEM_LIMIT):
    return pltpu.CompilerParams(dimension_semantics=sem, vmem_limit_bytes=vmem)


def _rope_slab(x, cos, sin_signed):
    lane = _iota(x.shape, 1) % HEAD_DIM
    up = pltpu.roll(x, LANES - ROT_DIM // 2, axis=1)
    down = pltpu.roll(x, ROT_DIM // 2, axis=1)
    partner = jnp.where(lane < ROT_DIM // 2, up, down)
    return x * cos + partner * sin_signed


def _inproj_kernel(rw_cols, q_cols, kv_cols, x_ref, g_ref, w_ref, cos_ref, sin_ref,
                   prw_ref, q_ref, k_ref, v_ref):
    x = x_ref[...]
    h = x * lax.rsqrt(jnp.mean(x * x, axis=-1, keepdims=True) + NORM_EPS) * g_ref[...]
    proj = _mm(h.astype(BF16), w_ref[...])
    prw_ref[...] = proj[:, :rw_cols]
    cos = cos_ref[...]
    sin = sin_ref[...]
    for c in range(q_cols // LANES):
        lo = rw_cols + c * LANES
        q_ref[:, c * LANES:(c + 1) * LANES] = _rope_slab(proj[:, lo:lo + LANES], cos, sin)
    ko = rw_cols + q_cols
    for c in range(kv_cols // LANES):
        k_ref[:, c * LANES:(c + 1) * LANES] = _rope_slab(proj[:, ko + c * LANES:ko + (c + 1) * LANES], cos, sin)
    v_ref[...] = proj[:, ko + kv_cols:ko + 2 * kv_cols]


def _inproj(x, g, w_bf, cos_t, sin_t, tm, rw_cols, q_cols, kv_cols):
    rows, d = x.shape
    cols = w_bf.shape[1]
    full = lambda i: (0, 0)
    row = lambda i: (i, 0)
    return pl.pallas_call(
        functools.partial(_inproj_kernel, rw_cols, q_cols, kv_cols),
        grid=(rows // tm,),
        in_specs=[pl.BlockSpec((tm, d), row), pl.BlockSpec((1, d), full),
                  pl.BlockSpec((d, cols), full),
                  pl.BlockSpec((tm, LANES), row), pl.BlockSpec((tm, LANES), row)],
        out_specs=[pl.BlockSpec((tm, rw_cols), row), pl.BlockSpec((tm, q_cols), row),
                   pl.BlockSpec((tm, kv_cols), row), pl.BlockSpec((tm, kv_cols), row)],
        out_shape=[jax.ShapeDtypeStruct((rows, rw_cols), F32), jax.ShapeDtypeStruct((rows, q_cols), F32),
                   jax.ShapeDtypeStruct((rows, kv_cols), F32), jax.ShapeDtypeStruct((rows, kv_cols), F32)],
        compiler_params=_cparams(("parallel",)),
        name="inproj",
    )(x, g, w_bf, cos_t, sin_t)


def _rwkv_tokenwise(pr, pk, pv, plo, pg, prev_r, prev_k, prev_v, prev_lo, prev_g,
                    mu_r, mu_k, mu_v, mu_lo, mu_g, w0, dw2, a0, aw2, gw2, kkp, kap, rkp, seg):
    r = pr + (prev_r - pr) * mu_r
    k = pk + (prev_k - pk) * mu_k
    v = pv + (prev_v - pv) * mu_v
    lo = plo + (prev_lo - plo) * mu_lo
    gd = pg + (prev_g - pg) * mu_g
    z = -(w0 + _dot1(jnp.tanh(lo), dw2))
    softplus = jnp.maximum(z, 0.0) + jnp.log(1.0 + jnp.exp(-jnp.abs(z)))
    logw = -jnp.exp(-softplus - 0.5)
    a = _sigmoid(a0 + _dot1(lo, aw2))
    g = _dot1(_sigmoid(gd), gw2)
    kk = k * kkp
    nrm = jnp.sqrt(_seg_sum(kk * kk, seg))
    kk = kk / jnp.maximum(nrm, 1e-12)
    k2 = k * (1.0 + (a - 1.0) * kap)
    bonus = _seg_sum(r * k2 * rkp, seg) * v
    return r, k2, v, logw, -kk, kk * a, g, bonus


def _seg_sum(x, seg):
    xh, xl = _split2(x)
    return _mm(xh, seg) + _mm(xl, seg)


def _rwkv_finish(y, bonus, g, lng, lnb, seg):
    mu = _seg_sum(y, seg) * (1.0 / HEAD_DIM)
    d = y - mu
    var = _seg_sum(d * d, seg) * (1.0 / HEAD_DIM)
    yn = d * lax.rsqrt(var + LNX_EPS) * lng + lnb
    return (yn + bonus) * g


def _rwkv_prompt_kernel(pps, pr_ref, pk_ref, pv_ref, plo_ref, pg_ref,
                        hr_ref, hk_ref, hv_ref, hlo_ref, hg_ref,
                        s0r_ref, s0k_ref, s0v_ref, s0lo_ref, s0g_ref,
                        mur_ref, muk_ref, muv_ref, mulo_ref, mug_ref,
                        w0_ref, dw2_ref, a0_ref, aw2_ref, gw2_ref, kk_ref, ka_ref, rk_ref,
                        lng_ref, lnb_ref, sin_ref,
                        y_ref, sout_ref, st_ref):
    i = pl.program_id(1)
    n_i = pl.num_programs(1)
    tt = pr_ref.shape[0]

    @pl.when(i == 0)
    def _():
        st_ref[...] = sin_ref[...]

    row = _iota((tt, PAIR), 0)

    def prev_of(cur, halo_row, s0_row):
        first = jnp.where(i == 0, s0_row, halo_row)
        return jnp.where(row == 0, first, pltpu.roll(cur, 1, axis=0))

    plo = plo_ref[...]
    pg = pg_ref[...]
    prev_lo = prev_of(plo, hlo_ref[7:8, :], s0lo_ref[...])
    prev_g = prev_of(pg, hg_ref[7:8, :], s0g_ref[...])
    ti = _iota((tt, tt), 0)
    tj = _iota((tt, tt), 1)
    same_chunk = (ti // CHUNK) == (tj // CHUNK)
    incl = same_chunk & (tj <= ti)
    strict = same_chunk & (tj < ti)
    seg = _seg_matrix()
    lane = _iota((tt, PAIR), 1)
    eye = (ti == tj).astype(F32)
    pairs = []
    for p in range(pps):
        ls = slice(p * PAIR, (p + 1) * PAIR)
        pr, pk, pv = pr_ref[:, ls], pk_ref[:, ls], pv_ref[:, ls]
        r, k2, v, logw, nkk, b, g, bonus = _rwkv_tokenwise(
            pr, pk, pv, plo, pg,
            prev_of(pr, hr_ref[7:8, ls], s0r_ref[:, ls]), prev_of(pk, hk_ref[7:8, ls], s0k_ref[:, ls]),
            prev_of(pv, hv_ref[7:8, ls], s0v_ref[:, ls]), prev_lo, prev_g,
            mur_ref[:, ls], muk_ref[:, ls], muv_ref[:, ls], mulo_ref[...], mug_ref[...],
            w0_ref[:, ls], dw2_ref[:, ls], a0_ref[:, ls], aw2_ref[:, ls], gw2_ref[:, ls],
            kk_ref[:, ls], ka_ref[:, ls], rk_ref[:, ls], seg)
        pairs.append(dict(ls=ls, r=r, k2=k2, v=v, logw=logw, nkk=nkk, b=b, g=g, bonus=bonus))

    incl_b = incl.astype(BF16)
    for q in pairs:
        q["cs"] = _dot_sel_l(incl_b, q["logw"])
    for q in pairs:
        cs = q["cs"]
        gam = jnp.exp(cs)
        inv = jnp.exp(-cs)
        q["a_t"] = jnp.exp(cs - q["logw"]) * q["nkk"]
        q["r_t"] = gam * q["r"]
        q["bt_T"] = (q["b"] * inv).T
        q["kt_T"] = (q["k2"] * inv).T
        q["gam_T"] = gam.T
        q["bk_T"] = jnp.concatenate([q["bt_T"], q["kt_T"]], axis=1).astype(BF16)

    heads = []
    for q in pairs:
        for hh in range(2):
            hm = (lane // HEAD_DIM) == hh
            heads.append(dict(q=q, a=jnp.where(hm, q["a_t"], 0.0), r=jnp.where(hm, q["r_t"], 0.0),
                              v=jnp.where(hm, q["v"], 0.0)))
    for h in heads:
        h["g"] = _mm(jnp.concatenate([h["a"], h["r"]], axis=0).astype(BF16), h["q"]["bk_T"])
    for h in heads:
        gmat = h["g"]
        l_ab = jnp.where(strict, gmat[:tt, :tt], 0.0)
        h["l_ak_m_rk"] = jnp.concatenate([jnp.where(strict, gmat[:tt, tt:], 0.0),
                                          jnp.where(incl, gmat[tt:, tt:], 0.0)], axis=0).astype(BF16)
        h["m_rb"] = jnp.where(incl, gmat[tt:, :tt], 0.0).astype(BF16)
        h["tm"] = eye + l_ab
        h["lp"] = l_ab.astype(BF16)
    for h in heads:
        h["lp"] = _mm(h["lp"], h["lp"]).astype(BF16)
    for _ in range(4):
        for h in heads:
            h["both"] = _mm(jnp.concatenate([h["tm"].astype(BF16), h["lp"]], axis=0), h["lp"])
        for h in heads:
            h["tm"] = h["tm"] + h["both"][:tt]
            h["lp"] = h["both"][tt:].astype(BF16)
    for h in heads:
        h["pq"] = _mm(h["l_ak_m_rk"], h["v"].astype(BF16))
        h["tm"] = h["tm"] + _mm(h["tm"].astype(BF16), h["lp"])
    for h in heads:
        h["tx"] = _mm(h["tm"].astype(BF16),
                      jnp.concatenate([h["a"], h["pq"][:tt]], axis=1).astype(BF16))
    for h in heads:
        h["rx"] = _mm(h["m_rb"], h["tx"].astype(BF16))
    for n, q in enumerate(pairs):
        h0, h1 = heads[2 * n], heads[2 * n + 1]
        q["tatp"] = (h0["tx"] + h1["tx"]).astype(BF16)
        ryc = (h0["rx"] + h1["rx"]) + jnp.concatenate([h0["r"] + h1["r"], h0["pq"][tt:] + h1["pq"][tt:]], axis=1)
        q["ry"] = ryc[:, :PAIR]
        q["yc"] = ryc[:, PAIR:]
        q["v_b"] = q["v"].astype(BF16)
        q["bt_b"] = q["bt_T"].astype(BF16)
        q["kt_b"] = q["kt_T"].astype(BF16)
        q["s"] = st_ref[n]

    bd = seg.astype(F32)
    eye_p = (_iota((PAIR, PAIR), 0) == _iota((PAIR, PAIR), 1)).astype(F32)
    col_t = _iota((PAIR, tt), 1)
    zb = jnp.zeros((PAIR, tt), BF16)
    n_chunks = tt // CHUNK
    for c in range(n_chunks):
        cm = (col_t // CHUNK) == c
        for q in pairs:
            bt_c = jnp.where(cm, q["bt_b"], zb)
            kt_c = jnp.where(cm, q["kt_b"], zb)
            dcol = q["gam_T"][:, (c + 1) * CHUNK - 1:(c + 1) * CHUNK]
            bx = _mm(bt_c, q["tatp"])
            q["mc", c] = (dcol * (eye_p + bd * bx[:, :PAIR])).astype(BF16)
            q["nc", c] = dcol * (bd * (bx[:, PAIR:] + _mm(kt_c, q["v_b"])))
    for c in range(n_chunks):
        sl = slice(c * CHUNK, (c + 1) * CHUNK)
        for q in pairs:
            s_b = q["s"].astype(BF16)
            q["y", c] = _mm(q["ry"][sl].astype(BF16), s_b) + q["yc"][sl]
            q["s"] = _mm(q["mc", c], s_b) + q["nc", c]
    for q in pairs:
        y = jnp.concatenate([q["y", c] for c in range(n_chunks)], axis=0)
        y_ref[:, q["ls"]] = _rwkv_finish(y, q["bonus"], q["g"], lng_ref[:, q["ls"]], lnb_ref[:, q["ls"]], seg)
    for n, q in enumerate(pairs):
        st_ref[n] = q["s"]

    @pl.when(i == n_i - 1)
    def _():
        sout_ref[...] = st_ref[...]


def _rwkv_prompt(prw, shift0, s0_pairs, pp, tt):
    T = prw.shape[0]
    n_pairs = s0_pairs.shape[0]
    pps = RW_PAIRS_PER_STEP
    n_grp = n_pairs // pps
    gw = pps * PAIR
    wcols = n_pairs * PAIR
    lo_col = 3 * wcols
    g_col = lo_col + PAIR
    hb = tt // 8

    def cur(off):
        return pl.BlockSpec((tt, gw), lambda p, i: (i, off // gw + p))

    def cur_fixed(col):
        return pl.BlockSpec((tt, PAIR), lambda p, i: (i, col // PAIR))

    def halo(off):
        return pl.BlockSpec((8, gw), lambda p, i: (jnp.maximum(i * hb - 1, 0), off // gw + p))

    def halo_fixed(col):
        return pl.BlockSpec((8, PAIR), lambda p, i: (jnp.maximum(i * hb - 1, 0), col // PAIR))

    def vec(off):
        return pl.BlockSpec((1, gw), lambda p, i: (0, off // gw + p))

    def vec_fixed(col):
        return pl.BlockSpec((1, PAIR), lambda p, i: (0, col // PAIR))

    def wmat(rows):
        return pl.BlockSpec((rows, gw), lambda p, i: (0, p))

    in_specs = ([cur(0), cur(wcols), cur(2 * wcols), cur_fixed(lo_col), cur_fixed(g_col)]
                + [halo(0), halo(wcols), halo(2 * wcols), halo_fixed(lo_col), halo_fixed(g_col)]
                + [vec(0), vec(wcols), vec(2 * wcols), vec_fixed(lo_col), vec_fixed(g_col)]
                + [vec(0), vec(wcols), vec(2 * wcols), vec_fixed(lo_col), vec_fixed(g_col)]
                + [vec(0), wmat(PAIR), vec(0), wmat(PAIR), wmat(PAIR), vec(0), vec(0), vec(0), vec(0), vec(0)]
                + [pl.BlockSpec((pps, PAIR, PAIR), lambda p, i: (p, 0, 0))])
    args = ([prw] * 5 + [prw] * 5 + [shift0] * 5 + [pp["mu"]] * 5
            + [pp["w0"], pp["dw2"], pp["a0"], pp["aw2"], pp["gw2"], pp["kk"], pp["ka"], pp["rk"],
               pp["lng"], pp["lnb"], s0_pairs])
    return pl.pallas_call(
        functools.partial(_rwkv_prompt_kernel, pps),
        grid=(n_grp, T // tt),
        in_specs=in_specs,
        out_specs=[pl.BlockSpec((tt, gw), lambda p, i: (i, p)),
                   pl.BlockSpec((pps, PAIR, PAIR), lambda p, i: (p, 0, 0))],
        out_shape=[jax.ShapeDtypeStruct((T, wcols), F32),
                   jax.ShapeDtypeStruct((n_pairs, PAIR, PAIR), F32)],
        scratch_shapes=[pltpu.VMEM((pps, PAIR, PAIR), F32)],
        compiler_params=_cparams(("parallel", "arbitrary")),
        name="rwkv_prompt",
    )(*args)


def _rwkv_step_kernel(slabs_per_step, pr_ref, pk_ref, pv_ref, plo_ref, pg_ref,
                      sr_ref, sk_ref, sv_ref, slo_ref, sg_ref,
                      mur_ref, muk_ref, muv_ref, mulo_ref, mug_ref,
                      w0_ref, dw2_ref, a0_ref, aw2_ref, gw2_ref, kk_ref, ka_ref, rk_ref,
                      lng_ref, lnb_ref, s_ref,
                      y_ref, snew_ref, yacc_ref):
    j = pl.program_id(1)
    n_j = pl.num_programs(1)
    seg = _seg_matrix()
    r, k2, v, logw, nkk, b, g, bonus = _rwkv_tokenwise(
        pr_ref[...], pk_ref[...], pv_ref[...], plo_ref[...], pg_ref[...],
        sr_ref[...], sk_ref[...], sv_ref[...], slo_ref[...], sg_ref[...],
        mur_ref[...], muk_ref[...], muv_ref[...], mulo_ref[...], mug_ref[...],
        w0_ref[...], dw2_ref[...], a0_ref[...], aw2_ref[...], gw2_ref[...],
        kk_ref[...], ka_ref[...], rk_ref[...], seg)
    w = jnp.exp(logw)

    @pl.when(j == 0)
    def _():
        yacc_ref[...] = jnp.zeros_like(yacc_ref)

    slabs_per_head = HEAD_DIM // 2
    ci = _iota((PAIR, PAIR), 0)
    li = _iota((PAIR, PAIR), 1)
    assert slabs_per_head % slabs_per_step == 0
    yacc = yacc_ref[...]
    hh = (j * slabs_per_step) // slabs_per_head
    dup = ((ci == hh * HEAD_DIM + li % HEAD_DIM)).astype(BF16)
    nkk_d, w_d, b_d, k_d, r_d = [_dot_sel_r(x, dup) for x in (nkk, w, b, k2, r)]
    slabs = range(slabs_per_step)
    i0 = [2 * ((j * slabs_per_step + t) % slabs_per_head) for t in slabs]
    s = [s_ref[:, t * PAIR:(t + 1) * PAIR] for t in slabs]
    sa = [_seg_sum(s[t] * nkk_d, seg) for t in slabs]
    v_bc = [_seg_sum(v, (ci == hh * HEAD_DIM + i0[t] + li // HEAD_DIM).astype(BF16)) for t in slabs]
    s_new = [s[t] * w_d + sa[t] * b_d + v_bc[t] * k_d for t in slabs]
    for t in slabs:
        snew_ref[:, t * PAIR:(t + 1) * PAIR] = s_new[t]
    yred = [_seg_sum(s_new[t] * r_d, seg) for t in slabs]
    ysel = [_seg_sum(yred[t], ((ci % HEAD_DIM == 0)
                               & (li == hh * HEAD_DIM + i0[t] + ci // HEAD_DIM)).astype(BF16)) for t in slabs]
    for t in slabs:
        yacc = yacc + ysel[t]
    yacc_ref[...] = yacc

    @pl.when(j == n_j - 1)
    def _():
        y_ref[...] = _rwkv_finish(yacc, bonus, g, lng_ref[...], lnb_ref[...], seg)


def _rwkv_step(prw, shift, s_flat, pp, n_pairs):
    B = prw.shape[0]
    lanes_per_pair = 2 * HEAD_DIM * HEAD_DIM
    blk = 1024
    slabs_per_step = blk // PAIR
    steps = lanes_per_pair // blk
    lo_blk = 3 * n_pairs
    g_blk = lo_blk + 1

    def cur(off):
        return pl.BlockSpec((B, PAIR), lambda p, j: (0, off + p))

    def cur_fixed(b_):
        return pl.BlockSpec((B, PAIR), lambda p, j: (0, b_))

    def vec(off):
        return pl.BlockSpec((1, PAIR), lambda p, j: (0, off + p))

    def vec_fixed(b_):
        return pl.BlockSpec((1, PAIR), lambda p, j: (0, b_))

    def wmat(rows):
        return pl.BlockSpec((rows, PAIR), lambda p, j: (0, p))

    sspec = pl.BlockSpec((B, blk), lambda p, j: (0, p * steps + j))
    in_specs = ([cur(0), cur(n_pairs), cur(2 * n_pairs), cur_fixed(lo_blk), cur_fixed(g_blk)] * 2
                + [vec(0), vec(n_pairs), vec(2 * n_pairs), vec_fixed(lo_blk), vec_fixed(g_blk)]
                + [vec(0), wmat(PAIR), vec(0), wmat(PAIR), wmat(PAIR), vec(0), vec(0), vec(0), vec(0), vec(0)]
                + [sspec])
    args = ([prw] * 5 + [shift] * 5 + [pp["mu"]] * 5
            + [pp["w0"], pp["dw2"], pp["a0"], pp["aw2"], pp["gw2"], pp["kk"], pp["ka"], pp["rk"],
               pp["lng"], pp["lnb"], s_flat])
    return pl.pallas_call(
        functools.partial(_rwkv_step_kernel, slabs_per_step),
        grid=(n_pairs, steps),
        in_specs=in_specs,
        out_specs=[pl.BlockSpec((B, PAIR), lambda p, j: (0, p)), sspec],
        out_shape=[jax.ShapeDtypeStruct((B, n_pairs * PAIR), F32),
                   jax.ShapeDtypeStruct(s_flat.shape, F32)],
        scratch_shapes=[pltpu.VMEM((B, PAIR), F32)],
        compiler_params=_cparams(("parallel", "arbitrary")),
        name="rwkv_step",
    )(*args)


def _attn_prompt_kernel(n_q, group, sink_ref, q_ref, kc_ref, kp_ref, vc_ref, vp_ref, o_ref):
    blk = q_ref.shape[0]
    q = q_ref[...] * (HEAD_DIM ** -0.5)
    kband = jnp.concatenate([kp_ref[...], kc_ref[...]], axis=0)
    vband = jnp.concatenate([vp_ref[...], vc_ref[...]], axis=0)
    i = pl.program_id(0)
    rq = _iota((blk, 2 * blk), 0)
    ck = _iota((blk, 2 * blk), 1)
    dist = rq - ck + blk
    kpos = i * blk - blk + ck
    valid = (dist >= 0) & (dist < WINDOW) & (kpos >= 0)
    heads = range(n_q)
    kb = [kband[:, g * HEAD_DIM:(g + 1) * HEAD_DIM].astype(BF16) for g in range(n_q // group)]
    vb = [vband[:, g * HEAD_DIM:(g + 1) * HEAD_DIM].astype(BF16) for g in range(n_q // group)]
    s = [jnp.where(valid, _mm(q[:, h * HEAD_DIM:(h + 1) * HEAD_DIM].astype(BF16), kb[h // group], NT), NEG_BIG)
         for h in heads]
    m = [jnp.maximum(jnp.max(s[h], axis=-1, keepdims=True), sink_ref[h]) for h in heads]
    p = [jnp.exp(s[h] - m[h]) for h in heads]
    denom = [jnp.sum(p[h], axis=-1, keepdims=True) + jnp.exp(sink_ref[h] - m[h]) for h in heads]
    o = [_mm(p[h].astype(BF16), vb[h // group]) for h in heads]
    for h in heads:
        o_ref[:, h * HEAD_DIM:(h + 1) * HEAD_DIM] = o[h] / denom[h]


def _attn_prompt(q, k, v, sinks, n_q, n_kv):
    T, qw = q.shape
    kvw = k.shape[1]
    blk = ATT_BLOCK
    curm = lambda i: (i, 0)
    prevm = lambda i: (jnp.maximum(i - 1, 0), 0)
    return pl.pallas_call(
        functools.partial(_attn_prompt_kernel, n_q, n_q // n_kv),
        grid=(T // blk,),
        in_specs=[pl.BlockSpec(memory_space=pltpu.SMEM),
                  pl.BlockSpec((blk, qw), curm),
                  pl.BlockSpec((blk, kvw), curm), pl.BlockSpec((blk, kvw), prevm),
                  pl.BlockSpec((blk, kvw), curm), pl.BlockSpec((blk, kvw), prevm)],
        out_specs=pl.BlockSpec((blk, qw), curm),
        out_shape=jax.ShapeDtypeStruct((T, qw), F32),
        compiler_params=_cparams(("parallel",)),
        name="attn_prompt",
    )(sinks, q, k, k, v, v)


def _attn_step_kernel(n_q, group, pos0, sink_ref, q_ref, kn_ref, vn_ref, kc_ref, vc_ref,
                      o_ref, ko_ref, vo_ref):
    bb, wlen, kvw = kc_ref.shape
    lane = _iota((n_q, kvw), 1)
    rowh = _iota((n_q, kvw), 0)
    mine = (lane // HEAD_DIM) == (rowh // group)
    dupm = (_iota((HEAD_DIM, kvw), 0) == _iota((HEAD_DIM, kvw), 1) % HEAD_DIM).astype(BF16)
    fold = (_iota((kvw, HEAD_DIM), 0) % HEAD_DIM == _iota((kvw, HEAD_DIM), 1)).astype(BF16)
    kidx = _iota((n_q, wlen), 1)
    dist = wlen - kidx
    valid = (dist < WINDOW) & (pos0 - dist >= 0)
    rk = _iota((wlen, kvw), 0)
    sink = sink_ref[...]
    for t in range(bb):
        qh = q_ref[t * n_q:(t + 1) * n_q, :] * (HEAD_DIM ** -0.5)
        qm = jnp.where(mine, _dot_sel_r(qh, dupm), 0.0)
        kc = kc_ref[t]
        vc = vc_ref[t]
        kn = kn_ref[t:t + 1, :]
        vn = vn_ref[t:t + 1, :]
        s = jnp.where(valid, _dot1(qm, kc, NT), NEG_BIG)
        s_new = jnp.sum(qm * kn, axis=-1, keepdims=True)
        m = jnp.maximum(jnp.maximum(jnp.max(s, axis=-1, keepdims=True), s_new), sink)
        p = jnp.exp(s - m)
        p_new = jnp.exp(s_new - m)
        denom = jnp.sum(p, axis=-1, keepdims=True) + p_new + jnp.exp(sink - m)
        res = (_dot1(p, vc) + p_new * vn) / denom
        o_ref[t * n_q:(t + 1) * n_q, :] = _dot_sel_r(jnp.where(mine, res, 0.0), fold)
        ko_ref[t] = jnp.where(rk == wlen - 1, kn, pltpu.roll(kc, wlen - 1, axis=0))
        vo_ref[t] = jnp.where(rk == wlen - 1, vn, pltpu.roll(vc, wlen - 1, axis=0))


def _attn_step(q2, k_new, v_new, k_cache, v_cache, sinks_col, n_q, n_kv, pos0):
    B, wlen, kvw = k_cache.shape
    bb = 8
    return pl.pallas_call(
        functools.partial(_attn_step_kernel, n_q, n_q // n_kv, pos0),
        grid=(B // bb,),
        in_specs=[pl.BlockSpec((n_q, 1), lambda i: (0, 0)),
                  pl.BlockSpec((bb * n_q, HEAD_DIM), lambda i: (i, 0)),
                  pl.BlockSpec((bb, kvw), lambda i: (i, 0)), pl.BlockSpec((bb, kvw), lambda i: (i, 0)),
                  pl.BlockSpec((bb, wlen, kvw), lambda i: (i, 0, 0)),
                  pl.BlockSpec((bb, wlen, kvw), lambda i: (i, 0, 0))],
        out_specs=[pl.BlockSpec((bb * n_q, HEAD_DIM), lambda i: (i, 0)),
                   pl.BlockSpec((bb, wlen, kvw), lambda i: (i, 0, 0)),
                   pl.BlockSpec((bb, wlen, kvw), lambda i: (i, 0, 0))],
        out_shape=[jax.ShapeDtypeStruct((B * n_q, HEAD_DIM), F32),
                   jax.ShapeDtypeStruct((B, wlen, kvw), F32),
                   jax.ShapeDtypeStruct((B, wlen, kvw), F32)],
        compiler_params=_cparams(("parallel",)),
        name="attn_step",
    )(sinks_col, q2, k_new, v_new, k_cache, v_cache)


def _post_kernel(x_ref, ya_ref, yb_ref, wa_ref, wb_ref, g2_ref, wr_ref, br_ref, cnt0_ref,
                 x1_ref, h2_ref, gate_ref, meta_ref, tb_ref, tl_ref, cnt_ref, carry_ref):
    i = pl.program_id(0)

    @pl.when(i == 0)
    def _():
        carry_ref[...] = cnt0_ref[...]

    mix = _mm(ya_ref[...].astype(BF16), wa_ref[...]) + _mm(yb_ref[...].astype(BF16), wb_ref[...])
    x1 = x_ref[...] + mix
    h2 = x1 * lax.rsqrt(jnp.mean(x1 * x1, axis=-1, keepdims=True) + NORM_EPS) * g2_ref[...]
    x1_ref[...] = x1
    h2_ref[...] = h2

    l = _dot1(h2, wr_ref[...]) + br_ref[...]
    tm = l.shape[0]
    lane = _iota(l.shape, 1)
    vals, idxs = [], []
    for _ in range(TOP_K):
        m = jnp.max(l, axis=-1, keepdims=True)
        sel = jnp.min(jnp.where(l == m, lane, LANES), axis=-1, keepdims=True)
        vals.append(m)
        idxs.append(sel)
        l = jnp.where(lane == sel, -jnp.inf, l)
    es = [jnp.exp(v - vals[0]) for v in vals]
    tot = es[0] + es[1] + es[2] + es[3]
    onehot = jnp.zeros(l.shape, F32)
    for sel in idxs:
        onehot = onehot + (lane == sel).astype(F32)
    strict = (_iota((tm, tm), 1) < _iota((tm, tm), 0)).astype(BF16)
    before = _mm(strict, onehot.astype(BF16))
    for k in range(TOP_K):
        gate_ref[:, k:k + 1] = es[k] / tot
        meta_ref[:, k:k + 1] = idxs[k]
        meta_ref[:, TOP_K + k:TOP_K + k + 1] = jnp.sum(
            jnp.where(lane == idxs[k], before, 0.0), axis=-1, keepdims=True).astype(jnp.int32)
    carry = carry_ref[...]
    cnt_t = jnp.sum(onehot, axis=0, keepdims=True)
    tb_ref[0] = carry.astype(jnp.int32)
    tl_ref[0] = cnt_t.astype(jnp.int32)
    carry_ref[...] = carry + cnt_t
    cnt_ref[...] = carry + cnt_t


def _post(x, ya, yb, wp, cnt0, tm):
    rows, d = x.shape
    half = ya.shape[1]
    n_t = rows // tm
    full = lambda i: (0, 0)
    row = lambda i: (i, 0)
    trow = lambda i: (i, 0, 0)
    return pl.pallas_call(
        _post_kernel,
        grid=(n_t,),
        in_specs=[pl.BlockSpec((tm, d), row), pl.BlockSpec((tm, half), row), pl.BlockSpec((tm, half), row),
                  pl.BlockSpec((half, d), full), pl.BlockSpec((half, d), full),
                  pl.BlockSpec((1, d), full), pl.BlockSpec((d, LANES), full), pl.BlockSpec((1, LANES), full),
                  pl.BlockSpec((1, LANES), full)],
        out_specs=[pl.BlockSpec((tm, d), row), pl.BlockSpec((tm, d), row),
                   pl.BlockSpec((tm, TOP_K), row), pl.BlockSpec((tm, 2 * TOP_K), row),
                   pl.BlockSpec((1, 1, LANES), trow), pl.BlockSpec((1, 1, LANES), trow),
                   pl.BlockSpec((1, LANES), full)],
        out_shape=[jax.ShapeDtypeStruct((rows, d), F32), jax.ShapeDtypeStruct((rows, d), F32),
                   jax.ShapeDtypeStruct((rows, TOP_K), F32),
                   jax.ShapeDtypeStruct((rows, 2 * TOP_K), jnp.int32),
                   jax.ShapeDtypeStruct((n_t, 1, LANES), jnp.int32),
                   jax.ShapeDtypeStruct((n_t, 1, LANES), jnp.int32),
                   jax.ShapeDtypeStruct((1, LANES), F32)],
        scratch_shapes=[pltpu.VMEM((1, LANES), F32)],
        compiler_params=_cparams(("arbitrary",)),
        name="post",
    )(x, ya, yb, wp["wa"], wp["wb"], wp["g2"], wp["wr"], wp["br"], cnt0)


def _n_windows(base, length):
    off = base & (SORT_ALIGN - 1)
    n = lax.shift_right_logical(off + length + (MOE_WIN - 1), MOE_WIN_SHIFT)
    return off, jnp.where(length > 0, n, 0)


def _window_targets(meta, tb_vec, tl_vec):
    tm = meta.shape[0]
    off, n_win = _n_windows(tb_vec, tl_vec)
    upper = (_iota((LANES, LANES), 0) < _iota((LANES, LANES), 1)).astype(BF16)
    slot_start = _mm(n_win.astype(F32).astype(BF16), upper)
    pos0 = slot_start * MOE_WIN + off.astype(F32)
    lane = _iota((tm, LANES), 1)
    tgts = []
    for k in range(TOP_K):
        p0 = jnp.sum(jnp.where(lane == meta[:, k:k + 1], pos0, 0.0), axis=-1, keepdims=True)
        tgts.append(p0.astype(jnp.int32) + meta[:, TOP_K + k:TOP_K + k + 1])
    return tgts


def _for_each_window(n_e, pstart_ref, tb_ref, tl_ref, fn, rows_ref, cnt_ref, b, per_expert_fn=None):
    def per_expert(e, slot0):
        base = tb_ref[0, 0, e]
        length = tl_ref[0, 0, e]
        off, n = _n_windows(base, length)
        row0 = pstart_ref[e] + base - off
        if per_expert_fn is not None:
            per_expert_fn(e, slot0, off, length, n)

        def per_window(w, c):
            row = row0 + w * MOE_WIN
            rows_ref[b, slot0 + w] = row
            fn(slot0 + w, pl.multiple_of(row, SORT_ALIGN))
            return c

        lax.fori_loop(0, n, per_window, 0)
        return slot0 + n

    cnt_ref[b] = lax.fori_loop(0, n_e, per_expert, 0)


def _for_recorded_windows(fn, rows_ref, cnt_ref, b):
    def body(slot, c):
        fn(slot, pl.multiple_of(rows_ref[b, slot], SORT_ALIGN))
        return c

    lax.fori_loop(0, cnt_ref[b], body, 0)


def _moe_slots(tm):
    n = -(-(tm * TOP_K + N_EXPERTS * (SORT_ALIGN - 1 + MOE_WIN - 1)) // MOE_WIN)
    per_lane_tile = LANES // MOE_WIN
    return -(-n // per_lane_tile) * per_lane_tile


def _scatter_kernel(n_e, continues, pstart_ref, z1_ref, z2_ref, tb_ref, tl_ref, tbv_ref, tlv_ref,
                    meta_ref, h_ref, *rest):
    if continues:
        cin_ref, _xs_alias, xs_ref, cout_ref, xw_ref, zero_ref, carry_ref, rows_ref, cnt_ref, sem, zsem = rest
    else:
        xs_ref, cout_ref, xw_ref, zero_ref, carry_ref, rows_ref, cnt_ref, sem, zsem = rest
    i = pl.program_id(0)
    n_i = pl.num_programs(0)
    tm = h_ref.shape[0]
    bm = zero_ref.shape[0]
    buf = i % 2
    n_rows_w = xw_ref.shape[1]

    @pl.when(i == 0)
    def _():
        if continues:
            carry_ref[...] = cin_ref[...]
        else:
            zero_ref[...] = jnp.zeros_like(zero_ref)

            def zcopy(row):
                return pltpu.make_async_copy(zero_ref, xs_ref.at[pl.ds(pl.multiple_of(row, SORT_ALIGN), bm)], zsem)

            for e in range(n_e):
                zcopy(z1_ref[e]).start()

                @pl.when(z2_ref[e] != z1_ref[e])
                def _():
                    zcopy(z2_ref[e]).start()
            for e in range(n_e):
                zcopy(z1_ref[e]).wait()

                @pl.when(z2_ref[e] != z1_ref[e])
                def _():
                    zcopy(z2_ref[e]).wait()

            carry_ref[...] = jnp.zeros_like(carry_ref)

    tgts = _window_targets(meta_ref[...], tbv_ref[0], tlv_ref[0])
    lane_s = _iota((tm, n_rows_w), 1)
    sel = jnp.zeros((tm, n_rows_w), F32)
    for tgt in tgts:
        sel = sel + (lane_s == tgt).astype(F32)
    xw_ref[buf] = _mm(sel.T.astype(BF16), h_ref[...].astype(BF16)).astype(BF16)

    def splice_carry(e, slot0, off, length, n):
        @pl.when(n > 0)
        def _():
            g0 = pl.multiple_of(slot0 * MOE_WIN, MOE_WIN)
            xw_ref[buf, pl.ds(g0, SORT_ALIGN), :] = xw_ref[buf, pl.ds(g0, SORT_ALIGN), :] + carry_ref[e]
            filled = off + length
            gl = pl.multiple_of(g0 + lax.shift_right_logical(filled, SORT_ALIGN_SHIFT) * SORT_ALIGN, SORT_ALIGN)
            last = xw_ref[buf, pl.ds(gl, SORT_ALIGN), :]
            carry_ref[e] = jnp.where((filled & (SORT_ALIGN - 1)) != 0, last, jnp.zeros_like(last))

    def copy(b, slot, row):
        return pltpu.make_async_copy(xw_ref.at[b, pl.ds(pl.multiple_of(slot * MOE_WIN, MOE_WIN), MOE_WIN)],
                                     xs_ref.at[pl.ds(row, MOE_WIN)], sem.at[b])

    @pl.when(i > 0)
    def _():
        _for_recorded_windows(lambda slot, row: copy(1 - buf, slot, row).wait(), rows_ref, cnt_ref, 1 - buf)

    _for_each_window(n_e, pstart_ref, tb_ref, tl_ref, lambda slot, row: copy(buf, slot, row).start(),
                     rows_ref, cnt_ref, buf, splice_carry)

    @pl.when(i == n_i - 1)
    def _():
        _for_recorded_windows(lambda slot, row: copy(buf, slot, row).wait(), rows_ref, cnt_ref, buf)
        cout_ref[...] = carry_ref[...]


def _scatter(pstart, z1, z2, tbase, tlen, meta, h2, prior, n_rows_sorted, bm, tm):
    rows, d = h2.shape
    n_e = pstart.shape[0]
    n_slots = _moe_slots(tm)
    smem = pl.BlockSpec(memory_space=pltpu.SMEM)
    tile3 = lambda i: (i, 0, 0)
    tile_smem = lambda im: pl.BlockSpec((1, 1, LANES), im, memory_space=pltpu.SMEM)
    carry_spec = pl.BlockSpec((n_e, SORT_ALIGN, d), lambda i: (0, 0, 0))
    in_specs = [smem, smem, smem,
                tile_smem(tile3), tile_smem(tile3),
                pl.BlockSpec((1, 1, LANES), tile3), pl.BlockSpec((1, 1, LANES), tile3),
                pl.BlockSpec((tm, 2 * TOP_K), lambda i: (i, 0)),
                pl.BlockSpec((tm, d), lambda i: (i, 0))]
    args = [pstart, z1, z2, tbase, tlen, tbase, tlen, meta, h2]
    aliases = {}
    if prior is not None:
        in_specs += [carry_spec, pl.BlockSpec(memory_space=pl.ANY)]
        aliases = {len(args) + 1: 0}
        args += list(prior)
    return pl.pallas_call(
        functools.partial(_scatter_kernel, n_e, prior is not None),
        grid=(rows // tm,),
        in_specs=in_specs,
        out_specs=[pl.BlockSpec(memory_space=pl.ANY), carry_spec],
        out_shape=[jax.ShapeDtypeStruct((n_rows_sorted, d), BF16),
                   jax.ShapeDtypeStruct((n_e, SORT_ALIGN, d), BF16)],
        input_output_aliases=aliases,
        scratch_shapes=[pltpu.VMEM((2, n_slots * MOE_WIN, d), BF16), pltpu.VMEM((bm, d), BF16),
                        pltpu.VMEM((n_e, SORT_ALIGN, d), BF16),
                        pltpu.SMEM((2, n_slots), jnp.int32), pltpu.SMEM((2,), jnp.int32),
                        pltpu.SemaphoreType.DMA((2,)), pltpu.SemaphoreType.DMA(())],
        compiler_params=_cparams(("arbitrary",)),
        name="moe_scatter",
    )(*args)


def _expert_kernel(d_ff, be_ref, nused_ref, xs_ref, w1_ref, b1_ref, w2_ref, b2_ref, ys_ref, w1b_ref, w2b_ref):
    i = pl.program_id(0)
    new_expert = jnp.logical_or(i == 0, be_ref[i] != be_ref[jnp.maximum(i - 1, 0)])

    @pl.when(jnp.logical_and(i < nused_ref[0], new_expert))
    def _():
        w1b_ref[...] = w1_ref[0].astype(BF16)
        w2b_ref[...] = w2_ref[0].astype(BF16)

    @pl.when(i < nused_ref[0])
    def _():
        h = _mm(xs_ref[...], w1b_ref[...]) + b1_ref[0]
        hg = jnp.minimum(h[:, :d_ff], SWIGLU_LIMIT)
        hu = jnp.clip(h[:, d_ff:], -SWIGLU_LIMIT, SWIGLU_LIMIT)
        act = hg * _sigmoid(SWIGLU_ALPHA * hg) * (hu + 1.0)
        ys_ref[...] = (_mm(act.astype(BF16), w2b_ref[...]) + b2_ref[0]).astype(ys_ref.dtype)

    @pl.when(i >= nused_ref[0])
    def _():
        ys_ref[...] = jnp.zeros_like(ys_ref)


def _experts(block_e, n_used, xs, w1, b1, w2, b2, bm):
    R, d = xs.shape
    d_ff = w2.shape[1]
    nb = R // bm

    def rows(i, be, nu):
        return (jnp.minimum(i, nu[0] - 1), 0)

    def wsel(i, be, nu):
        return (be[i], 0, 0)

    return pl.pallas_call(
        functools.partial(_expert_kernel, d_ff),
        grid_spec=pltpu.PrefetchScalarGridSpec(
            num_scalar_prefetch=2,
            grid=(nb,),
            in_specs=[pl.BlockSpec((bm, d), rows),
                      pl.BlockSpec((1, d, 2 * d_ff), wsel), pl.BlockSpec((1, 1, 2 * d_ff), wsel),
                      pl.BlockSpec((1, d_ff, d), wsel), pl.BlockSpec((1, 1, d), wsel)],
            out_specs=pl.BlockSpec((bm, d), lambda i, be, nu: (i, 0)),
            scratch_shapes=[pltpu.VMEM((d, 2 * d_ff), BF16), pltpu.VMEM((d_ff, d), BF16)]),
        out_shape=jax.ShapeDtypeStruct((R, d), xs.dtype),
        compiler_params=_cparams(("arbitrary",)),
        name="moe_experts",
    )(block_e, n_used, xs, w1, b1, w2, b2)


def _combine_kernel(n_e, pstart_ref, tb_ref, tl_ref, tbn_ref, tln_ref, tbv_ref, tlv_ref, meta_ref, gate_ref,
                    x1_ref, gf_ref, ys_ref, o_ref, win_ref, rows_ref, cnt_ref, sem):
    i = pl.program_id(0)
    n = pl.num_programs(0)
    tm = x1_ref.shape[0]
    buf = i % 2
    n_rows_w = win_ref.shape[1]

    def copy(b, slot, row):
        return pltpu.make_async_copy(
            ys_ref.at[pl.ds(row, MOE_WIN)],
            win_ref.at[b, pl.ds(pl.multiple_of(slot * MOE_WIN, MOE_WIN), MOE_WIN)], sem.at[b])

    @pl.when(i == 0)
    def _():
        win_ref[...] = jnp.zeros_like(win_ref)
        _for_each_window(n_e, pstart_ref, tb_ref, tl_ref, lambda slot, row: copy(buf, slot, row).start(),
                         rows_ref, cnt_ref, buf)

    @pl.when(i + 1 < n)
    def _():
        _for_each_window(n_e, pstart_ref, tbn_ref, tln_ref, lambda slot, row: copy(1 - buf, slot, row).start(),
                         rows_ref, cnt_ref, 1 - buf)

    tgts = _window_targets(meta_ref[...], tbv_ref[0], tlv_ref[0])
    gate = gate_ref[...]
    lane_s = _iota((tm, n_rows_w), 1)
    sel = jnp.zeros((tm, n_rows_w), F32)
    for k, tgt in enumerate(tgts):
        sel = sel + jnp.where(lane_s == tgt, gate[:, k:k + 1], 0.0)
    sel_hi, sel_lo = _split2(sel)
    _for_recorded_windows(lambda slot, row: copy(buf, slot, row).wait(), rows_ref, cnt_ref, buf)
    wb = win_ref[buf]
    y = x1_ref[...] + (_mm(sel_hi, wb) + _mm(sel_lo, wb))
    o_ref[...] = y * lax.rsqrt(jnp.mean(y * y, axis=-1, keepdims=True) + NORM_EPS) * gf_ref[...]


def _combine(pstart, tbase, tlen, meta, gate, x1, gf, ys, tm):
    rows, d = x1.shape
    n = rows // tm
    n_e = pstart.shape[0]
    cur3 = lambda i: (i, 0, 0)
    nxt3 = lambda i: (jnp.minimum(i + 1, n - 1), 0, 0)
    tile_smem = lambda im: pl.BlockSpec((1, 1, LANES), im, memory_space=pltpu.SMEM)
    return pl.pallas_call(
        functools.partial(_combine_kernel, n_e),
        grid=(n,),
        in_specs=[pl.BlockSpec(memory_space=pltpu.SMEM),
                  tile_smem(cur3), tile_smem(cur3), tile_smem(nxt3), tile_smem(nxt3),
                  pl.BlockSpec((1, 1, LANES), cur3), pl.BlockSpec((1, 1, LANES), cur3),
                  pl.BlockSpec((tm, 2 * TOP_K), lambda i: (i, 0)),
                  pl.BlockSpec((tm, TOP_K), lambda i: (i, 0)),
                  pl.BlockSpec((tm, d), lambda i: (i, 0)),
                  pl.BlockSpec((1, d), lambda i: (0, 0)),
                  pl.BlockSpec(memory_space=pl.ANY)],
        out_specs=pl.BlockSpec((tm, d), lambda i: (i, 0)),
        out_shape=jax.ShapeDtypeStruct((rows, d), F32),
        scratch_shapes=[pltpu.VMEM((2, _moe_slots(tm) * MOE_WIN, d), ys.dtype),
                        pltpu.SMEM((2, _moe_slots(tm)), jnp.int32), pltpu.SMEM((2,), jnp.int32),
                        pltpu.SemaphoreType.DMA((2,))],
        compiler_params=_cparams(("arbitrary",)),
        name="moe_combine",
    )(pstart, tbase, tlen, tbase, tlen, tbase, tlen, meta, gate, x1, gf, ys)


def _rope_tables(pos):
    half = ROT_DIM // 2
    inv = ROPE_THETA ** (-jnp.arange(0, ROT_DIM, 2, dtype=F32) / ROT_DIM)
    ang = inv[:, None] * pos.astype(F32)[None, :]
    cos, sin = jnp.cos(ang), jnp.sin(ang)
    n = pos.shape[0]
    pad1 = jnp.ones((HEAD_DIM - ROT_DIM, n), F32)
    pad0 = jnp.zeros((HEAD_DIM - ROT_DIM, n), F32)
    cos_h = jnp.concatenate([cos, cos, pad1], axis=0)
    sin_h = jnp.concatenate([-sin, sin, pad0], axis=0)
    reps = (LANES // HEAD_DIM, 1)
    return jnp.tile(cos_h, reps).T, jnp.tile(sin_h, reps).T


def _pairs_from_state(S):
    H = S.shape[0]
    St = jnp.swapaxes(S, 1, 2).reshape(H // 2, 2, HEAD_DIM, HEAD_DIM)
    z = jnp.zeros_like(St[:, 0])
    top = jnp.concatenate([St[:, 0], z], axis=2)
    bot = jnp.concatenate([z, St[:, 1]], axis=2)
    return jnp.concatenate([top, bot], axis=1)


def _state_from_pairs(Sp):
    a = Sp[:, :HEAD_DIM, :HEAD_DIM]
    b = Sp[:, HEAD_DIM:, HEAD_DIM:]
    St = jnp.stack([a, b], axis=1).reshape(-1, HEAD_DIM, HEAD_DIM)
    return jnp.swapaxes(St, 1, 2)


def kernel(x_prompt, x_sample, state_rwkv_wkv, state_rwkv_shift, cache_swa_k, cache_swa_v, norm1_g, w_in, mu_shift, decay_w0, decay_w2, aaa_a0, aaa_w2, gate_w2, k_k, k_a, r_k, lnx_g, lnx_b, attn_sinks, w_out, norm2_g, w_router, b_router, w_mlp1, b_mlp1, w_mlp2, b_mlp2, norm_f_g):
    depth = w_in.shape[0]
    assert depth == 1 and x_prompt.shape[0] == 1 and x_sample.shape[1] == 1
    T, d = x_prompt.shape[1], x_prompt.shape[2]
    B = x_sample.shape[0]
    past_len = PAST_LEN
    H = state_rwkv_wkv.shape[2]
    rw_w = H * HEAD_DIM
    n_pairs = H // 2
    n_q = attn_sinks.shape[1]
    n_kv = cache_swa_k.shape[3]
    q_cols = n_q * HEAD_DIM
    kv_cols = n_kv * HEAD_DIM
    rw_cols = state_rwkv_shift.shape[2]
    assert rw_cols == 3 * rw_w + 2 * HEAD_DIM + PAIR and kv_cols == LANES
    assert T % RW_TILE == 0 and B % ROW_TILE == 0 and B % 8 == 0
    wlen = cache_swa_k.shape[2]
    l = 0

    w_in_bf = w_in[l].astype(BF16)
    zero_half = jnp.zeros((HEAD_DIM, rw_w), F32)
    pp = dict(mu=mu_shift[l][None], w0=decay_w0[l][None],
              dw2=jnp.concatenate([decay_w2[l], zero_half], axis=0),
              a0=aaa_a0[l][None], aw2=jnp.concatenate([zero_half, aaa_w2[l]], axis=0),
              gw2=gate_w2[l], kk=k_k[l][None], ka=k_a[l][None], rk=r_k[l].reshape(1, rw_w),
              lng=lnx_g[l][None], lnb=lnx_b[l][None])
    w_out_bf = w_out[l].astype(BF16)
    n_e = w_router.shape[2]
    wr = jnp.pad(w_router[l], ((0, 0), (0, LANES - n_e)))
    br = jnp.concatenate([b_router[l], jnp.full((LANES - n_e,), NEG_BIG, F32)])[None]
    wp = dict(wa=w_out_bf[:rw_w], wb=w_out_bf[rw_w:], g2=norm2_g[l][None], wr=wr, br=br)
    g1 = norm1_g[l][None]

    xp = x_prompt[0]
    cos_p, sin_p = _rope_tables(jnp.arange(T))
    prw_p, q_p, k_p, v_p = _inproj(xp, g1, w_in_bf, cos_p, sin_p, 512, rw_cols, q_cols, kv_cols)
    s0_p = jnp.zeros((n_pairs, PAIR, PAIR), F32)
    shift0_p = jnp.zeros((1, rw_cols), F32)
    ya_p, sfin_p = _rwkv_prompt(prw_p, shift0_p, s0_p, pp, RW_TILE)
    yb_p = _attn_prompt(q_p, k_p, v_p, attn_sinks[l], n_q, n_kv)

    xs_ = x_sample[:, 0]
    cos_s, sin_s = _rope_tables(jnp.full((B,), past_len))
    prw_s, q_s, k_s, v_s = _inproj(xs_, g1, w_in_bf, cos_s, sin_s, ROW_TILE, rw_cols, q_cols, kv_cols)
    s_flat = state_rwkv_wkv[l].reshape(B, H * HEAD_DIM * HEAD_DIM)
    ya_s, snew_flat = _rwkv_step(prw_s, state_rwkv_shift[l], s_flat, pp, n_pairs)
    o2, kc_new, vc_new = _attn_step(q_s.reshape(B * n_q, HEAD_DIM), k_s, v_s,
                                    cache_swa_k[l].reshape(B, wlen, kv_cols),
                                    cache_swa_v[l].reshape(B, wlen, kv_cols),
                                    attn_sinks[l][:, None], n_q, n_kv, past_len)
    yb_s = o2.reshape(B, q_cols)

    rows = T + B
    x1_p, h2_p, gate_p, meta_p, tb_p, tl_p, cnt = _post(xp, ya_p, yb_p, wp, jnp.zeros((1, LANES), F32), MOE_TILE)
    x1_s, h2_s, gate_s, meta_s, tb_s, tl_s, cnt = _post(xs_, ya_s, yb_s, wp, cnt, ROW_TILE)

    counts = cnt[0, :n_e].astype(jnp.int32)
    padded = (counts + MOE_WIN + MOE_BM - 1) // MOE_BM * MOE_BM
    pend = jnp.cumsum(padded)
    pstart = (pend - padded).astype(jnp.int32)
    n_blocks = -(-(rows * TOP_K) // MOE_BM) + n_e + -(-(n_e * MOE_WIN) // MOE_BM)
    block_start = jnp.arange(n_blocks, dtype=jnp.int32) * MOE_BM
    block_e = jnp.minimum(jnp.sum((pend[None, :] <= block_start[:, None]).astype(jnp.int32), axis=1),
                          n_e - 1).astype(jnp.int32)
    n_used = (pend[-1] // MOE_BM).astype(jnp.int32)[None]
    z1 = (pstart + counts // MOE_BM * MOE_BM).astype(jnp.int32)
    z2 = (pend - MOE_BM).astype(jnp.int32)

    n_sorted = n_blocks * MOE_BM
    xs_sorted, carry = _scatter(pstart, z1, z2, tb_p, tl_p, meta_p, h2_p, None, n_sorted, MOE_BM, MOE_TILE)
    xs_sorted, _ = _scatter(pstart, z1, z2, tb_s, tl_s, meta_s, h2_s, (carry, xs_sorted), n_sorted, MOE_BM,
                            ROW_TILE)
    ys_sorted = _experts(block_e, n_used, xs_sorted, w_mlp1[l], b_mlp1[l][:, None], w_mlp2[l],
                         b_mlp2[l][:, None], MOE_BM)
    gf = norm_f_g[None]
    y_p = _combine(pstart, tb_p, tl_p, meta_p, gate_p, x1_p, gf, ys_sorted, MOE_TILE)
    y_s = _combine(pstart, tb_s, tl_s, meta_s, gate_s, x1_s, gf, ys_sorted, ROW_TILE)

    sdt = state_rwkv_wkv.dtype
    return (y_p[None], y_s[:, None],
            _state_from_pairs(sfin_p)[None, None].astype(sdt), prw_p[T - 1][None, None],
            k_p[T - min(WINDOW, T):].reshape(1, 1, -1, n_kv, HEAD_DIM),
            v_p[T - min(WINDOW, T):].reshape(1, 1, -1, n_kv, HEAD_DIM),
            snew_flat.reshape(1, B, H, HEAD_DIM, HEAD_DIM).astype(sdt), prw_s[None],
            kc_new.reshape(1, B, wlen, n_kv, HEAD_DIM), vc_new.reshape(1, B, wlen, n_kv, HEAD_DIM))
```

```python
import functools

import jax
import jax.numpy as jnp
from jax import lax
from jax.experimental import pallas as pl
from jax.experimental.pallas import tpu as pltpu

F32 = jnp.float32
BF16 = jnp.bfloat16

LANES = 128
HEAD_DIM = 64
PAIR = 2 * HEAD_DIM
CHUNK = 64
RW_TILE = 256
RW_PAIRS_PER_STEP = 4
ROT_DIM = 16
ROPE_THETA = 500000.0
WINDOW = 128
PAST_LEN = 16384
ATT_BLOCK = 128
N_EXPERTS = 32
TOP_K = 4
SWIGLU_ALPHA = 1.702
SWIGLU_LIMIT = 7.0
NORM_EPS = 1e-5
LNX_EPS = HEAD_DIM * 1e-5
MOE_BM = 512
ROW_TILE = 128
MOE_WIN_SHIFT = 5
MOE_WIN = 1 << MOE_WIN_SHIFT
SORT_ALIGN_SHIFT = 4
SORT_ALIGN = 1 << SORT_ALIGN_SHIFT
MOE_TILE = 256
NEG_BIG = -1e30
VMEM_LIMIT = 52 * 1024 * 1024

NN = (((1,), (0,)), ((), ()))
NT = (((1,), (1,)), ((), ()))


def _mm(a, b, dn=NN):
    return lax.dot_general(a, b, dn, preferred_element_type=F32)


def _split2(a):
    hi = a.astype(BF16)
    lo = (a - hi.astype(F32)).astype(BF16)
    return hi, lo


def _split3(a):
    hi = a.astype(BF16)
    r1 = a - hi.astype(F32)
    mid = r1.astype(BF16)
    lo = (r1 - mid.astype(F32)).astype(BF16)
    return hi, mid, lo


def _dot1(a, b, dn=NN):
    return _mm(a.astype(BF16), b.astype(BF16), dn)


def _dot_sel_l(sel, b, dn=NN):
    b0, b1, b2 = _split3(b)
    return _mm(sel, b0, dn) + (_mm(sel, b1, dn) + _mm(sel, b2, dn))


def _dot_sel_r(a, sel, dn=NN):
    a0, a1, a2 = _split3(a)
    return _mm(a0, sel, dn) + (_mm(a1, sel, dn) + _mm(a2, sel, dn))


def _iota(shape, dim):
    return lax.broadcasted_iota(jnp.int32, shape, dim)


def _seg_matrix():
    return ((_iota((PAIR, PAIR), 0) // HEAD_DIM) == (_iota((PAIR, PAIR), 1) // HEAD_DIM)).astype(BF16)


def _sigmoid(x):
    return 1.0 / (1.0 + jnp.exp(-x))


def _cparams(sem, vmem=VMEM_LIMIT):
    return pltpu.CompilerParams(dimension_semantics=sem, vmem_limit_bytes=vmem)


def _rope_slab(x, cos, sin_signed):
    lane = _iota(x.shape, 1) % HEAD_DIM
    up = pltpu.roll(x, LANES - ROT_DIM // 2, axis=1)
    down = pltpu.roll(x, ROT_DIM // 2, axis=1)
    partner = jnp.where(lane < ROT_DIM // 2, up, down)
    return x * cos + partner * sin_signed


def _inproj_kernel(rw_cols, q_cols, kv_cols, x_ref, g_ref, w_ref, cos_ref, sin_ref,
                   prw_ref, q_ref, k_ref, v_ref):
    x = x_ref[...]
    h = x * lax.rsqrt(jnp.mean(x * x, axis=-1, keepdims=True) + NORM_EPS) * g_ref[...]
    proj = _mm(h.astype(BF16), w_ref[...])
    prw_ref[...] = proj[:, :rw_cols]
    cos = cos_ref[...]
    sin = sin_ref[...]
    for c in range(q_cols // LANES):
        lo = rw_cols + c * LANES
        q_ref[:, c * LANES:(c + 1) * LANES] = _rope_slab(proj[:, lo:lo + LANES], cos, sin)
    ko = rw_cols + q_cols
    for c in range(kv_cols // LANES):
        k_ref[:, c * LANES:(c + 1) * LANES] = _rope_slab(proj[:, ko + c * LANES:ko + (c + 1) * LANES], cos, sin)
    v_ref[...] = proj[:, ko + kv_cols:ko + 2 * kv_cols]


def _inproj(x, g, w_bf, cos_t, sin_t, tm, rw_cols, q_cols, kv_cols):
    rows, d = x.shape
    cols = w_bf.shape[1]
    full = lambda i: (0, 0)
    row = lambda i: (i, 0)
    return pl.pallas_call(
        functools.partial(_inproj_kernel, rw_cols, q_cols, kv_cols),
        grid=(rows // tm,),
        in_specs=[pl.BlockSpec((tm, d), row), pl.BlockSpec((1, d), full),
                  pl.BlockSpec((d, cols), full),
                  pl.BlockSpec((tm, LANES), row), pl.BlockSpec((tm, LANES), row)],
        out_specs=[pl.BlockSpec((tm, rw_cols), row), pl.BlockSpec((tm, q_cols), row),
                   pl.BlockSpec((tm, kv_cols), row), pl.BlockSpec((tm, kv_cols), row)],
        out_shape=[jax.ShapeDtypeStruct((rows, rw_cols), F32), jax.ShapeDtypeStruct((rows, q_cols), F32),
                   jax.ShapeDtypeStruct((rows, kv_cols), F32), jax.ShapeDtypeStruct((rows, kv_cols), F32)],
        compiler_params=_cparams(("parallel",)),
        name="inproj",
    )(x, g, w_bf, cos_t, sin_t)


def _rwkv_tokenwise(pr, pk, pv, plo, pg, prev_r, prev_k, prev_v, prev_lo, prev_g,
                    mu_r, mu_k, mu_v, mu_lo, mu_g, w0, dw2, a0, aw2, gw2, kkp, kap, rkp, seg):
    r = pr + (prev_r - pr) * mu_r
    k = pk + (prev_k - pk) * mu_k
    v = pv + (prev_v - pv) * mu_v
    lo = plo + (prev_lo - plo) * mu_lo
    gd = pg + (prev_g - pg) * mu_g
    z = -(w0 + _dot1(jnp.tanh(lo), dw2))
    softplus = jnp.maximum(z, 0.0) + jnp.log(1.0 + jnp.exp(-jnp.abs(z)))
    logw = -jnp.exp(-softplus - 0.5)
    a = _sigmoid(a0 + _dot1(lo, aw2))
    g = _dot1(_sigmoid(gd), gw2)
    kk = k * kkp
    nrm = jnp.sqrt(_seg_sum(kk * kk, seg))
    kk = kk / jnp.maximum(nrm, 1e-12)
    k2 = k * (1.0 + (a - 1.0) * kap)
    bonus = _seg_sum(r * k2 * rkp, seg) * v
    return r, k2, v, logw, -kk, kk * a, g, bonus


def _seg_sum(x, seg):
    xh, xl = _split2(x)
    return _mm(xh, seg) + _mm(xl, seg)


def _rwkv_finish(y, bonus, g, lng, lnb, seg):
    mu = _seg_sum(y, seg) * (1.0 / HEAD_DIM)
    d = y - mu
    var = _seg_sum(d * d, seg) * (1.0 / HEAD_DIM)
    yn = d * lax.rsqrt(var + LNX_EPS) * lng + lnb
    return (yn + bonus) * g


def _rwkv_prompt_kernel(pps, pr_ref, pk_ref, pv_ref, plo_ref, pg_ref,
                        hr_ref, hk_ref, hv_ref, hlo_ref, hg_ref,
                        s0r_ref, s0k_ref, s0v_ref, s0lo_ref, s0g_ref,
                        mur_ref, muk_ref, muv_ref, mulo_ref, mug_ref,
                        w0_ref, dw2_ref, a0_ref, aw2_ref, gw2_ref, kk_ref, ka_ref, rk_ref,
                        lng_ref, lnb_ref, sin_ref,
                        sink_ref, q_ref, kc_ref, kp_ref, vc_ref, vp_ref,
                        y_ref, sout_ref, yb_ref, st_ref):
    i = pl.program_id(1)
    n_i = pl.num_programs(1)
    tt = pr_ref.shape[0]

    @pl.when(i == 0)
    def _():
        st_ref[...] = sin_ref[...]

    n_q = q_ref.shape[1] // HEAD_DIM
    attn = _attn_prompt_stages(i, n_q, n_q // (kc_ref.shape[1] // HEAD_DIM), sink_ref,
                               q_ref, kc_ref, kp_ref, vc_ref, vp_ref, yb_ref)

    row = _iota((tt, PAIR), 0)

    def prev_of(cur, halo_row, s0_row):
        first = jnp.where(i == 0, s0_row, halo_row)
        return jnp.where(row == 0, first, pltpu.roll(cur, 1, axis=0))

    plo = plo_ref[...]
    pg = pg_ref[...]
    prev_lo = prev_of(plo, hlo_ref[7:8, :], s0lo_ref[...])
    prev_g = prev_of(pg, hg_ref[7:8, :], s0g_ref[...])
    ti = _iota((tt, tt), 0)
    tj = _iota((tt, tt), 1)
    same_chunk = (ti // CHUNK) == (tj // CHUNK)
    incl = same_chunk & (tj <= ti)
    strict = same_chunk & (tj < ti)
    seg = _seg_matrix()
    lane = _iota((tt, PAIR), 1)
    eye = (ti == tj).astype(F32)
    pairs = []
    for p in range(pps):
        ls = slice(p * PAIR, (p + 1) * PAIR)
        pr, pk, pv = pr_ref[:, ls], pk_ref[:, ls], pv_ref[:, ls]
        r, k2, v, logw, nkk, b, g, bonus = _rwkv_tokenwise(
            pr, pk, pv, plo, pg,
            prev_of(pr, hr_ref[7:8, ls], s0r_ref[:, ls]), prev_of(pk, hk_ref[7:8, ls], s0k_ref[:, ls]),
            prev_of(pv, hv_ref[7:8, ls], s0v_ref[:, ls]), prev_lo, prev_g,
            mur_ref[:, ls], muk_ref[:, ls], muv_ref[:, ls], mulo_ref[...], mug_ref[...],
            w0_ref[:, ls], dw2_ref[:, ls], a0_ref[:, ls], aw2_ref[:, ls], gw2_ref[:, ls],
            kk_ref[:, ls], ka_ref[:, ls], rk_ref[:, ls], seg)
        pairs.append(dict(ls=ls, r=r, k2=k2, v=v, logw=logw, nkk=nkk, b=b, g=g, bonus=bonus))

    incl_b = incl.astype(BF16)
    for q in pairs:
        q["cs"] = _dot_sel_l(incl_b, q["logw"])
    for q in pairs:
        cs = q["cs"]
        gam = jnp.exp(cs)
        inv = jnp.exp(-cs)
        q["a_t"] = jnp.exp(cs - q["logw"]) * q["nkk"]
        q["r_t"] = gam * q["r"]
        q["bt_T"] = (q["b"] * inv).T
        q["kt_T"] = (q["k2"] * inv).T
        q["gam_T"] = gam.T
        q["bk_T"] = jnp.concatenate([q["bt_T"], q["kt_T"]], axis=1).astype(BF16)
    attn[0]()
    attn[1]()

    heads = []
    for q in pairs:
        for hh in range(2):
            hm = (lane // HEAD_DIM) == hh
            heads.append(dict(q=q, a=jnp.where(hm, q["a_t"], 0.0), r=jnp.where(hm, q["r_t"], 0.0),
                              v=jnp.where(hm, q["v"], 0.0)))
    for h in heads:
        h["g"] = _mm(jnp.concatenate([h["a"], h["r"]], axis=0).astype(BF16), h["q"]["bk_T"])
    for h in heads:
        gmat = h["g"]
        l_ab = jnp.where(strict, gmat[:tt, :tt], 0.0)
        h["l_ak_m_rk"] = jnp.concatenate([jnp.where(strict, gmat[:tt, tt:], 0.0),
                                          jnp.where(incl, gmat[tt:, tt:], 0.0)], axis=0).astype(BF16)
        h["m_rb"] = jnp.where(incl, gmat[tt:, :tt], 0.0).astype(BF16)
        h["tm"] = eye + l_ab
        h["lp"] = l_ab.astype(BF16)
    for h in heads:
        h["lp"] = _mm(h["lp"], h["lp"]).astype(BF16)
    for it in range(4):
        for h in heads:
            h["both"] = _mm(jnp.concatenate([h["tm"].astype(BF16), h["lp"]], axis=0), h["lp"])
        attn[2 + it]()
        for h in heads:
            h["tm"] = h["tm"] + h["both"][:tt]
            h["lp"] = h["both"][tt:].astype(BF16)
    for h in heads:
        h["pq"] = _mm(h["l_ak_m_rk"], h["v"].astype(BF16))
        h["tm"] = h["tm"] + _mm(h["tm"].astype(BF16), h["lp"])
    attn[6]()
    for h in heads:
        h["tx"] = _mm(h["tm"].astype(BF16),
                      jnp.concatenate([h["a"], h["pq"][:tt]], axis=1).astype(BF16))
    for h in heads:
        h["rx"] = _mm(h["m_rb"], h["tx"].astype(BF16))
    for n, q in enumerate(pairs):
        h0, h1 = heads[2 * n], heads[2 * n + 1]
        q["tatp"] = (h0["tx"] + h1["tx"]).astype(BF16)
        ryc = (h0["rx"] + h1["rx"]) + jnp.concatenate([h0["r"] + h1["r"], h0["pq"][tt:] + h1["pq"][tt:]], axis=1)
        q["ry"] = ryc[:, :PAIR]
        q["yc"] = ryc[:, PAIR:]
        q["v_b"] = q["v"].astype(BF16)
        q["bt_b"] = q["bt_T"].astype(BF16)
        q["kt_b"] = q["kt_T"].astype(BF16)
        q["s"] = st_ref[n]

    bd = seg.astype(F32)
    eye_p = (_iota((PAIR, PAIR), 0) == _iota((PAIR, PAIR), 1)).astype(F32)
    col_t = _iota((PAIR, tt), 1)
    zb = jnp.zeros((PAIR, tt), BF16)
    n_chunks = tt // CHUNK
    for c in range(n_chunks):
        cm = (col_t // CHUNK) == c
        for q in pairs:
            bt_c = jnp.where(cm, q["bt_b"], zb)
            kt_c = jnp.where(cm, q["kt_b"], zb)
            dcol = q["gam_T"][:, (c + 1) * CHUNK - 1:(c + 1) * CHUNK]
            bx = _mm(bt_c, q["tatp"])
            q["mc", c] = (dcol * (eye_p + bd * bx[:, :PAIR])).astype(BF16)
            q["nc", c] = dcol * (bd * (bx[:, PAIR:] + _mm(kt_c, q["v_b"])))
    for c in range(n_chunks):
        sl = slice(c * CHUNK, (c + 1) * CHUNK)
        for q in pairs:
            s_b = q["s"].astype(BF16)
            q["y", c] = _mm(q["ry"][sl].astype(BF16), s_b) + q["yc"][sl]
            q["s"] = _mm(q["mc", c], s_b) + q["nc", c]
    for q in pairs:
        y = jnp.concatenate([q["y", c] for c in range(n_chunks)], axis=0)
        y_ref[:, q["ls"]] = _rwkv_finish(y, q["bonus"], q["g"], lng_ref[:, q["ls"]], lnb_ref[:, q["ls"]], seg)
    for n, q in enumerate(pairs):
        st_ref[n] = q["s"]

    @pl.when(i == n_i - 1)
    def _():
        sout_ref[...] = st_ref[...]


def _mixers_prompt(prw, shift0, s0_pairs, pp, q, k, v, sinks, tt):
    T = prw.shape[0]
    n_pairs = s0_pairs.shape[0]
    pps = RW_PAIRS_PER_STEP
    n_grp = n_pairs // pps
    assert n_grp == 1 and tt % ATT_BLOCK == 0
    qw, kvw = q.shape[1], k.shape[1]
    ab = tt // ATT_BLOCK
    gw = pps * PAIR
    wcols = n_pairs * PAIR
    lo_col = 3 * wcols
    g_col = lo_col + PAIR
    hb = tt // 8

    def cur(off):
        return pl.BlockSpec((tt, gw), lambda p, i: (i, off // gw + p))

    def cur_fixed(col):
        return pl.BlockSpec((tt, PAIR), lambda p, i: (i, col // PAIR))

    def halo(off):
        return pl.BlockSpec((8, gw), lambda p, i: (jnp.maximum(i * hb - 1, 0), off // gw + p))

    def halo_fixed(col):
        return pl.BlockSpec((8, PAIR), lambda p, i: (jnp.maximum(i * hb - 1, 0), col // PAIR))

    def vec(off):
        return pl.BlockSpec((1, gw), lambda p, i: (0, off // gw + p))

    def vec_fixed(col):
        return pl.BlockSpec((1, PAIR), lambda p, i: (0, col // PAIR))

    def wmat(rows):
        return pl.BlockSpec((rows, gw), lambda p, i: (0, p))

    in_specs = ([cur(0), cur(wcols), cur(2 * wcols), cur_fixed(lo_col), cur_fixed(g_col)]
                + [halo(0), halo(wcols), halo(2 * wcols), halo_fixed(lo_col), halo_fixed(g_col)]
                + [vec(0), vec(wcols), vec(2 * wcols), vec_fixed(lo_col), vec_fixed(g_col)]
                + [vec(0), vec(wcols), vec(2 * wcols), vec_fixed(lo_col), vec_fixed(g_col)]
                + [vec(0), wmat(PAIR), vec(0), wmat(PAIR), wmat(PAIR), vec(0), vec(0), vec(0), vec(0), vec(0)]
                + [pl.BlockSpec((pps, PAIR, PAIR), lambda p, i: (p, 0, 0))])
    tile = lambda p, i: (i, 0)
    before = lambda p, i: (jnp.maximum(i * ab - 1, 0), 0)
    in_specs += [pl.BlockSpec(memory_space=pltpu.SMEM), pl.BlockSpec((tt, qw), tile),
                 pl.BlockSpec((tt, kvw), tile), pl.BlockSpec((ATT_BLOCK, kvw), before),
                 pl.BlockSpec((tt, kvw), tile), pl.BlockSpec((ATT_BLOCK, kvw), before)]
    args = ([prw] * 5 + [prw] * 5 + [shift0] * 5 + [pp["mu"]] * 5
            + [pp["w0"], pp["dw2"], pp["a0"], pp["aw2"], pp["gw2"], pp["kk"], pp["ka"], pp["rk"],
               pp["lng"], pp["lnb"], s0_pairs]
            + [sinks, q, k, k, v, v])
    return pl.pallas_call(
        functools.partial(_rwkv_prompt_kernel, pps),
        grid=(n_grp, T // tt),
        in_specs=in_specs,
        out_specs=[pl.BlockSpec((tt, gw), lambda p, i: (i, p)),
                   pl.BlockSpec((pps, PAIR, PAIR), lambda p, i: (p, 0, 0)),
                   pl.BlockSpec((tt, qw), tile)],
        out_shape=[jax.ShapeDtypeStruct((T, wcols), F32),
                   jax.ShapeDtypeStruct((n_pairs, PAIR, PAIR), F32),
                   jax.ShapeDtypeStruct((T, qw), F32)],
        scratch_shapes=[pltpu.VMEM((pps, PAIR, PAIR), F32)],
        compiler_params=_cparams(("parallel", "arbitrary")),
        name="mixers_prompt",
    )(*args)


def _rwkv_step_kernel(slabs_per_step, pr_ref, pk_ref, pv_ref, plo_ref, pg_ref,
                      sr_ref, sk_ref, sv_ref, slo_ref, sg_ref,
                      mur_ref, muk_ref, muv_ref, mulo_ref, mug_ref,
                      w0_ref, dw2_ref, a0_ref, aw2_ref, gw2_ref, kk_ref, ka_ref, rk_ref,
                      lng_ref, lnb_ref, s_ref,
                      y_ref, snew_ref, yacc_ref):
    j = pl.program_id(1)
    n_j = pl.num_programs(1)
    seg = _seg_matrix()
    r, k2, v, logw, nkk, b, g, bonus = _rwkv_tokenwise(
        pr_ref[...], pk_ref[...], pv_ref[...], plo_ref[...], pg_ref[...],
        sr_ref[...], sk_ref[...], sv_ref[...], slo_ref[...], sg_ref[...],
        mur_ref[...], muk_ref[...], muv_ref[...], mulo_ref[...], mug_ref[...],
        w0_ref[...], dw2_ref[...], a0_ref[...], aw2_ref[...], gw2_ref[...],
        kk_ref[...], ka_ref[...], rk_ref[...], seg)
    w = jnp.exp(logw)

    @pl.when(j == 0)
    def _():
        yacc_ref[...] = jnp.zeros_like(yacc_ref)

    slabs_per_head = HEAD_DIM // 2
    ci = _iota((PAIR, PAIR), 0)
    li = _iota((PAIR, PAIR), 1)
    assert slabs_per_head % slabs_per_step == 0
    yacc = yacc_ref[...]
    hh = (j * slabs_per_step) // slabs_per_head
    dup = ((ci == hh * HEAD_DIM + li % HEAD_DIM)).astype(BF16)
    nkk_d, w_d, b_d, k_d, r_d = [_dot_sel_r(x, dup) for x in (nkk, w, b, k2, r)]
    slabs = range(slabs_per_step)
    i0 = [2 * ((j * slabs_per_step + t) % slabs_per_head) for t in slabs]
    s = [s_ref[:, t * PAIR:(t + 1) * PAIR] for t in slabs]
    sa = [_seg_sum(s[t] * nkk_d, seg) for t in slabs]
    v_bc = [_seg_sum(v, (ci == hh * HEAD_DIM + i0[t] + li // HEAD_DIM).astype(BF16)) for t in slabs]
    s_new = [s[t] * w_d + sa[t] * b_d + v_bc[t] * k_d for t in slabs]
    for t in slabs:
        snew_ref[:, t * PAIR:(t + 1) * PAIR] = s_new[t]
    yred = [_seg_sum(s_new[t] * r_d, seg) for t in slabs]
    ysel = [_seg_sum(yred[t], ((ci % HEAD_DIM == 0)
                               & (li == hh * HEAD_DIM + i0[t] + ci // HEAD_DIM)).astype(BF16)) for t in slabs]
    for t in slabs:
        yacc = yacc + ysel[t]
    yacc_ref[...] = yacc

    @pl.when(j == n_j - 1)
    def _():
        y_ref[...] = _rwkv_finish(yacc, bonus, g, lng_ref[...], lnb_ref[...], seg)


def _rwkv_step(prw, shift, s_flat, pp, n_pairs):
    B = prw.shape[0]
    lanes_per_pair = 2 * HEAD_DIM * HEAD_DIM
    blk = 1024
    slabs_per_step = blk // PAIR
    steps = lanes_per_pair // blk
    lo_blk = 3 * n_pairs
    g_blk = lo_blk + 1

    def cur(off):
        return pl.BlockSpec((B, PAIR), lambda p, j: (0, off + p))

    def cur_fixed(b_):
        return pl.BlockSpec((B, PAIR), lambda p, j: (0, b_))

    def vec(off):
        return pl.BlockSpec((1, PAIR), lambda p, j: (0, off + p))

    def vec_fixed(b_):
        return pl.BlockSpec((1, PAIR), lambda p, j: (0, b_))

    def wmat(rows):
        return pl.BlockSpec((rows, PAIR), lambda p, j: (0, p))

    sspec = pl.BlockSpec((B, blk), lambda p, j: (0, p * steps + j))
    in_specs = ([cur(0), cur(n_pairs), cur(2 * n_pairs), cur_fixed(lo_blk), cur_fixed(g_blk)] * 2
                + [vec(0), vec(n_pairs), vec(2 * n_pairs), vec_fixed(lo_blk), vec_fixed(g_blk)]
                + [vec(0), wmat(PAIR), vec(0), wmat(PAIR), wmat(PAIR), vec(0), vec(0), vec(0), vec(0), vec(0)]
                + [sspec])
    args = ([prw] * 5 + [shift] * 5 + [pp["mu"]] * 5
            + [pp["w0"], pp["dw2"], pp["a0"], pp["aw2"], pp["gw2"], pp["kk"], pp["ka"], pp["rk"],
               pp["lng"], pp["lnb"], s_flat])
    return pl.pallas_call(
        functools.partial(_rwkv_step_kernel, slabs_per_step),
        grid=(n_pairs, steps),
        in_specs=in_specs,
        out_specs=[pl.BlockSpec((B, PAIR), lambda p, j: (0, p)), sspec],
        out_shape=[jax.ShapeDtypeStruct((B, n_pairs * PAIR), F32),
                   jax.ShapeDtypeStruct(s_flat.shape, F32)],
        scratch_shapes=[pltpu.VMEM((B, PAIR), F32)],
        compiler_params=_cparams(("parallel", "arbitrary")),
        name="rwkv_step",
    )(*args)


def _attn_prompt_stages(tile_idx, n_q, group, sink_ref, q_ref, kc_ref, kp_ref, vc_ref, vp_ref, o_ref):
    blk = ATT_BLOCK
    n_blk = q_ref.shape[0] // blk
    n_kv = n_q // group
    inst = [(j, h) for j in range(n_blk) for h in range(n_q)]
    st = {}

    def prepare():
        q = q_ref[...] * (HEAD_DIM ** -0.5)
        kc = kc_ref[...]
        vc = vc_ref[...]
        kall = jnp.concatenate([kp_ref[...], kc], axis=0)
        vall = jnp.concatenate([vp_ref[...], vc], axis=0)
        rq = _iota((blk, 2 * blk), 0)
        ck = _iota((blk, 2 * blk), 1)
        dist = rq - ck + blk
        in_window = (dist >= 0) & (dist < WINDOW)
        kpos0 = tile_idx * (n_blk * blk) - blk + ck
        st["valid"] = [in_window & (kpos0 >= 0)] + [in_window] * (n_blk - 1)
        st["q"] = {(j, h): q[j * blk:(j + 1) * blk, h * HEAD_DIM:(h + 1) * HEAD_DIM].astype(BF16) for j, h in inst}
        st["kb"] = {(j, g): kall[j * blk:(j + 2) * blk, g * HEAD_DIM:(g + 1) * HEAD_DIM].astype(BF16)
                    for j in range(n_blk) for g in range(n_kv)}
        st["vb"] = {(j, g): vall[j * blk:(j + 2) * blk, g * HEAD_DIM:(g + 1) * HEAD_DIM].astype(BF16)
                    for j in range(n_blk) for g in range(n_kv)}

    def scores():
        st["s"] = {(j, h): jnp.where(st["valid"][j], _mm(st["q"][j, h], st["kb"][j, h // group], NT), NEG_BIG)
                   for j, h in inst}

    def row_max():
        st["m"] = {(j, h): jnp.maximum(jnp.max(st["s"][j, h], axis=-1, keepdims=True), sink_ref[h]) for j, h in inst}

    def probs():
        st["p"] = {(j, h): jnp.exp(st["s"][j, h] - st["m"][j, h]) for j, h in inst}

    def denominators():
        st["d"] = {(j, h): jnp.sum(st["p"][j, h], axis=-1, keepdims=True) + jnp.exp(sink_ref[h] - st["m"][j, h])
                   for j, h in inst}

    def weighted_values():
        st["o"] = {(j, h): _mm(st["p"][j, h].astype(BF16), st["vb"][j, h // group]) for j, h in inst}

    def store():
        for j, h in inst:
            o_ref[j * blk:(j + 1) * blk, h * HEAD_DIM:(h + 1) * HEAD_DIM] = st["o"][j, h] / st["d"][j, h]

    return [prepare, scores, row_max, probs, denominators, weighted_values, store]


def _attn_step_kernel(n_q, group, pos0, sink_ref, q_ref, kn_ref, vn_ref, kc_ref, vc_ref,
                      o_ref, ko_ref, vo_ref):
    bb, wlen, kvw = kc_ref.shape
    lane = _iota((n_q, kvw), 1)
    rowh = _iota((n_q, kvw), 0)
    mine = (lane // HEAD_DIM) == (rowh // group)
    dupm = (_iota((HEAD_DIM, kvw), 0) == _iota((HEAD_DIM, kvw), 1) % HEAD_DIM).astype(BF16)
    fold = (_iota((kvw, HEAD_DIM), 0) % HEAD_DIM == _iota((kvw, HEAD_DIM), 1)).astype(BF16)
    kidx = _iota((n_q, wlen), 1)
    dist = wlen - kidx
    valid = (dist < WINDOW) & (pos0 - dist >= 0)
    rk = _iota((wlen, kvw), 0)
    sink = sink_ref[...]
    for t in range(bb):
        qh = q_ref[t * n_q:(t + 1) * n_q, :] * (HEAD_DIM ** -0.5)
        qm = jnp.where(mine, _dot_sel_r(qh, dupm), 0.0)
        kc = kc_ref[t]
        vc = vc_ref[t]
        kn = kn_ref[t:t + 1, :]
        vn = vn_ref[t:t + 1, :]
        s = jnp.where(valid, _dot1(qm, kc, NT), NEG_BIG)
        s_new = jnp.sum(qm * kn, axis=-1, keepdims=True)
        m = jnp.maximum(jnp.maximum(jnp.max(s, axis=-1, keepdims=True), s_new), sink)
        p = jnp.exp(s - m)
        p_new = jnp.exp(s_new - m)
        denom = jnp.sum(p, axis=-1, keepdims=True) + p_new + jnp.exp(sink - m)
        res = (_dot1(p, vc) + p_new * vn) / denom
        o_ref[t * n_q:(t + 1) * n_q, :] = _dot_sel_r(jnp.where(mine, res, 0.0), fold)
        ko_ref[t] = jnp.where(rk == wlen - 1, kn, pltpu.roll(kc, wlen - 1, axis=0))
        vo_ref[t] = jnp.where(rk == wlen - 1, vn, pltpu.roll(vc, wlen - 1, axis=0))


def _attn_step(q2, k_new, v_new, k_cache, v_cache, sinks_col, n_q, n_kv, pos0):
    B, wlen, kvw = k_cache.shape
    bb = 8
    return pl.pallas_call(
        functools.partial(_attn_step_kernel, n_q, n_q // n_kv, pos0),
        grid=(B // bb,),
        in_specs=[pl.BlockSpec((n_q, 1), lambda i: (0, 0)),
                  pl.BlockSpec((bb * n_q, HEAD_DIM), lambda i: (i, 0)),
                  pl.BlockSpec((bb, kvw), lambda i: (i, 0)), pl.BlockSpec((bb, kvw), lambda i: (i, 0)),
                  pl.BlockSpec((bb, wlen, kvw), lambda i: (i, 0, 0)),
                  pl.BlockSpec((bb, wlen, kvw), lambda i: (i, 0, 0))],
        out_specs=[pl.BlockSpec((bb * n_q, HEAD_DIM), lambda i: (i, 0)),
                   pl.BlockSpec((bb, wlen, kvw), lambda i: (i, 0, 0)),
                   pl.BlockSpec((bb, wlen, kvw), lambda i: (i, 0, 0))],
        out_shape=[jax.ShapeDtypeStruct((B * n_q, HEAD_DIM), F32),
                   jax.ShapeDtypeStruct((B, wlen, kvw), F32),
                   jax.ShapeDtypeStruct((B, wlen, kvw), F32)],
        compiler_params=_cparams(("parallel",)),
        name="attn_step",
    )(sinks_col, q2, k_new, v_new, k_cache, v_cache)


def _post_kernel(x_ref, ya_ref, yb_ref, wa_ref, wb_ref, g2_ref, wr_ref, br_ref, cnt0_ref,
                 x1_ref, h2_ref, gate_ref, meta_ref, tb_ref, tl_ref, cnt_ref, carry_ref):
    i = pl.program_id(0)

    @pl.when(i == 0)
    def _():
        carry_ref[...] = cnt0_ref[...]

    mix = _mm(ya_ref[...].astype(BF16), wa_ref[...]) + _mm(yb_ref[...].astype(BF16), wb_ref[...])
    x1 = x_ref[...] + mix
    h2 = x1 * lax.rsqrt(jnp.mean(x1 * x1, axis=-1, keepdims=True) + NORM_EPS) * g2_ref[...]
    x1_ref[...] = x1
    h2_ref[...] = h2

    l = _dot1(h2, wr_ref[...]) + br_ref[...]
    tm = l.shape[0]
    lane = _iota(l.shape, 1)
    vals, idxs = [], []
    for _ in range(TOP_K):
        m = jnp.max(l, axis=-1, keepdims=True)
        sel = jnp.min(jnp.where(l == m, lane, LANES), axis=-1, keepdims=True)
        vals.append(m)
        idxs.append(sel)
        l = jnp.where(lane == sel, -jnp.inf, l)
    es = [jnp.exp(v - vals[0]) for v in vals]
    tot = es[0] + es[1] + es[2] + es[3]
    onehot = jnp.zeros(l.shape, F32)
    for sel in idxs:
        onehot = onehot + (lane == sel).astype(F32)
    strict = (_iota((tm, tm), 1) < _iota((tm, tm), 0)).astype(BF16)
    before = _mm(strict, onehot.astype(BF16))
    for k in range(TOP_K):
        gate_ref[:, k:k + 1] = es[k] / tot
        meta_ref[:, k:k + 1] = idxs[k]
        meta_ref[:, TOP_K + k:TOP_K + k + 1] = jnp.sum(
            jnp.where(lane == idxs[k], before, 0.0), axis=-1, keepdims=True).astype(jnp.int32)
    carry = carry_ref[...]
    cnt_t = jnp.sum(onehot, axis=0, keepdims=True)
    tb_ref[0] = carry.astype(jnp.int32)
    tl_ref[0] = cnt_t.astype(jnp.int32)
    carry_ref[...] = carry + cnt_t
    cnt_ref[...] = carry + cnt_t


def _post(x, ya, yb, wp, cnt0, tm):
    rows, d = x.shape
    half = ya.shape[1]
    n_t = rows // tm
    full = lambda i: (0, 0)
    row = lambda i: (i, 0)
    trow = lambda i: (i, 0, 0)
    return pl.pallas_call(
        _post_kernel,
        grid=(n_t,),
        in_specs=[pl.BlockSpec((tm, d), row), pl.BlockSpec((tm, half), row), pl.BlockSpec((tm, half), row),
                  pl.BlockSpec((half, d), full), pl.BlockSpec((half, d), full),
                  pl.BlockSpec((1, d), full), pl.BlockSpec((d, LANES), full), pl.BlockSpec((1, LANES), full),
                  pl.BlockSpec((1, LANES), full)],
        out_specs=[pl.BlockSpec((tm, d), row), pl.BlockSpec((tm, d), row),
                   pl.BlockSpec((tm, TOP_K), row), pl.BlockSpec((tm, 2 * TOP_K), row),
                   pl.BlockSpec((1, 1, LANES), trow), pl.BlockSpec((1, 1, LANES), trow),
                   pl.BlockSpec((1, LANES), full)],
        out_shape=[jax.ShapeDtypeStruct((rows, d), F32), jax.ShapeDtypeStruct((rows, d), F32),
                   jax.ShapeDtypeStruct((rows, TOP_K), F32),
                   jax.ShapeDtypeStruct((rows, 2 * TOP_K), jnp.int32),
                   jax.ShapeDtypeStruct((n_t, 1, LANES), jnp.int32),
                   jax.ShapeDtypeStruct((n_t, 1, LANES), jnp.int32),
                   jax.ShapeDtypeStruct((1, LANES), F32)],
        scratch_shapes=[pltpu.VMEM((1, LANES), F32)],
        compiler_params=_cparams(("arbitrary",)),
        name="post",
    )(x, ya, yb, wp["wa"], wp["wb"], wp["g2"], wp["wr"], wp["br"], cnt0)


def _n_windows(base, length):
    off = base & (SORT_ALIGN - 1)
    n = lax.shift_right_logical(off + length + (MOE_WIN - 1), MOE_WIN_SHIFT)
    return off, jnp.where(length > 0, n, 0)


def _window_targets(meta, tb_vec, tl_vec):
    tm = meta.shape[0]
    off, n_win = _n_windows(tb_vec, tl_vec)
    upper = (_iota((LANES, LANES), 0) < _iota((LANES, LANES), 1)).astype(BF16)
    slot_start = _mm(n_win.astype(F32).astype(BF16), upper)
    pos0 = slot_start * MOE_WIN + off.astype(F32)
    lane = _iota((tm, LANES), 1)
    tgts = []
    for k in range(TOP_K):
        p0 = jnp.sum(jnp.where(lane == meta[:, k:k + 1], pos0, 0.0), axis=-1, keepdims=True)
        tgts.append(p0.astype(jnp.int32) + meta[:, TOP_K + k:TOP_K + k + 1])
    return tgts


def _for_each_window(n_e, pstart_ref, tb_ref, tl_ref, fn, rows_ref, cnt_ref, b, per_expert_fn=None):
    def per_expert(e, slot0):
        base = tb_ref[0, 0, e]
        length = tl_ref[0, 0, e]
        off, n = _n_windows(base, length)
        row0 = pstart_ref[e] + base - off
        if per_expert_fn is not None:
            per_expert_fn(e, slot0, off, length, n)

        def per_window(w, c):
            row = row0 + w * MOE_WIN
            rows_ref[b, slot0 + w] = row
            fn(slot0 + w, pl.multiple_of(row, SORT_ALIGN))
            return c

        lax.fori_loop(0, n, per_window, 0)
        return slot0 + n

    cnt_ref[b] = lax.fori_loop(0, n_e, per_expert, 0)


def _for_recorded_windows(fn, rows_ref, cnt_ref, b):
    def body(slot, c):
        fn(slot, pl.multiple_of(rows_ref[b, slot], SORT_ALIGN))
        return c

    lax.fori_loop(0, cnt_ref[b], body, 0)


def _moe_slots(tm):
    n = -(-(tm * TOP_K + N_EXPERTS * (SORT_ALIGN - 1 + MOE_WIN - 1)) // MOE_WIN)
    per_lane_tile = LANES // MOE_WIN
    return -(-n // per_lane_tile) * per_lane_tile


def _scatter_kernel(n_e, continues, pstart_ref, z1_ref, z2_ref, tb_ref, tl_ref, tbv_ref, tlv_ref,
                    meta_ref, h_ref, *rest):
    if continues:
        cin_ref, _xs_alias, xs_ref, cout_ref, xw_ref, zero_ref, carry_ref, rows_ref, cnt_ref, sem, zsem = rest
    else:
        xs_ref, cout_ref, xw_ref, zero_ref, carry_ref, rows_ref, cnt_ref, sem, zsem = rest
    i = pl.program_id(0)
    n_i = pl.num_programs(0)
    tm = h_ref.shape[0]
    bm = zero_ref.shape[0]
    buf = i % 2
    n_rows_w = xw_ref.shape[1]

    @pl.when(i == 0)
    def _():
        if continues:
            carry_ref[...] = cin_ref[...]
        else:
            zero_ref[...] = jnp.zeros_like(zero_ref)

            def zcopy(row):
                return pltpu.make_async_copy(zero_ref, xs_ref.at[pl.ds(pl.multiple_of(row, SORT_ALIGN), bm)], zsem)

            for e in range(n_e):
                zcopy(z1_ref[e]).start()

                @pl.when(z2_ref[e] != z1_ref[e])
                def _():
                    zcopy(z2_ref[e]).start()
            for e in range(n_e):
                zcopy(z1_ref[e]).wait()

                @pl.when(z2_ref[e] != z1_ref[e])
                def _():
                    zcopy(z2_ref[e]).wait()

            carry_ref[...] = jnp.zeros_like(carry_ref)

    tgts = _window_targets(meta_ref[...], tbv_ref[0], tlv_ref[0])
    lane_s = _iota((tm, n_rows_w), 1)
    sel = jnp.zeros((tm, n_rows_w), F32)
    for tgt in tgts:
        sel = sel + (lane_s == tgt).astype(F32)
    xw_ref[buf] = _mm(sel.T.astype(BF16), h_ref[...].astype(BF16)).astype(BF16)

    def splice_carry(e, slot0, off, length, n):
        @pl.when(n > 0)
        def _():
            g0 = pl.multiple_of(slot0 * MOE_WIN, MOE_WIN)
            xw_ref[buf, pl.ds(g0, SORT_ALIGN), :] = xw_ref[buf, pl.ds(g0, SORT_ALIGN), :] + carry_ref[e]
            filled = off + length
            gl = pl.multiple_of(g0 + lax.shift_right_logical(filled, SORT_ALIGN_SHIFT) * SORT_ALIGN, SORT_ALIGN)
            last = xw_ref[buf, pl.ds(gl, SORT_ALIGN), :]
            carry_ref[e] = jnp.where((filled & (SORT_ALIGN - 1)) != 0, last, jnp.zeros_like(last))

    def copy(b, slot, row):
        return pltpu.make_async_copy(xw_ref.at[b, pl.ds(pl.multiple_of(slot * MOE_WIN, MOE_WIN), MOE_WIN)],
                                     xs_ref.at[pl.ds(row, MOE_WIN)], sem.at[b])

    @pl.when(i > 0)
    def _():
        _for_recorded_windows(lambda slot, row: copy(1 - buf, slot, row).wait(), rows_ref, cnt_ref, 1 - buf)

    _for_each_window(n_e, pstart_ref, tb_ref, tl_ref, lambda slot, row: copy(buf, slot, row).start(),
                     rows_ref, cnt_ref, buf, splice_carry)

    @pl.when(i == n_i - 1)
    def _():
        _for_recorded_windows(lambda slot, row: copy(buf, slot, row).wait(), rows_ref, cnt_ref, buf)
        cout_ref[...] = carry_ref[...]


def _scatter(pstart, z1, z2, tbase, tlen, meta, h2, prior, n_rows_sorted, bm, tm):
    rows, d = h2.shape
    n_e = pstart.shape[0]
    n_slots = _moe_slots(tm)
    smem = pl.BlockSpec(memory_space=pltpu.SMEM)
    tile3 = lambda i: (i, 0, 0)
    tile_smem = lambda im: pl.BlockSpec((1, 1, LANES), im, memory_space=pltpu.SMEM)
    carry_spec = pl.BlockSpec((n_e, SORT_ALIGN, d), lambda i: (0, 0, 0))
    in_specs = [smem, smem, smem,
                tile_smem(tile3), tile_smem(tile3),
                pl.BlockSpec((1, 1, LANES), tile3), pl.BlockSpec((1, 1, LANES), tile3),
                pl.BlockSpec((tm, 2 * TOP_K), lambda i: (i, 0)),
                pl.BlockSpec((tm, d), lambda i: (i, 0))]
    args = [pstart, z1, z2, tbase, tlen, tbase, tlen, meta, h2]
    aliases = {}
    if prior is not None:
        in_specs += [carry_spec, pl.BlockSpec(memory_space=pl.ANY)]
        aliases = {len(args) + 1: 0}
        args += list(prior)
    return pl.pallas_call(
        functools.partial(_scatter_kernel, n_e, prior is not None),
        grid=(rows // tm,),
        in_specs=in_specs,
        out_specs=[pl.BlockSpec(memory_space=pl.ANY), carry_spec],
        out_shape=[jax.ShapeDtypeStruct((n_rows_sorted, d), BF16),
                   jax.ShapeDtypeStruct((n_e, SORT_ALIGN, d), BF16)],
        input_output_aliases=aliases,
        scratch_shapes=[pltpu.VMEM((2, n_slots * MOE_WIN, d), BF16), pltpu.VMEM((bm, d), BF16),
                        pltpu.VMEM((n_e, SORT_ALIGN, d), BF16),
                        pltpu.SMEM((2, n_slots), jnp.int32), pltpu.SMEM((2,), jnp.int32),
                        pltpu.SemaphoreType.DMA((2,)), pltpu.SemaphoreType.DMA(())],
        compiler_params=_cparams(("arbitrary",)),
        name="moe_scatter",
    )(*args)


def _expert_kernel(d_ff, be_ref, nused_ref, xs_ref, w1_ref, b1_ref, w2_ref, b2_ref, ys_ref, w1b_ref, w2b_ref):
    i = pl.program_id(0)
    new_expert = jnp.logical_or(i == 0, be_ref[i] != be_ref[jnp.maximum(i - 1, 0)])

    @pl.when(jnp.logical_and(i < nused_ref[0], new_expert))
    def _():
        w1b_ref[...] = w1_ref[0].astype(BF16)
        w2b_ref[...] = w2_ref[0].astype(BF16)

    @pl.when(i < nused_ref[0])
    def _():
        h = _mm(xs_ref[...], w1b_ref[...]) + b1_ref[0]
        hg = jnp.minimum(h[:, :d_ff], SWIGLU_LIMIT)
        hu = jnp.clip(h[:, d_ff:], -SWIGLU_LIMIT, SWIGLU_LIMIT)
        act = hg * _sigmoid(SWIGLU_ALPHA * hg) * (hu + 1.0)
        ys_ref[...] = (_mm(act.astype(BF16), w2b_ref[...]) + b2_ref[0]).astype(ys_ref.dtype)

    @pl.when(i >= nused_ref[0])
    def _():
        ys_ref[...] = jnp.zeros_like(ys_ref)


def _experts(block_e, n_used, xs, w1, b1, w2, b2, bm):
    R, d = xs.shape
    d_ff = w2.shape[1]
    nb = R // bm

    def rows(i, be, nu):
        return (jnp.minimum(i, nu[0] - 1), 0)

    def wsel(i, be, nu):
        return (be[i], 0, 0)

    return pl.pallas_call(
        functools.partial(_expert_kernel, d_ff),
        grid_spec=pltpu.PrefetchScalarGridSpec(
            num_scalar_prefetch=2,
            grid=(nb,),
            in_specs=[pl.BlockSpec((bm, d), rows),
                      pl.BlockSpec((1, d, 2 * d_ff), wsel), pl.BlockSpec((1, 1, 2 * d_ff), wsel),
                      pl.BlockSpec((1, d_ff, d), wsel), pl.BlockSpec((1, 1, d), wsel)],
            out_specs=pl.BlockSpec((bm, d), lambda i, be, nu: (i, 0)),
            scratch_shapes=[pltpu.VMEM((d, 2 * d_ff), BF16), pltpu.VMEM((d_ff, d), BF16)]),
        out_shape=jax.ShapeDtypeStruct((R, d), xs.dtype),
        compiler_params=_cparams(("arbitrary",)),
        name="moe_experts",
    )(block_e, n_used, xs, w1, b1, w2, b2)


def _combine_kernel(n_e, pstart_ref, tb_ref, tl_ref, tbn_ref, tln_ref, tbv_ref, tlv_ref, meta_ref, gate_ref,
                    x1_ref, gf_ref, ys_ref, o_ref, win_ref, rows_ref, cnt_ref, sem):
    i = pl.program_id(0)
    n = pl.num_programs(0)
    tm = x1_ref.shape[0]
    buf = i % 2
    n_rows_w = win_ref.shape[1]

    def copy(b, slot, row):
        return pltpu.make_async_copy(
            ys_ref.at[pl.ds(row, MOE_WIN)],
            win_ref.at[b, pl.ds(pl.multiple_of(slot * MOE_WIN, MOE_WIN), MOE_WIN)], sem.at[b])

    @pl.when(i == 0)
    def _():
        win_ref[...] = jnp.zeros_like(win_ref)
        _for_each_window(n_e, pstart_ref, tb_ref, tl_ref, lambda slot, row: copy(buf, slot, row).start(),
                         rows_ref, cnt_ref, buf)

    @pl.when(i + 1 < n)
    def _():
        _for_each_window(n_e, pstart_ref, tbn_ref, tln_ref, lambda slot, row: copy(1 - buf, slot, row).start(),
                         rows_ref, cnt_ref, 1 - buf)

    tgts = _window_targets(meta_ref[...], tbv_ref[0], tlv_ref[0])
    gate = gate_ref[...]
    lane_s = _iota((tm, n_rows_w), 1)
    sel = jnp.zeros((tm, n_rows_w), F32)
    for k, tgt in enumerate(tgts):
        sel = sel + jnp.where(lane_s == tgt, gate[:, k:k + 1], 0.0)
    sel_hi, sel_lo = _split2(sel)
    _for_recorded_windows(lambda slot, row: copy(buf, slot, row).wait(), rows_ref, cnt_ref, buf)
    wb = win_ref[buf]
    y = x1_ref[...] + (_mm(sel_hi, wb) + _mm(sel_lo, wb))
    o_ref[...] = y * lax.rsqrt(jnp.mean(y * y, axis=-1, keepdims=True) + NORM_EPS) * gf_ref[...]


def _combine(pstart, tbase, tlen, meta, gate, x1, gf, ys, tm):
    rows, d = x1.shape
    n = rows // tm
    n_e = pstart.shape[0]
    cur3 = lambda i: (i, 0, 0)
    nxt3 = lambda i: (jnp.minimum(i + 1, n - 1), 0, 0)
    tile_smem = lambda im: pl.BlockSpec((1, 1, LANES), im, memory_space=pltpu.SMEM)
    return pl.pallas_call(
        functools.partial(_combine_kernel, n_e),
        grid=(n,),
        in_specs=[pl.BlockSpec(memory_space=pltpu.SMEM),
                  tile_smem(cur3), tile_smem(cur3), tile_smem(nxt3), tile_smem(nxt3),
                  pl.BlockSpec((1, 1, LANES), cur3), pl.BlockSpec((1, 1, LANES), cur3),
                  pl.BlockSpec((tm, 2 * TOP_K), lambda i: (i, 0)),
                  pl.BlockSpec((tm, TOP_K), lambda i: (i, 0)),
                  pl.BlockSpec((tm, d), lambda i: (i, 0)),
                  pl.BlockSpec((1, d), lambda i: (0, 0)),
                  pl.BlockSpec(memory_space=pl.ANY)],
        out_specs=pl.BlockSpec((tm, d), lambda i: (i, 0)),
        out_shape=jax.ShapeDtypeStruct((rows, d), F32),
        scratch_shapes=[pltpu.VMEM((2, _moe_slots(tm) * MOE_WIN, d), ys.dtype),
                        pltpu.SMEM((2, _moe_slots(tm)), jnp.int32), pltpu.SMEM((2,), jnp.int32),
                        pltpu.SemaphoreType.DMA((2,))],
        compiler_params=_cparams(("arbitrary",)),
        name="moe_combine",
    )(pstart, tbase, tlen, tbase, tlen, tbase, tlen, meta, gate, x1, gf, ys)


def _rope_tables(pos):
    half = ROT_DIM // 2
    inv = ROPE_THETA ** (-jnp.arange(0, ROT_DIM, 2, dtype=F32) / ROT_DIM)
    ang = inv[:, None] * pos.astype(F32)[None, :]
    cos, sin = jnp.cos(ang), jnp.sin(ang)
    n = pos.shape[0]
    pad1 = jnp.ones((HEAD_DIM - ROT_DIM, n), F32)
    pad0 = jnp.zeros((HEAD_DIM - ROT_DIM, n), F32)
    cos_h = jnp.concatenate([cos, cos, pad1], axis=0)
    sin_h = jnp.concatenate([-sin, sin, pad0], axis=0)
    reps = (LANES // HEAD_DIM, 1)
    return jnp.tile(cos_h, reps).T, jnp.tile(sin_h, reps).T


def _pairs_from_state(S):
    H = S.shape[0]
    St = jnp.swapaxes(S, 1, 2).reshape(H // 2, 2, HEAD_DIM, HEAD_DIM)
    z = jnp.zeros_like(St[:, 0])
    top = jnp.concatenate([St[:, 0], z], axis=2)
    bot = jnp.concatenate([z, St[:, 1]], axis=2)
    return jnp.concatenate([top, bot], axis=1)


def _state_from_pairs(Sp):
    a = Sp[:, :HEAD_DIM, :HEAD_DIM]
    b = Sp[:, HEAD_DIM:, HEAD_DIM:]
    St = jnp.stack([a, b], axis=1).reshape(-1, HEAD_DIM, HEAD_DIM)
    return jnp.swapaxes(St, 1, 2)


def kernel(x_prompt, x_sample, state_rwkv_wkv, state_rwkv_shift, cache_swa_k, cache_swa_v, norm1_g, w_in, mu_shift, decay_w0, decay_w2, aaa_a0, aaa_w2, gate_w2, k_k, k_a, r_k, lnx_g, lnx_b, attn_sinks, w_out, norm2_g, w_router, b_router, w_mlp1, b_mlp1, w_mlp2, b_mlp2, norm_f_g):
    depth = w_in.shape[0]
    assert depth == 1 and x_prompt.shape[0] == 1 and x_sample.shape[1] == 1
    T, d = x_prompt.shape[1], x_prompt.shape[2]
    B = x_sample.shape[0]
    past_len = PAST_LEN
    H = state_rwkv_wkv.shape[2]
    rw_w = H * HEAD_DIM
    n_pairs = H // 2
    n_q = attn_sinks.shape[1]
    n_kv = cache_swa_k.shape[3]
    q_cols = n_q * HEAD_DIM
    kv_cols = n_kv * HEAD_DIM
    rw_cols = state_rwkv_shift.shape[2]
    assert rw_cols == 3 * rw_w + 2 * HEAD_DIM + PAIR and kv_cols == LANES
    assert T % RW_TILE == 0 and B % ROW_TILE == 0 and B % 8 == 0
    wlen = cache_swa_k.shape[2]
    l = 0

    w_in_bf = w_in[l].astype(BF16)
    zero_half = jnp.zeros((HEAD_DIM, rw_w), F32)
    pp = dict(mu=mu_shift[l][None], w0=decay_w0[l][None],
              dw2=jnp.concatenate([decay_w2[l], zero_half], axis=0),
              a0=aaa_a0[l][None], aw2=jnp.concatenate([zero_half, aaa_w2[l]], axis=0),
              gw2=gate_w2[l], kk=k_k[l][None], ka=k_a[l][None], rk=r_k[l].reshape(1, rw_w),
              lng=lnx_g[l][None], lnb=lnx_b[l][None])
    w_out_bf = w_out[l].astype(BF16)
    n_e = w_router.shape[2]
    wr = jnp.pad(w_router[l], ((0, 0), (0, LANES - n_e)))
    br = jnp.concatenate([b_router[l], jnp.full((LANES - n_e,), NEG_BIG, F32)])[None]
    wp = dict(wa=w_out_bf[:rw_w], wb=w_out_bf[rw_w:], g2=norm2_g[l][None], wr=wr, br=br)
    g1 = norm1_g[l][None]

    xp = x_prompt[0]
    cos_p, sin_p = _rope_tables(jnp.arange(T))
    prw_p, q_p, k_p, v_p = _inproj(xp, g1, w_in_bf, cos_p, sin_p, 512, rw_cols, q_cols, kv_cols)
    s0_p = jnp.zeros((n_pairs, PAIR, PAIR), F32)
    shift0_p = jnp.zeros((1, rw_cols), F32)
    ya_p, sfin_p, yb_p = _mixers_prompt(prw_p, shift0_p, s0_p, pp, q_p, k_p, v_p, attn_sinks[l], RW_TILE)

    xs_ = x_sample[:, 0]
    cos_s, sin_s = _rope_tables(jnp.full((B,), past_len))
    prw_s, q_s, k_s, v_s = _inproj(xs_, g1, w_in_bf, cos_s, sin_s, ROW_TILE, rw_cols, q_cols, kv_cols)
    s_flat = state_rwkv_wkv[l].reshape(B, H * HEAD_DIM * HEAD_DIM)
    ya_s, snew_flat = _rwkv_step(prw_s, state_rwkv_shift[l], s_flat, pp, n_pairs)
    o2, kc_new, vc_new = _attn_step(q_s.reshape(B * n_q, HEAD_DIM), k_s, v_s,
                                    cache_swa_k[l].reshape(B, wlen, kv_cols),
                                    cache_swa_v[l].reshape(B, wlen, kv_cols),
                                    attn_sinks[l][:, None], n_q, n_kv, past_len)
    yb_s = o2.reshape(B, q_cols)

    rows = T + B
    x1_p, h2_p, gate_p, meta_p, tb_p, tl_p, cnt = _post(xp, ya_p, yb_p, wp, jnp.zeros((1, LANES), F32), MOE_TILE)
    x1_s, h2_s, gate_s, meta_s, tb_s, tl_s, cnt = _post(xs_, ya_s, yb_s, wp, cnt, ROW_TILE)

    counts = cnt[0, :n_e].astype(jnp.int32)
    padded = (counts + MOE_WIN + MOE_BM - 1) // MOE_BM * MOE_BM
    pend = jnp.cumsum(padded)
    pstart = (pend - padded).astype(jnp.int32)
    n_blocks = -(-(rows * TOP_K) // MOE_BM) + n_e + -(-(n_e * MOE_WIN) // MOE_BM)
    block_start = jnp.arange(n_blocks, dtype=jnp.int32) * MOE_BM
    block_e = jnp.minimum(jnp.sum((pend[None, :] <= block_start[:, None]).astype(jnp.int32), axis=1),
                          n_e - 1).astype(jnp.int32)
    n_used = (pend[-1] // MOE_BM).astype(jnp.int32)[None]
    z1 = (pstart + counts // MOE_BM * MOE_BM).astype(jnp.int32)
    z2 = (pend - MOE_BM).astype(jnp.int32)

    n_sorted = n_blocks * MOE_BM
    xs_sorted, carry = _scatter(pstart, z1, z2, tb_p, tl_p, meta_p, h2_p, None, n_sorted, MOE_BM, MOE_TILE)
    xs_sorted, _ = _scatter(pstart, z1, z2, tb_s, tl_s, meta_s, h2_s, (carry, xs_sorted), n_sorted, MOE_BM,
                            ROW_TILE)
    ys_sorted = _experts(block_e, n_used, xs_sorted, w_mlp1[l], b_mlp1[l][:, None], w_mlp2[l],
                         b_mlp2[l][:, None], MOE_BM)
    gf = norm_f_g[None]
    y_p = _combine(pstart, tb_p, tl_p, meta_p, gate_p, x1_p, gf, ys_sorted, MOE_TILE)
    y_s = _combine(pstart, tb_s, tl_s, meta_s, gate_s, x1_s, gf, ys_sorted, ROW_TILE)

    sdt = state_rwkv_wkv.dtype
    return (y_p[None], y_s[:, None],
            _state_from_pairs(sfin_p)[None, None].astype(sdt), prw_p[T - 1][None, None],
            k_p[T - min(WINDOW, T):].reshape(1, 1, -1, n_kv, HEAD_DIM),
            v_p[T - min(WINDOW, T):].reshape(1, 1, -1, n_kv, HEAD_DIM),
            snew_flat.reshape(1, B, H, HEAD_DIM, HEAD_DIM).astype(sdt), prw_s[None],
            kc_new.reshape(1, B, wlen, n_kv, HEAD_DIM), vc_new.reshape(1, B, wlen, n_kv, HEAD_DIM))
```

```python
import functools

import jax
import jax.numpy as jnp
from jax import lax
from jax.experimental import pallas as pl
from jax.experimental.pallas import tpu as pltpu

F32 = jnp.float32
BF16 = jnp.bfloat16

LANES = 128
HEAD_DIM = 64
PAIR = 2 * HEAD_DIM
CHUNK = 64
RW_TILE = 256
RW_PAIRS_PER_STEP = 4
ROT_DIM = 16
ROPE_THETA = 500000.0
WINDOW = 128
PAST_LEN = 16384
ATT_BLOCK = 128
N_EXPERTS = 32
TOP_K = 4
SWIGLU_ALPHA = 1.702
SWIGLU_LIMIT = 7.0
NORM_EPS = 1e-5
LNX_EPS = HEAD_DIM * 1e-5
MOE_BM = 512
ROW_TILE = 128
MOE_WIN_SHIFT = 4
MOE_WIN = 1 << MOE_WIN_SHIFT
SORT_ALIGN_SHIFT = 4
SORT_ALIGN = 1 << SORT_ALIGN_SHIFT
MOE_TILE = 256
NEG_BIG = -1e30
VMEM_LIMIT = 52 * 1024 * 1024

NN = (((1,), (0,)), ((), ()))
NT = (((1,), (1,)), ((), ()))


def _mm(a, b, dn=NN):
    return lax.dot_general(a, b, dn, preferred_element_type=F32)


def _split2(a):
    hi = a.astype(BF16)
    lo = (a - hi.astype(F32)).astype(BF16)
    return hi, lo


def _split3(a):
    hi = a.astype(BF16)
    r1 = a - hi.astype(F32)
    mid = r1.astype(BF16)
    lo = (r1 - mid.astype(F32)).astype(BF16)
    return hi, mid, lo


def _dot1(a, b, dn=NN):
    return _mm(a.astype(BF16), b.astype(BF16), dn)


def _dot_sel_l(sel, b, dn=NN):
    b0, b1, b2 = _split3(b)
    return _mm(sel, b0, dn) + (_mm(sel, b1, dn) + _mm(sel, b2, dn))


def _dot_sel_r(a, sel, dn=NN):
    a0, a1, a2 = _split3(a)
    return _mm(a0, sel, dn) + (_mm(a1, sel, dn) + _mm(a2, sel, dn))


def _iota(shape, dim):
    return lax.broadcasted_iota(jnp.int32, shape, dim)


def _seg_matrix():
    return ((_iota((PAIR, PAIR), 0) // HEAD_DIM) == (_iota((PAIR, PAIR), 1) // HEAD_DIM)).astype(BF16)


def _sigmoid(x):
    return 1.0 / (1.0 + jnp.exp(-x))


def _cparams(sem, vmem=VMEM_LIMIT):
    return pltpu.CompilerParams(dimension_semantics=sem, vmem_limit_bytes=vmem)


def _rope_slab(x, cos, sin_signed):
    lane = _iota(x.shape, 1) % HEAD_DIM
    up = pltpu.roll(x, LANES - ROT_DIM // 2, axis=1)
    down = pltpu.roll(x, ROT_DIM // 2, axis=1)
    partner = jnp.where(lane < ROT_DIM // 2, up, down)
    return x * cos + partner * sin_signed


def _inproj_kernel(rw_cols, q_cols, kv_cols, x_ref, g_ref, w_ref, cos_ref, sin_ref,
                   prw_ref, q_ref, k_ref, v_ref):
    x = x_ref[...]
    h = x * lax.rsqrt(jnp.mean(x * x, axis=-1, keepdims=True) + NORM_EPS) * g_ref[...]
    proj = _mm(h.astype(BF16), w_ref[...])
    prw_ref[...] = proj[:, :rw_cols]
    cos = cos_ref[...]
    sin = sin_ref[...]
    for c in range(q_cols // LANES):
        lo = rw_cols + c * LANES
        q_ref[:, c * LANES:(c + 1) * LANES] = _rope_slab(proj[:, lo:lo + LANES], cos, sin)
    ko = rw_cols + q_cols
    for c in range(kv_cols // LANES):
        k_ref[:, c * LANES:(c + 1) * LANES] = _rope_slab(proj[:, ko + c * LANES:ko + (c + 1) * LANES], cos, sin)
    v_ref[...] = proj[:, ko + kv_cols:ko + 2 * kv_cols]


def _inproj(x, g, w_bf, cos_t, sin_t, tm, rw_cols, q_cols, kv_cols):
    rows, d = x.shape
    cols = w_bf.shape[1]
    full = lambda i: (0, 0)
    row = lambda i: (i, 0)
    return pl.pallas_call(
        functools.partial(_inproj_kernel, rw_cols, q_cols, kv_cols),
        grid=(rows // tm,),
        in_specs=[pl.BlockSpec((tm, d), row), pl.BlockSpec((1, d), full),
                  pl.BlockSpec((d, cols), full),
                  pl.BlockSpec((tm, LANES), row), pl.BlockSpec((tm, LANES), row)],
        out_specs=[pl.BlockSpec((tm, rw_cols), row), pl.BlockSpec((tm, q_cols), row),
                   pl.BlockSpec((tm, kv_cols), row), pl.BlockSpec((tm, kv_cols), row)],
        out_shape=[jax.ShapeDtypeStruct((rows, rw_cols), F32), jax.ShapeDtypeStruct((rows, q_cols), F32),
                   jax.ShapeDtypeStruct((rows, kv_cols), F32), jax.ShapeDtypeStruct((rows, kv_cols), F32)],
        compiler_params=_cparams(("parallel",)),
        name="inproj",
    )(x, g, w_bf, cos_t, sin_t)


def _rwkv_tokenwise(pr, pk, pv, plo, pg, prev_r, prev_k, prev_v, prev_lo, prev_g,
                    mu_r, mu_k, mu_v, mu_lo, mu_g, w0, dw2, a0, aw2, gw2, kkp, kap, rkp, seg):
    r = pr + (prev_r - pr) * mu_r
    k = pk + (prev_k - pk) * mu_k
    v = pv + (prev_v - pv) * mu_v
    lo = plo + (prev_lo - plo) * mu_lo
    gd = pg + (prev_g - pg) * mu_g
    z = -(w0 + _dot1(jnp.tanh(lo), dw2))
    softplus = jnp.maximum(z, 0.0) + jnp.log(1.0 + jnp.exp(-jnp.abs(z)))
    logw = -jnp.exp(-softplus - 0.5)
    a = _sigmoid(a0 + _dot1(lo, aw2))
    g = _dot1(_sigmoid(gd), gw2)
    kk = k * kkp
    nrm = jnp.sqrt(_seg_sum(kk * kk, seg))
    kk = kk / jnp.maximum(nrm, 1e-12)
    k2 = k * (1.0 + (a - 1.0) * kap)
    bonus = _seg_sum(r * k2 * rkp, seg) * v
    return r, k2, v, logw, -kk, kk * a, g, bonus


def _seg_sum(x, seg):
    xh, xl = _split2(x)
    return _mm(xh, seg) + _mm(xl, seg)


def _rwkv_finish(y, bonus, g, lng, lnb, seg):
    mu = _seg_sum(y, seg) * (1.0 / HEAD_DIM)
    d = y - mu
    var = _seg_sum(d * d, seg) * (1.0 / HEAD_DIM)
    yn = d * lax.rsqrt(var + LNX_EPS) * lng + lnb
    return (yn + bonus) * g


def _rwkv_prompt_kernel(pps, pr_ref, pk_ref, pv_ref, plo_ref, pg_ref,
                        hr_ref, hk_ref, hv_ref, hlo_ref, hg_ref,
                        s0r_ref, s0k_ref, s0v_ref, s0lo_ref, s0g_ref,
                        mur_ref, muk_ref, muv_ref, mulo_ref, mug_ref,
                        w0_ref, dw2_ref, a0_ref, aw2_ref, gw2_ref, kk_ref, ka_ref, rk_ref,
                        lng_ref, lnb_ref, sin_ref,
                        sink_ref, q_ref, kc_ref, kp_ref, vc_ref, vp_ref,
                        y_ref, sout_ref, yb_ref, st_ref):
    i = pl.program_id(1)
    n_i = pl.num_programs(1)
    tt = pr_ref.shape[0]

    @pl.when(i == 0)
    def _():
        st_ref[...] = sin_ref[...]

    n_q = q_ref.shape[1] // HEAD_DIM
    attn = _attn_prompt_stages(i, n_q, n_q // (kc_ref.shape[1] // HEAD_DIM), sink_ref,
                               q_ref, kc_ref, kp_ref, vc_ref, vp_ref, yb_ref)

    row = _iota((tt, PAIR), 0)

    def prev_of(cur, halo_row, s0_row):
        first = jnp.where(i == 0, s0_row, halo_row)
        return jnp.where(row == 0, first, pltpu.roll(cur, 1, axis=0))

    plo = plo_ref[...]
    pg = pg_ref[...]
    prev_lo = prev_of(plo, hlo_ref[7:8, :], s0lo_ref[...])
    prev_g = prev_of(pg, hg_ref[7:8, :], s0g_ref[...])
    ti = _iota((tt, tt), 0)
    tj = _iota((tt, tt), 1)
    same_chunk = (ti // CHUNK) == (tj // CHUNK)
    incl = same_chunk & (tj <= ti)
    strict = same_chunk & (tj < ti)
    seg = _seg_matrix()
    lane = _iota((tt, PAIR), 1)
    eye = (ti == tj).astype(F32)
    pairs = []
    for p in range(pps):
        ls = slice(p * PAIR, (p + 1) * PAIR)
        pr, pk, pv = pr_ref[:, ls], pk_ref[:, ls], pv_ref[:, ls]
        r, k2, v, logw, nkk, b, g, bonus = _rwkv_tokenwise(
            pr, pk, pv, plo, pg,
            prev_of(pr, hr_ref[7:8, ls], s0r_ref[:, ls]), prev_of(pk, hk_ref[7:8, ls], s0k_ref[:, ls]),
            prev_of(pv, hv_ref[7:8, ls], s0v_ref[:, ls]), prev_lo, prev_g,
            mur_ref[:, ls], muk_ref[:, ls], muv_ref[:, ls], mulo_ref[...], mug_ref[...],
            w0_ref[:, ls], dw2_ref[:, ls], a0_ref[:, ls], aw2_ref[:, ls], gw2_ref[:, ls],
            kk_ref[:, ls], ka_ref[:, ls], rk_ref[:, ls], seg)
        pairs.append(dict(ls=ls, r=r, k2=k2, v=v, logw=logw, nkk=nkk, b=b, g=g, bonus=bonus))

    incl_b = incl.astype(BF16)
    for q in pairs:
        q["cs"] = _dot_sel_l(incl_b, q["logw"])
    for q in pairs:
        cs = q["cs"]
        gam = jnp.exp(cs)
        inv = jnp.exp(-cs)
        q["a_t"] = jnp.exp(cs - q["logw"]) * q["nkk"]
        q["r_t"] = gam * q["r"]
        q["bt_T"] = (q["b"] * inv).T
        q["kt_T"] = (q["k2"] * inv).T
        q["gam_T"] = gam.T
        q["bk_T"] = jnp.concatenate([q["bt_T"], q["kt_T"]], axis=1).astype(BF16)
    attn[0]()
    attn[1]()

    heads = []
    for q in pairs:
        for hh in range(2):
            hm = (lane // HEAD_DIM) == hh
            heads.append(dict(q=q, a=jnp.where(hm, q["a_t"], 0.0), r=jnp.where(hm, q["r_t"], 0.0),
                              v=jnp.where(hm, q["v"], 0.0)))
    for h in heads:
        h["g"] = _mm(jnp.concatenate([h["a"], h["r"]], axis=0).astype(BF16), h["q"]["bk_T"])
    for h in heads:
        gmat = h["g"]
        l_ab = jnp.where(strict, gmat[:tt, :tt], 0.0)
        h["l_ak_m_rk"] = jnp.concatenate([jnp.where(strict, gmat[:tt, tt:], 0.0),
                                          jnp.where(incl, gmat[tt:, tt:], 0.0)], axis=0).astype(BF16)
        h["m_rb"] = jnp.where(incl, gmat[tt:, :tt], 0.0).astype(BF16)
        h["tm"] = eye + l_ab
        h["lp"] = l_ab.astype(BF16)
    for h in heads:
        h["lp"] = _mm(h["lp"], h["lp"]).astype(BF16)
    for it in range(4):
        for h in heads:
            h["both"] = _mm(jnp.concatenate([h["tm"].astype(BF16), h["lp"]], axis=0), h["lp"])
        attn[2 + it]()
        for h in heads:
            h["tm"] = h["tm"] + h["both"][:tt]
            h["lp"] = h["both"][tt:].astype(BF16)
    for h in heads:
        h["pq"] = _mm(h["l_ak_m_rk"], h["v"].astype(BF16))
        h["tm"] = h["tm"] + _mm(h["tm"].astype(BF16), h["lp"])
    attn[6]()
    for h in heads:
        h["tx"] = _mm(h["tm"].astype(BF16),
                      jnp.concatenate([h["a"], h["pq"][:tt]], axis=1).astype(BF16))
    for h in heads:
        h["rx"] = _mm(h["m_rb"], h["tx"].astype(BF16))
    for n, q in enumerate(pairs):
        h0, h1 = heads[2 * n], heads[2 * n + 1]
        q["tatp"] = (h0["tx"] + h1["tx"]).astype(BF16)
        ryc = (h0["rx"] + h1["rx"]) + jnp.concatenate([h0["r"] + h1["r"], h0["pq"][tt:] + h1["pq"][tt:]], axis=1)
        q["ry"] = ryc[:, :PAIR]
        q["yc"] = ryc[:, PAIR:]
        q["v_b"] = q["v"].astype(BF16)
        q["bt_b"] = q["bt_T"].astype(BF16)
        q["kt_b"] = q["kt_T"].astype(BF16)
        q["s"] = st_ref[n]

    bd = seg.astype(F32)
    eye_p = (_iota((PAIR, PAIR), 0) == _iota((PAIR, PAIR), 1)).astype(F32)
    col_t = _iota((PAIR, tt), 1)
    zb = jnp.zeros((PAIR, tt), BF16)
    n_chunks = tt // CHUNK
    for c in range(n_chunks):
        cm = (col_t // CHUNK) == c
        for q in pairs:
            bt_c = jnp.where(cm, q["bt_b"], zb)
            kt_c = jnp.where(cm, q["kt_b"], zb)
            dcol = q["gam_T"][:, (c + 1) * CHUNK - 1:(c + 1) * CHUNK]
            bx = _mm(bt_c, q["tatp"])
            q["mc", c] = (dcol * (eye_p + bd * bx[:, :PAIR])).astype(BF16)
            q["nc", c] = dcol * (bd * (bx[:, PAIR:] + _mm(kt_c, q["v_b"])))
    for c in range(n_chunks):
        sl = slice(c * CHUNK, (c + 1) * CHUNK)
        for q in pairs:
            s_b = q["s"].astype(BF16)
            q["y", c] = _mm(q["ry"][sl].astype(BF16), s_b) + q["yc"][sl]
            q["s"] = _mm(q["mc", c], s_b) + q["nc", c]
    for q in pairs:
        y = jnp.concatenate([q["y", c] for c in range(n_chunks)], axis=0)
        y_ref[:, q["ls"]] = _rwkv_finish(y, q["bonus"], q["g"], lng_ref[:, q["ls"]], lnb_ref[:, q["ls"]], seg)
    for n, q in enumerate(pairs):
        st_ref[n] = q["s"]

    @pl.when(i == n_i - 1)
    def _():
        sout_ref[...] = st_ref[...]


def _mixers_prompt(prw, shift0, s0_pairs, pp, q, k, v, sinks, tt):
    T = prw.shape[0]
    n_pairs = s0_pairs.shape[0]
    pps = RW_PAIRS_PER_STEP
    n_grp = n_pairs // pps
    assert n_grp == 1 and tt % ATT_BLOCK == 0
    qw, kvw = q.shape[1], k.shape[1]
    ab = tt // ATT_BLOCK
    gw = pps * PAIR
    wcols = n_pairs * PAIR
    lo_col = 3 * wcols
    g_col = lo_col + PAIR
    hb = tt // 8

    def cur(off):
        return pl.BlockSpec((tt, gw), lambda p, i: (i, off // gw + p))

    def cur_fixed(col):
        return pl.BlockSpec((tt, PAIR), lambda p, i: (i, col // PAIR))

    def halo(off):
        return pl.BlockSpec((8, gw), lambda p, i: (jnp.maximum(i * hb - 1, 0), off // gw + p))

    def halo_fixed(col):
        return pl.BlockSpec((8, PAIR), lambda p, i: (jnp.maximum(i * hb - 1, 0), col // PAIR))

    def vec(off):
        return pl.BlockSpec((1, gw), lambda p, i: (0, off // gw + p))

    def vec_fixed(col):
        return pl.BlockSpec((1, PAIR), lambda p, i: (0, col // PAIR))

    def wmat(rows):
        return pl.BlockSpec((rows, gw), lambda p, i: (0, p))

    in_specs = ([cur(0), cur(wcols), cur(2 * wcols), cur_fixed(lo_col), cur_fixed(g_col)]
                + [halo(0), halo(wcols), halo(2 * wcols), halo_fixed(lo_col), halo_fixed(g_col)]
                + [vec(0), vec(wcols), vec(2 * wcols), vec_fixed(lo_col), vec_fixed(g_col)]
                + [vec(0), vec(wcols), vec(2 * wcols), vec_fixed(lo_col), vec_fixed(g_col)]
                + [vec(0), wmat(PAIR), vec(0), wmat(PAIR), wmat(PAIR), vec(0), vec(0), vec(0), vec(0), vec(0)]
                + [pl.BlockSpec((pps, PAIR, PAIR), lambda p, i: (p, 0, 0))])
    tile = lambda p, i: (i, 0)
    before = lambda p, i: (jnp.maximum(i * ab - 1, 0), 0)
    in_specs += [pl.BlockSpec(memory_space=pltpu.SMEM), pl.BlockSpec((tt, qw), tile),
                 pl.BlockSpec((tt, kvw), tile), pl.BlockSpec((ATT_BLOCK, kvw), before),
                 pl.BlockSpec((tt, kvw), tile), pl.BlockSpec((ATT_BLOCK, kvw), before)]
    args = ([prw] * 5 + [prw] * 5 + [shift0] * 5 + [pp["mu"]] * 5
            + [pp["w0"], pp["dw2"], pp["a0"], pp["aw2"], pp["gw2"], pp["kk"], pp["ka"], pp["rk"],
               pp["lng"], pp["lnb"], s0_pairs]
            + [sinks, q, k, k, v, v])
    return pl.pallas_call(
        functools.partial(_rwkv_prompt_kernel, pps),
        grid=(n_grp, T // tt),
        in_specs=in_specs,
        out_specs=[pl.BlockSpec((tt, gw), lambda p, i: (i, p)),
                   pl.BlockSpec((pps, PAIR, PAIR), lambda p, i: (p, 0, 0)),
                   pl.BlockSpec((tt, qw), tile)],
        out_shape=[jax.ShapeDtypeStruct((T, wcols), F32),
                   jax.ShapeDtypeStruct((n_pairs, PAIR, PAIR), F32),
                   jax.ShapeDtypeStruct((T, qw), F32)],
        scratch_shapes=[pltpu.VMEM((pps, PAIR, PAIR), F32)],
        compiler_params=_cparams(("parallel", "arbitrary")),
        name="mixers_prompt",
    )(*args)


def _rwkv_step_kernel(slabs_per_step, pr_ref, pk_ref, pv_ref, plo_ref, pg_ref,
                      sr_ref, sk_ref, sv_ref, slo_ref, sg_ref,
                      mur_ref, muk_ref, muv_ref, mulo_ref, mug_ref,
                      w0_ref, dw2_ref, a0_ref, aw2_ref, gw2_ref, kk_ref, ka_ref, rk_ref,
                      lng_ref, lnb_ref, s_ref,
                      y_ref, snew_ref, yacc_ref):
    j = pl.program_id(1)
    n_j = pl.num_programs(1)
    seg = _seg_matrix()
    r, k2, v, logw, nkk, b, g, bonus = _rwkv_tokenwise(
        pr_ref[...], pk_ref[...], pv_ref[...], plo_ref[...], pg_ref[...],
        sr_ref[...], sk_ref[...], sv_ref[...], slo_ref[...], sg_ref[...],
        mur_ref[...], muk_ref[...], muv_ref[...], mulo_ref[...], mug_ref[...],
        w0_ref[...], dw2_ref[...], a0_ref[...], aw2_ref[...], gw2_ref[...],
        kk_ref[...], ka_ref[...], rk_ref[...], seg)
    w = jnp.exp(logw)

    @pl.when(j == 0)
    def _():
        yacc_ref[...] = jnp.zeros_like(yacc_ref)

    slabs_per_head = HEAD_DIM // 2
    ci = _iota((PAIR, PAIR), 0)
    li = _iota((PAIR, PAIR), 1)
    assert slabs_per_head % slabs_per_step == 0
    yacc = yacc_ref[...]
    hh = (j * slabs_per_step) // slabs_per_head
    dup = ((ci == hh * HEAD_DIM + li % HEAD_DIM)).astype(BF16)
    nkk_d, w_d, b_d, k_d, r_d = [_dot_sel_r(x, dup) for x in (nkk, w, b, k2, r)]
    slabs = range(slabs_per_step)
    i0 = [2 * ((j * slabs_per_step + t) % slabs_per_head) for t in slabs]
    s = [s_ref[:, t * PAIR:(t + 1) * PAIR] for t in slabs]
    sa = [_seg_sum(s[t] * nkk_d, seg) for t in slabs]
    v_bc = [_seg_sum(v, (ci == hh * HEAD_DIM + i0[t] + li // HEAD_DIM).astype(BF16)) for t in slabs]
    s_new = [s[t] * w_d + sa[t] * b_d + v_bc[t] * k_d for t in slabs]
    for t in slabs:
        snew_ref[:, t * PAIR:(t + 1) * PAIR] = s_new[t]
    yred = [_seg_sum(s_new[t] * r_d, seg) for t in slabs]
    ysel = [_seg_sum(yred[t], ((ci % HEAD_DIM == 0)
                               & (li == hh * HEAD_DIM + i0[t] + ci // HEAD_DIM)).astype(BF16)) for t in slabs]
    for t in slabs:
        yacc = yacc + ysel[t]
    yacc_ref[...] = yacc

    @pl.when(j == n_j - 1)
    def _():
        y_ref[...] = _rwkv_finish(yacc, bonus, g, lng_ref[...], lnb_ref[...], seg)


def _rwkv_step(prw, shift, s_flat, pp, n_pairs):
    B = prw.shape[0]
    lanes_per_pair = 2 * HEAD_DIM * HEAD_DIM
    blk = 1024
    slabs_per_step = blk // PAIR
    steps = lanes_per_pair // blk
    lo_blk = 3 * n_pairs
    g_blk = lo_blk + 1

    def cur(off):
        return pl.BlockSpec((B, PAIR), lambda p, j: (0, off + p))

    def cur_fixed(b_):
        return pl.BlockSpec((B, PAIR), lambda p, j: (0, b_))

    def vec(off):
        return pl.BlockSpec((1, PAIR), lambda p, j: (0, off + p))

    def vec_fixed(b_):
        return pl.BlockSpec((1, PAIR), lambda p, j: (0, b_))

    def wmat(rows):
        return pl.BlockSpec((rows, PAIR), lambda p, j: (0, p))

    sspec = pl.BlockSpec((B, blk), lambda p, j: (0, p * steps + j))
    in_specs = ([cur(0), cur(n_pairs), cur(2 * n_pairs), cur_fixed(lo_blk), cur_fixed(g_blk)] * 2
                + [vec(0), vec(n_pairs), vec(2 * n_pairs), vec_fixed(lo_blk), vec_fixed(g_blk)]
                + [vec(0), wmat(PAIR), vec(0), wmat(PAIR), wmat(PAIR), vec(0), vec(0), vec(0), vec(0), vec(0)]
                + [sspec])
    args = ([prw] * 5 + [shift] * 5 + [pp["mu"]] * 5
            + [pp["w0"], pp["dw2"], pp["a0"], pp["aw2"], pp["gw2"], pp["kk"], pp["ka"], pp["rk"],
               pp["lng"], pp["lnb"], s_flat])
    return pl.pallas_call(
        functools.partial(_rwkv_step_kernel, slabs_per_step),
        grid=(n_pairs, steps),
        in_specs=in_specs,
        out_specs=[pl.BlockSpec((B, PAIR), lambda p, j: (0, p)), sspec],
        out_shape=[jax.ShapeDtypeStruct((B, n_pairs * PAIR), F32),
                   jax.ShapeDtypeStruct(s_flat.shape, F32)],
        scratch_shapes=[pltpu.VMEM((B, PAIR), F32)],
        compiler_params=_cparams(("parallel", "arbitrary")),
        name="rwkv_step",
    )(*args)


def _attn_prompt_stages(tile_idx, n_q, group, sink_ref, q_ref, kc_ref, kp_ref, vc_ref, vp_ref, o_ref):
    blk = ATT_BLOCK
    n_blk = q_ref.shape[0] // blk
    n_kv = n_q // group
    inst = [(j, h) for j in range(n_blk) for h in range(n_q)]
    st = {}

    def prepare():
        q = q_ref[...] * (HEAD_DIM ** -0.5)
        kc = kc_ref[...]
        vc = vc_ref[...]
        kall = jnp.concatenate([kp_ref[...], kc], axis=0)
        vall = jnp.concatenate([vp_ref[...], vc], axis=0)
        rq = _iota((blk, 2 * blk), 0)
        ck = _iota((blk, 2 * blk), 1)
        dist = rq - ck + blk
        in_window = (dist >= 0) & (dist < WINDOW)
        kpos0 = tile_idx * (n_blk * blk) - blk + ck
        st["valid"] = [in_window & (kpos0 >= 0)] + [in_window] * (n_blk - 1)
        st["q"] = {(j, h): q[j * blk:(j + 1) * blk, h * HEAD_DIM:(h + 1) * HEAD_DIM].astype(BF16) for j, h in inst}
        st["kb"] = {(j, g): kall[j * blk:(j + 2) * blk, g * HEAD_DIM:(g + 1) * HEAD_DIM].astype(BF16)
                    for j in range(n_blk) for g in range(n_kv)}
        st["vb"] = {(j, g): vall[j * blk:(j + 2) * blk, g * HEAD_DIM:(g + 1) * HEAD_DIM].astype(BF16)
                    for j in range(n_blk) for g in range(n_kv)}

    def scores():
        st["s"] = {(j, h): jnp.where(st["valid"][j], _mm(st["q"][j, h], st["kb"][j, h // group], NT), NEG_BIG)
                   for j, h in inst}

    def row_max():
        st["m"] = {(j, h): jnp.maximum(jnp.max(st["s"][j, h], axis=-1, keepdims=True), sink_ref[h]) for j, h in inst}

    def probs():
        st["p"] = {(j, h): jnp.exp(st["s"][j, h] - st["m"][j, h]) for j, h in inst}

    def denominators():
        st["d"] = {(j, h): jnp.sum(st["p"][j, h], axis=-1, keepdims=True) + jnp.exp(sink_ref[h] - st["m"][j, h])
                   for j, h in inst}

    def weighted_values():
        st["o"] = {(j, h): _mm(st["p"][j, h].astype(BF16), st["vb"][j, h // group]) for j, h in inst}

    def store():
        for j, h in inst:
            o_ref[j * blk:(j + 1) * blk, h * HEAD_DIM:(h + 1) * HEAD_DIM] = st["o"][j, h] / st["d"][j, h]

    return [prepare, scores, row_max, probs, denominators, weighted_values, store]


def _attn_step_kernel(n_q, group, pos0, sink_ref, q_ref, kn_ref, vn_ref, kc_ref, vc_ref,
                      o_ref, ko_ref, vo_ref):
    bb, wlen, kvw = kc_ref.shape
    rows = bb * n_q
    lane = _iota((rows, kvw), 1)
    rowh = _iota((rows, kvw), 0) % n_q
    mine = (lane // HEAD_DIM) == (rowh // group)
    dupm = (_iota((HEAD_DIM, kvw), 0) == _iota((HEAD_DIM, kvw), 1) % HEAD_DIM).astype(BF16)
    fold = (_iota((kvw, HEAD_DIM), 0) % HEAD_DIM == _iota((kvw, HEAD_DIM), 1)).astype(BF16)
    kidx = _iota((n_q, wlen), 1)
    dist = wlen - kidx
    valid = (dist < WINDOW) & (pos0 - dist >= 0)
    rk = _iota((wlen, kvw), 0)
    sink = sink_ref[...]
    qm_all = jnp.where(mine, _dot_sel_r(q_ref[...] * (HEAD_DIM ** -0.5), dupm), 0.0)
    elems = range(bb)
    qm = [qm_all[t * n_q:(t + 1) * n_q] for t in elems]
    kc = [kc_ref[t] for t in elems]
    vc = [vc_ref[t] for t in elems]
    kn = [kn_ref[t:t + 1, :] for t in elems]
    vn = [vn_ref[t:t + 1, :] for t in elems]
    s = [jnp.where(valid, _dot1(qm[t], kc[t], NT), NEG_BIG) for t in elems]
    s_new = [jnp.sum(qm[t] * kn[t], axis=-1, keepdims=True) for t in elems]
    m = [jnp.maximum(jnp.maximum(jnp.max(s[t], axis=-1, keepdims=True), s_new[t]), sink) for t in elems]
    p = [jnp.exp(s[t] - m[t]) for t in elems]
    p_new = [jnp.exp(s_new[t] - m[t]) for t in elems]
    denom = [jnp.sum(p[t], axis=-1, keepdims=True) + p_new[t] + jnp.exp(sink - m[t]) for t in elems]
    res = [(_dot1(p[t], vc[t]) + p_new[t] * vn[t]) / denom[t] for t in elems]
    o_ref[...] = _dot_sel_r(jnp.where(mine, jnp.concatenate(res, axis=0), 0.0), fold)
    for t in elems:
        ko_ref[t] = jnp.where(rk == wlen - 1, kn[t], pltpu.roll(kc[t], wlen - 1, axis=0))
        vo_ref[t] = jnp.where(rk == wlen - 1, vn[t], pltpu.roll(vc[t], wlen - 1, axis=0))


def _attn_step(q2, k_new, v_new, k_cache, v_cache, sinks_col, n_q, n_kv, pos0):
    B, wlen, kvw = k_cache.shape
    bb = 8
    return pl.pallas_call(
        functools.partial(_attn_step_kernel, n_q, n_q // n_kv, pos0),
        grid=(B // bb,),
        in_specs=[pl.BlockSpec((n_q, 1), lambda i: (0, 0)),
                  pl.BlockSpec((bb * n_q, HEAD_DIM), lambda i: (i, 0)),
                  pl.BlockSpec((bb, kvw), lambda i: (i, 0)), pl.BlockSpec((bb, kvw), lambda i: (i, 0)),
                  pl.BlockSpec((bb, wlen, kvw), lambda i: (i, 0, 0)),
                  pl.BlockSpec((bb, wlen, kvw), lambda i: (i, 0, 0))],
        out_specs=[pl.BlockSpec((bb * n_q, HEAD_DIM), lambda i: (i, 0)),
                   pl.BlockSpec((bb, wlen, kvw), lambda i: (i, 0, 0)),
                   pl.BlockSpec((bb, wlen, kvw), lambda i: (i, 0, 0))],
        out_shape=[jax.ShapeDtypeStruct((B * n_q, HEAD_DIM), F32),
                   jax.ShapeDtypeStruct((B, wlen, kvw), F32),
                   jax.ShapeDtypeStruct((B, wlen, kvw), F32)],
        compiler_params=_cparams(("parallel",)),
        name="attn_step",
    )(sinks_col, q2, k_new, v_new, k_cache, v_cache)


def _post_kernel(x_ref, ya_ref, yb_ref, wa_ref, wb_ref, g2_ref, wr_ref, br_ref, cnt0_ref,
                 x1_ref, h2_ref, gate_ref, meta_ref, tb_ref, tl_ref, cnt_ref, carry_ref):
    i = pl.program_id(0)

    @pl.when(i == 0)
    def _():
        carry_ref[...] = cnt0_ref[...]

    mix = _mm(ya_ref[...].astype(BF16), wa_ref[...]) + _mm(yb_ref[...].astype(BF16), wb_ref[...])
    x1 = x_ref[...] + mix
    h2 = x1 * lax.rsqrt(jnp.mean(x1 * x1, axis=-1, keepdims=True) + NORM_EPS) * g2_ref[...]
    x1_ref[...] = x1
    h2_ref[...] = h2

    l = _dot1(h2, wr_ref[...]) + br_ref[...]
    tm = l.shape[0]
    lane = _iota(l.shape, 1)
    vals, idxs = [], []
    for _ in range(TOP_K):
        m = jnp.max(l, axis=-1, keepdims=True)
        sel = jnp.min(jnp.where(l == m, lane, LANES), axis=-1, keepdims=True)
        vals.append(m)
        idxs.append(sel)
        l = jnp.where(lane == sel, -jnp.inf, l)
    es = [jnp.exp(v - vals[0]) for v in vals]
    tot = es[0] + es[1] + es[2] + es[3]
    onehot = jnp.zeros(l.shape, F32)
    for sel in idxs:
        onehot = onehot + (lane == sel).astype(F32)
    strict = (_iota((tm, tm), 1) < _iota((tm, tm), 0)).astype(BF16)
    before = _mm(strict, onehot.astype(BF16))
    for k in range(TOP_K):
        gate_ref[:, k:k + 1] = es[k] / tot
        meta_ref[:, k:k + 1] = idxs[k]
        meta_ref[:, TOP_K + k:TOP_K + k + 1] = jnp.sum(
            jnp.where(lane == idxs[k], before, 0.0), axis=-1, keepdims=True).astype(jnp.int32)
    carry = carry_ref[...]
    cnt_t = jnp.sum(onehot, axis=0, keepdims=True)
    tb_ref[0] = carry.astype(jnp.int32)
    tl_ref[0] = cnt_t.astype(jnp.int32)
    carry_ref[...] = carry + cnt_t
    cnt_ref[...] = carry + cnt_t


def _post(x, ya, yb, wp, cnt0, tm):
    rows, d = x.shape
    half = ya.shape[1]
    n_t = rows // tm
    full = lambda i: (0, 0)
    row = lambda i: (i, 0)
    trow = lambda i: (i, 0, 0)
    return pl.pallas_call(
        _post_kernel,
        grid=(n_t,),
        in_specs=[pl.BlockSpec((tm, d), row), pl.BlockSpec((tm, half), row), pl.BlockSpec((tm, half), row),
                  pl.BlockSpec((half, d), full), pl.BlockSpec((half, d), full),
                  pl.BlockSpec((1, d), full), pl.BlockSpec((d, LANES), full), pl.BlockSpec((1, LANES), full),
                  pl.BlockSpec((1, LANES), full)],
        out_specs=[pl.BlockSpec((tm, d), row), pl.BlockSpec((tm, d), row),
                   pl.BlockSpec((tm, TOP_K), row), pl.BlockSpec((tm, 2 * TOP_K), row),
                   pl.BlockSpec((1, 1, LANES), trow), pl.BlockSpec((1, 1, LANES), trow),
                   pl.BlockSpec((1, LANES), full)],
        out_shape=[jax.ShapeDtypeStruct((rows, d), F32), jax.ShapeDtypeStruct((rows, d), F32),
                   jax.ShapeDtypeStruct((rows, TOP_K), F32),
                   jax.ShapeDtypeStruct((rows, 2 * TOP_K), jnp.int32),
                   jax.ShapeDtypeStruct((n_t, 1, LANES), jnp.int32),
                   jax.ShapeDtypeStruct((n_t, 1, LANES), jnp.int32),
                   jax.ShapeDtypeStruct((1, LANES), F32)],
        scratch_shapes=[pltpu.VMEM((1, LANES), F32)],
        compiler_params=_cparams(("arbitrary",)),
        name="post",
    )(x, ya, yb, wp["wa"], wp["wb"], wp["g2"], wp["wr"], wp["br"], cnt0)


def _n_windows(base, length):
    off = base & (SORT_ALIGN - 1)
    n = lax.shift_right_logical(off + length + (MOE_WIN - 1), MOE_WIN_SHIFT)
    return off, jnp.where(length > 0, n, 0)


def _window_targets(meta, tb_vec, tl_vec):
    tm = meta.shape[0]
    off, n_win = _n_windows(tb_vec, tl_vec)
    upper = (_iota((LANES, LANES), 0) < _iota((LANES, LANES), 1)).astype(BF16)
    slot_start = _mm(n_win.astype(F32).astype(BF16), upper)
    pos0 = slot_start * MOE_WIN + off.astype(F32)
    lane = _iota((tm, LANES), 1)
    tgts = []
    for k in range(TOP_K):
        p0 = jnp.sum(jnp.where(lane == meta[:, k:k + 1], pos0, 0.0), axis=-1, keepdims=True)
        tgts.append(p0.astype(jnp.int32) + meta[:, TOP_K + k:TOP_K + k + 1])
    return tgts


def _for_each_window(n_e, pstart_ref, tb_ref, tl_ref, fn, rows_ref, cnt_ref, b, per_expert_fn=None):
    def per_expert(e, slot0):
        base = tb_ref[0, 0, e]
        length = tl_ref[0, 0, e]
        off, n = _n_windows(base, length)
        row0 = pstart_ref[e] + base - off
        if per_expert_fn is not None:
            per_expert_fn(e, slot0, off, length, n)

        def per_window(w, c):
            row = row0 + w * MOE_WIN
            rows_ref[b, slot0 + w] = row
            fn(slot0 + w, pl.multiple_of(row, SORT_ALIGN))
            return c

        lax.fori_loop(0, n, per_window, 0)
        return slot0 + n

    cnt_ref[b] = lax.fori_loop(0, n_e, per_expert, 0)


def _for_recorded_windows(fn, rows_ref, cnt_ref, b):
    def body(slot, c):
        fn(slot, pl.multiple_of(rows_ref[b, slot], SORT_ALIGN))
        return c

    lax.fori_loop(0, cnt_ref[b], body, 0)


def _moe_slots(tm):
    n = -(-(tm * TOP_K + N_EXPERTS * (SORT_ALIGN - 1 + MOE_WIN - 1)) // MOE_WIN)
    per_lane_tile = LANES // MOE_WIN
    return -(-n // per_lane_tile) * per_lane_tile


def _scatter_kernel(n_e, continues, pstart_ref, z1_ref, z2_ref, tb_ref, tl_ref, tbv_ref, tlv_ref,
                    meta_ref, h_ref, *rest):
    if continues:
        cin_ref, _xs_alias, xs_ref, cout_ref, xw_ref, zero_ref, carry_ref, rows_ref, cnt_ref, sem, zsem = rest
    else:
        xs_ref, cout_ref, xw_ref, zero_ref, carry_ref, rows_ref, cnt_ref, sem, zsem = rest
    i = pl.program_id(0)
    n_i = pl.num_programs(0)
    tm = h_ref.shape[0]
    bm = zero_ref.shape[0]
    buf = i % 2
    n_rows_w = xw_ref.shape[1]

    @pl.when(i == 0)
    def _():
        if continues:
            carry_ref[...] = cin_ref[...]
        else:
            zero_ref[...] = jnp.zeros_like(zero_ref)

            def zcopy(row):
                return pltpu.make_async_copy(zero_ref, xs_ref.at[pl.ds(pl.multiple_of(row, SORT_ALIGN), bm)], zsem)

            for e in range(n_e):
                zcopy(z1_ref[e]).start()

                @pl.when(z2_ref[e] != z1_ref[e])
                def _():
                    zcopy(z2_ref[e]).start()
            for e in range(n_e):
                zcopy(z1_ref[e]).wait()

                @pl.when(z2_ref[e] != z1_ref[e])
                def _():
                    zcopy(z2_ref[e]).wait()

            carry_ref[...] = jnp.zeros_like(carry_ref)

    tgts = _window_targets(meta_ref[...], tbv_ref[0], tlv_ref[0])
    lane_s = _iota((tm, n_rows_w), 1)
    sel = jnp.zeros((tm, n_rows_w), F32)
    for tgt in tgts:
        sel = sel + (lane_s == tgt).astype(F32)
    xw_ref[buf] = _mm(sel.T.astype(BF16), h_ref[...].astype(BF16)).astype(BF16)

    def splice_carry(e, slot0, off, length, n):
        @pl.when(n > 0)
        def _():
            g0 = pl.multiple_of(slot0 * MOE_WIN, MOE_WIN)
            xw_ref[buf, pl.ds(g0, SORT_ALIGN), :] = xw_ref[buf, pl.ds(g0, SORT_ALIGN), :] + carry_ref[e]
            filled = off + length
            gl = pl.multiple_of(g0 + lax.shift_right_logical(filled, SORT_ALIGN_SHIFT) * SORT_ALIGN, SORT_ALIGN)
            last = xw_ref[buf, pl.ds(gl, SORT_ALIGN), :]
            carry_ref[e] = jnp.where((filled & (SORT_ALIGN - 1)) != 0, last, jnp.zeros_like(last))

    def copy(b, slot, row):
        return pltpu.make_async_copy(xw_ref.at[b, pl.ds(pl.multiple_of(slot * MOE_WIN, MOE_WIN), MOE_WIN)],
                                     xs_ref.at[pl.ds(row, MOE_WIN)], sem.at[b])

    @pl.when(i > 0)
    def _():
        _for_recorded_windows(lambda slot, row: copy(1 - buf, slot, row).wait(), rows_ref, cnt_ref, 1 - buf)

    _for_each_window(n_e, pstart_ref, tb_ref, tl_ref, lambda slot, row: copy(buf, slot, row).start(),
                     rows_ref, cnt_ref, buf, splice_carry)

    @pl.when(i == n_i - 1)
    def _():
        _for_recorded_windows(lambda slot, row: copy(buf, slot, row).wait(), rows_ref, cnt_ref, buf)
        cout_ref[...] = carry_ref[...]


def _scatter(pstart, z1, z2, tbase, tlen, meta, h2, prior, n_rows_sorted, bm, tm):
    rows, d = h2.shape
    n_e = pstart.shape[0]
    n_slots = _moe_slots(tm)
    smem = pl.BlockSpec(memory_space=pltpu.SMEM)
    tile3 = lambda i: (i, 0, 0)
    tile_smem = lambda im: pl.BlockSpec((1, 1, LANES), im, memory_space=pltpu.SMEM)
    carry_spec = pl.BlockSpec((n_e, SORT_ALIGN, d), lambda i: (0, 0, 0))
    in_specs = [smem, smem, smem,
                tile_smem(tile3), tile_smem(tile3),
                pl.BlockSpec((1, 1, LANES), tile3), pl.BlockSpec((1, 1, LANES), tile3),
                pl.BlockSpec((tm, 2 * TOP_K), lambda i: (i, 0)),
                pl.BlockSpec((tm, d), lambda i: (i, 0))]
    args = [pstart, z1, z2, tbase, tlen, tbase, tlen, meta, h2]
    aliases = {}
    if prior is not None:
        in_specs += [carry_spec, pl.BlockSpec(memory_space=pl.ANY)]
        aliases = {len(args) + 1: 0}
        args += list(prior)
    return pl.pallas_call(
        functools.partial(_scatter_kernel, n_e, prior is not None),
        grid=(rows // tm,),
        in_specs=in_specs,
        out_specs=[pl.BlockSpec(memory_space=pl.ANY), carry_spec],
        out_shape=[jax.ShapeDtypeStruct((n_rows_sorted, d), BF16),
                   jax.ShapeDtypeStruct((n_e, SORT_ALIGN, d), BF16)],
        input_output_aliases=aliases,
        scratch_shapes=[pltpu.VMEM((2, n_slots * MOE_WIN, d), BF16), pltpu.VMEM((bm, d), BF16),
                        pltpu.VMEM((n_e, SORT_ALIGN, d), BF16),
                        pltpu.SMEM((2, n_slots), jnp.int32), pltpu.SMEM((2,), jnp.int32),
                        pltpu.SemaphoreType.DMA((2,)), pltpu.SemaphoreType.DMA(())],
        compiler_params=_cparams(("arbitrary",)),
        name="moe_scatter",
    )(*args)


def _expert_kernel(d_ff, be_ref, nused_ref, xs_ref, w1_ref, b1_ref, w2_ref, b2_ref, ys_ref, w1b_ref, w2b_ref):
    i = pl.program_id(0)
    new_expert = jnp.logical_or(i == 0, be_ref[i] != be_ref[jnp.maximum(i - 1, 0)])

    @pl.when(jnp.logical_and(i < nused_ref[0], new_expert))
    def _():
        w1b_ref[...] = w1_ref[0].astype(BF16)
        w2b_ref[...] = w2_ref[0].astype(BF16)

    @pl.when(i < nused_ref[0])
    def _():
        h = _mm(xs_ref[...], w1b_ref[...]) + b1_ref[0]
        hg = jnp.minimum(h[:, :d_ff], SWIGLU_LIMIT)
        hu = jnp.clip(h[:, d_ff:], -SWIGLU_LIMIT, SWIGLU_LIMIT)
        act = hg * _sigmoid(SWIGLU_ALPHA * hg) * (hu + 1.0)
        ys_ref[...] = (_mm(act.astype(BF16), w2b_ref[...]) + b2_ref[0]).astype(ys_ref.dtype)

    @pl.when(i >= nused_ref[0])
    def _():
        ys_ref[...] = jnp.zeros_like(ys_ref)


def _experts(block_e, n_used, xs, w1, b1, w2, b2, bm):
    R, d = xs.shape
    d_ff = w2.shape[1]
    nb = R // bm

    def rows(i, be, nu):
        return (jnp.minimum(i, nu[0] - 1), 0)

    def wsel(i, be, nu):
        return (be[i], 0, 0)

    return pl.pallas_call(
        functools.partial(_expert_kernel, d_ff),
        grid_spec=pltpu.PrefetchScalarGridSpec(
            num_scalar_prefetch=2,
            grid=(nb,),
            in_specs=[pl.BlockSpec((bm, d), rows),
                      pl.BlockSpec((1, d, 2 * d_ff), wsel), pl.BlockSpec((1, 1, 2 * d_ff), wsel),
                      pl.BlockSpec((1, d_ff, d), wsel), pl.BlockSpec((1, 1, d), wsel)],
            out_specs=pl.BlockSpec((bm, d), lambda i, be, nu: (i, 0)),
            scratch_shapes=[pltpu.VMEM((d, 2 * d_ff), BF16), pltpu.VMEM((d_ff, d), BF16)]),
        out_shape=jax.ShapeDtypeStruct((R, d), xs.dtype),
        compiler_params=_cparams(("arbitrary",)),
        name="moe_experts",
    )(block_e, n_used, xs, w1, b1, w2, b2)


def _combine_kernel(n_e, pstart_ref, tb_ref, tl_ref, tbn_ref, tln_ref, tbv_ref, tlv_ref, meta_ref, gate_ref,
                    x1_ref, gf_ref, ys_ref, o_ref, win_ref, rows_ref, cnt_ref, sem):
    i = pl.program_id(0)
    n = pl.num_programs(0)
    tm = x1_ref.shape[0]
    buf = i % 2
    n_rows_w = win_ref.shape[1]

    def copy(b, slot, row):
        return pltpu.make_async_copy(
            ys_ref.at[pl.ds(row, MOE_WIN)],
            win_ref.at[b, pl.ds(pl.multiple_of(slot * MOE_WIN, MOE_WIN), MOE_WIN)], sem.at[b])

    @pl.when(i == 0)
    def _():
        win_ref[...] = jnp.zeros_like(win_ref)
        _for_each_window(n_e, pstart_ref, tb_ref, tl_ref, lambda slot, row: copy(buf, slot, row).start(),
                         rows_ref, cnt_ref, buf)

    @pl.when(i + 1 < n)
    def _():
        _for_each_window(n_e, pstart_ref, tbn_ref, tln_ref, lambda slot, row: copy(1 - buf, slot, row).start(),
                         rows_ref, cnt_ref, 1 - buf)

    tgts = _window_targets(meta_ref[...], tbv_ref[0], tlv_ref[0])
    gate = gate_ref[...]
    lane_s = _iota((tm, n_rows_w), 1)
    sel = jnp.zeros((tm, n_rows_w), F32)
    for k, tgt in enumerate(tgts):
        sel = sel + jnp.where(lane_s == tgt, gate[:, k:k + 1], 0.0)
    sel_hi, sel_lo = _split2(sel)
    _for_recorded_windows(lambda slot, row: copy(buf, slot, row).wait(), rows_ref, cnt_ref, buf)
    wb = win_ref[buf]
    y = x1_ref[...] + (_mm(sel_hi, wb) + _mm(sel_lo, wb))
    o_ref[...] = y * lax.rsqrt(jnp.mean(y * y, axis=-1, keepdims=True) + NORM_EPS) * gf_ref[...]


def _combine(pstart, tbase, tlen, meta, gate, x1, gf, ys, tm):
    rows, d = x1.shape
    n = rows // tm
    n_e = pstart.shape[0]
    cur3 = lambda i: (i, 0, 0)
    nxt3 = lambda i: (jnp.minimum(i + 1, n - 1), 0, 0)
    tile_smem = lambda im: pl.BlockSpec((1, 1, LANES), im, memory_space=pltpu.SMEM)
    return pl.pallas_call(
        functools.partial(_combine_kernel, n_e),
        grid=(n,),
        in_specs=[pl.BlockSpec(memory_space=pltpu.SMEM),
                  tile_smem(cur3), tile_smem(cur3), tile_smem(nxt3), tile_smem(nxt3),
                  pl.BlockSpec((1, 1, LANES), cur3), pl.BlockSpec((1, 1, LANES), cur3),
                  pl.BlockSpec((tm, 2 * TOP_K), lambda i: (i, 0)),
                  pl.BlockSpec((tm, TOP_K), lambda i: (i, 0)),
                  pl.BlockSpec((tm, d), lambda i: (i, 0)),
                  pl.BlockSpec((1, d), lambda i: (0, 0)),
                  pl.BlockSpec(memory_space=pl.ANY)],
        out_specs=pl.BlockSpec((tm, d), lambda i: (i, 0)),
        out_shape=jax.ShapeDtypeStruct((rows, d), F32),
        scratch_shapes=[pltpu.VMEM((2, _moe_slots(tm) * MOE_WIN, d), ys.dtype),
                        pltpu.SMEM((2, _moe_slots(tm)), jnp.int32), pltpu.SMEM((2,), jnp.int32),
                        pltpu.SemaphoreType.DMA((2,))],
        compiler_params=_cparams(("arbitrary",)),
        name="moe_combine",
    )(pstart, tbase, tlen, tbase, tlen, tbase, tlen, meta, gate, x1, gf, ys)


def _rope_tables(pos):
    half = ROT_DIM // 2
    inv = ROPE_THETA ** (-jnp.arange(0, ROT_DIM, 2, dtype=F32) / ROT_DIM)
    ang = inv[:, None] * pos.astype(F32)[None, :]
    cos, sin = jnp.cos(ang), jnp.sin(ang)
    n = pos.shape[0]
    pad1 = jnp.ones((HEAD_DIM - ROT_DIM, n), F32)
    pad0 = jnp.zeros((HEAD_DIM - ROT_DIM, n), F32)
    cos_h = jnp.concatenate([cos, cos, pad1], axis=0)
    sin_h = jnp.concatenate([-sin, sin, pad0], axis=0)
    reps = (LANES // HEAD_DIM, 1)
    return jnp.tile(cos_h, reps).T, jnp.tile(sin_h, reps).T


def _pairs_from_state(S):
    H = S.shape[0]
    St = jnp.swapaxes(S, 1, 2).reshape(H // 2, 2, HEAD_DIM, HEAD_DIM)
    z = jnp.zeros_like(St[:, 0])
    top = jnp.concatenate([St[:, 0], z], axis=2)
    bot = jnp.concatenate([z, St[:, 1]], axis=2)
    return jnp.concatenate([top, bot], axis=1)


def _state_from_pairs(Sp):
    a = Sp[:, :HEAD_DIM, :HEAD_DIM]
    b = Sp[:, HEAD_DIM:, HEAD_DIM:]
    St = jnp.stack([a, b], axis=1).reshape(-1, HEAD_DIM, HEAD_DIM)
    return jnp.swapaxes(St, 1, 2)


def kernel(x_prompt, x_sample, state_rwkv_wkv, state_rwkv_shift, cache_swa_k, cache_swa_v, norm1_g, w_in, mu_shift, decay_w0, decay_w2, aaa_a0, aaa_w2, gate_w2, k_k, k_a, r_k, lnx_g, lnx_b, attn_sinks, w_out, norm2_g, w_router, b_router, w_mlp1, b_mlp1, w_mlp2, b_mlp2, norm_f_g):
    depth = w_in.shape[0]
    assert depth == 1 and x_prompt.shape[0] == 1 and x_sample.shape[1] == 1
    T, d = x_prompt.shape[1], x_prompt.shape[2]
    B = x_sample.shape[0]
    past_len = PAST_LEN
    H = state_rwkv_wkv.shape[2]
    rw_w = H * HEAD_DIM
    n_pairs = H // 2
    n_q = attn_sinks.shape[1]
    n_kv = cache_swa_k.shape[3]
    q_cols = n_q * HEAD_DIM
    kv_cols = n_kv * HEAD_DIM
    rw_cols = state_rwkv_shift.shape[2]
    assert rw_cols == 3 * rw_w + 2 * HEAD_DIM + PAIR and kv_cols == LANES
    assert T % RW_TILE == 0 and B % ROW_TILE == 0 and B % 8 == 0
    wlen = cache_swa_k.shape[2]
    l = 0

    w_in_bf = w_in[l].astype(BF16)
    zero_half = jnp.zeros((HEAD_DIM, rw_w), F32)
    pp = dict(mu=mu_shift[l][None], w0=decay_w0[l][None],
              dw2=jnp.concatenate([decay_w2[l], zero_half], axis=0),
              a0=aaa_a0[l][None], aw2=jnp.concatenate([zero_half, aaa_w2[l]], axis=0),
              gw2=gate_w2[l], kk=k_k[l][None], ka=k_a[l][None], rk=r_k[l].reshape(1, rw_w),
              lng=lnx_g[l][None], lnb=lnx_b[l][None])
    w_out_bf = w_out[l].astype(BF16)
    n_e = w_router.shape[2]
    wr = jnp.pad(w_router[l], ((0, 0), (0, LANES - n_e)))
    br = jnp.concatenate([b_router[l], jnp.full((LANES - n_e,), NEG_BIG, F32)])[None]
    wp = dict(wa=w_out_bf[:rw_w], wb=w_out_bf[rw_w:], g2=norm2_g[l][None], wr=wr, br=br)
    g1 = norm1_g[l][None]

    xp = x_prompt[0]
    cos_p, sin_p = _rope_tables(jnp.arange(T))
    prw_p, q_p, k_p, v_p = _inproj(xp, g1, w_in_bf, cos_p, sin_p, 512, rw_cols, q_cols, kv_cols)
    s0_p = jnp.zeros((n_pairs, PAIR, PAIR), F32)
    shift0_p = jnp.zeros((1, rw_cols), F32)
    ya_p, sfin_p, yb_p = _mixers_prompt(prw_p, shift0_p, s0_p, pp, q_p, k_p, v_p, attn_sinks[l], RW_TILE)

    xs_ = x_sample[:, 0]
    cos_s, sin_s = _rope_tables(jnp.full((B,), past_len))
    prw_s, q_s, k_s, v_s = _inproj(xs_, g1, w_in_bf, cos_s, sin_s, ROW_TILE, rw_cols, q_cols, kv_cols)
    s_flat = state_rwkv_wkv[l].reshape(B, H * HEAD_DIM * HEAD_DIM)
    ya_s, snew_flat = _rwkv_step(prw_s, state_rwkv_shift[l], s_flat, pp, n_pairs)
    o2, kc_new, vc_new = _attn_step(q_s.reshape(B * n_q, HEAD_DIM), k_s, v_s,
                                    cache_swa_k[l].reshape(B, wlen, kv_cols),
                                    cache_swa_v[l].reshape(B, wlen, kv_cols),
                                    attn_sinks[l][:, None], n_q, n_kv, past_len)
    yb_s = o2.reshape(B, q_cols)

    rows = T + B
    x1_p, h2_p, gate_p, meta_p, tb_p, tl_p, cnt = _post(xp, ya_p, yb_p, wp, jnp.zeros((1, LANES), F32), MOE_TILE)
    x1_s, h2_s, gate_s, meta_s, tb_s, tl_s, cnt = _post(xs_, ya_s, yb_s, wp, cnt, ROW_TILE)

    counts = cnt[0, :n_e].astype(jnp.int32)
    padded = (counts + MOE_WIN + MOE_BM - 1) // MOE_BM * MOE_BM
    pend = jnp.cumsum(padded)
    pstart = (pend - padded).astype(jnp.int32)
    n_blocks = -(-(rows * TOP_K) // MOE_BM) + n_e + -(-(n_e * MOE_WIN) // MOE_BM)
    block_start = jnp.arange(n_blocks, dtype=jnp.int32) * MOE_BM
    block_e = jnp.minimum(jnp.sum((pend[None, :] <= block_start[:, None]).astype(jnp.int32), axis=1),
                          n_e - 1).astype(jnp.int32)
    n_used = (pend[-1] // MOE_BM).astype(jnp.int32)[None]
    z1 = (pstart + counts // MOE_BM * MOE_BM).astype(jnp.int32)
    z2 = (pend - MOE_BM).astype(jnp.int32)

    n_sorted = n_blocks * MOE_BM
    xs_sorted, carry = _scatter(pstart, z1, z2, tb_p, tl_p, meta_p, h2_p, None, n_sorted, MOE_BM, MOE_TILE)
    xs_sorted, _ = _scatter(pstart, z1, z2, tb_s, tl_s, meta_s, h2_s, (carry, xs_sorted), n_sorted, MOE_BM,
                            ROW_TILE)
    ys_sorted = _experts(block_e, n_used, xs_sorted, w_mlp1[l], b_mlp1[l][:, None], w_mlp2[l],
                         b_mlp2[l][:, None], MOE_BM)
    gf = norm_f_g[None]
    y_p = _combine(pstart, tb_p, tl_p, meta_p, gate_p, x1_p, gf, ys_sorted, MOE_TILE)
    y_s = _combine(pstart, tb_s, tl_s, meta_s, gate_s, x1_s, gf, ys_sorted, ROW_TILE)

    sdt = state_rwkv_wkv.dtype
    return (y_p[None], y_s[:, None],
            _state_from_pairs(sfin_p)[None, None].astype(sdt), prw_p[T - 1][None, None],
            k_p[T - min(WINDOW, T):].reshape(1, 1, -1, n_kv, HEAD_DIM),
            v_p[T - min(WINDOW, T):].reshape(1, 1, -1, n_kv, HEAD_DIM),
            snew_flat.reshape(1, B, H, HEAD_DIM, HEAD_DIM).astype(sdt), prw_s[None],
            kc_new.reshape(1, B, wlen, n_kv, HEAD_DIM), vc_new.reshape(1, B, wlen, n_kv, HEAD_DIM))
```

```python
import functools

import jax
import jax.numpy as jnp
from jax import lax
from jax.experimental import pallas as pl
from jax.experimental.pallas import tpu as pltpu

F32 = jnp.float32
BF16 = jnp.bfloat16

LANES = 128
HEAD_DIM = 64
PAIR = 2 * HEAD_DIM
CHUNK = 64
RW_TILE = 256
RW_PAIRS_PER_STEP = 4
ROT_DIM = 16
ROPE_THETA = 500000.0
WINDOW = 128
PAST_LEN = 16384
ATT_BLOCK = 128
N_EXPERTS = 32
TOP_K = 4
SWIGLU_ALPHA = 1.702
SWIGLU_LIMIT = 7.0
NORM_EPS = 1e-5
LNX_EPS = HEAD_DIM * 1e-5
MOE_BM = 512
ROW_TILE = 128
MOE_WIN_SHIFT = 4
MOE_WIN = 1 << MOE_WIN_SHIFT
SORT_ALIGN_SHIFT = 4
SORT_ALIGN = 1 << SORT_ALIGN_SHIFT
MOE_TILE = 256
NEG_BIG = -1e30
VMEM_LIMIT = 52 * 1024 * 1024

NN = (((1,), (0,)), ((), ()))
NT = (((1,), (1,)), ((), ()))


def _mm(a, b, dn=NN):
    return lax.dot_general(a, b, dn, preferred_element_type=F32)


def _split2(a):
    hi = a.astype(BF16)
    lo = (a - hi.astype(F32)).astype(BF16)
    return hi, lo


def _split3(a):
    hi = a.astype(BF16)
    r1 = a - hi.astype(F32)
    mid = r1.astype(BF16)
    lo = (r1 - mid.astype(F32)).astype(BF16)
    return hi, mid, lo


def _dot1(a, b, dn=NN):
    return _mm(a.astype(BF16), b.astype(BF16), dn)


def _dot_sel_l(sel, b, dn=NN):
    b0, b1, b2 = _split3(b)
    return _mm(sel, b0, dn) + (_mm(sel, b1, dn) + _mm(sel, b2, dn))


def _dot_sel_r(a, sel, dn=NN):
    a0, a1, a2 = _split3(a)
    return _mm(a0, sel, dn) + (_mm(a1, sel, dn) + _mm(a2, sel, dn))


def _iota(shape, dim):
    return lax.broadcasted_iota(jnp.int32, shape, dim)


def _seg_matrix():
    return ((_iota((PAIR, PAIR), 0) // HEAD_DIM) == (_iota((PAIR, PAIR), 1) // HEAD_DIM)).astype(BF16)


def _sigmoid(x):
    return 1.0 / (1.0 + jnp.exp(-x))


def _cparams(sem, vmem=VMEM_LIMIT):
    return pltpu.CompilerParams(dimension_semantics=sem, vmem_limit_bytes=vmem)


def _rope_slab(x, cos, sin_signed):
    lane = _iota(x.shape, 1) % HEAD_DIM
    up = pltpu.roll(x, LANES - ROT_DIM // 2, axis=1)
    down = pltpu.roll(x, ROT_DIM // 2, axis=1)
    partner = jnp.where(lane < ROT_DIM // 2, up, down)
    return x * cos + partner * sin_signed


def _inproj_kernel(rw_cols, q_cols, kv_cols, x_ref, g_ref, w_ref, cos_ref, sin_ref,
                   prw_ref, q_ref, k_ref, v_ref):
    x = x_ref[...]
    h = x * lax.rsqrt(jnp.mean(x * x, axis=-1, keepdims=True) + NORM_EPS) * g_ref[...]
    proj = _mm(h.astype(BF16), w_ref[...])
    prw_ref[...] = proj[:, :rw_cols]
    cos = cos_ref[...]
    sin = sin_ref[...]
    for c in range(q_cols // LANES):
        lo = rw_cols + c * LANES
        q_ref[:, c * LANES:(c + 1) * LANES] = _rope_slab(proj[:, lo:lo + LANES], cos, sin)
    ko = rw_cols + q_cols
    for c in range(kv_cols // LANES):
        k_ref[:, c * LANES:(c + 1) * LANES] = _rope_slab(proj[:, ko + c * LANES:ko + (c + 1) * LANES], cos, sin)
    v_ref[...] = proj[:, ko + kv_cols:ko + 2 * kv_cols]


def _inproj(x, g, w_bf, cos_t, sin_t, tm, rw_cols, q_cols, kv_cols):
    rows, d = x.shape
    cols = w_bf.shape[1]
    full = lambda i: (0, 0)
    row = lambda i: (i, 0)
    return pl.pallas_call(
        functools.partial(_inproj_kernel, rw_cols, q_cols, kv_cols),
        grid=(rows // tm,),
        in_specs=[pl.BlockSpec((tm, d), row), pl.BlockSpec((1, d), full),
                  pl.BlockSpec((d, cols), full),
                  pl.BlockSpec((tm, LANES), row), pl.BlockSpec((tm, LANES), row)],
        out_specs=[pl.BlockSpec((tm, rw_cols), row), pl.BlockSpec((tm, q_cols), row),
                   pl.BlockSpec((tm, kv_cols), row), pl.BlockSpec((tm, kv_cols), row)],
        out_shape=[jax.ShapeDtypeStruct((rows, rw_cols), F32), jax.ShapeDtypeStruct((rows, q_cols), F32),
                   jax.ShapeDtypeStruct((rows, kv_cols), F32), jax.ShapeDtypeStruct((rows, kv_cols), F32)],
        compiler_params=_cparams(("parallel",)),
        name="inproj",
    )(x, g, w_bf, cos_t, sin_t)


def _rwkv_tokenwise(pr, pk, pv, plo, pg, prev_r, prev_k, prev_v, prev_lo, prev_g,
                    mu_r, mu_k, mu_v, mu_lo, mu_g, w0, dw2, a0, aw2, gw2, kkp, kap, rkp, seg):
    r = pr + (prev_r - pr) * mu_r
    k = pk + (prev_k - pk) * mu_k
    v = pv + (prev_v - pv) * mu_v
    lo = plo + (prev_lo - plo) * mu_lo
    gd = pg + (prev_g - pg) * mu_g
    z = -(w0 + _dot1(jnp.tanh(lo), dw2))
    softplus = jnp.maximum(z, 0.0) + jnp.log(1.0 + jnp.exp(-jnp.abs(z)))
    logw = -jnp.exp(-softplus - 0.5)
    a = _sigmoid(a0 + _dot1(lo, aw2))
    g = _dot1(_sigmoid(gd), gw2)
    kk = k * kkp
    nrm = jnp.sqrt(_seg_sum(kk * kk, seg))
    kk = kk / jnp.maximum(nrm, 1e-12)
    k2 = k * (1.0 + (a - 1.0) * kap)
    bonus = _seg_sum(r * k2 * rkp, seg) * v
    return r, k2, v, logw, -kk, kk * a, g, bonus


def _seg_sum(x, seg):
    xh, xl = _split2(x)
    return _mm(xh, seg) + _mm(xl, seg)


def _rwkv_finish(y, bonus, g, lng, lnb, seg):
    mu = _seg_sum(y, seg) * (1.0 / HEAD_DIM)
    d = y - mu
    var = _seg_sum(d * d, seg) * (1.0 / HEAD_DIM)
    yn = d * lax.rsqrt(var + LNX_EPS) * lng + lnb
    return (yn + bonus) * g


def _rwkv_prompt_kernel(pps, pr_ref, pk_ref, pv_ref, plo_ref, pg_ref,
                        hr_ref, hk_ref, hv_ref, hlo_ref, hg_ref,
                        s0r_ref, s0k_ref, s0v_ref, s0lo_ref, s0g_ref,
                        mur_ref, muk_ref, muv_ref, mulo_ref, mug_ref,
                        w0_ref, dw2_ref, a0_ref, aw2_ref, gw2_ref, kk_ref, ka_ref, rk_ref,
                        lng_ref, lnb_ref, sin_ref,
                        sink_ref, q_ref, kc_ref, kp_ref, vc_ref, vp_ref,
                        x_ref, wa_ref, wb_ref, g2_ref, wr_ref, br_ref,
                        sout_ref, x1_ref, h2_ref, gate_ref, meta_ref, tb_ref, tl_ref, cnt_ref,
                        st_ref, ya_sc, yb_sc, carry_ref):
    i = pl.program_id(1)
    n_i = pl.num_programs(1)
    tt = pr_ref.shape[0]
    slot = i % 2
    y_ref = ya_sc.at[slot]
    yb_ref = yb_sc.at[slot]

    @pl.when(i == 0)
    def _():
        st_ref[...] = sin_ref[...]
        ya_sc[...] = jnp.zeros_like(ya_sc)
        yb_sc[...] = jnp.zeros_like(yb_sc)
        carry_ref[...] = jnp.zeros_like(carry_ref)

    n_q = q_ref.shape[1] // HEAD_DIM
    attn = _attn_prompt_stages(i, n_q, n_q // (kc_ref.shape[1] // HEAD_DIM), sink_ref,
                               q_ref, kc_ref, kp_ref, vc_ref, vp_ref, yb_ref)
    post = _post_stages((i > 0).astype(F32), x_ref, ya_sc.at[1 - slot], yb_sc.at[1 - slot],
                        wa_ref, wb_ref, g2_ref, wr_ref, br_ref,
                        x1_ref, h2_ref, gate_ref, meta_ref, tb_ref, tl_ref, cnt_ref, carry_ref)

    row = _iota((tt, PAIR), 0)

    def prev_of(cur, halo_row, s0_row):
        first = jnp.where(i == 0, s0_row, halo_row)
        return jnp.where(row == 0, first, pltpu.roll(cur, 1, axis=0))

    plo = plo_ref[...]
    pg = pg_ref[...]
    prev_lo = prev_of(plo, hlo_ref[7:8, :], s0lo_ref[...])
    prev_g = prev_of(pg, hg_ref[7:8, :], s0g_ref[...])
    ti = _iota((tt, tt), 0)
    tj = _iota((tt, tt), 1)
    same_chunk = (ti // CHUNK) == (tj // CHUNK)
    incl = same_chunk & (tj <= ti)
    strict = same_chunk & (tj < ti)
    seg = _seg_matrix()
    lane = _iota((tt, PAIR), 1)
    eye = (ti == tj).astype(F32)
    pairs = []
    for p in range(pps):
        ls = slice(p * PAIR, (p + 1) * PAIR)
        pr, pk, pv = pr_ref[:, ls], pk_ref[:, ls], pv_ref[:, ls]
        r, k2, v, logw, nkk, b, g, bonus = _rwkv_tokenwise(
            pr, pk, pv, plo, pg,
            prev_of(pr, hr_ref[7:8, ls], s0r_ref[:, ls]), prev_of(pk, hk_ref[7:8, ls], s0k_ref[:, ls]),
            prev_of(pv, hv_ref[7:8, ls], s0v_ref[:, ls]), prev_lo, prev_g,
            mur_ref[:, ls], muk_ref[:, ls], muv_ref[:, ls], mulo_ref[...], mug_ref[...],
            w0_ref[:, ls], dw2_ref[:, ls], a0_ref[:, ls], aw2_ref[:, ls], gw2_ref[:, ls],
            kk_ref[:, ls], ka_ref[:, ls], rk_ref[:, ls], seg)
        pairs.append(dict(ls=ls, r=r, k2=k2, v=v, logw=logw, nkk=nkk, b=b, g=g, bonus=bonus))

    incl_b = incl.astype(BF16)
    for q in pairs:
        q["cs"] = _dot_sel_l(incl_b, q["logw"])
    for q in pairs:
        cs = q["cs"]
        gam = jnp.exp(cs)
        inv = jnp.exp(-cs)
        q["a_t"] = jnp.exp(cs - q["logw"]) * q["nkk"]
        q["r_t"] = gam * q["r"]
        q["bt_T"] = (q["b"] * inv).T
        q["kt_T"] = (q["k2"] * inv).T
        q["gam_T"] = gam.T
        q["bk_T"] = jnp.concatenate([q["bt_T"], q["kt_T"]], axis=1).astype(BF16)
    attn[0]()
    attn[1]()
    post[0]()

    heads = []
    for q in pairs:
        for hh in range(2):
            hm = (lane // HEAD_DIM) == hh
            heads.append(dict(q=q, a=jnp.where(hm, q["a_t"], 0.0), r=jnp.where(hm, q["r_t"], 0.0),
                              v=jnp.where(hm, q["v"], 0.0)))
    for h in heads:
        h["g"] = _mm(jnp.concatenate([h["a"], h["r"]], axis=0).astype(BF16), h["q"]["bk_T"])
    for h in heads:
        gmat = h["g"]
        l_ab = jnp.where(strict, gmat[:tt, :tt], 0.0)
        h["l_ak_m_rk"] = jnp.concatenate([jnp.where(strict, gmat[:tt, tt:], 0.0),
                                          jnp.where(incl, gmat[tt:, tt:], 0.0)], axis=0).astype(BF16)
        h["m_rb"] = jnp.where(incl, gmat[tt:, :tt], 0.0).astype(BF16)
        h["tm"] = eye + l_ab
        h["lp"] = l_ab.astype(BF16)
    post[1]()
    for h in heads:
        h["lp"] = _mm(h["lp"], h["lp"]).astype(BF16)
    for it in range(4):
        for h in heads:
            h["both"] = _mm(jnp.concatenate([h["tm"].astype(BF16), h["lp"]], axis=0), h["lp"])
        attn[2 + it]()
        post[2 + it]()
        for h in heads:
            h["tm"] = h["tm"] + h["both"][:tt]
            h["lp"] = h["both"][tt:].astype(BF16)
    for h in heads:
        h["pq"] = _mm(h["l_ak_m_rk"], h["v"].astype(BF16))
        h["tm"] = h["tm"] + _mm(h["tm"].astype(BF16), h["lp"])
    attn[6]()
    post[6]()
    for h in heads:
        h["tx"] = _mm(h["tm"].astype(BF16),
                      jnp.concatenate([h["a"], h["pq"][:tt]], axis=1).astype(BF16))
    for h in heads:
        h["rx"] = _mm(h["m_rb"], h["tx"].astype(BF16))
    for n, q in enumerate(pairs):
        h0, h1 = heads[2 * n], heads[2 * n + 1]
        q["tatp"] = (h0["tx"] + h1["tx"]).astype(BF16)
        ryc = (h0["rx"] + h1["rx"]) + jnp.concatenate([h0["r"] + h1["r"], h0["pq"][tt:] + h1["pq"][tt:]], axis=1)
        q["ry"] = ryc[:, :PAIR]
        q["yc"] = ryc[:, PAIR:]
        q["v_b"] = q["v"].astype(BF16)
        q["bt_b"] = q["bt_T"].astype(BF16)
        q["kt_b"] = q["kt_T"].astype(BF16)
        q["s"] = st_ref[n]

    bd = seg.astype(F32)
    eye_p = (_iota((PAIR, PAIR), 0) == _iota((PAIR, PAIR), 1)).astype(F32)
    col_t = _iota((PAIR, tt), 1)
    zb = jnp.zeros((PAIR, tt), BF16)
    n_chunks = tt // CHUNK
    for c in range(n_chunks):
        cm = (col_t // CHUNK) == c
        for q in pairs:
            bt_c = jnp.where(cm, q["bt_b"], zb)
            kt_c = jnp.where(cm, q["kt_b"], zb)
            dcol = q["gam_T"][:, (c + 1) * CHUNK - 1:(c + 1) * CHUNK]
            bx = _mm(bt_c, q["tatp"])
            q["mc", c] = (dcol * (eye_p + bd * bx[:, :PAIR])).astype(BF16)
            q["nc", c] = dcol * (bd * (bx[:, PAIR:] + _mm(kt_c, q["v_b"])))
    for c in range(n_chunks):
        sl = slice(c * CHUNK, (c + 1) * CHUNK)
        for q in pairs:
            s_b = q["s"].astype(BF16)
            q["y", c] = _mm(q["ry"][sl].astype(BF16), s_b) + q["yc"][sl]
            q["s"] = _mm(q["mc", c], s_b) + q["nc", c]
    for q in pairs:
        y = jnp.concatenate([q["y", c] for c in range(n_chunks)], axis=0)
        y_ref[:, q["ls"]] = _rwkv_finish(y, q["bonus"], q["g"], lng_ref[:, q["ls"]], lnb_ref[:, q["ls"]], seg)
    for n, q in enumerate(pairs):
        st_ref[n] = q["s"]

    @pl.when(i == n_i - 2)
    def _():
        sout_ref[...] = st_ref[...]


def _layer_prompt(x, prw, shift0, s0_pairs, pp, q, k, v, sinks, wp, tt):
    T, d = x.shape
    n_t = T // tt
    n_pairs = s0_pairs.shape[0]
    pps = RW_PAIRS_PER_STEP
    n_grp = n_pairs // pps
    assert n_grp == 1 and tt % ATT_BLOCK == 0
    qw, kvw = q.shape[1], k.shape[1]
    ab = tt // ATT_BLOCK
    gw = pps * PAIR
    wcols = n_pairs * PAIR
    lo_col = 3 * wcols
    g_col = lo_col + PAIR
    hb = tt // 8
    mix_tile = lambda i: jnp.minimum(i, n_t - 1)
    post_tile = lambda i: jnp.maximum(i - 1, 0)

    def cur(off):
        return pl.BlockSpec((tt, gw), lambda p, i: (mix_tile(i), off // gw + p))

    def cur_fixed(col):
        return pl.BlockSpec((tt, PAIR), lambda p, i: (mix_tile(i), col // PAIR))

    def halo(off):
        return pl.BlockSpec((8, gw), lambda p, i: (jnp.maximum(mix_tile(i) * hb - 1, 0), off // gw + p))

    def halo_fixed(col):
        return pl.BlockSpec((8, PAIR), lambda p, i: (jnp.maximum(mix_tile(i) * hb - 1, 0), col // PAIR))

    def vec(off):
        return pl.BlockSpec((1, gw), lambda p, i: (0, off // gw + p))

    def vec_fixed(col):
        return pl.BlockSpec((1, PAIR), lambda p, i: (0, col // PAIR))

    def wmat(rows):
        return pl.BlockSpec((rows, gw), lambda p, i: (0, p))

    in_specs = ([cur(0), cur(wcols), cur(2 * wcols), cur_fixed(lo_col), cur_fixed(g_col)]
                + [halo(0), halo(wcols), halo(2 * wcols), halo_fixed(lo_col), halo_fixed(g_col)]
                + [vec(0), vec(wcols), vec(2 * wcols), vec_fixed(lo_col), vec_fixed(g_col)]
                + [vec(0), vec(wcols), vec(2 * wcols), vec_fixed(lo_col), vec_fixed(g_col)]
                + [vec(0), wmat(PAIR), vec(0), wmat(PAIR), wmat(PAIR), vec(0), vec(0), vec(0), vec(0), vec(0)]
                + [pl.BlockSpec((pps, PAIR, PAIR), lambda p, i: (p, 0, 0))])
    tile = lambda p, i: (mix_tile(i), 0)
    before = lambda p, i: (jnp.maximum(mix_tile(i) * ab - 1, 0), 0)
    full = lambda p, i: (0, 0)
    ptile = lambda p, i: (post_tile(i), 0)
    ptile3 = lambda p, i: (post_tile(i), 0, 0)
    half = wp["wa"].shape[0]
    in_specs += [pl.BlockSpec(memory_space=pltpu.SMEM), pl.BlockSpec((tt, qw), tile),
                 pl.BlockSpec((tt, kvw), tile), pl.BlockSpec((ATT_BLOCK, kvw), before),
                 pl.BlockSpec((tt, kvw), tile), pl.BlockSpec((ATT_BLOCK, kvw), before)]
    in_specs += [pl.BlockSpec((tt, d), ptile), pl.BlockSpec((half, d), full), pl.BlockSpec((half, d), full),
                 pl.BlockSpec((1, d), full), pl.BlockSpec((d, LANES), full), pl.BlockSpec((1, LANES), full)]
    args = ([prw] * 5 + [prw] * 5 + [shift0] * 5 + [pp["mu"]] * 5
            + [pp["w0"], pp["dw2"], pp["a0"], pp["aw2"], pp["gw2"], pp["kk"], pp["ka"], pp["rk"],
               pp["lng"], pp["lnb"], s0_pairs]
            + [sinks, q, k, k, v, v]
            + [x, wp["wa"], wp["wb"], wp["g2"], wp["wr"], wp["br"]])
    return pl.pallas_call(
        functools.partial(_rwkv_prompt_kernel, pps),
        grid=(n_grp, n_t + 1),
        in_specs=in_specs,
        out_specs=[pl.BlockSpec((pps, PAIR, PAIR), lambda p, i: (p, 0, 0)),
                   pl.BlockSpec((tt, d), ptile), pl.BlockSpec((tt, d), ptile),
                   pl.BlockSpec((tt, TOP_K), ptile), pl.BlockSpec((tt, 2 * TOP_K), ptile),
                   pl.BlockSpec((1, 1, LANES), ptile3), pl.BlockSpec((1, 1, LANES), ptile3),
                   pl.BlockSpec((1, LANES), full)],
        out_shape=[jax.ShapeDtypeStruct((n_pairs, PAIR, PAIR), F32),
                   jax.ShapeDtypeStruct((T, d), F32), jax.ShapeDtypeStruct((T, d), F32),
                   jax.ShapeDtypeStruct((T, TOP_K), F32), jax.ShapeDtypeStruct((T, 2 * TOP_K), jnp.int32),
                   jax.ShapeDtypeStruct((n_t, 1, LANES), jnp.int32),
                   jax.ShapeDtypeStruct((n_t, 1, LANES), jnp.int32),
                   jax.ShapeDtypeStruct((1, LANES), F32)],
        scratch_shapes=[pltpu.VMEM((pps, PAIR, PAIR), F32),
                        pltpu.VMEM((2, tt, wcols), F32), pltpu.VMEM((2, tt, qw), F32),
                        pltpu.VMEM((1, LANES), F32)],
        compiler_params=_cparams(("arbitrary", "arbitrary")),
        name="layer_prompt",
    )(*args)


def _rwkv_step_kernel(slabs_per_step, pr_ref, pk_ref, pv_ref, plo_ref, pg_ref,
                      sr_ref, sk_ref, sv_ref, slo_ref, sg_ref,
                      mur_ref, muk_ref, muv_ref, mulo_ref, mug_ref,
                      w0_ref, dw2_ref, a0_ref, aw2_ref, gw2_ref, kk_ref, ka_ref, rk_ref,
                      lng_ref, lnb_ref, s_ref,
                      y_ref, snew_ref, yacc_ref):
    j = pl.program_id(1)
    n_j = pl.num_programs(1)
    seg = _seg_matrix()
    r, k2, v, logw, nkk, b, g, bonus = _rwkv_tokenwise(
        pr_ref[...], pk_ref[...], pv_ref[...], plo_ref[...], pg_ref[...],
        sr_ref[...], sk_ref[...], sv_ref[...], slo_ref[...], sg_ref[...],
        mur_ref[...], muk_ref[...], muv_ref[...], mulo_ref[...], mug_ref[...],
        w0_ref[...], dw2_ref[...], a0_ref[...], aw2_ref[...], gw2_ref[...],
        kk_ref[...], ka_ref[...], rk_ref[...], seg)
    w = jnp.exp(logw)

    @pl.when(j == 0)
    def _():
        yacc_ref[...] = jnp.zeros_like(yacc_ref)

    slabs_per_head = HEAD_DIM // 2
    ci = _iota((PAIR, PAIR), 0)
    li = _iota((PAIR, PAIR), 1)
    assert slabs_per_head % slabs_per_step == 0
    yacc = yacc_ref[...]
    hh = (j * slabs_per_step) // slabs_per_head
    dup = ((ci == hh * HEAD_DIM + li % HEAD_DIM)).astype(BF16)
    nkk_d, w_d, b_d, k_d, r_d = [_dot_sel_r(x, dup) for x in (nkk, w, b, k2, r)]
    slabs = range(slabs_per_step)
    i0 = [2 * ((j * slabs_per_step + t) % slabs_per_head) for t in slabs]
    s = [s_ref[:, t * PAIR:(t + 1) * PAIR] for t in slabs]
    sa = [_seg_sum(s[t] * nkk_d, seg) for t in slabs]
    v_bc = [_seg_sum(v, (ci == hh * HEAD_DIM + i0[t] + li // HEAD_DIM).astype(BF16)) for t in slabs]
    s_new = [s[t] * w_d + sa[t] * b_d + v_bc[t] * k_d for t in slabs]
    for t in slabs:
        snew_ref[:, t * PAIR:(t + 1) * PAIR] = s_new[t]
    yred = [_seg_sum(s_new[t] * r_d, seg) for t in slabs]
    ysel = [_seg_sum(yred[t], ((ci % HEAD_DIM == 0)
                               & (li == hh * HEAD_DIM + i0[t] + ci // HEAD_DIM)).astype(BF16)) for t in slabs]
    for t in slabs:
        yacc = yacc + ysel[t]
    yacc_ref[...] = yacc

    @pl.when(j == n_j - 1)
    def _():
        y_ref[...] = _rwkv_finish(yacc, bonus, g, lng_ref[...], lnb_ref[...], seg)


def _rwkv_step(prw, shift, s_flat, pp, n_pairs):
    B = prw.shape[0]
    lanes_per_pair = 2 * HEAD_DIM * HEAD_DIM
    blk = 1024
    slabs_per_step = blk // PAIR
    steps = lanes_per_pair // blk
    lo_blk = 3 * n_pairs
    g_blk = lo_blk + 1

    def cur(off):
        return pl.BlockSpec((B, PAIR), lambda p, j: (0, off + p))

    def cur_fixed(b_):
        return pl.BlockSpec((B, PAIR), lambda p, j: (0, b_))

    def vec(off):
        return pl.BlockSpec((1, PAIR), lambda p, j: (0, off + p))

    def vec_fixed(b_):
        return pl.BlockSpec((1, PAIR), lambda p, j: (0, b_))

    def wmat(rows):
        return pl.BlockSpec((rows, PAIR), lambda p, j: (0, p))

    sspec = pl.BlockSpec((B, blk), lambda p, j: (0, p * steps + j))
    in_specs = ([cur(0), cur(n_pairs), cur(2 * n_pairs), cur_fixed(lo_blk), cur_fixed(g_blk)] * 2
                + [vec(0), vec(n_pairs), vec(2 * n_pairs), vec_fixed(lo_blk), vec_fixed(g_blk)]
                + [vec(0), wmat(PAIR), vec(0), wmat(PAIR), wmat(PAIR), vec(0), vec(0), vec(0), vec(0), vec(0)]
                + [sspec])
    args = ([prw] * 5 + [shift] * 5 + [pp["mu"]] * 5
            + [pp["w0"], pp["dw2"], pp["a0"], pp["aw2"], pp["gw2"], pp["kk"], pp["ka"], pp["rk"],
               pp["lng"], pp["lnb"], s_flat])
    return pl.pallas_call(
        functools.partial(_rwkv_step_kernel, slabs_per_step),
        grid=(n_pairs, steps),
        in_specs=in_specs,
        out_specs=[pl.BlockSpec((B, PAIR), lambda p, j: (0, p)), sspec],
        out_shape=[jax.ShapeDtypeStruct((B, n_pairs * PAIR), F32),
                   jax.ShapeDtypeStruct(s_flat.shape, F32)],
        scratch_shapes=[pltpu.VMEM((B, PAIR), F32)],
        compiler_params=_cparams(("parallel", "arbitrary")),
        name="rwkv_step",
    )(*args)


def _attn_prompt_stages(tile_idx, n_q, group, sink_ref, q_ref, kc_ref, kp_ref, vc_ref, vp_ref, o_ref):
    blk = ATT_BLOCK
    n_blk = q_ref.shape[0] // blk
    n_kv = n_q // group
    inst = [(j, h) for j in range(n_blk) for h in range(n_q)]
    st = {}

    def prepare():
        q = q_ref[...] * (HEAD_DIM ** -0.5)
        kc = kc_ref[...]
        vc = vc_ref[...]
        kall = jnp.concatenate([kp_ref[...], kc], axis=0)
        vall = jnp.concatenate([vp_ref[...], vc], axis=0)
        rq = _iota((blk, 2 * blk), 0)
        ck = _iota((blk, 2 * blk), 1)
        dist = rq - ck + blk
        in_window = (dist >= 0) & (dist < WINDOW)
        kpos0 = tile_idx * (n_blk * blk) - blk + ck
        st["valid"] = [in_window & (kpos0 >= 0)] + [in_window] * (n_blk - 1)
        st["q"] = {(j, h): q[j * blk:(j + 1) * blk, h * HEAD_DIM:(h + 1) * HEAD_DIM].astype(BF16) for j, h in inst}
        st["kb"] = {(j, g): kall[j * blk:(j + 2) * blk, g * HEAD_DIM:(g + 1) * HEAD_DIM].astype(BF16)
                    for j in range(n_blk) for g in range(n_kv)}
        st["vb"] = {(j, g): vall[j * blk:(j + 2) * blk, g * HEAD_DIM:(g + 1) * HEAD_DIM].astype(BF16)
                    for j in range(n_blk) for g in range(n_kv)}

    def scores():
        st["s"] = {(j, h): jnp.where(st["valid"][j], _mm(st["q"][j, h], st["kb"][j, h // group], NT), NEG_BIG)
                   for j, h in inst}

    def row_max():
        st["m"] = {(j, h): jnp.maximum(jnp.max(st["s"][j, h], axis=-1, keepdims=True), sink_ref[h]) for j, h in inst}

    def probs():
        st["p"] = {(j, h): jnp.exp(st["s"][j, h] - st["m"][j, h]) for j, h in inst}

    def denominators():
        st["d"] = {(j, h): jnp.sum(st["p"][j, h], axis=-1, keepdims=True) + jnp.exp(sink_ref[h] - st["m"][j, h])
                   for j, h in inst}

    def weighted_values():
        st["o"] = {(j, h): _mm(st["p"][j, h].astype(BF16), st["vb"][j, h // group]) for j, h in inst}

    def store():
        for j, h in inst:
            o_ref[j * blk:(j + 1) * blk, h * HEAD_DIM:(h + 1) * HEAD_DIM] = st["o"][j, h] / st["d"][j, h]

    return [prepare, scores, row_max, probs, denominators, weighted_values, store]


def _attn_step_kernel(n_q, group, pos0, sink_ref, q_ref, kn_ref, vn_ref, kc_ref, vc_ref,
                      o_ref, ko_ref, vo_ref):
    bb, wlen, kvw = kc_ref.shape
    rows = bb * n_q
    lane = _iota((rows, kvw), 1)
    rowh = _iota((rows, kvw), 0) % n_q
    mine = (lane // HEAD_DIM) == (rowh // group)
    dupm = (_iota((HEAD_DIM, kvw), 0) == _iota((HEAD_DIM, kvw), 1) % HEAD_DIM).astype(BF16)
    fold = (_iota((kvw, HEAD_DIM), 0) % HEAD_DIM == _iota((kvw, HEAD_DIM), 1)).astype(BF16)
    kidx = _iota((n_q, wlen), 1)
    dist = wlen - kidx
    valid = (dist < WINDOW) & (pos0 - dist >= 0)
    rk = _iota((wlen, kvw), 0)
    sink = sink_ref[...]
    qm_all = jnp.where(mine, _dot_sel_r(q_ref[...] * (HEAD_DIM ** -0.5), dupm), 0.0)
    elems = range(bb)
    qm = [qm_all[t * n_q:(t + 1) * n_q] for t in elems]
    kc = [kc_ref[t] for t in elems]
    vc = [vc_ref[t] for t in elems]
    kn = [kn_ref[t:t + 1, :] for t in elems]
    vn = [vn_ref[t:t + 1, :] for t in elems]
    s = [jnp.where(valid, _dot1(qm[t], kc[t], NT), NEG_BIG) for t in elems]
    s_new = [jnp.sum(qm[t] * kn[t], axis=-1, keepdims=True) for t in elems]
    m = [jnp.maximum(jnp.maximum(jnp.max(s[t], axis=-1, keepdims=True), s_new[t]), sink) for t in elems]
    p = [jnp.exp(s[t] - m[t]) for t in elems]
    p_new = [jnp.exp(s_new[t] - m[t]) for t in elems]
    denom = [jnp.sum(p[t], axis=-1, keepdims=True) + p_new[t] + jnp.exp(sink - m[t]) for t in elems]
    res = [(_dot1(p[t], vc[t]) + p_new[t] * vn[t]) / denom[t] for t in elems]
    o_ref[...] = _dot_sel_r(jnp.where(mine, jnp.concatenate(res, axis=0), 0.0), fold)
    for t in elems:
        ko_ref[t] = jnp.where(rk == wlen - 1, kn[t], pltpu.roll(kc[t], wlen - 1, axis=0))
        vo_ref[t] = jnp.where(rk == wlen - 1, vn[t], pltpu.roll(vc[t], wlen - 1, axis=0))


def _attn_step(q2, k_new, v_new, k_cache, v_cache, sinks_col, n_q, n_kv, pos0):
    B, wlen, kvw = k_cache.shape
    bb = 8
    return pl.pallas_call(
        functools.partial(_attn_step_kernel, n_q, n_q // n_kv, pos0),
        grid=(B // bb,),
        in_specs=[pl.BlockSpec((n_q, 1), lambda i: (0, 0)),
                  pl.BlockSpec((bb * n_q, HEAD_DIM), lambda i: (i, 0)),
                  pl.BlockSpec((bb, kvw), lambda i: (i, 0)), pl.BlockSpec((bb, kvw), lambda i: (i, 0)),
                  pl.BlockSpec((bb, wlen, kvw), lambda i: (i, 0, 0)),
                  pl.BlockSpec((bb, wlen, kvw), lambda i: (i, 0, 0))],
        out_specs=[pl.BlockSpec((bb * n_q, HEAD_DIM), lambda i: (i, 0)),
                   pl.BlockSpec((bb, wlen, kvw), lambda i: (i, 0, 0)),
                   pl.BlockSpec((bb, wlen, kvw), lambda i: (i, 0, 0))],
        out_shape=[jax.ShapeDtypeStruct((B * n_q, HEAD_DIM), F32),
                   jax.ShapeDtypeStruct((B, wlen, kvw), F32),
                   jax.ShapeDtypeStruct((B, wlen, kvw), F32)],
        compiler_params=_cparams(("parallel",)),
        name="attn_step",
    )(sinks_col, q2, k_new, v_new, k_cache, v_cache)


def _post_stages(counted, x_ref, ya_ref, yb_ref, wa_ref, wb_ref, g2_ref, wr_ref, br_ref,
                 x1_ref, h2_ref, gate_ref, meta_ref, tb_ref, tl_ref, cnt_ref, carry_ref):
    st = {}

    def residual_and_norm():
        mix = _mm(ya_ref[...].astype(BF16), wa_ref[...]) + _mm(yb_ref[...].astype(BF16), wb_ref[...])
        x1 = x_ref[...] + mix
        h2 = x1 * lax.rsqrt(jnp.mean(x1 * x1, axis=-1, keepdims=True) + NORM_EPS) * g2_ref[...]
        x1_ref[...] = x1
        h2_ref[...] = h2
        st["h2"] = h2

    def logits():
        st["l"] = _dot1(st["h2"], wr_ref[...]) + br_ref[...]
        st["vals"], st["idxs"] = [], []

    def next_expert():
        l = st["l"]
        lane = _iota(l.shape, 1)
        m = jnp.max(l, axis=-1, keepdims=True)
        sel = jnp.min(jnp.where(l == m, lane, LANES), axis=-1, keepdims=True)
        st["vals"].append(m)
        st["idxs"].append(sel)
        st["l"] = jnp.where(lane == sel, -jnp.inf, l)

    def gates_and_ranks():
        vals, idxs = st["vals"], st["idxs"]
        tm = vals[0].shape[0]
        lane = _iota((tm, LANES), 1)
        es = [jnp.exp(v - vals[0]) for v in vals]
        tot = es[0] + es[1] + es[2] + es[3]
        onehot = jnp.zeros((tm, LANES), F32)
        for sel in idxs:
            onehot = onehot + (lane == sel).astype(F32)
        strict = (_iota((tm, tm), 1) < _iota((tm, tm), 0)).astype(BF16)
        before = _mm(strict, onehot.astype(BF16))
        for k in range(TOP_K):
            gate_ref[:, k:k + 1] = es[k] / tot
            meta_ref[:, k:k + 1] = idxs[k]
            meta_ref[:, TOP_K + k:TOP_K + k + 1] = jnp.sum(
                jnp.where(lane == idxs[k], before, 0.0), axis=-1, keepdims=True).astype(jnp.int32)
        carry = carry_ref[...]
        cnt_t = jnp.sum(onehot, axis=0, keepdims=True) * counted
        tb_ref[0] = carry.astype(jnp.int32)
        tl_ref[0] = cnt_t.astype(jnp.int32)
        carry_ref[...] = carry + cnt_t
        cnt_ref[...] = carry + cnt_t

    return [residual_and_norm, logits] + [next_expert] * TOP_K + [gates_and_ranks]


def _post_kernel(x_ref, ya_ref, yb_ref, wa_ref, wb_ref, g2_ref, wr_ref, br_ref, cnt0_ref,
                 x1_ref, h2_ref, gate_ref, meta_ref, tb_ref, tl_ref, cnt_ref, carry_ref):
    @pl.when(pl.program_id(0) == 0)
    def _():
        carry_ref[...] = cnt0_ref[...]

    for stage in _post_stages(1.0, x_ref, ya_ref, yb_ref, wa_ref, wb_ref, g2_ref, wr_ref, br_ref,
                              x1_ref, h2_ref, gate_ref, meta_ref, tb_ref, tl_ref, cnt_ref, carry_ref):
        stage()


def _post(x, ya, yb, wp, cnt0, tm):
    rows, d = x.shape
    half = ya.shape[1]
    n_t = rows // tm
    full = lambda i: (0, 0)
    row = lambda i: (i, 0)
    trow = lambda i: (i, 0, 0)
    return pl.pallas_call(
        _post_kernel,
        grid=(n_t,),
        in_specs=[pl.BlockSpec((tm, d), row), pl.BlockSpec((tm, half), row), pl.BlockSpec((tm, half), row),
                  pl.BlockSpec((half, d), full), pl.BlockSpec((half, d), full),
                  pl.BlockSpec((1, d), full), pl.BlockSpec((d, LANES), full), pl.BlockSpec((1, LANES), full),
                  pl.BlockSpec((1, LANES), full)],
        out_specs=[pl.BlockSpec((tm, d), row), pl.BlockSpec((tm, d), row),
                   pl.BlockSpec((tm, TOP_K), row), pl.BlockSpec((tm, 2 * TOP_K), row),
                   pl.BlockSpec((1, 1, LANES), trow), pl.BlockSpec((1, 1, LANES), trow),
                   pl.BlockSpec((1, LANES), full)],
        out_shape=[jax.ShapeDtypeStruct((rows, d), F32), jax.ShapeDtypeStruct((rows, d), F32),
                   jax.ShapeDtypeStruct((rows, TOP_K), F32),
                   jax.ShapeDtypeStruct((rows, 2 * TOP_K), jnp.int32),
                   jax.ShapeDtypeStruct((n_t, 1, LANES), jnp.int32),
                   jax.ShapeDtypeStruct((n_t, 1, LANES), jnp.int32),
                   jax.ShapeDtypeStruct((1, LANES), F32)],
        scratch_shapes=[pltpu.VMEM((1, LANES), F32)],
        compiler_params=_cparams(("arbitrary",)),
        name="post",
    )(x, ya, yb, wp["wa"], wp["wb"], wp["g2"], wp["wr"], wp["br"], cnt0)


def _n_windows(base, length):
    off = base & (SORT_ALIGN - 1)
    n = lax.shift_right_logical(off + length + (MOE_WIN - 1), MOE_WIN_SHIFT)
    return off, jnp.where(length > 0, n, 0)


def _window_targets(meta, tb_vec, tl_vec):
    tm = meta.shape[0]
    off, n_win = _n_windows(tb_vec, tl_vec)
    upper = (_iota((LANES, LANES), 0) < _iota((LANES, LANES), 1)).astype(BF16)
    slot_start = _mm(n_win.astype(F32).astype(BF16), upper)
    pos0 = slot_start * MOE_WIN + off.astype(F32)
    lane = _iota((tm, LANES), 1)
    tgts = []
    for k in range(TOP_K):
        p0 = jnp.sum(jnp.where(lane == meta[:, k:k + 1], pos0, 0.0), axis=-1, keepdims=True)
        tgts.append(p0.astype(jnp.int32) + meta[:, TOP_K + k:TOP_K + k + 1])
    return tgts


def _for_each_window(n_e, pstart_ref, tb_ref, tl_ref, fn, rows_ref, cnt_ref, b, per_expert_fn=None):
    def per_expert(e, slot0):
        base = tb_ref[0, 0, e]
        length = tl_ref[0, 0, e]
        off, n = _n_windows(base, length)
        row0 = pstart_ref[e] + base - off
        if per_expert_fn is not None:
            per_expert_fn(e, slot0, off, length, n)

        def per_window(w, c):
            row = row0 + w * MOE_WIN
            rows_ref[b, slot0 + w] = row
            fn(slot0 + w, pl.multiple_of(row, SORT_ALIGN))
            return c

        lax.fori_loop(0, n, per_window, 0)
        return slot0 + n

    cnt_ref[b] = lax.fori_loop(0, n_e, per_expert, 0)


def _for_recorded_windows(fn, rows_ref, cnt_ref, b):
    def body(slot, c):
        fn(slot, pl.multiple_of(rows_ref[b, slot], SORT_ALIGN))
        return c

    lax.fori_loop(0, cnt_ref[b], body, 0)


def _moe_slots(tm):
    n = -(-(tm * TOP_K + N_EXPERTS * (SORT_ALIGN - 1 + MOE_WIN - 1)) // MOE_WIN)
    per_lane_tile = LANES // MOE_WIN
    return -(-n // per_lane_tile) * per_lane_tile


def _scatter_kernel(n_e, continues, pstart_ref, z1_ref, z2_ref, tb_ref, tl_ref, tbv_ref, tlv_ref,
                    meta_ref, h_ref, *rest):
    if continues:
        cin_ref, _xs_alias, xs_ref, cout_ref, xw_ref, zero_ref, carry_ref, rows_ref, cnt_ref, sem, zsem = rest
    else:
        xs_ref, cout_ref, xw_ref, zero_ref, carry_ref, rows_ref, cnt_ref, sem, zsem = rest
    i = pl.program_id(0)
    n_i = pl.num_programs(0)
    tm = h_ref.shape[0]
    bm = zero_ref.shape[0]
    buf = i % 2
    n_rows_w = xw_ref.shape[1]

    @pl.when(i == 0)
    def _():
        if continues:
            carry_ref[...] = cin_ref[...]
        else:
            zero_ref[...] = jnp.zeros_like(zero_ref)

            def zcopy(row):
                return pltpu.make_async_copy(zero_ref, xs_ref.at[pl.ds(pl.multiple_of(row, SORT_ALIGN), bm)], zsem)

            for e in range(n_e):
                zcopy(z1_ref[e]).start()

                @pl.when(z2_ref[e] != z1_ref[e])
                def _():
                    zcopy(z2_ref[e]).start()
            for e in range(n_e):
                zcopy(z1_ref[e]).wait()

                @pl.when(z2_ref[e] != z1_ref[e])
                def _():
                    zcopy(z2_ref[e]).wait()

            carry_ref[...] = jnp.zeros_like(carry_ref)

    tgts = _window_targets(meta_ref[...], tbv_ref[0], tlv_ref[0])
    lane_s = _iota((tm, n_rows_w), 1)
    sel = jnp.zeros((tm, n_rows_w), F32)
    for tgt in tgts:
        sel = sel + (lane_s == tgt).astype(F32)
    xw_ref[buf] = _mm(sel.T.astype(BF16), h_ref[...].astype(BF16)).astype(BF16)

    def splice_carry(e, slot0, off, length, n):
        @pl.when(n > 0)
        def _():
            g0 = pl.multiple_of(slot0 * MOE_WIN, MOE_WIN)
            xw_ref[buf, pl.ds(g0, SORT_ALIGN), :] = xw_ref[buf, pl.ds(g0, SORT_ALIGN), :] + carry_ref[e]
            filled = off + length
            gl = pl.multiple_of(g0 + lax.shift_right_logical(filled, SORT_ALIGN_SHIFT) * SORT_ALIGN, SORT_ALIGN)
            last = xw_ref[buf, pl.ds(gl, SORT_ALIGN), :]
            carry_ref[e] = jnp.where((filled & (SORT_ALIGN - 1)) != 0, last, jnp.zeros_like(last))

    def copy(b, slot, row):
        return pltpu.make_async_copy(xw_ref.at[b, pl.ds(pl.multiple_of(slot * MOE_WIN, MOE_WIN), MOE_WIN)],
                                     xs_ref.at[pl.ds(row, MOE_WIN)], sem.at[b])

    @pl.when(i > 0)
    def _():
        _for_recorded_windows(lambda slot, row: copy(1 - buf, slot, row).wait(), rows_ref, cnt_ref, 1 - buf)

    _for_each_window(n_e, pstart_ref, tb_ref, tl_ref, lambda slot, row: copy(buf, slot, row).start(),
                     rows_ref, cnt_ref, buf, splice_carry)

    @pl.when(i == n_i - 1)
    def _():
        _for_recorded_windows(lambda slot, row: copy(buf, slot, row).wait(), rows_ref, cnt_ref, buf)
        cout_ref[...] = carry_ref[...]


def _scatter(pstart, z1, z2, tbase, tlen, meta, h2, prior, n_rows_sorted, bm, tm):
    rows, d = h2.shape
    n_e = pstart.shape[0]
    n_slots = _moe_slots(tm)
    smem = pl.BlockSpec(memory_space=pltpu.SMEM)
    tile3 = lambda i: (i, 0, 0)
    tile_smem = lambda im: pl.BlockSpec((1, 1, LANES), im, memory_space=pltpu.SMEM)
    carry_spec = pl.BlockSpec((n_e, SORT_ALIGN, d), lambda i: (0, 0, 0))
    in_specs = [smem, smem, smem,
                tile_smem(tile3), tile_smem(tile3),
                pl.BlockSpec((1, 1, LANES), tile3), pl.BlockSpec((1, 1, LANES), tile3),
                pl.BlockSpec((tm, 2 * TOP_K), lambda i: (i, 0)),
                pl.BlockSpec((tm, d), lambda i: (i, 0))]
    args = [pstart, z1, z2, tbase, tlen, tbase, tlen, meta, h2]
    aliases = {}
    if prior is not None:
        in_specs += [carry_spec, pl.BlockSpec(memory_space=pl.ANY)]
        aliases = {len(args) + 1: 0}
        args += list(prior)
    return pl.pallas_call(
        functools.partial(_scatter_kernel, n_e, prior is not None),
        grid=(rows // tm,),
        in_specs=in_specs,
        out_specs=[pl.BlockSpec(memory_space=pl.ANY), carry_spec],
        out_shape=[jax.ShapeDtypeStruct((n_rows_sorted, d), BF16),
                   jax.ShapeDtypeStruct((n_e, SORT_ALIGN, d), BF16)],
        input_output_aliases=aliases,
        scratch_shapes=[pltpu.VMEM((2, n_slots * MOE_WIN, d), BF16), pltpu.VMEM((bm, d), BF16),
                        pltpu.VMEM((n_e, SORT_ALIGN, d), BF16),
                        pltpu.SMEM((2, n_slots), jnp.int32), pltpu.SMEM((2,), jnp.int32),
                        pltpu.SemaphoreType.DMA((2,)), pltpu.SemaphoreType.DMA(())],
        compiler_params=_cparams(("arbitrary",)),
        name="moe_scatter",
    )(*args)


def _expert_kernel(d_ff, be_ref, nused_ref, xs_ref, w1_ref, b1_ref, w2_ref, b2_ref, ys_ref, w1b_ref, w2b_ref):
    i = pl.program_id(0)
    new_expert = jnp.logical_or(i == 0, be_ref[i] != be_ref[jnp.maximum(i - 1, 0)])

    @pl.when(jnp.logical_and(i < nused_ref[0], new_expert))
    def _():
        w1b_ref[...] = w1_ref[0].astype(BF16)
        w2b_ref[...] = w2_ref[0].astype(BF16)

    @pl.when(i < nused_ref[0])
    def _():
        h = _mm(xs_ref[...], w1b_ref[...]) + b1_ref[0]
        hg = jnp.minimum(h[:, :d_ff], SWIGLU_LIMIT)
        hu = jnp.clip(h[:, d_ff:], -SWIGLU_LIMIT, SWIGLU_LIMIT)
        act = hg * _sigmoid(SWIGLU_ALPHA * hg) * (hu + 1.0)
        ys_ref[...] = (_mm(act.astype(BF16), w2b_ref[...]) + b2_ref[0]).astype(ys_ref.dtype)

    @pl.when(i >= nused_ref[0])
    def _():
        ys_ref[...] = jnp.zeros_like(ys_ref)


def _experts(block_e, n_used, xs, w1, b1, w2, b2, bm):
    R, d = xs.shape
    d_ff = w2.shape[1]
    nb = R // bm

    def rows(i, be, nu):
        return (jnp.minimum(i, nu[0] - 1), 0)

    def wsel(i, be, nu):
        return (be[i], 0, 0)

    return pl.pallas_call(
        functools.partial(_expert_kernel, d_ff),
        grid_spec=pltpu.PrefetchScalarGridSpec(
            num_scalar_prefetch=2,
            grid=(nb,),
            in_specs=[pl.BlockSpec((bm, d), rows),
                      pl.BlockSpec((1, d, 2 * d_ff), wsel), pl.BlockSpec((1, 1, 2 * d_ff), wsel),
                      pl.BlockSpec((1, d_ff, d), wsel), pl.BlockSpec((1, 1, d), wsel)],
            out_specs=pl.BlockSpec((bm, d), lambda i, be, nu: (i, 0)),
            scratch_shapes=[pltpu.VMEM((d, 2 * d_ff), BF16), pltpu.VMEM((d_ff, d), BF16)]),
        out_shape=jax.ShapeDtypeStruct((R, d), xs.dtype),
        compiler_params=_cparams(("arbitrary",)),
        name="moe_experts",
    )(block_e, n_used, xs, w1, b1, w2, b2)


def _combine_kernel(n_e, pstart_ref, tb_ref, tl_ref, tbn_ref, tln_ref, tbv_ref, tlv_ref, meta_ref, gate_ref,
                    x1_ref, gf_ref, ys_ref, o_ref, win_ref, rows_ref, cnt_ref, sem):
    i = pl.program_id(0)
    n = pl.num_programs(0)
    tm = x1_ref.shape[0]
    buf = i % 2
    n_rows_w = win_ref.shape[1]

    def copy(b, slot, row):
        return pltpu.make_async_copy(
            ys_ref.at[pl.ds(row, MOE_WIN)],
            win_ref.at[b, pl.ds(pl.multiple_of(slot * MOE_WIN, MOE_WIN), MOE_WIN)], sem.at[b])

    @pl.when(i == 0)
    def _():
        win_ref[...] = jnp.zeros_like(win_ref)
        _for_each_window(n_e, pstart_ref, tb_ref, tl_ref, lambda slot, row: copy(buf, slot, row).start(),
                         rows_ref, cnt_ref, buf)

    @pl.when(i + 1 < n)
    def _():
        _for_each_window(n_e, pstart_ref, tbn_ref, tln_ref, lambda slot, row: copy(1 - buf, slot, row).start(),
                         rows_ref, cnt_ref, 1 - buf)

    tgts = _window_targets(meta_ref[...], tbv_ref[0], tlv_ref[0])
    gate = gate_ref[...]
    lane_s = _iota((tm, n_rows_w), 1)
    sel = jnp.zeros((tm, n_rows_w), F32)
    for k, tgt in enumerate(tgts):
        sel = sel + jnp.where(lane_s == tgt, gate[:, k:k + 1], 0.0)
    sel_hi, sel_lo = _split2(sel)
    _for_recorded_windows(lambda slot, row: copy(buf, slot, row).wait(), rows_ref, cnt_ref, buf)
    wb = win_ref[buf]
    y = x1_ref[...] + (_mm(sel_hi, wb) + _mm(sel_lo, wb))
    o_ref[...] = y * lax.rsqrt(jnp.mean(y * y, axis=-1, keepdims=True) + NORM_EPS) * gf_ref[...]


def _combine(pstart, tbase, tlen, meta, gate, x1, gf, ys, tm):
    rows, d = x1.shape
    n = rows // tm
    n_e = pstart.shape[0]
    cur3 = lambda i: (i, 0, 0)
    nxt3 = lambda i: (jnp.minimum(i + 1, n - 1), 0, 0)
    tile_smem = lambda im: pl.BlockSpec((1, 1, LANES), im, memory_space=pltpu.SMEM)
    return pl.pallas_call(
        functools.partial(_combine_kernel, n_e),
        grid=(n,),
        in_specs=[pl.BlockSpec(memory_space=pltpu.SMEM),
                  tile_smem(cur3), tile_smem(cur3), tile_smem(nxt3), tile_smem(nxt3),
                  pl.BlockSpec((1, 1, LANES), cur3), pl.BlockSpec((1, 1, LANES), cur3),
                  pl.BlockSpec((tm, 2 * TOP_K), lambda i: (i, 0)),
                  pl.BlockSpec((tm, TOP_K), lambda i: (i, 0)),
                  pl.BlockSpec((tm, d), lambda i: (i, 0)),
                  pl.BlockSpec((1, d), lambda i: (0, 0)),
                  pl.BlockSpec(memory_space=pl.ANY)],
        out_specs=pl.BlockSpec((tm, d), lambda i: (i, 0)),
        out_shape=jax.ShapeDtypeStruct((rows, d), F32),
        scratch_shapes=[pltpu.VMEM((2, _moe_slots(tm) * MOE_WIN, d), ys.dtype),
                        pltpu.SMEM((2, _moe_slots(tm)), jnp.int32), pltpu.SMEM((2,), jnp.int32),
                        pltpu.SemaphoreType.DMA((2,))],
        compiler_params=_cparams(("arbitrary",)),
        name="moe_combine",
    )(pstart, tbase, tlen, tbase, tlen, tbase, tlen, meta, gate, x1, gf, ys)


def _rope_tables(pos):
    half = ROT_DIM // 2
    inv = ROPE_THETA ** (-jnp.arange(0, ROT_DIM, 2, dtype=F32) / ROT_DIM)
    ang = inv[:, None] * pos.astype(F32)[None, :]
    cos, sin = jnp.cos(ang), jnp.sin(ang)
    n = pos.shape[0]
    pad1 = jnp.ones((HEAD_DIM - ROT_DIM, n), F32)
    pad0 = jnp.zeros((HEAD_DIM - ROT_DIM, n), F32)
    cos_h = jnp.concatenate([cos, cos, pad1], axis=0)
    sin_h = jnp.concatenate([-sin, sin, pad0], axis=0)
    reps = (LANES // HEAD_DIM, 1)
    return jnp.tile(cos_h, reps).T, jnp.tile(sin_h, reps).T


def _pairs_from_state(S):
    H = S.shape[0]
    St = jnp.swapaxes(S, 1, 2).reshape(H // 2, 2, HEAD_DIM, HEAD_DIM)
    z = jnp.zeros_like(St[:, 0])
    top = jnp.concatenate([St[:, 0], z], axis=2)
    bot = jnp.concatenate([z, St[:, 1]], axis=2)
    return jnp.concatenate([top, bot], axis=1)


def _state_from_pairs(Sp):
    a = Sp[:, :HEAD_DIM, :HEAD_DIM]
    b = Sp[:, HEAD_DIM:, HEAD_DIM:]
    St = jnp.stack([a, b], axis=1).reshape(-1, HEAD_DIM, HEAD_DIM)
    return jnp.swapaxes(St, 1, 2)


def kernel(x_prompt, x_sample, state_rwkv_wkv, state_rwkv_shift, cache_swa_k, cache_swa_v, norm1_g, w_in, mu_shift, decay_w0, decay_w2, aaa_a0, aaa_w2, gate_w2, k_k, k_a, r_k, lnx_g, lnx_b, attn_sinks, w_out, norm2_g, w_router, b_router, w_mlp1, b_mlp1, w_mlp2, b_mlp2, norm_f_g):
    depth = w_in.shape[0]
    assert depth == 1 and x_prompt.shape[0] == 1 and x_sample.shape[1] == 1
    T, d = x_prompt.shape[1], x_prompt.shape[2]
    B = x_sample.shape[0]
    past_len = PAST_LEN
    H = state_rwkv_wkv.shape[2]
    rw_w = H * HEAD_DIM
    n_pairs = H // 2
    n_q = attn_sinks.shape[1]
    n_kv = cache_swa_k.shape[3]
    q_cols = n_q * HEAD_DIM
    kv_cols = n_kv * HEAD_DIM
    rw_cols = state_rwkv_shift.shape[2]
    assert rw_cols == 3 * rw_w + 2 * HEAD_DIM + PAIR and kv_cols == LANES
    assert T % RW_TILE == 0 and B % ROW_TILE == 0 and B % 8 == 0
    wlen = cache_swa_k.shape[2]
    l = 0

    w_in_bf = w_in[l].astype(BF16)
    zero_half = jnp.zeros((HEAD_DIM, rw_w), F32)
    pp = dict(mu=mu_shift[l][None], w0=decay_w0[l][None],
              dw2=jnp.concatenate([decay_w2[l], zero_half], axis=0),
              a0=aaa_a0[l][None], aw2=jnp.concatenate([zero_half, aaa_w2[l]], axis=0),
              gw2=gate_w2[l], kk=k_k[l][None], ka=k_a[l][None], rk=r_k[l].reshape(1, rw_w),
              lng=lnx_g[l][None], lnb=lnx_b[l][None])
    w_out_bf = w_out[l].astype(BF16)
    n_e = w_router.shape[2]
    wr = jnp.pad(w_router[l], ((0, 0), (0, LANES - n_e)))
    br = jnp.concatenate([b_router[l], jnp.full((LANES - n_e,), NEG_BIG, F32)])[None]
    wp = dict(wa=w_out_bf[:rw_w], wb=w_out_bf[rw_w:], g2=norm2_g[l][None], wr=wr, br=br)
    g1 = norm1_g[l][None]

    xp = x_prompt[0]
    cos_p, sin_p = _rope_tables(jnp.arange(T))
    prw_p, q_p, k_p, v_p = _inproj(xp, g1, w_in_bf, cos_p, sin_p, 512, rw_cols, q_cols, kv_cols)
    s0_p = jnp.zeros((n_pairs, PAIR, PAIR), F32)
    shift0_p = jnp.zeros((1, rw_cols), F32)
    assert RW_TILE == MOE_TILE
    sfin_p, x1_p, h2_p, gate_p, meta_p, tb_p, tl_p, cnt = _layer_prompt(
        xp, prw_p, shift0_p, s0_p, pp, q_p, k_p, v_p, attn_sinks[l], wp, RW_TILE)

    xs_ = x_sample[:, 0]
    cos_s, sin_s = _rope_tables(jnp.full((B,), past_len))
    prw_s, q_s, k_s, v_s = _inproj(xs_, g1, w_in_bf, cos_s, sin_s, ROW_TILE, rw_cols, q_cols, kv_cols)
    s_flat = state_rwkv_wkv[l].reshape(B, H * HEAD_DIM * HEAD_DIM)
    ya_s, snew_flat = _rwkv_step(prw_s, state_rwkv_shift[l], s_flat, pp, n_pairs)
    o2, kc_new, vc_new = _attn_step(q_s.reshape(B * n_q, HEAD_DIM), k_s, v_s,
                                    cache_swa_k[l].reshape(B, wlen, kv_cols),
                                    cache_swa_v[l].reshape(B, wlen, kv_cols),
                                    attn_sinks[l][:, None], n_q, n_kv, past_len)
    yb_s = o2.reshape(B, q_cols)

    rows = T + B
    x1_s, h2_s, gate_s, meta_s, tb_s, tl_s, cnt = _post(xs_, ya_s, yb_s, wp, cnt, ROW_TILE)

    counts = cnt[0, :n_e].astype(jnp.int32)
    padded = (counts + MOE_WIN + MOE_BM - 1) // MOE_BM * MOE_BM
    pend = jnp.cumsum(padded)
    pstart = (pend - padded).astype(jnp.int32)
    n_blocks = -(-(rows * TOP_K) // MOE_BM) + n_e + -(-(n_e * MOE_WIN) // MOE_BM)
    block_start = jnp.arange(n_blocks, dtype=jnp.int32) * MOE_BM
    block_e = jnp.minimum(jnp.sum((pend[None, :] <= block_start[:, None]).astype(jnp.int32), axis=1),
                          n_e - 1).astype(jnp.int32)
    n_used = (pend[-1] // MOE_BM).astype(jnp.int32)[None]
    z1 = (pstart + counts // MOE_BM * MOE_BM).astype(jnp.int32)
    z2 = (pend - MOE_BM).astype(jnp.int32)

    n_sorted = n_blocks * MOE_BM
    xs_sorted, carry = _scatter(pstart, z1, z2, tb_p, tl_p, meta_p, h2_p, None, n_sorted, MOE_BM, MOE_TILE)
    xs_sorted, _ = _scatter(pstart, z1, z2, tb_s, tl_s, meta_s, h2_s, (carry, xs_sorted), n_sorted, MOE_BM,
                            ROW_TILE)
    ys_sorted = _experts(block_e, n_used, xs_sorted, w_mlp1[l], b_mlp1[l][:, None], w_mlp2[l],
                         b_mlp2[l][:, None], MOE_BM)
    gf = norm_f_g[None]
    y_p = _combine(pstart, tb_p, tl_p, meta_p, gate_p, x1_p, gf, ys_sorted, MOE_TILE)
    y_s = _combine(pstart, tb_s, tl_s, meta_s, gate_s, x1_s, gf, ys_sorted, ROW_TILE)

    sdt = state_rwkv_wkv.dtype
    return (y_p[None], y_s[:, None],
            _state_from_pairs(sfin_p)[None, None].astype(sdt), prw_p[T - 1][None, None],
            k_p[T - min(WINDOW, T):].reshape(1, 1, -1, n_kv, HEAD_DIM),
            v_p[T - min(WINDOW, T):].reshape(1, 1, -1, n_kv, HEAD_DIM),
            snew_flat.reshape(1, B, H, HEAD_DIM, HEAD_DIM).astype(sdt), prw_s[None],
            kc_new.reshape(1, B, wlen, n_kv, HEAD_DIM), vc_new.reshape(1, B, wlen, n_kv, HEAD_DIM))
```

```python
import functools

import jax
import jax.numpy as jnp
from jax import lax
from jax.experimental import pallas as pl
from jax.experimental.pallas import tpu as pltpu

F32 = jnp.float32
BF16 = jnp.bfloat16

LANES = 128
SUBLANES = 8
HEAD_DIM = 64
PAIR = 2 * HEAD_DIM
CHUNK = 64
RW_TILE = 256
RW_PAIRS_PER_STEP = 4
ROT_DIM = 16
ROPE_THETA = 500000.0
WINDOW = 128
PAST_LEN = 16384
ATT_BLOCK = 128
N_EXPERTS = 32
TOP_K = 4
SWIGLU_ALPHA = 1.702
SWIGLU_LIMIT = 7.0
NORM_EPS = 1e-5
LNX_EPS = HEAD_DIM * 1e-5
INPROJ_TILE = 512
STEP_STATE_LANES = 1024
STEP_ATT_BATCH = 8
MOE_BM = 512
ROW_TILE = 128
MOE_WIN_SHIFT = 4
MOE_WIN = 1 << MOE_WIN_SHIFT
SORT_ALIGN_SHIFT = 4
SORT_ALIGN = 1 << SORT_ALIGN_SHIFT
MOE_TILE = 256
MOE_SEL_CHUNK = 256
NEG_BIG = -1e30
VMEM_LIMIT = 52 * 1024 * 1024

NN = (((1,), (0,)), ((), ()))
NT = (((1,), (1,)), ((), ()))


def _mm(a, b, dn=NN):
    return lax.dot_general(a, b, dn, preferred_element_type=F32)


def _split2(a):
    hi = a.astype(BF16)
    lo = (a - hi.astype(F32)).astype(BF16)
    return hi, lo


def _split3(a):
    hi = a.astype(BF16)
    r1 = a - hi.astype(F32)
    mid = r1.astype(BF16)
    lo = (r1 - mid.astype(F32)).astype(BF16)
    return hi, mid, lo


def _dot1(a, b, dn=NN):
    return _mm(a.astype(BF16), b.astype(BF16), dn)


def _dot_sel_l(sel, b, dn=NN):
    b0, b1, b2 = _split3(b)
    return _mm(sel, b0, dn) + (_mm(sel, b1, dn) + _mm(sel, b2, dn))


def _dot_sel_r(a, sel, dn=NN):
    a0, a1, a2 = _split3(a)
    return _mm(a0, sel, dn) + (_mm(a1, sel, dn) + _mm(a2, sel, dn))


def _iota(shape, dim):
    return lax.broadcasted_iota(jnp.int32, shape, dim)


def _seg_matrix():
    return ((_iota((PAIR, PAIR), 0) // HEAD_DIM) == (_iota((PAIR, PAIR), 1) // HEAD_DIM)).astype(BF16)


def _sigmoid(x):
    return 1.0 / (1.0 + jnp.exp(-x))


def _cparams(sem, vmem=VMEM_LIMIT):
    return pltpu.CompilerParams(dimension_semantics=sem, vmem_limit_bytes=vmem)


def _rope_slab(x, cos, sin_signed):
    lane = _iota(x.shape, 1) % HEAD_DIM
    up = pltpu.roll(x, LANES - ROT_DIM // 2, axis=1)
    down = pltpu.roll(x, ROT_DIM // 2, axis=1)
    partner = jnp.where(lane < ROT_DIM // 2, up, down)
    return x * cos + partner * sin_signed


def _inproj_kernel(rw_cols, q_cols, kv_cols, x_ref, g_ref, w_ref, cos_ref, sin_ref,
                   prw_ref, q_ref, k_ref, v_ref):
    x = x_ref[...]
    h = x * lax.rsqrt(jnp.mean(x * x, axis=-1, keepdims=True) + NORM_EPS) * g_ref[...]
    proj = _mm(h.astype(BF16), w_ref[...])
    prw_ref[...] = proj[:, :rw_cols]
    cos = cos_ref[...]
    sin = sin_ref[...]
    for c in range(q_cols // LANES):
        lo = rw_cols + c * LANES
        q_ref[:, c * LANES:(c + 1) * LANES] = _rope_slab(proj[:, lo:lo + LANES], cos, sin)
    ko = rw_cols + q_cols
    for c in range(kv_cols // LANES):
        k_ref[:, c * LANES:(c + 1) * LANES] = _rope_slab(proj[:, ko + c * LANES:ko + (c + 1) * LANES], cos, sin)
    v_ref[...] = proj[:, ko + kv_cols:ko + 2 * kv_cols]


def _inproj(x, g, w_bf, cos_t, sin_t, tm, rw_cols, q_cols, kv_cols):
    rows, d = x.shape
    cols = w_bf.shape[1]
    full = lambda i: (0, 0)
    row = lambda i: (i, 0)
    return pl.pallas_call(
        functools.partial(_inproj_kernel, rw_cols, q_cols, kv_cols),
        grid=(rows // tm,),
        in_specs=[pl.BlockSpec((tm, d), row), pl.BlockSpec((1, d), full),
                  pl.BlockSpec((d, cols), full),
                  pl.BlockSpec((tm, LANES), row), pl.BlockSpec((tm, LANES), row)],
        out_specs=[pl.BlockSpec((tm, rw_cols), row), pl.BlockSpec((tm, q_cols), row),
                   pl.BlockSpec((tm, kv_cols), row), pl.BlockSpec((tm, kv_cols), row)],
        out_shape=[jax.ShapeDtypeStruct((rows, rw_cols), F32), jax.ShapeDtypeStruct((rows, q_cols), F32),
                   jax.ShapeDtypeStruct((rows, kv_cols), F32), jax.ShapeDtypeStruct((rows, kv_cols), F32)],
        compiler_params=_cparams(("parallel",)),
        name="inproj",
    )(x, g, w_bf, cos_t, sin_t)


def _rwkv_tokenwise(pr, pk, pv, plo, pg, prev_r, prev_k, prev_v, prev_lo, prev_g,
                    mu_r, mu_k, mu_v, mu_lo, mu_g, w0, dw2, a0, aw2, gw2, kkp, kap, rkp, seg):
    r = pr + (prev_r - pr) * mu_r
    k = pk + (prev_k - pk) * mu_k
    v = pv + (prev_v - pv) * mu_v
    lo = plo + (prev_lo - plo) * mu_lo
    gd = pg + (prev_g - pg) * mu_g
    z = -(w0 + _dot1(jnp.tanh(lo), dw2))
    softplus = jnp.maximum(z, 0.0) + jnp.log(1.0 + jnp.exp(-jnp.abs(z)))
    logw = -jnp.exp(-softplus - 0.5)
    a = _sigmoid(a0 + _dot1(lo, aw2))
    g = _dot1(_sigmoid(gd), gw2)
    kk = k * kkp
    nrm = jnp.sqrt(_seg_sum(kk * kk, seg))
    kk = kk / jnp.maximum(nrm, 1e-12)
    k2 = k * (1.0 + (a - 1.0) * kap)
    bonus = _seg_sum(r * k2 * rkp, seg) * v
    return r, k2, v, logw, -kk, kk * a, g, bonus


def _seg_sum(x, seg):
    xh, xl = _split2(x)
    return _mm(xh, seg) + _mm(xl, seg)


def _rwkv_finish(y, bonus, g, lng, lnb, seg):
    mu = _seg_sum(y, seg) * (1.0 / HEAD_DIM)
    d = y - mu
    var = _seg_sum(d * d, seg) * (1.0 / HEAD_DIM)
    yn = d * lax.rsqrt(var + LNX_EPS) * lng + lnb
    return (yn + bonus) * g


def _layer_prompt_kernel(pps, pr_ref, pk_ref, pv_ref, plo_ref, pg_ref,
                        hr_ref, hk_ref, hv_ref, hlo_ref, hg_ref,
                        s0r_ref, s0k_ref, s0v_ref, s0lo_ref, s0g_ref,
                        mur_ref, muk_ref, muv_ref, mulo_ref, mug_ref,
                        w0_ref, dw2_ref, a0_ref, aw2_ref, gw2_ref, kk_ref, ka_ref, rk_ref,
                        lng_ref, lnb_ref, sin_ref,
                        sink_ref, q_ref, kc_ref, kp_ref, vc_ref, vp_ref,
                        x_ref, wa_ref, wb_ref, g2_ref, wr_ref, br_ref,
                        sout_ref, x1_ref, h2_ref, gate_ref, meta_ref, tb_ref, tl_ref, cnt_ref,
                        st_ref, ya_sc, yb_sc, carry_ref):
    i = pl.program_id(1)
    n_i = pl.num_programs(1)
    tt = pr_ref.shape[0]
    slot = i % 2
    y_ref = ya_sc.at[slot]
    yb_ref = yb_sc.at[slot]

    @pl.when(i == 0)
    def _():
        st_ref[...] = sin_ref[...]
        ya_sc[...] = jnp.zeros_like(ya_sc)
        yb_sc[...] = jnp.zeros_like(yb_sc)
        carry_ref[...] = jnp.zeros_like(carry_ref)

    n_q = q_ref.shape[1] // HEAD_DIM
    attn = _attn_prompt_stages(i, n_q, n_q // (kc_ref.shape[1] // HEAD_DIM), sink_ref,
                               q_ref, kc_ref, kp_ref, vc_ref, vp_ref, yb_ref)
    post = _post_stages((i > 0).astype(F32), x_ref, ya_sc.at[1 - slot], yb_sc.at[1 - slot],
                        wa_ref, wb_ref, g2_ref, wr_ref, br_ref,
                        x1_ref, h2_ref, gate_ref, meta_ref, tb_ref, tl_ref, cnt_ref, carry_ref)

    row = _iota((tt, PAIR), 0)

    def prev_of(cur, halo_row, s0_row):
        first = jnp.where(i == 0, s0_row, halo_row)
        return jnp.where(row == 0, first, pltpu.roll(cur, 1, axis=0))

    plo = plo_ref[...]
    pg = pg_ref[...]
    last = slice(SUBLANES - 1, SUBLANES)
    prev_lo = prev_of(plo, hlo_ref[last, :], s0lo_ref[...])
    prev_g = prev_of(pg, hg_ref[last, :], s0g_ref[...])
    ti = _iota((tt, tt), 0)
    tj = _iota((tt, tt), 1)
    same_chunk = (ti // CHUNK) == (tj // CHUNK)
    incl = same_chunk & (tj <= ti)
    strict = same_chunk & (tj < ti)
    seg = _seg_matrix()
    lane = _iota((tt, PAIR), 1)
    eye = (ti == tj).astype(F32)
    pairs = []
    for p in range(pps):
        ls = slice(p * PAIR, (p + 1) * PAIR)
        pr, pk, pv = pr_ref[:, ls], pk_ref[:, ls], pv_ref[:, ls]
        r, k2, v, logw, nkk, b, g, bonus = _rwkv_tokenwise(
            pr, pk, pv, plo, pg,
            prev_of(pr, hr_ref[last, ls], s0r_ref[:, ls]), prev_of(pk, hk_ref[last, ls], s0k_ref[:, ls]),
            prev_of(pv, hv_ref[last, ls], s0v_ref[:, ls]), prev_lo, prev_g,
            mur_ref[:, ls], muk_ref[:, ls], muv_ref[:, ls], mulo_ref[...], mug_ref[...],
            w0_ref[:, ls], dw2_ref[:, ls], a0_ref[:, ls], aw2_ref[:, ls], gw2_ref[:, ls],
            kk_ref[:, ls], ka_ref[:, ls], rk_ref[:, ls], seg)
        pairs.append(dict(ls=ls, r=r, k2=k2, v=v, logw=logw, nkk=nkk, b=b, g=g, bonus=bonus))

    incl_b = incl.astype(BF16)
    for q in pairs:
        q["cs"] = _dot_sel_l(incl_b, q["logw"])
    for q in pairs:
        cs = q["cs"]
        gam = jnp.exp(cs)
        inv = jnp.exp(-cs)
        q["a_t"] = jnp.exp(cs - q["logw"]) * q["nkk"]
        q["r_t"] = gam * q["r"]
        q["bt_T"] = (q["b"] * inv).T
        q["kt_T"] = (q["k2"] * inv).T
        q["gam_T"] = gam.T
        q["bk_T"] = jnp.concatenate([q["bt_T"], q["kt_T"]], axis=1).astype(BF16)
    attn[0]()
    attn[1]()
    post[0]()

    heads = []
    for q in pairs:
        for hh in range(2):
            hm = (lane // HEAD_DIM) == hh
            heads.append(dict(q=q, a=jnp.where(hm, q["a_t"], 0.0), r=jnp.where(hm, q["r_t"], 0.0),
                              v=jnp.where(hm, q["v"], 0.0)))
    for h in heads:
        h["g"] = _mm(jnp.concatenate([h["a"], h["r"]], axis=0).astype(BF16), h["q"]["bk_T"])
    for h in heads:
        gmat = h["g"]
        l_ab = jnp.where(strict, gmat[:tt, :tt], 0.0)
        h["l_ak_m_rk"] = jnp.concatenate([jnp.where(strict, gmat[:tt, tt:], 0.0),
                                          jnp.where(incl, gmat[tt:, tt:], 0.0)], axis=0).astype(BF16)
        h["m_rb"] = jnp.where(incl, gmat[tt:, :tt], 0.0).astype(BF16)
        h["tm"] = eye + l_ab
        h["lp"] = l_ab.astype(BF16)
    post[1]()
    for h in heads:
        h["lp"] = _mm(h["lp"], h["lp"]).astype(BF16)
    for it in range(4):
        for h in heads:
            h["both"] = _mm(jnp.concatenate([h["tm"].astype(BF16), h["lp"]], axis=0), h["lp"])
        attn[2 + it]()
        post[2 + it]()
        for h in heads:
            h["tm"] = h["tm"] + h["both"][:tt]
            h["lp"] = h["both"][tt:].astype(BF16)
    for h in heads:
        h["pq"] = _mm(h["l_ak_m_rk"], h["v"].astype(BF16))
        h["tm"] = h["tm"] + _mm(h["tm"].astype(BF16), h["lp"])
    attn[6]()
    post[6]()
    for h in heads:
        h["tx"] = _mm(h["tm"].astype(BF16),
                      jnp.concatenate([h["a"], h["pq"][:tt]], axis=1).astype(BF16))
    for h in heads:
        h["rx"] = _mm(h["m_rb"], h["tx"].astype(BF16))
    for n, q in enumerate(pairs):
        h0, h1 = heads[2 * n], heads[2 * n + 1]
        q["tatp"] = (h0["tx"] + h1["tx"]).astype(BF16)
        ryc = (h0["rx"] + h1["rx"]) + jnp.concatenate([h0["r"] + h1["r"], h0["pq"][tt:] + h1["pq"][tt:]], axis=1)
        q["ry"] = ryc[:, :PAIR]
        q["yc"] = ryc[:, PAIR:]
        q["v_b"] = q["v"].astype(BF16)
        q["bt_b"] = q["bt_T"].astype(BF16)
        q["kt_b"] = q["kt_T"].astype(BF16)
        q["s"] = st_ref[n]

    bd = seg.astype(F32)
    eye_p = (_iota((PAIR, PAIR), 0) == _iota((PAIR, PAIR), 1)).astype(F32)
    col_t = _iota((PAIR, tt), 1)
    zb = jnp.zeros((PAIR, tt), BF16)
    n_chunks = tt // CHUNK
    for c in range(n_chunks):
        cm = (col_t // CHUNK) == c
        for q in pairs:
            bt_c = jnp.where(cm, q["bt_b"], zb)
            kt_c = jnp.where(cm, q["kt_b"], zb)
            dcol = q["gam_T"][:, (c + 1) * CHUNK - 1:(c + 1) * CHUNK]
            bx = _mm(bt_c, q["tatp"])
            q["mc", c] = (dcol * (eye_p + bd * bx[:, :PAIR])).astype(BF16)
            q["nc", c] = dcol * (bd * (bx[:, PAIR:] + _mm(kt_c, q["v_b"])))
    for c in range(n_chunks):
        sl = slice(c * CHUNK, (c + 1) * CHUNK)
        for q in pairs:
            s_b = q["s"].astype(BF16)
            q["y", c] = _mm(q["ry"][sl].astype(BF16), s_b) + q["yc"][sl]
            q["s"] = _mm(q["mc", c], s_b) + q["nc", c]
    for q in pairs:
        y = jnp.concatenate([q["y", c] for c in range(n_chunks)], axis=0)
        y_ref[:, q["ls"]] = _rwkv_finish(y, q["bonus"], q["g"], lng_ref[:, q["ls"]], lnb_ref[:, q["ls"]], seg)
    for n, q in enumerate(pairs):
        st_ref[n] = q["s"]

    @pl.when(i == n_i - 2)
    def _():
        sout_ref[...] = st_ref[...]


def _layer_prompt(x, prw, shift0, s0_pairs, pp, q, k, v, sinks, wp, tt):
    T, d = x.shape
    n_t = T // tt
    n_pairs = s0_pairs.shape[0]
    pps = RW_PAIRS_PER_STEP
    n_grp = n_pairs // pps
    assert n_grp == 1 and tt % ATT_BLOCK == 0
    qw, kvw = q.shape[1], k.shape[1]
    ab = tt // ATT_BLOCK
    gw = pps * PAIR
    wcols = n_pairs * PAIR
    lo_col = 3 * wcols
    g_col = lo_col + PAIR
    hb = tt // SUBLANES
    mix_tile = lambda i: jnp.minimum(i, n_t - 1)
    post_tile = lambda i: jnp.maximum(i - 1, 0)

    def cur(off):
        return pl.BlockSpec((tt, gw), lambda p, i: (mix_tile(i), off // gw + p))

    def cur_fixed(col):
        return pl.BlockSpec((tt, PAIR), lambda p, i: (mix_tile(i), col // PAIR))

    def halo(off):
        return pl.BlockSpec((SUBLANES, gw), lambda p, i: (jnp.maximum(mix_tile(i) * hb - 1, 0), off // gw + p))

    def halo_fixed(col):
        return pl.BlockSpec((SUBLANES, PAIR), lambda p, i: (jnp.maximum(mix_tile(i) * hb - 1, 0), col // PAIR))

    def vec(off):
        return pl.BlockSpec((1, gw), lambda p, i: (0, off // gw + p))

    def vec_fixed(col):
        return pl.BlockSpec((1, PAIR), lambda p, i: (0, col // PAIR))

    def wmat(rows):
        return pl.BlockSpec((rows, gw), lambda p, i: (0, p))

    in_specs = ([cur(0), cur(wcols), cur(2 * wcols), cur_fixed(lo_col), cur_fixed(g_col)]
                + [halo(0), halo(wcols), halo(2 * wcols), halo_fixed(lo_col), halo_fixed(g_col)]
                + [vec(0), vec(wcols), vec(2 * wcols), vec_fixed(lo_col), vec_fixed(g_col)]
                + [vec(0), vec(wcols), vec(2 * wcols), vec_fixed(lo_col), vec_fixed(g_col)]
                + [vec(0), wmat(PAIR), vec(0), wmat(PAIR), wmat(PAIR), vec(0), vec(0), vec(0), vec(0), vec(0)]
                + [pl.BlockSpec((pps, PAIR, PAIR), lambda p, i: (p, 0, 0))])
    tile = lambda p, i: (mix_tile(i), 0)
    before = lambda p, i: (jnp.maximum(mix_tile(i) * ab - 1, 0), 0)
    full = lambda p, i: (0, 0)
    ptile = lambda p, i: (post_tile(i), 0)
    ptile3 = lambda p, i: (post_tile(i), 0, 0)
    half = wp["wa"].shape[0]
    in_specs += [pl.BlockSpec(memory_space=pltpu.SMEM), pl.BlockSpec((tt, qw), tile),
                 pl.BlockSpec((tt, kvw), tile), pl.BlockSpec((ATT_BLOCK, kvw), before),
                 pl.BlockSpec((tt, kvw), tile), pl.BlockSpec((ATT_BLOCK, kvw), before)]
    in_specs += [pl.BlockSpec((tt, d), ptile), pl.BlockSpec((half, d), full), pl.BlockSpec((half, d), full),
                 pl.BlockSpec((1, d), full), pl.BlockSpec((d, LANES), full), pl.BlockSpec((1, LANES), full)]
    args = ([prw] * 5 + [prw] * 5 + [shift0] * 5 + [pp["mu"]] * 5
            + [pp["w0"], pp["dw2"], pp["a0"], pp["aw2"], pp["gw2"], pp["kk"], pp["ka"], pp["rk"],
               pp["lng"], pp["lnb"], s0_pairs]
            + [sinks, q, k, k, v, v]
            + [x, wp["wa"], wp["wb"], wp["g2"], wp["wr"], wp["br"]])
    return pl.pallas_call(
        functools.partial(_layer_prompt_kernel, pps),
        grid=(n_grp, n_t + 1),
        in_specs=in_specs,
        out_specs=[pl.BlockSpec((pps, PAIR, PAIR), lambda p, i: (p, 0, 0)),
                   pl.BlockSpec((tt, d), ptile), pl.BlockSpec((tt, d), ptile),
                   pl.BlockSpec((tt, TOP_K), ptile), pl.BlockSpec((tt, 2 * TOP_K), ptile),
                   pl.BlockSpec((1, 1, LANES), ptile3), pl.BlockSpec((1, 1, LANES), ptile3),
                   pl.BlockSpec((1, LANES), full)],
        out_shape=[jax.ShapeDtypeStruct((n_pairs, PAIR, PAIR), F32),
                   jax.ShapeDtypeStruct((T, d), F32), jax.ShapeDtypeStruct((T, d), F32),
                   jax.ShapeDtypeStruct((T, TOP_K), F32), jax.ShapeDtypeStruct((T, 2 * TOP_K), jnp.int32),
                   jax.ShapeDtypeStruct((n_t, 1, LANES), jnp.int32),
                   jax.ShapeDtypeStruct((n_t, 1, LANES), jnp.int32),
                   jax.ShapeDtypeStruct((1, LANES), F32)],
        scratch_shapes=[pltpu.VMEM((pps, PAIR, PAIR), F32),
                        pltpu.VMEM((2, tt, wcols), F32), pltpu.VMEM((2, tt, qw), F32),
                        pltpu.VMEM((1, LANES), F32)],
        compiler_params=_cparams(("arbitrary", "arbitrary")),
        name="layer_prompt",
    )(*args)


def _rwkv_step_kernel(slabs_per_step, pr_ref, pk_ref, pv_ref, plo_ref, pg_ref,
                      sr_ref, sk_ref, sv_ref, slo_ref, sg_ref,
                      mur_ref, muk_ref, muv_ref, mulo_ref, mug_ref,
                      w0_ref, dw2_ref, a0_ref, aw2_ref, gw2_ref, kk_ref, ka_ref, rk_ref,
                      lng_ref, lnb_ref, s_ref,
                      y_ref, snew_ref, yacc_ref):
    j = pl.program_id(1)
    n_j = pl.num_programs(1)
    seg = _seg_matrix()
    r, k2, v, logw, nkk, b, g, bonus = _rwkv_tokenwise(
        pr_ref[...], pk_ref[...], pv_ref[...], plo_ref[...], pg_ref[...],
        sr_ref[...], sk_ref[...], sv_ref[...], slo_ref[...], sg_ref[...],
        mur_ref[...], muk_ref[...], muv_ref[...], mulo_ref[...], mug_ref[...],
        w0_ref[...], dw2_ref[...], a0_ref[...], aw2_ref[...], gw2_ref[...],
        kk_ref[...], ka_ref[...], rk_ref[...], seg)
    w = jnp.exp(logw)

    @pl.when(j == 0)
    def _():
        yacc_ref[...] = jnp.zeros_like(yacc_ref)

    slabs_per_head = HEAD_DIM // 2
    ci = _iota((PAIR, PAIR), 0)
    li = _iota((PAIR, PAIR), 1)
    assert slabs_per_head % slabs_per_step == 0
    yacc = yacc_ref[...]
    hh = (j * slabs_per_step) // slabs_per_head
    dup = ((ci == hh * HEAD_DIM + li % HEAD_DIM)).astype(BF16)
    nkk_d, w_d, b_d, k_d, r_d = [_dot_sel_r(x, dup) for x in (nkk, w, b, k2, r)]
    slabs = range(slabs_per_step)
    i0 = [2 * ((j * slabs_per_step + t) % slabs_per_head) for t in slabs]
    s = [s_ref[:, t * PAIR:(t + 1) * PAIR] for t in slabs]
    sa = [_seg_sum(s[t] * nkk_d, seg) for t in slabs]
    v_bc = [_seg_sum(v, (ci == hh * HEAD_DIM + i0[t] + li // HEAD_DIM).astype(BF16)) for t in slabs]
    s_new = [s[t] * w_d + sa[t] * b_d + v_bc[t] * k_d for t in slabs]
    for t in slabs:
        snew_ref[:, t * PAIR:(t + 1) * PAIR] = s_new[t]
    yred = [_seg_sum(s_new[t] * r_d, seg) for t in slabs]
    ysel = [_seg_sum(yred[t], ((ci % HEAD_DIM == 0)
                               & (li == hh * HEAD_DIM + i0[t] + ci // HEAD_DIM)).astype(BF16)) for t in slabs]
    for t in slabs:
        yacc = yacc + ysel[t]
    yacc_ref[...] = yacc

    @pl.when(j == n_j - 1)
    def _():
        y_ref[...] = _rwkv_finish(yacc, bonus, g, lng_ref[...], lnb_ref[...], seg)


def _rwkv_step(prw, shift, s_flat, pp, n_pairs):
    B = prw.shape[0]
    lanes_per_pair = 2 * HEAD_DIM * HEAD_DIM
    blk = STEP_STATE_LANES
    slabs_per_step = blk // PAIR
    steps = lanes_per_pair // blk
    lo_blk = 3 * n_pairs
    g_blk = lo_blk + 1

    def cur(off):
        return pl.BlockSpec((B, PAIR), lambda p, j: (0, off + p))

    def cur_fixed(b_):
        return pl.BlockSpec((B, PAIR), lambda p, j: (0, b_))

    def vec(off):
        return pl.BlockSpec((1, PAIR), lambda p, j: (0, off + p))

    def vec_fixed(b_):
        return pl.BlockSpec((1, PAIR), lambda p, j: (0, b_))

    def wmat(rows):
        return pl.BlockSpec((rows, PAIR), lambda p, j: (0, p))

    sspec = pl.BlockSpec((B, blk), lambda p, j: (0, p * steps + j))
    in_specs = ([cur(0), cur(n_pairs), cur(2 * n_pairs), cur_fixed(lo_blk), cur_fixed(g_blk)] * 2
                + [vec(0), vec(n_pairs), vec(2 * n_pairs), vec_fixed(lo_blk), vec_fixed(g_blk)]
                + [vec(0), wmat(PAIR), vec(0), wmat(PAIR), wmat(PAIR), vec(0), vec(0), vec(0), vec(0), vec(0)]
                + [sspec])
    args = ([prw] * 5 + [shift] * 5 + [pp["mu"]] * 5
            + [pp["w0"], pp["dw2"], pp["a0"], pp["aw2"], pp["gw2"], pp["kk"], pp["ka"], pp["rk"],
               pp["lng"], pp["lnb"], s_flat])
    return pl.pallas_call(
        functools.partial(_rwkv_step_kernel, slabs_per_step),
        grid=(n_pairs, steps),
        in_specs=in_specs,
        out_specs=[pl.BlockSpec((B, PAIR), lambda p, j: (0, p)), sspec],
        out_shape=[jax.ShapeDtypeStruct((B, n_pairs * PAIR), F32),
                   jax.ShapeDtypeStruct(s_flat.shape, F32)],
        scratch_shapes=[pltpu.VMEM((B, PAIR), F32)],
        compiler_params=_cparams(("parallel", "arbitrary")),
        name="rwkv_step",
    )(*args)


def _attn_prompt_stages(tile_idx, n_q, group, sink_ref, q_ref, kc_ref, kp_ref, vc_ref, vp_ref, o_ref):
    blk = ATT_BLOCK
    n_blk = q_ref.shape[0] // blk
    n_kv = n_q // group
    inst = [(j, h) for j in range(n_blk) for h in range(n_q)]
    st = {}

    def prepare():
        q = q_ref[...] * (HEAD_DIM ** -0.5)
        kc = kc_ref[...]
        vc = vc_ref[...]
        kall = jnp.concatenate([kp_ref[...], kc], axis=0)
        vall = jnp.concatenate([vp_ref[...], vc], axis=0)
        rq = _iota((blk, 2 * blk), 0)
        ck = _iota((blk, 2 * blk), 1)
        dist = rq - ck + blk
        in_window = (dist >= 0) & (dist < WINDOW)
        kpos0 = tile_idx * (n_blk * blk) - blk + ck
        st["valid"] = [in_window & (kpos0 >= 0)] + [in_window] * (n_blk - 1)
        st["q"] = {(j, h): q[j * blk:(j + 1) * blk, h * HEAD_DIM:(h + 1) * HEAD_DIM].astype(BF16) for j, h in inst}
        st["kb"] = {(j, g): kall[j * blk:(j + 2) * blk, g * HEAD_DIM:(g + 1) * HEAD_DIM].astype(BF16)
                    for j in range(n_blk) for g in range(n_kv)}
        st["vb"] = {(j, g): vall[j * blk:(j + 2) * blk, g * HEAD_DIM:(g + 1) * HEAD_DIM].astype(BF16)
                    for j in range(n_blk) for g in range(n_kv)}

    def scores():
        st["s"] = {(j, h): jnp.where(st["valid"][j], _mm(st["q"][j, h], st["kb"][j, h // group], NT), NEG_BIG)
                   for j, h in inst}

    def row_max():
        st["m"] = {(j, h): jnp.maximum(jnp.max(st["s"][j, h], axis=-1, keepdims=True), sink_ref[h]) for j, h in inst}

    def probs():
        st["p"] = {(j, h): jnp.exp(st["s"][j, h] - st["m"][j, h]) for j, h in inst}

    def denominators():
        st["d"] = {(j, h): jnp.sum(st["p"][j, h], axis=-1, keepdims=True) + jnp.exp(sink_ref[h] - st["m"][j, h])
                   for j, h in inst}

    def weighted_values():
        st["o"] = {(j, h): _mm(st["p"][j, h].astype(BF16), st["vb"][j, h // group]) for j, h in inst}

    def store():
        for j, h in inst:
            o_ref[j * blk:(j + 1) * blk, h * HEAD_DIM:(h + 1) * HEAD_DIM] = st["o"][j, h] / st["d"][j, h]

    return [prepare, scores, row_max, probs, denominators, weighted_values, store]


def _attn_step_kernel(n_q, group, pos0, sink_ref, q_ref, kn_ref, vn_ref, kc_ref, vc_ref,
                      o_ref, ko_ref, vo_ref):
    bb, wlen, kvw = kc_ref.shape
    rows = bb * n_q
    lane = _iota((rows, kvw), 1)
    rowh = _iota((rows, kvw), 0) % n_q
    mine = (lane // HEAD_DIM) == (rowh // group)
    dupm = (_iota((HEAD_DIM, kvw), 0) == _iota((HEAD_DIM, kvw), 1) % HEAD_DIM).astype(BF16)
    fold = (_iota((kvw, HEAD_DIM), 0) % HEAD_DIM == _iota((kvw, HEAD_DIM), 1)).astype(BF16)
    kidx = _iota((n_q, wlen), 1)
    dist = wlen - kidx
    valid = (dist < WINDOW) & (pos0 - dist >= 0)
    rk = _iota((wlen, kvw), 0)
    sink = sink_ref[...]
    qm_all = jnp.where(mine, _dot_sel_r(q_ref[...] * (HEAD_DIM ** -0.5), dupm), 0.0)
    elems = range(bb)
    qm = [qm_all[t * n_q:(t + 1) * n_q] for t in elems]
    kc = [kc_ref[t] for t in elems]
    vc = [vc_ref[t] for t in elems]
    kn = [kn_ref[t:t + 1, :] for t in elems]
    vn = [vn_ref[t:t + 1, :] for t in elems]
    s = [jnp.where(valid, _dot1(qm[t], kc[t], NT), NEG_BIG) for t in elems]
    s_new = [jnp.sum(qm[t] * kn[t], axis=-1, keepdims=True) for t in elems]
    m = [jnp.maximum(jnp.maximum(jnp.max(s[t], axis=-1, keepdims=True), s_new[t]), sink) for t in elems]
    p = [jnp.exp(s[t] - m[t]) for t in elems]
    p_new = [jnp.exp(s_new[t] - m[t]) for t in elems]
    denom = [jnp.sum(p[t], axis=-1, keepdims=True) + p_new[t] + jnp.exp(sink - m[t]) for t in elems]
    res = [(_dot1(p[t], vc[t]) + p_new[t] * vn[t]) / denom[t] for t in elems]
    o_ref[...] = _dot_sel_r(jnp.where(mine, jnp.concatenate(res, axis=0), 0.0), fold)
    for t in elems:
        ko_ref[t] = jnp.where(rk == wlen - 1, kn[t], pltpu.roll(kc[t], wlen - 1, axis=0))
        vo_ref[t] = jnp.where(rk == wlen - 1, vn[t], pltpu.roll(vc[t], wlen - 1, axis=0))


def _attn_step(q2, k_new, v_new, k_cache, v_cache, sinks_col, n_q, n_kv, pos0):
    B, wlen, kvw = k_cache.shape
    bb = STEP_ATT_BATCH
    return pl.pallas_call(
        functools.partial(_attn_step_kernel, n_q, n_q // n_kv, pos0),
        grid=(B // bb,),
        in_specs=[pl.BlockSpec((n_q, 1), lambda i: (0, 0)),
                  pl.BlockSpec((bb * n_q, HEAD_DIM), lambda i: (i, 0)),
                  pl.BlockSpec((bb, kvw), lambda i: (i, 0)), pl.BlockSpec((bb, kvw), lambda i: (i, 0)),
                  pl.BlockSpec((bb, wlen, kvw), lambda i: (i, 0, 0)),
                  pl.BlockSpec((bb, wlen, kvw), lambda i: (i, 0, 0))],
        out_specs=[pl.BlockSpec((bb * n_q, HEAD_DIM), lambda i: (i, 0)),
                   pl.BlockSpec((bb, wlen, kvw), lambda i: (i, 0, 0)),
                   pl.BlockSpec((bb, wlen, kvw), lambda i: (i, 0, 0))],
        out_shape=[jax.ShapeDtypeStruct((B * n_q, HEAD_DIM), F32),
                   jax.ShapeDtypeStruct((B, wlen, kvw), F32),
                   jax.ShapeDtypeStruct((B, wlen, kvw), F32)],
        compiler_params=_cparams(("parallel",)),
        name="attn_step",
    )(sinks_col, q2, k_new, v_new, k_cache, v_cache)


def _post_stages(counted, x_ref, ya_ref, yb_ref, wa_ref, wb_ref, g2_ref, wr_ref, br_ref,
                 x1_ref, h2_ref, gate_ref, meta_ref, tb_ref, tl_ref, cnt_ref, carry_ref):
    st = {}

    def residual_and_norm():
        mix = _mm(ya_ref[...].astype(BF16), wa_ref[...]) + _mm(yb_ref[...].astype(BF16), wb_ref[...])
        x1 = x_ref[...] + mix
        h2 = x1 * lax.rsqrt(jnp.mean(x1 * x1, axis=-1, keepdims=True) + NORM_EPS) * g2_ref[...]
        x1_ref[...] = x1
        h2_ref[...] = h2
        st["h2"] = h2

    def logits():
        st["l"] = _dot1(st["h2"], wr_ref[...]) + br_ref[...]
        st["vals"], st["idxs"] = [], []

    def next_expert():
        l = st["l"]
        lane = _iota(l.shape, 1)
        m = jnp.max(l, axis=-1, keepdims=True)
        sel = jnp.min(jnp.where(l == m, lane, LANES), axis=-1, keepdims=True)
        st["vals"].append(m)
        st["idxs"].append(sel)
        st["l"] = jnp.where(lane == sel, -jnp.inf, l)

    def gates_and_ranks():
        vals, idxs = st["vals"], st["idxs"]
        tm = vals[0].shape[0]
        lane = _iota((tm, LANES), 1)
        es = [jnp.exp(v - vals[0]) for v in vals]
        tot = es[0] + es[1] + es[2] + es[3]
        onehot = jnp.zeros((tm, LANES), F32)
        for sel in idxs:
            onehot = onehot + (lane == sel).astype(F32)
        strict = (_iota((tm, tm), 1) < _iota((tm, tm), 0)).astype(BF16)
        before = _mm(strict, onehot.astype(BF16))
        for k in range(TOP_K):
            gate_ref[:, k:k + 1] = es[k] / tot
            meta_ref[:, k:k + 1] = idxs[k]
            meta_ref[:, TOP_K + k:TOP_K + k + 1] = jnp.sum(
                jnp.where(lane == idxs[k], before, 0.0), axis=-1, keepdims=True).astype(jnp.int32)
        carry = carry_ref[...]
        cnt_t = jnp.sum(onehot, axis=0, keepdims=True) * counted
        tb_ref[0] = carry.astype(jnp.int32)
        tl_ref[0] = cnt_t.astype(jnp.int32)
        carry_ref[...] = carry + cnt_t
        cnt_ref[...] = carry + cnt_t

    return [residual_and_norm, logits] + [next_expert] * TOP_K + [gates_and_ranks]


def _post_kernel(x_ref, ya_ref, yb_ref, wa_ref, wb_ref, g2_ref, wr_ref, br_ref, cnt0_ref,
                 x1_ref, h2_ref, gate_ref, meta_ref, tb_ref, tl_ref, cnt_ref, carry_ref):
    @pl.when(pl.program_id(0) == 0)
    def _():
        carry_ref[...] = cnt0_ref[...]

    for stage in _post_stages(1.0, x_ref, ya_ref, yb_ref, wa_ref, wb_ref, g2_ref, wr_ref, br_ref,
                              x1_ref, h2_ref, gate_ref, meta_ref, tb_ref, tl_ref, cnt_ref, carry_ref):
        stage()


def _post(x, ya, yb, wp, cnt0, tm):
    rows, d = x.shape
    half = ya.shape[1]
    n_t = rows // tm
    full = lambda i: (0, 0)
    row = lambda i: (i, 0)
    trow = lambda i: (i, 0, 0)
    return pl.pallas_call(
        _post_kernel,
        grid=(n_t,),
        in_specs=[pl.BlockSpec((tm, d), row), pl.BlockSpec((tm, half), row), pl.BlockSpec((tm, half), row),
                  pl.BlockSpec((half, d), full), pl.BlockSpec((half, d), full),
                  pl.BlockSpec((1, d), full), pl.BlockSpec((d, LANES), full), pl.BlockSpec((1, LANES), full),
                  pl.BlockSpec((1, LANES), full)],
        out_specs=[pl.BlockSpec((tm, d), row), pl.BlockSpec((tm, d), row),
                   pl.BlockSpec((tm, TOP_K), row), pl.BlockSpec((tm, 2 * TOP_K), row),
                   pl.BlockSpec((1, 1, LANES), trow), pl.BlockSpec((1, 1, LANES), trow),
                   pl.BlockSpec((1, LANES), full)],
        out_shape=[jax.ShapeDtypeStruct((rows, d), F32), jax.ShapeDtypeStruct((rows, d), F32),
                   jax.ShapeDtypeStruct((rows, TOP_K), F32),
                   jax.ShapeDtypeStruct((rows, 2 * TOP_K), jnp.int32),
                   jax.ShapeDtypeStruct((n_t, 1, LANES), jnp.int32),
                   jax.ShapeDtypeStruct((n_t, 1, LANES), jnp.int32),
                   jax.ShapeDtypeStruct((1, LANES), F32)],
        scratch_shapes=[pltpu.VMEM((1, LANES), F32)],
        compiler_params=_cparams(("arbitrary",)),
        name="post",
    )(x, ya, yb, wp["wa"], wp["wb"], wp["g2"], wp["wr"], wp["br"], cnt0)


def _n_windows(base, length):
    off = base & (SORT_ALIGN - 1)
    n = lax.shift_right_logical(off + length + (MOE_WIN - 1), MOE_WIN_SHIFT)
    return off, jnp.where(length > 0, n, 0)


def _window_targets(meta, tb_vec, tl_vec):
    tm = meta.shape[0]
    off, n_win = _n_windows(tb_vec, tl_vec)
    upper = (_iota((LANES, LANES), 0) < _iota((LANES, LANES), 1)).astype(BF16)
    slot_start = _mm(n_win.astype(F32).astype(BF16), upper)
    pos0 = slot_start * MOE_WIN + off.astype(F32)
    lane = _iota((tm, LANES), 1)
    tgts = []
    for k in range(TOP_K):
        p0 = jnp.sum(jnp.where(lane == meta[:, k:k + 1], pos0, 0.0), axis=-1, keepdims=True)
        tgts.append(p0.astype(jnp.int32) + meta[:, TOP_K + k:TOP_K + k + 1])
    return tgts


def _for_each_window(n_e, pstart_ref, tb_ref, tl_ref, fn, rows_ref, cnt_ref, b, per_expert_fn=None):
    def per_expert(e, slot0):
        base = tb_ref[0, 0, e]
        length = tl_ref[0, 0, e]
        off, n = _n_windows(base, length)
        row0 = pstart_ref[e] + base - off
        if per_expert_fn is not None:
            per_expert_fn(e, slot0, off, length, n)

        def per_window(w, c):
            row = row0 + w * MOE_WIN
            rows_ref[b, slot0 + w] = row
            fn(slot0 + w, pl.multiple_of(row, SORT_ALIGN))
            return c

        lax.fori_loop(0, n, per_window, 0)
        return slot0 + n

    cnt_ref[b] = lax.fori_loop(0, n_e, per_expert, 0)


def _for_recorded_windows(fn, rows_ref, cnt_ref, b):
    def body(slot, c):
        fn(slot, pl.multiple_of(rows_ref[b, slot], SORT_ALIGN))
        return c

    lax.fori_loop(0, cnt_ref[b], body, 0)


def _moe_slots(tm):
    n = -(-(tm * TOP_K + N_EXPERTS * (SORT_ALIGN - 1 + MOE_WIN - 1)) // MOE_WIN)
    per_lane_tile = LANES // MOE_WIN
    return -(-n // per_lane_tile) * per_lane_tile


def _scatter_kernel(n_e, continues, pstart_ref, z1_ref, z2_ref, tb_ref, tl_ref, tbv_ref, tlv_ref,
                    meta_ref, h_ref, *rest):
    if continues:
        cin_ref, _xs_alias, xs_ref, cout_ref, xw_ref, zero_ref, carry_ref, rows_ref, cnt_ref, sem, zsem = rest
    else:
        xs_ref, cout_ref, xw_ref, zero_ref, carry_ref, rows_ref, cnt_ref, sem, zsem = rest
    i = pl.program_id(0)
    n_i = pl.num_programs(0)
    tm = h_ref.shape[0]
    bm = zero_ref.shape[0]
    buf = i % 2
    n_rows_w = xw_ref.shape[1]

    @pl.when(i == 0)
    def _():
        if continues:
            carry_ref[...] = cin_ref[...]
        else:
            zero_ref[...] = jnp.zeros_like(zero_ref)

            def zcopy(row):
                return pltpu.make_async_copy(zero_ref, xs_ref.at[pl.ds(pl.multiple_of(row, SORT_ALIGN), bm)], zsem)

            for e in range(n_e):
                zcopy(z1_ref[e]).start()

                @pl.when(z2_ref[e] != z1_ref[e])
                def _():
                    zcopy(z2_ref[e]).start()
            for e in range(n_e):
                zcopy(z1_ref[e]).wait()

                @pl.when(z2_ref[e] != z1_ref[e])
                def _():
                    zcopy(z2_ref[e]).wait()

            carry_ref[...] = jnp.zeros_like(carry_ref)

    tgts = _window_targets(meta_ref[...], tbv_ref[0], tlv_ref[0])
    hb = h_ref[...].astype(BF16)
    for c0 in range(0, n_rows_w, MOE_SEL_CHUNK):
        w = min(MOE_SEL_CHUNK, n_rows_w - c0)
        lane_c = _iota((tm, w), 1) + c0
        sel = jnp.zeros((tm, w), F32)
        for tgt in tgts:
            sel = sel + (lane_c == tgt).astype(F32)
        xw_ref[buf, c0:c0 + w, :] = _mm(sel.T.astype(BF16), hb).astype(BF16)

    def splice_carry(e, slot0, off, length, n):
        @pl.when(n > 0)
        def _():
            g0 = pl.multiple_of(slot0 * MOE_WIN, MOE_WIN)
            xw_ref[buf, pl.ds(g0, SORT_ALIGN), :] = xw_ref[buf, pl.ds(g0, SORT_ALIGN), :] + carry_ref[e]
            filled = off + length
            gl = pl.multiple_of(g0 + lax.shift_right_logical(filled, SORT_ALIGN_SHIFT) * SORT_ALIGN, SORT_ALIGN)
            last = xw_ref[buf, pl.ds(gl, SORT_ALIGN), :]
            carry_ref[e] = jnp.where((filled & (SORT_ALIGN - 1)) != 0, last, jnp.zeros_like(last))

    def copy(b, slot, row):
        return pltpu.make_async_copy(xw_ref.at[b, pl.ds(pl.multiple_of(slot * MOE_WIN, MOE_WIN), MOE_WIN)],
                                     xs_ref.at[pl.ds(row, MOE_WIN)], sem.at[b])

    @pl.when(i > 0)
    def _():
        _for_recorded_windows(lambda slot, row: copy(1 - buf, slot, row).wait(), rows_ref, cnt_ref, 1 - buf)

    _for_each_window(n_e, pstart_ref, tb_ref, tl_ref, lambda slot, row: copy(buf, slot, row).start(),
                     rows_ref, cnt_ref, buf, splice_carry)

    @pl.when(i == n_i - 1)
    def _():
        _for_recorded_windows(lambda slot, row: copy(buf, slot, row).wait(), rows_ref, cnt_ref, buf)
        cout_ref[...] = carry_ref[...]


def _scatter(pstart, z1, z2, tbase, tlen, meta, h2, prior, n_rows_sorted, bm, tm):
    rows, d = h2.shape
    n_e = pstart.shape[0]
    n_slots = _moe_slots(tm)
    smem = pl.BlockSpec(memory_space=pltpu.SMEM)
    tile3 = lambda i: (i, 0, 0)
    tile_smem = lambda im: pl.BlockSpec((1, 1, LANES), im, memory_space=pltpu.SMEM)
    carry_spec = pl.BlockSpec((n_e, SORT_ALIGN, d), lambda i: (0, 0, 0))
    in_specs = [smem, smem, smem,
                tile_smem(tile3), tile_smem(tile3),
                pl.BlockSpec((1, 1, LANES), tile3), pl.BlockSpec((1, 1, LANES), tile3),
                pl.BlockSpec((tm, 2 * TOP_K), lambda i: (i, 0)),
                pl.BlockSpec((tm, d), lambda i: (i, 0))]
    args = [pstart, z1, z2, tbase, tlen, tbase, tlen, meta, h2]
    aliases = {}
    if prior is not None:
        in_specs += [carry_spec, pl.BlockSpec(memory_space=pl.ANY)]
        aliases = {len(args) + 1: 0}
        args += list(prior)
    return pl.pallas_call(
        functools.partial(_scatter_kernel, n_e, prior is not None),
        grid=(rows // tm,),
        in_specs=in_specs,
        out_specs=[pl.BlockSpec(memory_space=pl.ANY), carry_spec],
        out_shape=[jax.ShapeDtypeStruct((n_rows_sorted, d), BF16),
                   jax.ShapeDtypeStruct((n_e, SORT_ALIGN, d), BF16)],
        input_output_aliases=aliases,
        scratch_shapes=[pltpu.VMEM((2, n_slots * MOE_WIN, d), BF16), pltpu.VMEM((bm, d), BF16),
                        pltpu.VMEM((n_e, SORT_ALIGN, d), BF16),
                        pltpu.SMEM((2, n_slots), jnp.int32), pltpu.SMEM((2,), jnp.int32),
                        pltpu.SemaphoreType.DMA((2,)), pltpu.SemaphoreType.DMA(())],
        compiler_params=_cparams(("arbitrary",)),
        name="moe_scatter",
    )(*args)


def _expert_kernel(d_ff, be_ref, nused_ref, xs_ref, w1_ref, b1_ref, w2_ref, b2_ref, ys_ref, w1b_ref, w2b_ref):
    i = pl.program_id(0)
    new_expert = jnp.logical_or(i == 0, be_ref[i] != be_ref[jnp.maximum(i - 1, 0)])

    @pl.when(jnp.logical_and(i < nused_ref[0], new_expert))
    def _():
        w1b_ref[...] = w1_ref[0].astype(BF16)
        w2b_ref[...] = w2_ref[0].astype(BF16)

    @pl.when(i < nused_ref[0])
    def _():
        h = _mm(xs_ref[...], w1b_ref[...]) + b1_ref[0]
        hg = jnp.minimum(h[:, :d_ff], SWIGLU_LIMIT)
        hu = jnp.clip(h[:, d_ff:], -SWIGLU_LIMIT, SWIGLU_LIMIT)
        act = hg * _sigmoid(SWIGLU_ALPHA * hg) * (hu + 1.0)
        ys_ref[...] = (_mm(act.astype(BF16), w2b_ref[...]) + b2_ref[0]).astype(ys_ref.dtype)

    @pl.when(i >= nused_ref[0])
    def _():
        ys_ref[...] = jnp.zeros_like(ys_ref)


def _experts(block_e, n_used, xs, w1, b1, w2, b2, bm):
    R, d = xs.shape
    d_ff = w2.shape[1]
    nb = R // bm

    def rows(i, be, nu):
        return (jnp.minimum(i, nu[0] - 1), 0)

    def wsel(i, be, nu):
        return (be[i], 0, 0)

    return pl.pallas_call(
        functools.partial(_expert_kernel, d_ff),
        grid_spec=pltpu.PrefetchScalarGridSpec(
            num_scalar_prefetch=2,
            grid=(nb,),
            in_specs=[pl.BlockSpec((bm, d), rows),
                      pl.BlockSpec((1, d, 2 * d_ff), wsel), pl.BlockSpec((1, 1, 2 * d_ff), wsel),
                      pl.BlockSpec((1, d_ff, d), wsel), pl.BlockSpec((1, 1, d), wsel)],
            out_specs=pl.BlockSpec((bm, d), lambda i, be, nu: (i, 0)),
            scratch_shapes=[pltpu.VMEM((d, 2 * d_ff), BF16), pltpu.VMEM((d_ff, d), BF16)]),
        out_shape=jax.ShapeDtypeStruct((R, d), xs.dtype),
        compiler_params=_cparams(("arbitrary",)),
        name="moe_experts",
    )(block_e, n_used, xs, w1, b1, w2, b2)


def _combine_kernel(n_e, pstart_ref, tb_ref, tl_ref, tbn_ref, tln_ref, tbv_ref, tlv_ref, meta_ref, gate_ref,
                    x1_ref, gf_ref, ys_ref, o_ref, win_ref, rows_ref, cnt_ref, sem):
    i = pl.program_id(0)
    n = pl.num_programs(0)
    tm = x1_ref.shape[0]
    buf = i % 2
    n_rows_w = win_ref.shape[1]

    def copy(b, slot, row):
        return pltpu.make_async_copy(
            ys_ref.at[pl.ds(row, MOE_WIN)],
            win_ref.at[b, pl.ds(pl.multiple_of(slot * MOE_WIN, MOE_WIN), MOE_WIN)], sem.at[b])

    @pl.when(i == 0)
    def _():
        win_ref[...] = jnp.zeros_like(win_ref)
        _for_each_window(n_e, pstart_ref, tb_ref, tl_ref, lambda slot, row: copy(buf, slot, row).start(),
                         rows_ref, cnt_ref, buf)

    @pl.when(i + 1 < n)
    def _():
        _for_each_window(n_e, pstart_ref, tbn_ref, tln_ref, lambda slot, row: copy(1 - buf, slot, row).start(),
                         rows_ref, cnt_ref, 1 - buf)

    tgts = _window_targets(meta_ref[...], tbv_ref[0], tlv_ref[0])
    gate = gate_ref[...]
    _for_recorded_windows(lambda slot, row: copy(buf, slot, row).wait(), rows_ref, cnt_ref, buf)
    y = x1_ref[...]
    for c0 in range(0, n_rows_w, MOE_SEL_CHUNK):
        w = min(MOE_SEL_CHUNK, n_rows_w - c0)
        lane_c = _iota((tm, w), 1) + c0
        sel = jnp.zeros((tm, w), F32)
        for k, tgt in enumerate(tgts):
            sel = sel + jnp.where(lane_c == tgt, gate[:, k:k + 1], 0.0)
        sel_hi, sel_lo = _split2(sel)
        wb = win_ref[buf, c0:c0 + w, :]
        y = y + (_mm(sel_hi, wb) + _mm(sel_lo, wb))
    o_ref[...] = y * lax.rsqrt(jnp.mean(y * y, axis=-1, keepdims=True) + NORM_EPS) * gf_ref[...]


def _combine(pstart, tbase, tlen, meta, gate, x1, gf, ys, tm):
    rows, d = x1.shape
    n = rows // tm
    n_e = pstart.shape[0]
    cur3 = lambda i: (i, 0, 0)
    nxt3 = lambda i: (jnp.minimum(i + 1, n - 1), 0, 0)
    tile_smem = lambda im: pl.BlockSpec((1, 1, LANES), im, memory_space=pltpu.SMEM)
    return pl.pallas_call(
        functools.partial(_combine_kernel, n_e),
        grid=(n,),
        in_specs=[pl.BlockSpec(memory_space=pltpu.SMEM),
                  tile_smem(cur3), tile_smem(cur3), tile_smem(nxt3), tile_smem(nxt3),
                  pl.BlockSpec((1, 1, LANES), cur3), pl.BlockSpec((1, 1, LANES), cur3),
                  pl.BlockSpec((tm, 2 * TOP_K), lambda i: (i, 0)),
                  pl.BlockSpec((tm, TOP_K), lambda i: (i, 0)),
                  pl.BlockSpec((tm, d), lambda i: (i, 0)),
                  pl.BlockSpec((1, d), lambda i: (0, 0)),
                  pl.BlockSpec(memory_space=pl.ANY)],
        out_specs=pl.BlockSpec((tm, d), lambda i: (i, 0)),
        out_shape=jax.ShapeDtypeStruct((rows, d), F32),
        scratch_shapes=[pltpu.VMEM((2, _moe_slots(tm) * MOE_WIN, d), ys.dtype),
                        pltpu.SMEM((2, _moe_slots(tm)), jnp.int32), pltpu.SMEM((2,), jnp.int32),
                        pltpu.SemaphoreType.DMA((2,))],
        compiler_params=_cparams(("arbitrary",)),
        name="moe_combine",
    )(pstart, tbase, tlen, tbase, tlen, tbase, tlen, meta, gate, x1, gf, ys)


def _rope_tables(pos):
    half = ROT_DIM // 2
    inv = ROPE_THETA ** (-jnp.arange(0, ROT_DIM, 2, dtype=F32) / ROT_DIM)
    ang = inv[:, None] * pos.astype(F32)[None, :]
    cos, sin = jnp.cos(ang), jnp.sin(ang)
    n = pos.shape[0]
    pad1 = jnp.ones((HEAD_DIM - ROT_DIM, n), F32)
    pad0 = jnp.zeros((HEAD_DIM - ROT_DIM, n), F32)
    cos_h = jnp.concatenate([cos, cos, pad1], axis=0)
    sin_h = jnp.concatenate([-sin, sin, pad0], axis=0)
    reps = (LANES // HEAD_DIM, 1)
    return jnp.tile(cos_h, reps).T, jnp.tile(sin_h, reps).T


def _pairs_from_state(S):
    H = S.shape[0]
    St = jnp.swapaxes(S, 1, 2).reshape(H // 2, 2, HEAD_DIM, HEAD_DIM)
    z = jnp.zeros_like(St[:, 0])
    top = jnp.concatenate([St[:, 0], z], axis=2)
    bot = jnp.concatenate([z, St[:, 1]], axis=2)
    return jnp.concatenate([top, bot], axis=1)


def _state_from_pairs(Sp):
    a = Sp[:, :HEAD_DIM, :HEAD_DIM]
    b = Sp[:, HEAD_DIM:, HEAD_DIM:]
    St = jnp.stack([a, b], axis=1).reshape(-1, HEAD_DIM, HEAD_DIM)
    return jnp.swapaxes(St, 1, 2)


def kernel(x_prompt, x_sample, state_rwkv_wkv, state_rwkv_shift, cache_swa_k, cache_swa_v, norm1_g, w_in, mu_shift, decay_w0, decay_w2, aaa_a0, aaa_w2, gate_w2, k_k, k_a, r_k, lnx_g, lnx_b, attn_sinks, w_out, norm2_g, w_router, b_router, w_mlp1, b_mlp1, w_mlp2, b_mlp2, norm_f_g):
    depth = w_in.shape[0]
    assert depth == 1 and x_prompt.shape[0] == 1 and x_sample.shape[1] == 1
    T, d = x_prompt.shape[1], x_prompt.shape[2]
    B = x_sample.shape[0]
    past_len = PAST_LEN
    H = state_rwkv_wkv.shape[2]
    rw_w = H * HEAD_DIM
    n_pairs = H // 2
    n_q = attn_sinks.shape[1]
    n_kv = cache_swa_k.shape[3]
    q_cols = n_q * HEAD_DIM
    kv_cols = n_kv * HEAD_DIM
    rw_cols = state_rwkv_shift.shape[2]
    assert rw_cols == 3 * rw_w + 2 * HEAD_DIM + PAIR and kv_cols == LANES
    assert T % RW_TILE == 0 and B % ROW_TILE == 0 and B % 8 == 0
    wlen = cache_swa_k.shape[2]
    l = 0

    w_in_bf = w_in[l].astype(BF16)
    zero_half = jnp.zeros((HEAD_DIM, rw_w), F32)
    pp = dict(mu=mu_shift[l][None], w0=decay_w0[l][None],
              dw2=jnp.concatenate([decay_w2[l], zero_half], axis=0),
              a0=aaa_a0[l][None], aw2=jnp.concatenate([zero_half, aaa_w2[l]], axis=0),
              gw2=gate_w2[l], kk=k_k[l][None], ka=k_a[l][None], rk=r_k[l].reshape(1, rw_w),
              lng=lnx_g[l][None], lnb=lnx_b[l][None])
    w_out_bf = w_out[l].astype(BF16)
    n_e = w_router.shape[2]
    wr = jnp.pad(w_router[l], ((0, 0), (0, LANES - n_e)))
    br = jnp.concatenate([b_router[l], jnp.full((LANES - n_e,), NEG_BIG, F32)])[None]
    wp = dict(wa=w_out_bf[:rw_w], wb=w_out_bf[rw_w:], g2=norm2_g[l][None], wr=wr, br=br)
    g1 = norm1_g[l][None]

    xp = x_prompt[0]
    cos_p, sin_p = _rope_tables(jnp.arange(T))
    prw_p, q_p, k_p, v_p = _inproj(xp, g1, w_in_bf, cos_p, sin_p, INPROJ_TILE, rw_cols, q_cols, kv_cols)
    s0_p = jnp.zeros((n_pairs, PAIR, PAIR), F32)
    shift0_p = jnp.zeros((1, rw_cols), F32)
    assert RW_TILE == MOE_TILE
    sfin_p, x1_p, h2_p, gate_p, meta_p, tb_p, tl_p, cnt = _layer_prompt(
        xp, prw_p, shift0_p, s0_p, pp, q_p, k_p, v_p, attn_sinks[l], wp, RW_TILE)

    xs_ = x_sample[:, 0]
    cos_s, sin_s = _rope_tables(jnp.full((B,), past_len))
    prw_s, q_s, k_s, v_s = _inproj(xs_, g1, w_in_bf, cos_s, sin_s, ROW_TILE, rw_cols, q_cols, kv_cols)
    s_flat = state_rwkv_wkv[l].reshape(B, H * HEAD_DIM * HEAD_DIM)
    ya_s, snew_flat = _rwkv_step(prw_s, state_rwkv_shift[l], s_flat, pp, n_pairs)
    o2, kc_new, vc_new = _attn_step(q_s.reshape(B * n_q, HEAD_DIM), k_s, v_s,
                                    cache_swa_k[l].reshape(B, wlen, kv_cols),
                                    cache_swa_v[l].reshape(B, wlen, kv_cols),
                                    attn_sinks[l][:, None], n_q, n_kv, past_len)
    yb_s = o2.reshape(B, q_cols)

    rows = T + B
    x1_s, h2_s, gate_s, meta_s, tb_s, tl_s, cnt = _post(xs_, ya_s, yb_s, wp, cnt, ROW_TILE)

    counts = cnt[0, :n_e].astype(jnp.int32)
    padded = (counts + MOE_WIN + MOE_BM - 1) // MOE_BM * MOE_BM
    pend = jnp.cumsum(padded)
    pstart = (pend - padded).astype(jnp.int32)
    n_blocks = -(-(rows * TOP_K) // MOE_BM) + n_e + -(-(n_e * MOE_WIN) // MOE_BM)
    block_start = jnp.arange(n_blocks, dtype=jnp.int32) * MOE_BM
    block_e = jnp.minimum(jnp.sum((pend[None, :] <= block_start[:, None]).astype(jnp.int32), axis=1),
                          n_e - 1).astype(jnp.int32)
    n_used = (pend[-1] // MOE_BM).astype(jnp.int32)[None]
    z1 = (pstart + counts // MOE_BM * MOE_BM).astype(jnp.int32)
    z2 = (pend - MOE_BM).astype(jnp.int32)

    n_sorted = n_blocks * MOE_BM
    xs_sorted, carry = _scatter(pstart, z1, z2, tb_p, tl_p, meta_p, h2_p, None, n_sorted, MOE_BM, MOE_TILE)
    xs_sorted, _ = _scatter(pstart, z1, z2, tb_s, tl_s, meta_s, h2_s, (carry, xs_sorted), n_sorted, MOE_BM,
                            ROW_TILE)
    ys_sorted = _experts(block_e, n_used, xs_sorted, w_mlp1[l], b_mlp1[l][:, None], w_mlp2[l],
                         b_mlp2[l][:, None], MOE_BM)
    gf = norm_f_g[None]
    y_p = _combine(pstart, tb_p, tl_p, meta_p, gate_p, x1_p, gf, ys_sorted, MOE_TILE)
    y_s = _combine(pstart, tb_s, tl_s, meta_s, gate_s, x1_s, gf, ys_sorted, ROW_TILE)

    sdt = state_rwkv_wkv.dtype
    return (y_p[None], y_s[:, None],
            _state_from_pairs(sfin_p)[None, None].astype(sdt), prw_p[T - 1][None, None],
            k_p[T - min(WINDOW, T):].reshape(1, 1, -1, n_kv, HEAD_DIM),
            v_p[T - min(WINDOW, T):].reshape(1, 1, -1, n_kv, HEAD_DIM),
            snew_flat.reshape(1, B, H, HEAD_DIM, HEAD_DIM).astype(sdt), prw_s[None],
            kc_new.reshape(1, B, wlen, n_kv, HEAD_DIM), vc_new.reshape(1, B, wlen, n_kv, HEAD_DIM))
```

```python
import functools

import jax
import jax.numpy as jnp
from jax import lax
from jax.experimental import pallas as pl
from jax.experimental.pallas import tpu as pltpu

F32 = jnp.float32
BF16 = jnp.bfloat16

LANES = 128
SUBLANES = 8
HEAD_DIM = 64
PAIR = 2 * HEAD_DIM
CHUNK = 64
RW_TILE = 256
RW_PAIRS_PER_STEP = 4
ROT_DIM = 16
ROPE_THETA = 500000.0
WINDOW = 128
PAST_LEN = 16384
ATT_BLOCK = 128
N_EXPERTS = 32
TOP_K = 4
SWIGLU_ALPHA = 1.702
SWIGLU_LIMIT = 7.0
NORM_EPS = 1e-5
LNX_EPS = HEAD_DIM * 1e-5
INPROJ_TILE = 512
STEP_STATE_LANES = 1024
STEP_ATT_BATCH = 8
MOE_BM = 512
ROW_TILE = 128
MOE_WIN_SHIFT = 4
MOE_WIN = 1 << MOE_WIN_SHIFT
SORT_ALIGN_SHIFT = 4
SORT_ALIGN = 1 << SORT_ALIGN_SHIFT
MOE_TILE = 256
MOE_SEL_CHUNK = 256
NEG_BIG = -1e30
VMEM_LIMIT = 52 * 1024 * 1024

NN = (((1,), (0,)), ((), ()))
NT = (((1,), (1,)), ((), ()))


def _mm(a, b, dn=NN):
    return lax.dot_general(a, b, dn, preferred_element_type=F32)


def _split2(a):
    hi = a.astype(BF16)
    lo = (a - hi.astype(F32)).astype(BF16)
    return hi, lo


def _split3(a):
    hi = a.astype(BF16)
    r1 = a - hi.astype(F32)
    mid = r1.astype(BF16)
    lo = (r1 - mid.astype(F32)).astype(BF16)
    return hi, mid, lo


def _dot1(a, b, dn=NN):
    return _mm(a.astype(BF16), b.astype(BF16), dn)


def _dot_sel_l(sel, b, dn=NN):
    b0, b1, b2 = _split3(b)
    return _mm(sel, b0, dn) + (_mm(sel, b1, dn) + _mm(sel, b2, dn))


def _dot_sel_r(a, sel, dn=NN):
    a0, a1, a2 = _split3(a)
    return _mm(a0, sel, dn) + (_mm(a1, sel, dn) + _mm(a2, sel, dn))


def _iota(shape, dim):
    return lax.broadcasted_iota(jnp.int32, shape, dim)


def _seg_matrix():
    return ((_iota((PAIR, PAIR), 0) // HEAD_DIM) == (_iota((PAIR, PAIR), 1) // HEAD_DIM)).astype(BF16)


def _sigmoid(x):
    return 1.0 / (1.0 + jnp.exp(-x))


def _cparams(sem, vmem=VMEM_LIMIT):
    return pltpu.CompilerParams(dimension_semantics=sem, vmem_limit_bytes=vmem)


def _rope_slab(x, cos, sin_signed):
    lane = _iota(x.shape, 1) % HEAD_DIM
    up = pltpu.roll(x, LANES - ROT_DIM // 2, axis=1)
    down = pltpu.roll(x, ROT_DIM // 2, axis=1)
    partner = jnp.where(lane < ROT_DIM // 2, up, down)
    return x * cos + partner * sin_signed


def _inproj_kernel(rw_cols, q_cols, kv_cols, x_ref, g_ref, w_ref, cos_ref, sin_ref,
                   prw_ref, q_ref, k_ref, v_ref):
    x = x_ref[...]
    h = x * lax.rsqrt(jnp.mean(x * x, axis=-1, keepdims=True) + NORM_EPS) * g_ref[...]
    proj = _mm(h.astype(BF16), w_ref[...])
    prw_ref[...] = proj[:, :rw_cols]
    cos = cos_ref[...]
    sin = sin_ref[...]
    for c in range(q_cols // LANES):
        lo = rw_cols + c * LANES
        q_ref[:, c * LANES:(c + 1) * LANES] = _rope_slab(proj[:, lo:lo + LANES], cos, sin)
    ko = rw_cols + q_cols
    for c in range(kv_cols // LANES):
        k_ref[:, c * LANES:(c + 1) * LANES] = _rope_slab(proj[:, ko + c * LANES:ko + (c + 1) * LANES], cos, sin)
    v_ref[...] = proj[:, ko + kv_cols:ko + 2 * kv_cols]


def _inproj(x, g, w_bf, cos_t, sin_t, tm, rw_cols, q_cols, kv_cols):
    rows, d = x.shape
    cols = w_bf.shape[1]
    full = lambda i: (0, 0)
    row = lambda i: (i, 0)
    return pl.pallas_call(
        functools.partial(_inproj_kernel, rw_cols, q_cols, kv_cols),
        grid=(rows // tm,),
        in_specs=[pl.BlockSpec((tm, d), row), pl.BlockSpec((1, d), full),
                  pl.BlockSpec((d, cols), full),
                  pl.BlockSpec((tm, LANES), row), pl.BlockSpec((tm, LANES), row)],
        out_specs=[pl.BlockSpec((tm, rw_cols), row), pl.BlockSpec((tm, q_cols), row),
                   pl.BlockSpec((tm, kv_cols), row), pl.BlockSpec((tm, kv_cols), row)],
        out_shape=[jax.ShapeDtypeStruct((rows, rw_cols), F32), jax.ShapeDtypeStruct((rows, q_cols), F32),
                   jax.ShapeDtypeStruct((rows, kv_cols), F32), jax.ShapeDtypeStruct((rows, kv_cols), F32)],
        compiler_params=_cparams(("parallel",)),
        name="inproj",
    )(x, g, w_bf, cos_t, sin_t)


def _rwkv_tokenwise(pr, pk, pv, plo, pg, prev_r, prev_k, prev_v, prev_lo, prev_g,
                    mu_r, mu_k, mu_v, mu_lo, mu_g, w0, dw2, a0, aw2, gw2, kkp, kap, rkp, seg):
    r = pr + (prev_r - pr) * mu_r
    k = pk + (prev_k - pk) * mu_k
    v = pv + (prev_v - pv) * mu_v
    lo = plo + (prev_lo - plo) * mu_lo
    gd = pg + (prev_g - pg) * mu_g
    z = -(w0 + _dot1(jnp.tanh(lo), dw2))
    softplus = jnp.maximum(z, 0.0) + jnp.log(1.0 + jnp.exp(-jnp.abs(z)))
    logw = -jnp.exp(-softplus - 0.5)
    a = _sigmoid(a0 + _dot1(lo, aw2))
    g = _dot1(_sigmoid(gd), gw2)
    kk = k * kkp
    nrm = jnp.sqrt(_seg_sum(kk * kk, seg))
    kk = kk / jnp.maximum(nrm, 1e-12)
    k2 = k * (1.0 + (a - 1.0) * kap)
    bonus = _seg_sum(r * k2 * rkp, seg) * v
    return r, k2, v, logw, -kk, kk * a, g, bonus


def _seg_sum(x, seg):
    xh, xl = _split2(x)
    return _mm(xh, seg) + _mm(xl, seg)


def _rwkv_finish(y, bonus, g, lng, lnb, seg):
    mu = _seg_sum(y, seg) * (1.0 / HEAD_DIM)
    d = y - mu
    var = _seg_sum(d * d, seg) * (1.0 / HEAD_DIM)
    yn = d * lax.rsqrt(var + LNX_EPS) * lng + lnb
    return (yn + bonus) * g


def _layer_prompt_kernel(pps, pr_ref, pk_ref, pv_ref, plo_ref, pg_ref,
                        hr_ref, hk_ref, hv_ref, hlo_ref, hg_ref,
                        s0r_ref, s0k_ref, s0v_ref, s0lo_ref, s0g_ref,
                        mur_ref, muk_ref, muv_ref, mulo_ref, mug_ref,
                        w0_ref, dw2_ref, a0_ref, aw2_ref, gw2_ref, kk_ref, ka_ref, rk_ref,
                        lng_ref, lnb_ref, sin_ref,
                        sink_ref, q_ref, kc_ref, kp_ref, vc_ref, vp_ref,
                        x_ref, wa_ref, wb_ref, g2_ref, wr_ref, br_ref,
                        sout_ref, x1_ref, h2_ref, gate_ref, meta_ref, tb_ref, tl_ref, cnt_ref,
                        st_ref, ya_sc, yb_sc, carry_ref):
    i = pl.program_id(1)
    n_i = pl.num_programs(1)
    tt = pr_ref.shape[0]
    slot = i % 2
    y_ref = ya_sc.at[slot]
    yb_ref = yb_sc.at[slot]

    @pl.when(i == 0)
    def _():
        st_ref[...] = sin_ref[...]
        ya_sc[...] = jnp.zeros_like(ya_sc)
        yb_sc[...] = jnp.zeros_like(yb_sc)
        carry_ref[...] = jnp.zeros_like(carry_ref)

    n_q = q_ref.shape[1] // HEAD_DIM
    attn = _attn_prompt_stages(i, n_q, n_q // (kc_ref.shape[1] // HEAD_DIM), sink_ref,
                               q_ref, kc_ref, kp_ref, vc_ref, vp_ref, yb_ref)
    post = _post_stages((i > 0).astype(F32), x_ref, ya_sc.at[1 - slot], yb_sc.at[1 - slot],
                        wa_ref, wb_ref, g2_ref, wr_ref, br_ref,
                        x1_ref, h2_ref, gate_ref, meta_ref, tb_ref, tl_ref, cnt_ref, carry_ref)

    row = _iota((tt, PAIR), 0)

    def prev_of(cur, halo_row, s0_row):
        first = jnp.where(i == 0, s0_row, halo_row)
        return jnp.where(row == 0, first, pltpu.roll(cur, 1, axis=0))

    plo = plo_ref[...]
    pg = pg_ref[...]
    last = slice(SUBLANES - 1, SUBLANES)
    prev_lo = prev_of(plo, hlo_ref[last, :], s0lo_ref[...])
    prev_g = prev_of(pg, hg_ref[last, :], s0g_ref[...])
    ti = _iota((tt, tt), 0)
    tj = _iota((tt, tt), 1)
    same_chunk = (ti // CHUNK) == (tj // CHUNK)
    incl = same_chunk & (tj <= ti)
    strict = same_chunk & (tj < ti)
    seg = _seg_matrix()
    lane = _iota((tt, PAIR), 1)
    eye = (ti == tj).astype(F32)
    pairs = []
    for p in range(pps):
        ls = slice(p * PAIR, (p + 1) * PAIR)
        pr, pk, pv = pr_ref[:, ls], pk_ref[:, ls], pv_ref[:, ls]
        r, k2, v, logw, nkk, b, g, bonus = _rwkv_tokenwise(
            pr, pk, pv, plo, pg,
            prev_of(pr, hr_ref[last, ls], s0r_ref[:, ls]), prev_of(pk, hk_ref[last, ls], s0k_ref[:, ls]),
            prev_of(pv, hv_ref[last, ls], s0v_ref[:, ls]), prev_lo, prev_g,
            mur_ref[:, ls], muk_ref[:, ls], muv_ref[:, ls], mulo_ref[...], mug_ref[...],
            w0_ref[:, ls], dw2_ref[:, ls], a0_ref[:, ls], aw2_ref[:, ls], gw2_ref[:, ls],
            kk_ref[:, ls], ka_ref[:, ls], rk_ref[:, ls], seg)
        pairs.append(dict(ls=ls, r=r, k2=k2, v=v, logw=logw, nkk=nkk, b=b, g=g, bonus=bonus))

    incl_b = incl.astype(BF16)
    for q in pairs:
        q["cs"] = _dot_sel_l(incl_b, q["logw"])
    for q in pairs:
        cs = q["cs"]
        gam = jnp.exp(cs)
        inv = jnp.exp(-cs)
        q["a_t"] = jnp.exp(cs - q["logw"]) * q["nkk"]
        q["r_t"] = gam * q["r"]
        q["bt_T"] = (q["b"] * inv).T
        q["kt_T"] = (q["k2"] * inv).T
        q["gam_T"] = gam.T
        q["bk_T"] = jnp.concatenate([q["bt_T"], q["kt_T"]], axis=1).astype(BF16)
    attn[0]()
    attn[1]()
    post[0]()

    heads = []
    for q in pairs:
        for hh in range(2):
            hm = (lane // HEAD_DIM) == hh
            heads.append(dict(q=q, a=jnp.where(hm, q["a_t"], 0.0), r=jnp.where(hm, q["r_t"], 0.0),
                              v=jnp.where(hm, q["v"], 0.0)))
    for h in heads:
        h["g"] = _mm(jnp.concatenate([h["a"], h["r"]], axis=0).astype(BF16), h["q"]["bk_T"])
    for h in heads:
        gmat = h["g"]
        l_ab = jnp.where(strict, gmat[:tt, :tt], 0.0)
        h["l_ak_m_rk"] = jnp.concatenate([jnp.where(strict, gmat[:tt, tt:], 0.0),
                                          jnp.where(incl, gmat[tt:, tt:], 0.0)], axis=0).astype(BF16)
        h["m_rb"] = jnp.where(incl, gmat[tt:, :tt], 0.0).astype(BF16)
        h["tm"] = eye + l_ab
        h["lp"] = l_ab.astype(BF16)
    post[1]()
    for h in heads:
        h["lp"] = _mm(h["lp"], h["lp"]).astype(BF16)
    for it in range(4):
        for h in heads:
            h["both"] = _mm(jnp.concatenate([h["tm"].astype(BF16), h["lp"]], axis=0), h["lp"])
        attn[2 + it]()
        post[2 + it]()
        for h in heads:
            h["tm"] = h["tm"] + h["both"][:tt]
            h["lp"] = h["both"][tt:].astype(BF16)
    for h in heads:
        h["pq"] = _mm(h["l_ak_m_rk"], h["v"].astype(BF16))
        h["tm"] = h["tm"] + _mm(h["tm"].astype(BF16), h["lp"])
    attn[6]()
    post[6]()
    for h in heads:
        h["tx"] = _mm(h["tm"].astype(BF16),
                      jnp.concatenate([h["a"], h["pq"][:tt]], axis=1).astype(BF16))
    for h in heads:
        h["rx"] = _mm(h["m_rb"], h["tx"].astype(BF16))
    for n, q in enumerate(pairs):
        h0, h1 = heads[2 * n], heads[2 * n + 1]
        q["tatp"] = (h0["tx"] + h1["tx"]).astype(BF16)
        ryc = (h0["rx"] + h1["rx"]) + jnp.concatenate([h0["r"] + h1["r"], h0["pq"][tt:] + h1["pq"][tt:]], axis=1)
        q["ry"] = ryc[:, :PAIR]
        q["yc"] = ryc[:, PAIR:]
        q["v_b"] = q["v"].astype(BF16)
        q["bt_b"] = q["bt_T"].astype(BF16)
        q["kt_b"] = q["kt_T"].astype(BF16)
        q["s"] = st_ref[n]

    bd = seg.astype(F32)
    eye_p = (_iota((PAIR, PAIR), 0) == _iota((PAIR, PAIR), 1)).astype(F32)
    col_t = _iota((PAIR, tt), 1)
    zb = jnp.zeros((PAIR, tt), BF16)
    n_chunks = tt // CHUNK
    for c in range(n_chunks):
        cm = (col_t // CHUNK) == c
        for q in pairs:
            bt_c = jnp.where(cm, q["bt_b"], zb)
            kt_c = jnp.where(cm, q["kt_b"], zb)
            dcol = q["gam_T"][:, (c + 1) * CHUNK - 1:(c + 1) * CHUNK]
            bx = _mm(bt_c, q["tatp"])
            q["mc", c] = (dcol * (eye_p + bd * bx[:, :PAIR])).astype(BF16)
            q["nc", c] = dcol * (bd * (bx[:, PAIR:] + _mm(kt_c, q["v_b"])))
    for c in range(n_chunks):
        sl = slice(c * CHUNK, (c + 1) * CHUNK)
        for q in pairs:
            s_b = q["s"].astype(BF16)
            q["y", c] = _mm(q["ry"][sl].astype(BF16), s_b) + q["yc"][sl]
            q["s"] = _mm(q["mc", c], s_b) + q["nc", c]
    for q in pairs:
        y = jnp.concatenate([q["y", c] for c in range(n_chunks)], axis=0)
        y_ref[:, q["ls"]] = _rwkv_finish(y, q["bonus"], q["g"], lng_ref[:, q["ls"]], lnb_ref[:, q["ls"]], seg)
    for n, q in enumerate(pairs):
        st_ref[n] = q["s"]

    @pl.when(i == n_i - 2)
    def _():
        sout_ref[...] = st_ref[...]


def _layer_prompt(x, prw, shift0, s0_pairs, pp, q, k, v, sinks, wp, tt):
    T, d = x.shape
    n_t = T // tt
    n_pairs = s0_pairs.shape[0]
    pps = RW_PAIRS_PER_STEP
    n_grp = n_pairs // pps
    assert n_grp == 1 and tt % ATT_BLOCK == 0
    qw, kvw = q.shape[1], k.shape[1]
    ab = tt // ATT_BLOCK
    gw = pps * PAIR
    wcols = n_pairs * PAIR
    lo_col = 3 * wcols
    g_col = lo_col + PAIR
    hb = tt // SUBLANES
    mix_tile = lambda i: jnp.minimum(i, n_t - 1)
    post_tile = lambda i: jnp.maximum(i - 1, 0)

    def cur(off):
        return pl.BlockSpec((tt, gw), lambda p, i: (mix_tile(i), off // gw + p))

    def cur_fixed(col):
        return pl.BlockSpec((tt, PAIR), lambda p, i: (mix_tile(i), col // PAIR))

    def halo(off):
        return pl.BlockSpec((SUBLANES, gw), lambda p, i: (jnp.maximum(mix_tile(i) * hb - 1, 0), off // gw + p))

    def halo_fixed(col):
        return pl.BlockSpec((SUBLANES, PAIR), lambda p, i: (jnp.maximum(mix_tile(i) * hb - 1, 0), col // PAIR))

    def vec(off):
        return pl.BlockSpec((1, gw), lambda p, i: (0, off // gw + p))

    def vec_fixed(col):
        return pl.BlockSpec((1, PAIR), lambda p, i: (0, col // PAIR))

    def wmat(rows):
        return pl.BlockSpec((rows, gw), lambda p, i: (0, p))

    in_specs = ([cur(0), cur(wcols), cur(2 * wcols), cur_fixed(lo_col), cur_fixed(g_col)]
                + [halo(0), halo(wcols), halo(2 * wcols), halo_fixed(lo_col), halo_fixed(g_col)]
                + [vec(0), vec(wcols), vec(2 * wcols), vec_fixed(lo_col), vec_fixed(g_col)]
                + [vec(0), vec(wcols), vec(2 * wcols), vec_fixed(lo_col), vec_fixed(g_col)]
                + [vec(0), wmat(PAIR), vec(0), wmat(PAIR), wmat(PAIR), vec(0), vec(0), vec(0), vec(0), vec(0)]
                + [pl.BlockSpec((pps, PAIR, PAIR), lambda p, i: (p, 0, 0))])
    tile = lambda p, i: (mix_tile(i), 0)
    before = lambda p, i: (jnp.maximum(mix_tile(i) * ab - 1, 0), 0)
    full = lambda p, i: (0, 0)
    ptile = lambda p, i: (post_tile(i), 0)
    ptile3 = lambda p, i: (post_tile(i), 0, 0)
    half = wp["wa"].shape[0]
    in_specs += [pl.BlockSpec(memory_space=pltpu.SMEM), pl.BlockSpec((tt, qw), tile),
                 pl.BlockSpec((tt, kvw), tile), pl.BlockSpec((ATT_BLOCK, kvw), before),
                 pl.BlockSpec((tt, kvw), tile), pl.BlockSpec((ATT_BLOCK, kvw), before)]
    in_specs += [pl.BlockSpec((tt, d), ptile), pl.BlockSpec((half, d), full), pl.BlockSpec((half, d), full),
                 pl.BlockSpec((1, d), full), pl.BlockSpec((d, LANES), full), pl.BlockSpec((1, LANES), full)]
    args = ([prw] * 5 + [prw] * 5 + [shift0] * 5 + [pp["mu"]] * 5
            + [pp["w0"], pp["dw2"], pp["a0"], pp["aw2"], pp["gw2"], pp["kk"], pp["ka"], pp["rk"],
               pp["lng"], pp["lnb"], s0_pairs]
            + [sinks, q, k, k, v, v]
            + [x, wp["wa"], wp["wb"], wp["g2"], wp["wr"], wp["br"]])
    return pl.pallas_call(
        functools.partial(_layer_prompt_kernel, pps),
        grid=(n_grp, n_t + 1),
        in_specs=in_specs,
        out_specs=[pl.BlockSpec((pps, PAIR, PAIR), lambda p, i: (p, 0, 0)),
                   pl.BlockSpec((tt, d), ptile), pl.BlockSpec((tt, d), ptile),
                   pl.BlockSpec((tt, TOP_K), ptile), pl.BlockSpec((tt, 2 * TOP_K), ptile),
                   pl.BlockSpec((1, 1, LANES), ptile3), pl.BlockSpec((1, 1, LANES), ptile3),
                   pl.BlockSpec((1, LANES), full)],
        out_shape=[jax.ShapeDtypeStruct((n_pairs, PAIR, PAIR), F32),
                   jax.ShapeDtypeStruct((T, d), F32), jax.ShapeDtypeStruct((T, d), F32),
                   jax.ShapeDtypeStruct((T, TOP_K), F32), jax.ShapeDtypeStruct((T, 2 * TOP_K), jnp.int32),
                   jax.ShapeDtypeStruct((n_t, 1, LANES), jnp.int32),
                   jax.ShapeDtypeStruct((n_t, 1, LANES), jnp.int32),
                   jax.ShapeDtypeStruct((1, LANES), F32)],
        scratch_shapes=[pltpu.VMEM((pps, PAIR, PAIR), F32),
                        pltpu.VMEM((2, tt, wcols), F32), pltpu.VMEM((2, tt, qw), F32),
                        pltpu.VMEM((1, LANES), F32)],
        compiler_params=_cparams(("arbitrary", "arbitrary")),
        name="layer_prompt",
    )(*args)


def _rwkv_step_kernel(slabs_per_step, pr_ref, pk_ref, pv_ref, plo_ref, pg_ref,
                      sr_ref, sk_ref, sv_ref, slo_ref, sg_ref,
                      mur_ref, muk_ref, muv_ref, mulo_ref, mug_ref,
                      w0_ref, dw2_ref, a0_ref, aw2_ref, gw2_ref, kk_ref, ka_ref, rk_ref,
                      lng_ref, lnb_ref, s_ref,
                      y_ref, snew_ref, yacc_ref):
    j = pl.program_id(1)
    n_j = pl.num_programs(1)
    seg = _seg_matrix()
    r, k2, v, logw, nkk, b, g, bonus = _rwkv_tokenwise(
        pr_ref[...], pk_ref[...], pv_ref[...], plo_ref[...], pg_ref[...],
        sr_ref[...], sk_ref[...], sv_ref[...], slo_ref[...], sg_ref[...],
        mur_ref[...], muk_ref[...], muv_ref[...], mulo_ref[...], mug_ref[...],
        w0_ref[...], dw2_ref[...], a0_ref[...], aw2_ref[...], gw2_ref[...],
        kk_ref[...], ka_ref[...], rk_ref[...], seg)
    w = jnp.exp(logw)

    @pl.when(j == 0)
    def _():
        yacc_ref[...] = jnp.zeros_like(yacc_ref)

    slabs_per_head = HEAD_DIM // 2
    ci = _iota((PAIR, PAIR), 0)
    li = _iota((PAIR, PAIR), 1)
    assert slabs_per_head % slabs_per_step == 0
    yacc = yacc_ref[...]
    hh = (j * slabs_per_step) // slabs_per_head
    dup = ((ci == hh * HEAD_DIM + li % HEAD_DIM)).astype(BF16)
    nkk_d, w_d, b_d, k_d, r_d = [_dot_sel_r(x, dup) for x in (nkk, w, b, k2, r)]
    slabs = range(slabs_per_step)
    i0 = [2 * ((j * slabs_per_step + t) % slabs_per_head) for t in slabs]
    s = [s_ref[:, t * PAIR:(t + 1) * PAIR] for t in slabs]
    sa = [_seg_sum(s[t] * nkk_d, seg) for t in slabs]
    v_bc = [_seg_sum(v, (ci == hh * HEAD_DIM + i0[t] + li // HEAD_DIM).astype(BF16)) for t in slabs]
    s_new = [s[t] * w_d + sa[t] * b_d + v_bc[t] * k_d for t in slabs]
    for t in slabs:
        snew_ref[:, t * PAIR:(t + 1) * PAIR] = s_new[t]
    yred = [_seg_sum(s_new[t] * r_d, seg) for t in slabs]
    ysel = [_seg_sum(yred[t], ((ci % HEAD_DIM == 0)
                               & (li == hh * HEAD_DIM + i0[t] + ci // HEAD_DIM)).astype(BF16)) for t in slabs]
    for t in slabs:
        yacc = yacc + ysel[t]
    yacc_ref[...] = yacc

    @pl.when(j == n_j - 1)
    def _():
        y_ref[...] = _rwkv_finish(yacc, bonus, g, lng_ref[...], lnb_ref[...], seg)


def _rwkv_step(prw, shift, s_flat, pp, n_pairs):
    B = prw.shape[0]
    lanes_per_pair = 2 * HEAD_DIM * HEAD_DIM
    blk = STEP_STATE_LANES
    slabs_per_step = blk // PAIR
    steps = lanes_per_pair // blk
    lo_blk = 3 * n_pairs
    g_blk = lo_blk + 1

    def cur(off):
        return pl.BlockSpec((B, PAIR), lambda p, j: (0, off + p))

    def cur_fixed(b_):
        return pl.BlockSpec((B, PAIR), lambda p, j: (0, b_))

    def vec(off):
        return pl.BlockSpec((1, PAIR), lambda p, j: (0, off + p))

    def vec_fixed(b_):
        return pl.BlockSpec((1, PAIR), lambda p, j: (0, b_))

    def wmat(rows):
        return pl.BlockSpec((rows, PAIR), lambda p, j: (0, p))

    sspec = pl.BlockSpec((B, blk), lambda p, j: (0, p * steps + j))
    in_specs = ([cur(0), cur(n_pairs), cur(2 * n_pairs), cur_fixed(lo_blk), cur_fixed(g_blk)] * 2
                + [vec(0), vec(n_pairs), vec(2 * n_pairs), vec_fixed(lo_blk), vec_fixed(g_blk)]
                + [vec(0), wmat(PAIR), vec(0), wmat(PAIR), wmat(PAIR), vec(0), vec(0), vec(0), vec(0), vec(0)]
                + [sspec])
    args = ([prw] * 5 + [shift] * 5 + [pp["mu"]] * 5
            + [pp["w0"], pp["dw2"], pp["a0"], pp["aw2"], pp["gw2"], pp["kk"], pp["ka"], pp["rk"],
               pp["lng"], pp["lnb"], s_flat])
    return pl.pallas_call(
        functools.partial(_rwkv_step_kernel, slabs_per_step),
        grid=(n_pairs, steps),
        in_specs=in_specs,
        out_specs=[pl.BlockSpec((B, PAIR), lambda p, j: (0, p)), sspec],
        out_shape=[jax.ShapeDtypeStruct((B, n_pairs * PAIR), F32),
                   jax.ShapeDtypeStruct(s_flat.shape, F32)],
        scratch_shapes=[pltpu.VMEM((B, PAIR), F32)],
        compiler_params=_cparams(("parallel", "arbitrary")),
        name="rwkv_step",
    )(*args)


def _attn_prompt_stages(tile_idx, n_q, group, sink_ref, q_ref, kc_ref, kp_ref, vc_ref, vp_ref, o_ref):
    blk = ATT_BLOCK
    n_blk = q_ref.shape[0] // blk
    n_kv = n_q // group
    inst = [(j, h) for j in range(n_blk) for h in range(n_q)]
    st = {}

    def prepare():
        q = q_ref[...] * (HEAD_DIM ** -0.5)
        kc = kc_ref[...]
        vc = vc_ref[...]
        kall = jnp.concatenate([kp_ref[...], kc], axis=0)
        vall = jnp.concatenate([vp_ref[...], vc], axis=0)
        rq = _iota((blk, 2 * blk), 0)
        ck = _iota((blk, 2 * blk), 1)
        dist = rq - ck + blk
        in_window = (dist >= 0) & (dist < WINDOW)
        kpos0 = tile_idx * (n_blk * blk) - blk + ck
        st["valid"] = [in_window & (kpos0 >= 0)] + [in_window] * (n_blk - 1)
        st["q"] = {(j, h): q[j * blk:(j + 1) * blk, h * HEAD_DIM:(h + 1) * HEAD_DIM].astype(BF16) for j, h in inst}
        st["kb"] = {(j, g): kall[j * blk:(j + 2) * blk, g * HEAD_DIM:(g + 1) * HEAD_DIM].astype(BF16)
                    for j in range(n_blk) for g in range(n_kv)}
        st["vb"] = {(j, g): vall[j * blk:(j + 2) * blk, g * HEAD_DIM:(g + 1) * HEAD_DIM].astype(BF16)
                    for j in range(n_blk) for g in range(n_kv)}

    def scores():
        st["s"] = {(j, h): jnp.where(st["valid"][j], _mm(st["q"][j, h], st["kb"][j, h // group], NT), NEG_BIG)
                   for j, h in inst}

    def row_max():
        st["m"] = {(j, h): jnp.maximum(jnp.max(st["s"][j, h], axis=-1, keepdims=True), sink_ref[h]) for j, h in inst}

    def probs():
        st["p"] = {(j, h): jnp.exp(st["s"][j, h] - st["m"][j, h]) for j, h in inst}

    def denominators():
        st["d"] = {(j, h): jnp.sum(st["p"][j, h], axis=-1, keepdims=True) + jnp.exp(sink_ref[h] - st["m"][j, h])
                   for j, h in inst}

    def weighted_values():
        st["o"] = {(j, h): _mm(st["p"][j, h].astype(BF16), st["vb"][j, h // group]) for j, h in inst}

    def store():
        for j, h in inst:
            o_ref[j * blk:(j + 1) * blk, h * HEAD_DIM:(h + 1) * HEAD_DIM] = st["o"][j, h] / st["d"][j, h]

    return [prepare, scores, row_max, probs, denominators, weighted_values, store]


def _attn_step_kernel(n_q, group, pos0, sink_ref, q_ref, kn_ref, vn_ref, kc_ref, vc_ref,
                      o_ref, ko_ref, vo_ref):
    bb, wlen, kvw = kc_ref.shape
    rows = bb * n_q
    lane = _iota((rows, kvw), 1)
    rowh = _iota((rows, kvw), 0) % n_q
    mine = (lane // HEAD_DIM) == (rowh // group)
    dupm = (_iota((HEAD_DIM, kvw), 0) == _iota((HEAD_DIM, kvw), 1) % HEAD_DIM).astype(BF16)
    fold = (_iota((kvw, HEAD_DIM), 0) % HEAD_DIM == _iota((kvw, HEAD_DIM), 1)).astype(BF16)
    kidx = _iota((n_q, wlen), 1)
    dist = wlen - kidx
    valid = (dist < WINDOW) & (pos0 - dist >= 0)
    rk = _iota((wlen, kvw), 0)
    sink = sink_ref[...]
    qm_all = jnp.where(mine, _dot_sel_r(q_ref[...] * (HEAD_DIM ** -0.5), dupm), 0.0)
    elems = range(bb)
    qm = [qm_all[t * n_q:(t + 1) * n_q] for t in elems]
    kc = [kc_ref[t] for t in elems]
    vc = [vc_ref[t] for t in elems]
    kn = [kn_ref[t:t + 1, :] for t in elems]
    vn = [vn_ref[t:t + 1, :] for t in elems]
    s = [jnp.where(valid, _dot1(qm[t], kc[t], NT), NEG_BIG) for t in elems]
    s_new = [jnp.sum(qm[t] * kn[t], axis=-1, keepdims=True) for t in elems]
    m = [jnp.maximum(jnp.maximum(jnp.max(s[t], axis=-1, keepdims=True), s_new[t]), sink) for t in elems]
    p = [jnp.exp(s[t] - m[t]) for t in elems]
    p_new = [jnp.exp(s_new[t] - m[t]) for t in elems]
    denom = [jnp.sum(p[t], axis=-1, keepdims=True) + p_new[t] + jnp.exp(sink - m[t]) for t in elems]
    res = [(_dot1(p[t], vc[t]) + p_new[t] * vn[t]) / denom[t] for t in elems]
    o_ref[...] = _dot_sel_r(jnp.where(mine, jnp.concatenate(res, axis=0), 0.0), fold)
    for t in elems:
        ko_ref[t] = jnp.where(rk == wlen - 1, kn[t], pltpu.roll(kc[t], wlen - 1, axis=0))
        vo_ref[t] = jnp.where(rk == wlen - 1, vn[t], pltpu.roll(vc[t], wlen - 1, axis=0))


def _attn_step(q2, k_new, v_new, k_cache, v_cache, sinks_col, n_q, n_kv, pos0):
    B, wlen, kvw = k_cache.shape
    bb = STEP_ATT_BATCH
    return pl.pallas_call(
        functools.partial(_attn_step_kernel, n_q, n_q // n_kv, pos0),
        grid=(B // bb,),
        in_specs=[pl.BlockSpec((n_q, 1), lambda i: (0, 0)),
                  pl.BlockSpec((bb * n_q, HEAD_DIM), lambda i: (i, 0)),
                  pl.BlockSpec((bb, kvw), lambda i: (i, 0)), pl.BlockSpec((bb, kvw), lambda i: (i, 0)),
                  pl.BlockSpec((bb, wlen, kvw), lambda i: (i, 0, 0)),
                  pl.BlockSpec((bb, wlen, kvw), lambda i: (i, 0, 0))],
        out_specs=[pl.BlockSpec((bb * n_q, HEAD_DIM), lambda i: (i, 0)),
                   pl.BlockSpec((bb, wlen, kvw), lambda i: (i, 0, 0)),
                   pl.BlockSpec((bb, wlen, kvw), lambda i: (i, 0, 0))],
        out_shape=[jax.ShapeDtypeStruct((B * n_q, HEAD_DIM), F32),
                   jax.ShapeDtypeStruct((B, wlen, kvw), F32),
                   jax.ShapeDtypeStruct((B, wlen, kvw), F32)],
        compiler_params=_cparams(("parallel",)),
        name="attn_step",
    )(sinks_col, q2, k_new, v_new, k_cache, v_cache)


def _post_stages(counted, x_ref, ya_ref, yb_ref, wa_ref, wb_ref, g2_ref, wr_ref, br_ref,
                 x1_ref, h2_ref, gate_ref, meta_ref, tb_ref, tl_ref, cnt_ref, carry_ref):
    st = {}

    def residual_and_norm():
        mix = _mm(ya_ref[...].astype(BF16), wa_ref[...]) + _mm(yb_ref[...].astype(BF16), wb_ref[...])
        x1 = x_ref[...] + mix
        h2 = x1 * lax.rsqrt(jnp.mean(x1 * x1, axis=-1, keepdims=True) + NORM_EPS) * g2_ref[...]
        x1_ref[...] = x1
        h2_ref[...] = h2
        st["h2"] = h2

    def logits():
        st["l"] = _dot1(st["h2"], wr_ref[...]) + br_ref[...]
        st["vals"], st["idxs"] = [], []

    def next_expert():
        l = st["l"]
        lane = _iota(l.shape, 1)
        m = jnp.max(l, axis=-1, keepdims=True)
        sel = jnp.min(jnp.where(l == m, lane, LANES), axis=-1, keepdims=True)
        st["vals"].append(m)
        st["idxs"].append(sel)
        st["l"] = jnp.where(lane == sel, -jnp.inf, l)

    def gates_and_ranks():
        vals, idxs = st["vals"], st["idxs"]
        tm = vals[0].shape[0]
        lane = _iota((tm, LANES), 1)
        es = [jnp.exp(v - vals[0]) for v in vals]
        tot = es[0] + es[1] + es[2] + es[3]
        onehot = jnp.zeros((tm, LANES), F32)
        for sel in idxs:
            onehot = onehot + (lane == sel).astype(F32)
        strict = (_iota((tm, tm), 1) < _iota((tm, tm), 0)).astype(BF16)
        before = _mm(strict, onehot.astype(BF16))
        for k in range(TOP_K):
            gate_ref[:, k:k + 1] = es[k] / tot
            meta_ref[:, k:k + 1] = idxs[k]
            meta_ref[:, TOP_K + k:TOP_K + k + 1] = jnp.sum(
                jnp.where(lane == idxs[k], before, 0.0), axis=-1, keepdims=True).astype(jnp.int32)
        carry = carry_ref[...]
        cnt_t = jnp.sum(onehot, axis=0, keepdims=True) * counted
        tb_ref[0] = carry.astype(jnp.int32)
        tl_ref[0] = cnt_t.astype(jnp.int32)
        carry_ref[...] = carry + cnt_t
        cnt_ref[...] = carry + cnt_t

    return [residual_and_norm, logits] + [next_expert] * TOP_K + [gates_and_ranks]


def _post_kernel(x_ref, ya_ref, yb_ref, wa_ref, wb_ref, g2_ref, wr_ref, br_ref, cnt0_ref,
                 x1_ref, h2_ref, gate_ref, meta_ref, tb_ref, tl_ref, cnt_ref, carry_ref):
    @pl.when(pl.program_id(0) == 0)
    def _():
        carry_ref[...] = cnt0_ref[...]

    for stage in _post_stages(1.0, x_ref, ya_ref, yb_ref, wa_ref, wb_ref, g2_ref, wr_ref, br_ref,
                              x1_ref, h2_ref, gate_ref, meta_ref, tb_ref, tl_ref, cnt_ref, carry_ref):
        stage()


def _post(x, ya, yb, wp, cnt0, tm):
    rows, d = x.shape
    half = ya.shape[1]
    n_t = rows // tm
    full = lambda i: (0, 0)
    row = lambda i: (i, 0)
    trow = lambda i: (i, 0, 0)
    return pl.pallas_call(
        _post_kernel,
        grid=(n_t,),
        in_specs=[pl.BlockSpec((tm, d), row), pl.BlockSpec((tm, half), row), pl.BlockSpec((tm, half), row),
                  pl.BlockSpec((half, d), full), pl.BlockSpec((half, d), full),
                  pl.BlockSpec((1, d), full), pl.BlockSpec((d, LANES), full), pl.BlockSpec((1, LANES), full),
                  pl.BlockSpec((1, LANES), full)],
        out_specs=[pl.BlockSpec((tm, d), row), pl.BlockSpec((tm, d), row),
                   pl.BlockSpec((tm, TOP_K), row), pl.BlockSpec((tm, 2 * TOP_K), row),
                   pl.BlockSpec((1, 1, LANES), trow), pl.BlockSpec((1, 1, LANES), trow),
                   pl.BlockSpec((1, LANES), full)],
        out_shape=[jax.ShapeDtypeStruct((rows, d), F32), jax.ShapeDtypeStruct((rows, d), F32),
                   jax.ShapeDtypeStruct((rows, TOP_K), F32),
                   jax.ShapeDtypeStruct((rows, 2 * TOP_K), jnp.int32),
                   jax.ShapeDtypeStruct((n_t, 1, LANES), jnp.int32),
                   jax.ShapeDtypeStruct((n_t, 1, LANES), jnp.int32),
                   jax.ShapeDtypeStruct((1, LANES), F32)],
        scratch_shapes=[pltpu.VMEM((1, LANES), F32)],
        compiler_params=_cparams(("arbitrary",)),
        name="post",
    )(x, ya, yb, wp["wa"], wp["wb"], wp["g2"], wp["wr"], wp["br"], cnt0)


def _n_windows(base, length):
    off = base & (SORT_ALIGN - 1)
    n = lax.shift_right_logical(off + length + (MOE_WIN - 1), MOE_WIN_SHIFT)
    return off, jnp.where(length > 0, n, 0)


def _window_targets(meta, tb_vec, tl_vec):
    tm = meta.shape[0]
    off, n_win = _n_windows(tb_vec, tl_vec)
    upper = (_iota((LANES, LANES), 0) < _iota((LANES, LANES), 1)).astype(BF16)
    slot_start = _mm(n_win.astype(F32).astype(BF16), upper)
    pos0 = slot_start * MOE_WIN + off.astype(F32)
    lane = _iota((tm, LANES), 1)
    tgts = []
    for k in range(TOP_K):
        p0 = jnp.sum(jnp.where(lane == meta[:, k:k + 1], pos0, 0.0), axis=-1, keepdims=True)
        tgts.append(p0.astype(jnp.int32) + meta[:, TOP_K + k:TOP_K + k + 1])
    return tgts


def _for_each_window(n_e, pstart_ref, tb_ref, tl_ref, fn, rows_ref, cnt_ref, b, per_expert_fn=None):
    def per_expert(e, slot0):
        base = tb_ref[0, 0, e]
        length = tl_ref[0, 0, e]
        off, n = _n_windows(base, length)
        row0 = pstart_ref[e] + base - off
        if per_expert_fn is not None:
            per_expert_fn(e, slot0, off, length, n)

        def per_window(w, c):
            row = row0 + w * MOE_WIN
            rows_ref[b, slot0 + w] = row
            fn(slot0 + w, pl.multiple_of(row, SORT_ALIGN))
            return c

        lax.fori_loop(0, n, per_window, 0)
        return slot0 + n

    cnt_ref[b] = lax.fori_loop(0, n_e, per_expert, 0)


def _for_recorded_windows(fn, rows_ref, cnt_ref, b):
    def body(slot, c):
        fn(slot, pl.multiple_of(rows_ref[b, slot], SORT_ALIGN))
        return c

    lax.fori_loop(0, cnt_ref[b], body, 0)


def _moe_slots(tm):
    n = -(-(tm * TOP_K + N_EXPERTS * (SORT_ALIGN - 1 + MOE_WIN - 1)) // MOE_WIN)
    per_lane_tile = LANES // MOE_WIN
    return -(-n // per_lane_tile) * per_lane_tile


def _scatter_kernel(n_e, continues, pstart_ref, z1_ref, z2_ref, tb_ref, tl_ref, tbv_ref, tlv_ref,
                    meta_ref, h_ref, *rest):
    if continues:
        cin_ref, _xs_alias, xs_ref, cout_ref, xw_ref, zero_ref, carry_ref, rows_ref, cnt_ref, sem, zsem = rest
    else:
        xs_ref, cout_ref, xw_ref, zero_ref, carry_ref, rows_ref, cnt_ref, sem, zsem = rest
    i = pl.program_id(0)
    n_i = pl.num_programs(0)
    tm = h_ref.shape[0]
    bm = zero_ref.shape[0]
    buf = i % 2
    n_rows_w = xw_ref.shape[1]

    @pl.when(i == 0)
    def _():
        if continues:
            carry_ref[...] = cin_ref[...]
        else:
            zero_ref[...] = jnp.zeros_like(zero_ref)

            def zcopy(row):
                return pltpu.make_async_copy(zero_ref, xs_ref.at[pl.ds(pl.multiple_of(row, SORT_ALIGN), bm)], zsem)

            for e in range(n_e):
                zcopy(z1_ref[e]).start()

                @pl.when(z2_ref[e] != z1_ref[e])
                def _():
                    zcopy(z2_ref[e]).start()
            for e in range(n_e):
                zcopy(z1_ref[e]).wait()

                @pl.when(z2_ref[e] != z1_ref[e])
                def _():
                    zcopy(z2_ref[e]).wait()

            carry_ref[...] = jnp.zeros_like(carry_ref)

    tgts = _window_targets(meta_ref[...], tbv_ref[0], tlv_ref[0])
    hb = h_ref[...].astype(BF16)
    for c0 in range(0, n_rows_w, MOE_SEL_CHUNK):
        w = min(MOE_SEL_CHUNK, n_rows_w - c0)
        lane_c = _iota((tm, w), 1) + c0
        sel = jnp.zeros((tm, w), F32)
        for tgt in tgts:
            sel = sel + (lane_c == tgt).astype(F32)
        xw_ref[buf, c0:c0 + w, :] = _mm(sel.T.astype(BF16), hb).astype(BF16)

    def splice_carry(e, slot0, off, length, n):
        @pl.when(n > 0)
        def _():
            g0 = pl.multiple_of(slot0 * MOE_WIN, MOE_WIN)
            xw_ref[buf, pl.ds(g0, SORT_ALIGN), :] = xw_ref[buf, pl.ds(g0, SORT_ALIGN), :] + carry_ref[e]
            filled = off + length
            gl = pl.multiple_of(g0 + lax.shift_right_logical(filled, SORT_ALIGN_SHIFT) * SORT_ALIGN, SORT_ALIGN)
            last = xw_ref[buf, pl.ds(gl, SORT_ALIGN), :]
            carry_ref[e] = jnp.where((filled & (SORT_ALIGN - 1)) != 0, last, jnp.zeros_like(last))

    def copy(b, slot, row):
        return pltpu.make_async_copy(xw_ref.at[b, pl.ds(pl.multiple_of(slot * MOE_WIN, MOE_WIN), MOE_WIN)],
                                     xs_ref.at[pl.ds(row, MOE_WIN)], sem.at[b])

    @pl.when(i > 0)
    def _():
        _for_recorded_windows(lambda slot, row: copy(1 - buf, slot, row).wait(), rows_ref, cnt_ref, 1 - buf)

    _for_each_window(n_e, pstart_ref, tb_ref, tl_ref, lambda slot, row: copy(buf, slot, row).start(),
                     rows_ref, cnt_ref, buf, splice_carry)

    @pl.when(i == n_i - 1)
    def _():
        _for_recorded_windows(lambda slot, row: copy(buf, slot, row).wait(), rows_ref, cnt_ref, buf)
        cout_ref[...] = carry_ref[...]


def _scatter(pstart, z1, z2, tbase, tlen, meta, h2, prior, n_rows_sorted, bm, tm):
    rows, d = h2.shape
    n_e = pstart.shape[0]
    n_slots = _moe_slots(tm)
    smem = pl.BlockSpec(memory_space=pltpu.SMEM)
    tile3 = lambda i: (i, 0, 0)
    tile_smem = lambda im: pl.BlockSpec((1, 1, LANES), im, memory_space=pltpu.SMEM)
    carry_spec = pl.BlockSpec((n_e, SORT_ALIGN, d), lambda i: (0, 0, 0))
    in_specs = [smem, smem, smem,
                tile_smem(tile3), tile_smem(tile3),
                pl.BlockSpec((1, 1, LANES), tile3), pl.BlockSpec((1, 1, LANES), tile3),
                pl.BlockSpec((tm, 2 * TOP_K), lambda i: (i, 0)),
                pl.BlockSpec((tm, d), lambda i: (i, 0))]
    args = [pstart, z1, z2, tbase, tlen, tbase, tlen, meta, h2]
    aliases = {}
    if prior is not None:
        in_specs += [carry_spec, pl.BlockSpec(memory_space=pl.ANY)]
        aliases = {len(args) + 1: 0}
        args += list(prior)
    return pl.pallas_call(
        functools.partial(_scatter_kernel, n_e, prior is not None),
        grid=(rows // tm,),
        in_specs=in_specs,
        out_specs=[pl.BlockSpec(memory_space=pl.ANY), carry_spec],
        out_shape=[jax.ShapeDtypeStruct((n_rows_sorted, d), BF16),
                   jax.ShapeDtypeStruct((n_e, SORT_ALIGN, d), BF16)],
        input_output_aliases=aliases,
        scratch_shapes=[pltpu.VMEM((2, n_slots * MOE_WIN, d), BF16), pltpu.VMEM((bm, d), BF16),
                        pltpu.VMEM((n_e, SORT_ALIGN, d), BF16),
                        pltpu.SMEM((2, n_slots), jnp.int32), pltpu.SMEM((2,), jnp.int32),
                        pltpu.SemaphoreType.DMA((2,)), pltpu.SemaphoreType.DMA(())],
        compiler_params=_cparams(("arbitrary",)),
        name="moe_scatter",
    )(*args)


def _expert_kernel(d_ff, be_ref, nused_ref, xs_ref, w1_ref, b1_ref, w2_ref, b2_ref, ys_ref, w1b_ref, w2b_ref):
    i = pl.program_id(0)
    new_expert = jnp.logical_or(i == 0, be_ref[i] != be_ref[jnp.maximum(i - 1, 0)])

    @pl.when(jnp.logical_and(i < nused_ref[0], new_expert))
    def _():
        w1b_ref[...] = w1_ref[0].astype(BF16)
        w2b_ref[...] = w2_ref[0].astype(BF16)

    @pl.when(i < nused_ref[0])
    def _():
        h = _mm(xs_ref[...], w1b_ref[...]) + b1_ref[0]
        hg = jnp.minimum(h[:, :d_ff], SWIGLU_LIMIT)
        hu = jnp.clip(h[:, d_ff:], -SWIGLU_LIMIT, SWIGLU_LIMIT)
        act = hg * _sigmoid(SWIGLU_ALPHA * hg) * (hu + 1.0)
        ys_ref[...] = (_mm(act.astype(BF16), w2b_ref[...]) + b2_ref[0]).astype(ys_ref.dtype)

    @pl.when(i >= nused_ref[0])
    def _():
        ys_ref[...] = jnp.zeros_like(ys_ref)


def _experts(block_e, n_used, xs, w1, b1, w2, b2, bm):
    R, d = xs.shape
    d_ff = w2.shape[1]
    nb = R // bm

    def rows(i, be, nu):
        return (jnp.minimum(i, nu[0] - 1), 0)

    def wsel(i, be, nu):
        return (be[i], 0, 0)

    return pl.pallas_call(
        functools.partial(_expert_kernel, d_ff),
        grid_spec=pltpu.PrefetchScalarGridSpec(
            num_scalar_prefetch=2,
            grid=(nb,),
            in_specs=[pl.BlockSpec((bm, d), rows),
                      pl.BlockSpec((1, d, 2 * d_ff), wsel), pl.BlockSpec((1, 1, 2 * d_ff), wsel),
                      pl.BlockSpec((1, d_ff, d), wsel), pl.BlockSpec((1, 1, d), wsel)],
            out_specs=pl.BlockSpec((bm, d), lambda i, be, nu: (i, 0)),
            scratch_shapes=[pltpu.VMEM((d, 2 * d_ff), BF16), pltpu.VMEM((d_ff, d), BF16)]),
        out_shape=jax.ShapeDtypeStruct((R, d), xs.dtype),
        compiler_params=_cparams(("arbitrary",)),
        name="moe_experts",
    )(block_e, n_used, xs, w1, b1, w2, b2)


def _combine_kernel(n_e, pstart_ref, tb_ref, tl_ref, tbn_ref, tln_ref, tbv_ref, tlv_ref, meta_ref, gate_ref,
                    x1_ref, gf_ref, ys_ref, o_ref, win_ref, rows_ref, cnt_ref, sem):
    i = pl.program_id(0)
    n = pl.num_programs(0)
    tm = x1_ref.shape[0]
    buf = i % 2
    n_rows_w = win_ref.shape[1]

    def copy(b, slot, row):
        return pltpu.make_async_copy(
            ys_ref.at[pl.ds(row, MOE_WIN)],
            win_ref.at[b, pl.ds(pl.multiple_of(slot * MOE_WIN, MOE_WIN), MOE_WIN)], sem.at[b])

    @pl.when(i == 0)
    def _():
        win_ref[...] = jnp.zeros_like(win_ref)
        _for_each_window(n_e, pstart_ref, tb_ref, tl_ref, lambda slot, row: copy(buf, slot, row).start(),
                         rows_ref, cnt_ref, buf)

    @pl.when(i + 1 < n)
    def _():
        _for_each_window(n_e, pstart_ref, tbn_ref, tln_ref, lambda slot, row: copy(1 - buf, slot, row).start(),
                         rows_ref, cnt_ref, 1 - buf)

    tgts = _window_targets(meta_ref[...], tbv_ref[0], tlv_ref[0])
    gate = gate_ref[...]
    _for_recorded_windows(lambda slot, row: copy(buf, slot, row).wait(), rows_ref, cnt_ref, buf)
    y = x1_ref[...]
    for c0 in range(0, n_rows_w, MOE_SEL_CHUNK):
        w = min(MOE_SEL_CHUNK, n_rows_w - c0)
        lane_c = _iota((tm, w), 1) + c0
        sel = jnp.zeros((tm, w), F32)
        for k, tgt in enumerate(tgts):
            sel = sel + jnp.where(lane_c == tgt, gate[:, k:k + 1], 0.0)
        sel_hi, sel_lo = _split2(sel)
        wb = win_ref[buf, c0:c0 + w, :]
        y = y + (_mm(sel_hi, wb) + _mm(sel_lo, wb))
    o_ref[...] = y * lax.rsqrt(jnp.mean(y * y, axis=-1, keepdims=True) + NORM_EPS) * gf_ref[...]


def _combine(pstart, tbase, tlen, meta, gate, x1, gf, ys, tm):
    rows, d = x1.shape
    n = rows // tm
    n_e = pstart.shape[0]
    cur3 = lambda i: (i, 0, 0)
    nxt3 = lambda i: (jnp.minimum(i + 1, n - 1), 0, 0)
    tile_smem = lambda im: pl.BlockSpec((1, 1, LANES), im, memory_space=pltpu.SMEM)
    return pl.pallas_call(
        functools.partial(_combine_kernel, n_e),
        grid=(n,),
        in_specs=[pl.BlockSpec(memory_space=pltpu.SMEM),
                  tile_smem(cur3), tile_smem(cur3), tile_smem(nxt3), tile_smem(nxt3),
                  pl.BlockSpec((1, 1, LANES), cur3), pl.BlockSpec((1, 1, LANES), cur3),
                  pl.BlockSpec((tm, 2 * TOP_K), lambda i: (i, 0)),
                  pl.BlockSpec((tm, TOP_K), lambda i: (i, 0)),
                  pl.BlockSpec((tm, d), lambda i: (i, 0)),
                  pl.BlockSpec((1, d), lambda i: (0, 0)),
                  pl.BlockSpec(memory_space=pl.ANY)],
        out_specs=pl.BlockSpec((tm, d), lambda i: (i, 0)),
        out_shape=jax.ShapeDtypeStruct((rows, d), F32),
        scratch_shapes=[pltpu.VMEM((2, _moe_slots(tm) * MOE_WIN, d), ys.dtype),
                        pltpu.SMEM((2, _moe_slots(tm)), jnp.int32), pltpu.SMEM((2,), jnp.int32),
                        pltpu.SemaphoreType.DMA((2,))],
        compiler_params=_cparams(("arbitrary",)),
        name="moe_combine",
    )(pstart, tbase, tlen, tbase, tlen, tbase, tlen, meta, gate, x1, gf, ys)


def _rope_tables(pos):
    half = ROT_DIM // 2
    inv = ROPE_THETA ** (-jnp.arange(0, ROT_DIM, 2, dtype=F32) / ROT_DIM)
    ang = inv[:, None] * pos.astype(F32)[None, :]
    cos, sin = jnp.cos(ang), jnp.sin(ang)
    n = pos.shape[0]
    pad1 = jnp.ones((HEAD_DIM - ROT_DIM, n), F32)
    pad0 = jnp.zeros((HEAD_DIM - ROT_DIM, n), F32)
    cos_h = jnp.concatenate([cos, cos, pad1], axis=0)
    sin_h = jnp.concatenate([-sin, sin, pad0], axis=0)
    reps = (LANES // HEAD_DIM, 1)
    return jnp.tile(cos_h, reps).T, jnp.tile(sin_h, reps).T


def _state_from_pairs(Sp):
    a = Sp[:, :HEAD_DIM, :HEAD_DIM]
    b = Sp[:, HEAD_DIM:, HEAD_DIM:]
    St = jnp.stack([a, b], axis=1).reshape(-1, HEAD_DIM, HEAD_DIM)
    return jnp.swapaxes(St, 1, 2)


def kernel(x_prompt, x_sample, state_rwkv_wkv, state_rwkv_shift, cache_swa_k, cache_swa_v, norm1_g, w_in, mu_shift, decay_w0, decay_w2, aaa_a0, aaa_w2, gate_w2, k_k, k_a, r_k, lnx_g, lnx_b, attn_sinks, w_out, norm2_g, w_router, b_router, w_mlp1, b_mlp1, w_mlp2, b_mlp2, norm_f_g):
    depth = w_in.shape[0]
    assert depth == 1 and x_prompt.shape[0] == 1 and x_sample.shape[1] == 1
    T, d = x_prompt.shape[1], x_prompt.shape[2]
    B = x_sample.shape[0]
    past_len = PAST_LEN
    H = state_rwkv_wkv.shape[2]
    rw_w = H * HEAD_DIM
    n_pairs = H // 2
    n_q = attn_sinks.shape[1]
    n_kv = cache_swa_k.shape[3]
    q_cols = n_q * HEAD_DIM
    kv_cols = n_kv * HEAD_DIM
    rw_cols = state_rwkv_shift.shape[2]
    assert rw_cols == 3 * rw_w + 2 * HEAD_DIM + PAIR and kv_cols == LANES
    assert T % RW_TILE == 0 and B % ROW_TILE == 0 and B % 8 == 0
    wlen = cache_swa_k.shape[2]
    l = 0

    w_in_bf = w_in[l].astype(BF16)
    zero_half = jnp.zeros((HEAD_DIM, rw_w), F32)
    pp = dict(mu=mu_shift[l][None], w0=decay_w0[l][None],
              dw2=jnp.concatenate([decay_w2[l], zero_half], axis=0),
              a0=aaa_a0[l][None], aw2=jnp.concatenate([zero_half, aaa_w2[l]], axis=0),
              gw2=gate_w2[l], kk=k_k[l][None], ka=k_a[l][None], rk=r_k[l].reshape(1, rw_w),
              lng=lnx_g[l][None], lnb=lnx_b[l][None])
    w_out_bf = w_out[l].astype(BF16)
    n_e = w_router.shape[2]
    wr = jnp.pad(w_router[l], ((0, 0), (0, LANES - n_e)))
    br = jnp.concatenate([b_router[l], jnp.full((LANES - n_e,), NEG_BIG, F32)])[None]
    wp = dict(wa=w_out_bf[:rw_w], wb=w_out_bf[rw_w:], g2=norm2_g[l][None], wr=wr, br=br)
    g1 = norm1_g[l][None]

    xp = x_prompt[0]
    cos_p, sin_p = _rope_tables(jnp.arange(T))
    prw_p, q_p, k_p, v_p = _inproj(xp, g1, w_in_bf, cos_p, sin_p, INPROJ_TILE, rw_cols, q_cols, kv_cols)
    s0_p = jnp.zeros((n_pairs, PAIR, PAIR), F32)
    shift0_p = jnp.zeros((1, rw_cols), F32)
    assert RW_TILE == MOE_TILE
    sfin_p, x1_p, h2_p, gate_p, meta_p, tb_p, tl_p, cnt = _layer_prompt(
        xp, prw_p, shift0_p, s0_p, pp, q_p, k_p, v_p, attn_sinks[l], wp, RW_TILE)

    xs_ = x_sample[:, 0]
    cos_s, sin_s = _rope_tables(jnp.full((B,), past_len))
    prw_s, q_s, k_s, v_s = _inproj(xs_, g1, w_in_bf, cos_s, sin_s, ROW_TILE, rw_cols, q_cols, kv_cols)
    s_flat = state_rwkv_wkv[l].reshape(B, H * HEAD_DIM * HEAD_DIM)
    ya_s, snew_flat = _rwkv_step(prw_s, state_rwkv_shift[l], s_flat, pp, n_pairs)
    o2, kc_new, vc_new = _attn_step(q_s.reshape(B * n_q, HEAD_DIM), k_s, v_s,
                                    cache_swa_k[l].reshape(B, wlen, kv_cols),
                                    cache_swa_v[l].reshape(B, wlen, kv_cols),
                                    attn_sinks[l][:, None], n_q, n_kv, past_len)
    yb_s = o2.reshape(B, q_cols)

    rows = T + B
    x1_s, h2_s, gate_s, meta_s, tb_s, tl_s, cnt = _post(xs_, ya_s, yb_s, wp, cnt, ROW_TILE)

    counts = cnt[0, :n_e].astype(jnp.int32)
    padded = (counts + MOE_WIN + MOE_BM - 1) // MOE_BM * MOE_BM
    pend = jnp.cumsum(padded)
    pstart = (pend - padded).astype(jnp.int32)
    n_blocks = -(-(rows * TOP_K) // MOE_BM) + n_e + -(-(n_e * MOE_WIN) // MOE_BM)
    block_start = jnp.arange(n_blocks, dtype=jnp.int32) * MOE_BM
    block_e = jnp.minimum(jnp.sum((pend[None, :] <= block_start[:, None]).astype(jnp.int32), axis=1),
                          n_e - 1).astype(jnp.int32)
    n_used = (pend[-1] // MOE_BM).astype(jnp.int32)[None]
    z1 = (pstart + counts // MOE_BM * MOE_BM).astype(jnp.int32)
    z2 = (pend - MOE_BM).astype(jnp.int32)

    n_sorted = n_blocks * MOE_BM
    xs_sorted, carry = _scatter(pstart, z1, z2, tb_p, tl_p, meta_p, h2_p, None, n_sorted, MOE_BM, MOE_TILE)
    xs_sorted, _ = _scatter(pstart, z1, z2, tb_s, tl_s, meta_s, h2_s, (carry, xs_sorted), n_sorted, MOE_BM,
                            ROW_TILE)
    ys_sorted = _experts(block_e, n_used, xs_sorted, w_mlp1[l], b_mlp1[l][:, None], w_mlp2[l],
                         b_mlp2[l][:, None], MOE_BM)
    gf = norm_f_g[None]
    y_p = _combine(pstart, tb_p, tl_p, meta_p, gate_p, x1_p, gf, ys_sorted, MOE_TILE)
    y_s = _combine(pstart, tb_s, tl_s, meta_s, gate_s, x1_s, gf, ys_sorted, ROW_TILE)

    sdt = state_rwkv_wkv.dtype
    return (y_p[None], y_s[:, None],
            _state_from_pairs(sfin_p)[None, None].astype(sdt), prw_p[T - 1][None, None],
            k_p[T - min(WINDOW, T):].reshape(1, 1, -1, n_kv, HEAD_DIM),
            v_p[T - min(WINDOW, T):].reshape(1, 1, -1, n_kv, HEAD_DIM),
            snew_flat.reshape(1, B, H, HEAD_DIM, HEAD_DIM).astype(sdt), prw_s[None],
            kc_new.reshape(1, B, wlen, n_kv, HEAD_DIM), vc_new.reshape(1, B, wlen, n_kv, HEAD_DIM))
```

```python
import functools

import jax
import jax.numpy as jnp
from jax import lax
from jax.experimental import pallas as pl
from jax.experimental.pallas import tpu as pltpu

F32 = jnp.float32
BF16 = jnp.bfloat16

LANES = 128
SUBLANES = 8
HEAD_DIM = 64
PAIR = 2 * HEAD_DIM
CHUNK = 64
RW_TILE = 256
RW_PAIRS_PER_STEP = 4
ROT_DIM = 16
ROPE_THETA = 500000.0
WINDOW = 128
PAST_LEN = 16384
ATT_BLOCK = 128
N_EXPERTS = 32
TOP_K = 4
SWIGLU_ALPHA = 1.702
SWIGLU_LIMIT = 7.0
NORM_EPS = 1e-5
LNX_EPS = HEAD_DIM * 1e-5
INPROJ_TILE = 512
STEP_STATE_LANES = 1024
STEP_ATT_BATCH = 8
MOE_BM = 512
ROW_TILE = 128
MOE_WIN_SHIFT = 4
MOE_WIN = 1 << MOE_WIN_SHIFT
SORT_ALIGN_SHIFT = 4
SORT_ALIGN = 1 << SORT_ALIGN_SHIFT
MOE_TILE = 256
MOE_SEL_CHUNK = 256
NEG_BIG = -1e30
VMEM_LIMIT = 52 * 1024 * 1024

NN = (((1,), (0,)), ((), ()))
NT = (((1,), (1,)), ((), ()))


def _mm(a, b, dn=NN):
    return lax.dot_general(a, b, dn, preferred_element_type=F32)


def _split2(a):
    hi = a.astype(BF16)
    lo = (a - hi.astype(F32)).astype(BF16)
    return hi, lo


def _split3(a):
    hi = a.astype(BF16)
    r1 = a - hi.astype(F32)
    mid = r1.astype(BF16)
    lo = (r1 - mid.astype(F32)).astype(BF16)
    return hi, mid, lo


def _dot1(a, b, dn=NN):
    return _mm(a.astype(BF16), b.astype(BF16), dn)


def _dot_sel_l(sel, b, dn=NN):
    b0, b1, b2 = _split3(b)
    return _mm(sel, b0, dn) + (_mm(sel, b1, dn) + _mm(sel, b2, dn))


def _dot_sel_r(a, sel, dn=NN):
    a0, a1, a2 = _split3(a)
    return _mm(a0, sel, dn) + (_mm(a1, sel, dn) + _mm(a2, sel, dn))


def _iota(shape, dim):
    return lax.broadcasted_iota(jnp.int32, shape, dim)


def _seg_matrix():
    return ((_iota((PAIR, PAIR), 0) // HEAD_DIM) == (_iota((PAIR, PAIR), 1) // HEAD_DIM)).astype(BF16)


def _sigmoid(x):
    return 1.0 / (1.0 + jnp.exp(-x))


def _cparams(sem, vmem=VMEM_LIMIT):
    return pltpu.CompilerParams(dimension_semantics=sem, vmem_limit_bytes=vmem)


def _rope_slab(x, cos, sin_signed):
    lane = _iota(x.shape, 1) % HEAD_DIM
    up = pltpu.roll(x, LANES - ROT_DIM // 2, axis=1)
    down = pltpu.roll(x, ROT_DIM // 2, axis=1)
    partner = jnp.where(lane < ROT_DIM // 2, up, down)
    return x * cos + partner * sin_signed


def _inproj_kernel(rw_cols, q_cols, kv_cols, x_ref, g_ref, w_ref, cos_ref, sin_ref,
                   prw_ref, q_ref, k_ref, v_ref):
    x = x_ref[...]
    h = x * lax.rsqrt(jnp.mean(x * x, axis=-1, keepdims=True) + NORM_EPS) * g_ref[...]
    proj = _mm(h.astype(BF16), w_ref[...])
    prw_ref[...] = proj[:, :rw_cols]
    cos = cos_ref[...]
    sin = sin_ref[...]
    for c in range(q_cols // LANES):
        lo = rw_cols + c * LANES
        q_ref[:, c * LANES:(c + 1) * LANES] = _rope_slab(proj[:, lo:lo + LANES], cos, sin)
    ko = rw_cols + q_cols
    for c in range(kv_cols // LANES):
        k_ref[:, c * LANES:(c + 1) * LANES] = _rope_slab(proj[:, ko + c * LANES:ko + (c + 1) * LANES], cos, sin)
    v_ref[...] = proj[:, ko + kv_cols:ko + 2 * kv_cols]


def _inproj(x, g, w_bf, cos_t, sin_t, tm, rw_cols, q_cols, kv_cols):
    rows, d = x.shape
    cols = w_bf.shape[1]
    full = lambda i: (0, 0)
    row = lambda i: (i, 0)
    return pl.pallas_call(
        functools.partial(_inproj_kernel, rw_cols, q_cols, kv_cols),
        grid=(rows // tm,),
        in_specs=[pl.BlockSpec((tm, d), row), pl.BlockSpec((1, d), full),
                  pl.BlockSpec((d, cols), full),
                  pl.BlockSpec((tm, LANES), row), pl.BlockSpec((tm, LANES), row)],
        out_specs=[pl.BlockSpec((tm, rw_cols), row), pl.BlockSpec((tm, q_cols), row),
                   pl.BlockSpec((tm, kv_cols), row), pl.BlockSpec((tm, kv_cols), row)],
        out_shape=[jax.ShapeDtypeStruct((rows, rw_cols), F32), jax.ShapeDtypeStruct((rows, q_cols), F32),
                   jax.ShapeDtypeStruct((rows, kv_cols), F32), jax.ShapeDtypeStruct((rows, kv_cols), F32)],
        compiler_params=_cparams(("parallel",)),
        name="inproj",
    )(x, g, w_bf, cos_t, sin_t)


def _rwkv_tokenwise(pr, pk, pv, plo, pg, prev_r, prev_k, prev_v, prev_lo, prev_g,
                    mu_r, mu_k, mu_v, mu_lo, mu_g, w0, dw2, a0, aw2, gw2, kkp, kap, rkp, seg):
    r = pr + (prev_r - pr) * mu_r
    k = pk + (prev_k - pk) * mu_k
    v = pv + (prev_v - pv) * mu_v
    lo = plo + (prev_lo - plo) * mu_lo
    gd = pg + (prev_g - pg) * mu_g
    z = -(w0 + _dot1(jnp.tanh(lo), dw2))
    softplus = jnp.maximum(z, 0.0) + jnp.log(1.0 + jnp.exp(-jnp.abs(z)))
    logw = -jnp.exp(-softplus - 0.5)
    a = _sigmoid(a0 + _dot1(lo, aw2))
    g = _dot1(_sigmoid(gd), gw2)
    kk = k * kkp
    nrm = jnp.sqrt(_seg_sum(kk * kk, seg))
    kk = kk / jnp.maximum(nrm, 1e-12)
    k2 = k * (1.0 + (a - 1.0) * kap)
    bonus = _seg_sum(r * k2 * rkp, seg) * v
    return r, k2, v, logw, -kk, kk * a, g, bonus


def _seg_sum(x, seg):
    xh, xl = _split2(x)
    return _mm(xh, seg) + _mm(xl, seg)


def _rwkv_finish(y, bonus, g, lng, lnb, seg):
    mu = _seg_sum(y, seg) * (1.0 / HEAD_DIM)
    d = y - mu
    var = _seg_sum(d * d, seg) * (1.0 / HEAD_DIM)
    yn = d * lax.rsqrt(var + LNX_EPS) * lng + lnb
    return (yn + bonus) * g


def _layer_prompt_kernel(pps, pr_ref, pk_ref, pv_ref, plo_ref, pg_ref,
                        hr_ref, hk_ref, hv_ref, hlo_ref, hg_ref,
                        s0r_ref, s0k_ref, s0v_ref, s0lo_ref, s0g_ref,
                        mur_ref, muk_ref, muv_ref, mulo_ref, mug_ref,
                        w0_ref, dw2_ref, a0_ref, aw2_ref, gw2_ref, kk_ref, ka_ref, rk_ref,
                        lng_ref, lnb_ref, sin_ref,
                        sink_ref, q_ref, kc_ref, kp_ref, vc_ref, vp_ref,
                        x_ref, wa_ref, wb_ref, g2_ref, wr_ref, br_ref,
                        sout_ref, x1_ref, h2_ref, gate_ref, meta_ref, tb_ref, tl_ref, cnt_ref,
                        st_ref, ya_sc, yb_sc, carry_ref):
    i = pl.program_id(1)
    n_i = pl.num_programs(1)
    tt = pr_ref.shape[0]
    slot = i % 2
    y_ref = ya_sc.at[slot]
    yb_ref = yb_sc.at[slot]

    @pl.when(i == 0)
    def _():
        st_ref[...] = sin_ref[...]
        ya_sc[...] = jnp.zeros_like(ya_sc)
        yb_sc[...] = jnp.zeros_like(yb_sc)
        carry_ref[...] = jnp.zeros_like(carry_ref)

    n_q = q_ref.shape[1] // HEAD_DIM
    attn = _attn_prompt_stages(i, n_q, n_q // (kc_ref.shape[1] // HEAD_DIM), sink_ref,
                               q_ref, kc_ref, kp_ref, vc_ref, vp_ref, yb_ref)
    post = _post_stages((i > 0).astype(F32), x_ref, ya_sc.at[1 - slot], yb_sc.at[1 - slot],
                        wa_ref, wb_ref, g2_ref, wr_ref, br_ref,
                        x1_ref, h2_ref, gate_ref, meta_ref, tb_ref, tl_ref, cnt_ref, carry_ref)

    row = _iota((tt, PAIR), 0)

    def prev_of(cur, halo_row, s0_row):
        first = jnp.where(i == 0, s0_row, halo_row)
        return jnp.where(row == 0, first, pltpu.roll(cur, 1, axis=0))

    plo = plo_ref[...]
    pg = pg_ref[...]
    last = slice(SUBLANES - 1, SUBLANES)
    prev_lo = prev_of(plo, hlo_ref[last, :], s0lo_ref[...])
    prev_g = prev_of(pg, hg_ref[last, :], s0g_ref[...])
    ti = _iota((tt, tt), 0)
    tj = _iota((tt, tt), 1)
    same_chunk = (ti // CHUNK) == (tj // CHUNK)
    incl = same_chunk & (tj <= ti)
    strict = same_chunk & (tj < ti)
    seg = _seg_matrix()
    lane = _iota((tt, PAIR), 1)
    eye = (ti == tj).astype(F32)
    pairs = []
    for p in range(pps):
        ls = slice(p * PAIR, (p + 1) * PAIR)
        pr, pk, pv = pr_ref[:, ls], pk_ref[:, ls], pv_ref[:, ls]
        r, k2, v, logw, nkk, b, g, bonus = _rwkv_tokenwise(
            pr, pk, pv, plo, pg,
            prev_of(pr, hr_ref[last, ls], s0r_ref[:, ls]), prev_of(pk, hk_ref[last, ls], s0k_ref[:, ls]),
            prev_of(pv, hv_ref[last, ls], s0v_ref[:, ls]), prev_lo, prev_g,
            mur_ref[:, ls], muk_ref[:, ls], muv_ref[:, ls], mulo_ref[...], mug_ref[...],
            w0_ref[:, ls], dw2_ref[:, ls], a0_ref[:, ls], aw2_ref[:, ls], gw2_ref[:, ls],
            kk_ref[:, ls], ka_ref[:, ls], rk_ref[:, ls], seg)
        pairs.append(dict(ls=ls, r=r, k2=k2, v=v, logw=logw, nkk=nkk, b=b, g=g, bonus=bonus))

    incl_b = incl.astype(BF16)
    for q in pairs:
        q["cs"] = _dot_sel_l(incl_b, q["logw"])
    for q in pairs:
        cs = q["cs"]
        gam = jnp.exp(cs)
        inv = jnp.exp(-cs)
        q["a_t"] = jnp.exp(cs - q["logw"]) * q["nkk"]
        q["r_t"] = gam * q["r"]
        q["bt_T"] = (q["b"] * inv).T
        q["kt_T"] = (q["k2"] * inv).T
        q["gam_T"] = gam.T
        q["bk_T"] = jnp.concatenate([q["bt_T"], q["kt_T"]], axis=1).astype(BF16)
    attn[0]()
    attn[1]()
    post[0]()

    heads = []
    for q in pairs:
        for hh in range(2):
            hm = (lane // HEAD_DIM) == hh
            heads.append(dict(q=q, a=jnp.where(hm, q["a_t"], 0.0), r=jnp.where(hm, q["r_t"], 0.0),
                              v=jnp.where(hm, q["v"], 0.0)))
    for h in heads:
        h["g"] = _mm(jnp.concatenate([h["a"], h["r"]], axis=0).astype(BF16), h["q"]["bk_T"])
    for h in heads:
        gmat = h["g"]
        l_ab = jnp.where(strict, gmat[:tt, :tt], 0.0)
        h["l_ak_m_rk"] = jnp.concatenate([jnp.where(strict, gmat[:tt, tt:], 0.0),
                                          jnp.where(incl, gmat[tt:, tt:], 0.0)], axis=0).astype(BF16)
        h["m_rb"] = jnp.where(incl, gmat[tt:, :tt], 0.0).astype(BF16)
        h["tm"] = eye + l_ab
        h["lp"] = l_ab.astype(BF16)
    post[1]()
    for h in heads:
        h["lp"] = _mm(h["lp"], h["lp"]).astype(BF16)
    for it in range(4):
        for h in heads:
            h["both"] = _mm(jnp.concatenate([h["tm"].astype(BF16), h["lp"]], axis=0), h["lp"])
        attn[2 + it]()
        post[2 + it]()
        for h in heads:
            h["tm"] = h["tm"] + h["both"][:tt]
            h["lp"] = h["both"][tt:].astype(BF16)
    for h in heads:
        h["pq"] = _mm(h["l_ak_m_rk"], h["v"].astype(BF16))
        h["tm"] = h["tm"] + _mm(h["tm"].astype(BF16), h["lp"])
    attn[6]()
    post[6]()
    for h in heads:
        h["tx"] = _mm(h["tm"].astype(BF16),
                      jnp.concatenate([h["a"], h["pq"][:tt]], axis=1).astype(BF16))
    for h in heads:
        h["rx"] = _mm(h["m_rb"], h["tx"].astype(BF16))
    for n, q in enumerate(pairs):
        h0, h1 = heads[2 * n], heads[2 * n + 1]
        q["tatp"] = (h0["tx"] + h1["tx"]).astype(BF16)
        ryc = (h0["rx"] + h1["rx"]) + jnp.concatenate([h0["r"] + h1["r"], h0["pq"][tt:] + h1["pq"][tt:]], axis=1)
        q["ry"] = ryc[:, :PAIR]
        q["yc"] = ryc[:, PAIR:]
        q["v_b"] = q["v"].astype(BF16)
        q["bt_b"] = q["bt_T"].astype(BF16)
        q["kt_b"] = q["kt_T"].astype(BF16)
        q["s"] = st_ref[n]

    bd = seg.astype(F32)
    eye_p = (_iota((PAIR, PAIR), 0) == _iota((PAIR, PAIR), 1)).astype(F32)
    col_t = _iota((PAIR, tt), 1)
    zb = jnp.zeros((PAIR, tt), BF16)
    n_chunks = tt // CHUNK
    for c in range(n_chunks):
        cm = (col_t // CHUNK) == c
        for q in pairs:
            bt_c = jnp.where(cm, q["bt_b"], zb)
            kt_c = jnp.where(cm, q["kt_b"], zb)
            dcol = q["gam_T"][:, (c + 1) * CHUNK - 1:(c + 1) * CHUNK]
            bx = _mm(bt_c, q["tatp"])
            q["mc", c] = (dcol * (eye_p + bd * bx[:, :PAIR])).astype(BF16)
            q["nc", c] = dcol * (bd * (bx[:, PAIR:] + _mm(kt_c, q["v_b"])))
    for c in range(n_chunks):
        sl = slice(c * CHUNK, (c + 1) * CHUNK)
        for q in pairs:
            s_b = q["s"].astype(BF16)
            q["y", c] = _mm(q["ry"][sl].astype(BF16), s_b) + q["yc"][sl]
            q["s"] = _mm(q["mc", c], s_b) + q["nc", c]
    for q in pairs:
        y = jnp.concatenate([q["y", c] for c in range(n_chunks)], axis=0)
        y_ref[:, q["ls"]] = _rwkv_finish(y, q["bonus"], q["g"], lng_ref[:, q["ls"]], lnb_ref[:, q["ls"]], seg)
    for n, q in enumerate(pairs):
        st_ref[n] = q["s"]

    @pl.when(i == n_i - 2)
    def _():
        sout_ref[...] = st_ref[...]


def _layer_prompt(x, prw, shift0, s0_pairs, pp, q, k, v, sinks, wp, tt):
    T, d = x.shape
    n_t = T // tt
    n_pairs = s0_pairs.shape[0]
    pps = RW_PAIRS_PER_STEP
    n_grp = n_pairs // pps
    assert n_grp == 1 and tt % ATT_BLOCK == 0
    qw, kvw = q.shape[1], k.shape[1]
    ab = tt // ATT_BLOCK
    gw = pps * PAIR
    wcols = n_pairs * PAIR
    lo_col = 3 * wcols
    g_col = lo_col + PAIR
    hb = tt // SUBLANES
    mix_tile = lambda i: jnp.minimum(i, n_t - 1)
    post_tile = lambda i: jnp.maximum(i - 1, 0)

    def cur(off):
        return pl.BlockSpec((tt, gw), lambda p, i: (mix_tile(i), off // gw + p))

    def cur_fixed(col):
        return pl.BlockSpec((tt, PAIR), lambda p, i: (mix_tile(i), col // PAIR))

    def halo(off):
        return pl.BlockSpec((SUBLANES, gw), lambda p, i: (jnp.maximum(mix_tile(i) * hb - 1, 0), off // gw + p))

    def halo_fixed(col):
        return pl.BlockSpec((SUBLANES, PAIR), lambda p, i: (jnp.maximum(mix_tile(i) * hb - 1, 0), col // PAIR))

    def vec(off):
        return pl.BlockSpec((1, gw), lambda p, i: (0, off // gw + p))

    def vec_fixed(col):
        return pl.BlockSpec((1, PAIR), lambda p, i: (0, col // PAIR))

    def wmat(rows):
        return pl.BlockSpec((rows, gw), lambda p, i: (0, p))

    in_specs = ([cur(0), cur(wcols), cur(2 * wcols), cur_fixed(lo_col), cur_fixed(g_col)]
                + [halo(0), halo(wcols), halo(2 * wcols), halo_fixed(lo_col), halo_fixed(g_col)]
                + [vec(0), vec(wcols), vec(2 * wcols), vec_fixed(lo_col), vec_fixed(g_col)]
                + [vec(0), vec(wcols), vec(2 * wcols), vec_fixed(lo_col), vec_fixed(g_col)]
                + [vec(0), wmat(PAIR), vec(0), wmat(PAIR), wmat(PAIR), vec(0), vec(0), vec(0), vec(0), vec(0)]
                + [pl.BlockSpec((pps, PAIR, PAIR), lambda p, i: (p, 0, 0))])
    tile = lambda p, i: (mix_tile(i), 0)
    before = lambda p, i: (jnp.maximum(mix_tile(i) * ab - 1, 0), 0)
    full = lambda p, i: (0, 0)
    ptile = lambda p, i: (post_tile(i), 0)
    ptile3 = lambda p, i: (post_tile(i), 0, 0)
    half = wp["wa"].shape[0]
    in_specs += [pl.BlockSpec(memory_space=pltpu.SMEM), pl.BlockSpec((tt, qw), tile),
                 pl.BlockSpec((tt, kvw), tile), pl.BlockSpec((ATT_BLOCK, kvw), before),
                 pl.BlockSpec((tt, kvw), tile), pl.BlockSpec((ATT_BLOCK, kvw), before)]
    in_specs += [pl.BlockSpec((tt, d), ptile), pl.BlockSpec((half, d), full), pl.BlockSpec((half, d), full),
                 pl.BlockSpec((1, d), full), pl.BlockSpec((d, LANES), full), pl.BlockSpec((1, LANES), full)]
    args = ([prw] * 5 + [prw] * 5 + [shift0] * 5 + [pp["mu"]] * 5
            + [pp["w0"], pp["dw2"], pp["a0"], pp["aw2"], pp["gw2"], pp["kk"], pp["ka"], pp["rk"],
               pp["lng"], pp["lnb"], s0_pairs]
            + [sinks, q, k, k, v, v]
            + [x, wp["wa"], wp["wb"], wp["g2"], wp["wr"], wp["br"]])
    return pl.pallas_call(
        functools.partial(_layer_prompt_kernel, pps),
        grid=(n_grp, n_t + 1),
        in_specs=in_specs,
        out_specs=[pl.BlockSpec((pps, PAIR, PAIR), lambda p, i: (p, 0, 0)),
                   pl.BlockSpec((tt, d), ptile), pl.BlockSpec((tt, d), ptile),
                   pl.BlockSpec((tt, TOP_K), ptile), pl.BlockSpec((tt, 2 * TOP_K), ptile),
                   pl.BlockSpec((1, 1, LANES), ptile3), pl.BlockSpec((1, 1, LANES), ptile3),
                   pl.BlockSpec((1, LANES), full)],
        out_shape=[jax.ShapeDtypeStruct((n_pairs, PAIR, PAIR), F32),
                   jax.ShapeDtypeStruct((T, d), F32), jax.ShapeDtypeStruct((T, d), F32),
                   jax.ShapeDtypeStruct((T, TOP_K), F32), jax.ShapeDtypeStruct((T, 2 * TOP_K), jnp.int32),
                   jax.ShapeDtypeStruct((n_t, 1, LANES), jnp.int32),
                   jax.ShapeDtypeStruct((n_t, 1, LANES), jnp.int32),
                   jax.ShapeDtypeStruct((1, LANES), F32)],
        scratch_shapes=[pltpu.VMEM((pps, PAIR, PAIR), F32),
                        pltpu.VMEM((2, tt, wcols), F32), pltpu.VMEM((2, tt, qw), F32),
                        pltpu.VMEM((1, LANES), F32)],
        compiler_params=_cparams(("arbitrary", "arbitrary")),
        name="layer_prompt",
    )(*args)


def _rwkv_step_kernel(slabs_per_step, pr_ref, pk_ref, pv_ref, plo_ref, pg_ref,
                      sr_ref, sk_ref, sv_ref, slo_ref, sg_ref,
                      mur_ref, muk_ref, muv_ref, mulo_ref, mug_ref,
                      w0_ref, dw2_ref, a0_ref, aw2_ref, gw2_ref, kk_ref, ka_ref, rk_ref,
                      lng_ref, lnb_ref, s_ref,
                      y_ref, snew_ref, yacc_ref):
    j = pl.program_id(1)
    n_j = pl.num_programs(1)
    seg = _seg_matrix()
    r, k2, v, logw, nkk, b, g, bonus = _rwkv_tokenwise(
        pr_ref[...], pk_ref[...], pv_ref[...], plo_ref[...], pg_ref[...],
        sr_ref[...], sk_ref[...], sv_ref[...], slo_ref[...], sg_ref[...],
        mur_ref[...], muk_ref[...], muv_ref[...], mulo_ref[...], mug_ref[...],
        w0_ref[...], dw2_ref[...], a0_ref[...], aw2_ref[...], gw2_ref[...],
        kk_ref[...], ka_ref[...], rk_ref[...], seg)
    w = jnp.exp(logw)

    @pl.when(j == 0)
    def _():
        yacc_ref[...] = jnp.zeros_like(yacc_ref)

    slabs_per_head = HEAD_DIM // 2
    ci = _iota((PAIR, PAIR), 0)
    li = _iota((PAIR, PAIR), 1)
    assert slabs_per_head % slabs_per_step == 0
    yacc = yacc_ref[...]
    hh = (j * slabs_per_step) // slabs_per_head
    dup = ((ci == hh * HEAD_DIM + li % HEAD_DIM)).astype(BF16)
    nkk_d, w_d, b_d, k_d, r_d = [_dot_sel_r(x, dup) for x in (nkk, w, b, k2, r)]
    slabs = range(slabs_per_step)
    i0 = [2 * ((j * slabs_per_step + t) % slabs_per_head) for t in slabs]
    s = [s_ref[:, t * PAIR:(t + 1) * PAIR] for t in slabs]
    sa = [_seg_sum(s[t] * nkk_d, seg) for t in slabs]
    v_bc = [_seg_sum(v, (ci == hh * HEAD_DIM + i0[t] + li // HEAD_DIM).astype(BF16)) for t in slabs]
    s_new = [s[t] * w_d + sa[t] * b_d + v_bc[t] * k_d for t in slabs]
    for t in slabs:
        snew_ref[:, t * PAIR:(t + 1) * PAIR] = s_new[t]
    yred = [_seg_sum(s_new[t] * r_d, seg) for t in slabs]
    ysel = [_seg_sum(yred[t], ((ci % HEAD_DIM == 0)
                               & (li == hh * HEAD_DIM + i0[t] + ci // HEAD_DIM)).astype(BF16)) for t in slabs]
    for t in slabs:
        yacc = yacc + ysel[t]
    yacc_ref[...] = yacc

    @pl.when(j == n_j - 1)
    def _():
        y_ref[...] = _rwkv_finish(yacc, bonus, g, lng_ref[...], lnb_ref[...], seg)


def _rwkv_step(prw, shift, s_flat, pp, n_pairs):
    B = prw.shape[0]
    lanes_per_pair = 2 * HEAD_DIM * HEAD_DIM
    blk = STEP_STATE_LANES
    slabs_per_step = blk // PAIR
    steps = lanes_per_pair // blk
    lo_blk = 3 * n_pairs
    g_blk = lo_blk + 1

    def cur(off):
        return pl.BlockSpec((B, PAIR), lambda p, j: (0, off + p))

    def cur_fixed(b_):
        return pl.BlockSpec((B, PAIR), lambda p, j: (0, b_))

    def vec(off):
        return pl.BlockSpec((1, PAIR), lambda p, j: (0, off + p))

    def vec_fixed(b_):
        return pl.BlockSpec((1, PAIR), lambda p, j: (0, b_))

    def wmat(rows):
        return pl.BlockSpec((rows, PAIR), lambda p, j: (0, p))

    sspec = pl.BlockSpec((B, blk), lambda p, j: (0, p * steps + j))
    in_specs = ([cur(0), cur(n_pairs), cur(2 * n_pairs), cur_fixed(lo_blk), cur_fixed(g_blk)] * 2
                + [vec(0), vec(n_pairs), vec(2 * n_pairs), vec_fixed(lo_blk), vec_fixed(g_blk)]
                + [vec(0), wmat(PAIR), vec(0), wmat(PAIR), wmat(PAIR), vec(0), vec(0), vec(0), vec(0), vec(0)]
                + [sspec])
    args = ([prw] * 5 + [shift] * 5 + [pp["mu"]] * 5
            + [pp["w0"], pp["dw2"], pp["a0"], pp["aw2"], pp["gw2"], pp["kk"], pp["ka"], pp["rk"],
               pp["lng"], pp["lnb"], s_flat])
    return pl.pallas_call(
        functools.partial(_rwkv_step_kernel, slabs_per_step),
        grid=(n_pairs, steps),
        in_specs=in_specs,
        out_specs=[pl.BlockSpec((B, PAIR), lambda p, j: (0, p)), sspec],
        out_shape=[jax.ShapeDtypeStruct((B, n_pairs * PAIR), F32),
                   jax.ShapeDtypeStruct(s_flat.shape, F32)],
        scratch_shapes=[pltpu.VMEM((B, PAIR), F32)],
        compiler_params=_cparams(("parallel", "arbitrary")),
        name="rwkv_step",
    )(*args)


def _attn_prompt_stages(tile_idx, n_q, group, sink_ref, q_ref, kc_ref, kp_ref, vc_ref, vp_ref, o_ref):
    blk = ATT_BLOCK
    n_blk = q_ref.shape[0] // blk
    n_kv = n_q // group
    inst = [(j, h) for j in range(n_blk) for h in range(n_q)]
    st = {}

    def prepare():
        q = q_ref[...] * (HEAD_DIM ** -0.5)
        kc = kc_ref[...]
        vc = vc_ref[...]
        kall = jnp.concatenate([kp_ref[...], kc], axis=0)
        vall = jnp.concatenate([vp_ref[...], vc], axis=0)
        rq = _iota((blk, 2 * blk), 0)
        ck = _iota((blk, 2 * blk), 1)
        dist = rq - ck + blk
        in_window = (dist >= 0) & (dist < WINDOW)
        kpos0 = tile_idx * (n_blk * blk) - blk + ck
        st["valid"] = [in_window & (kpos0 >= 0)] + [in_window] * (n_blk - 1)
        st["q"] = {(j, h): q[j * blk:(j + 1) * blk, h * HEAD_DIM:(h + 1) * HEAD_DIM].astype(BF16) for j, h in inst}
        st["kb"] = {(j, g): kall[j * blk:(j + 2) * blk, g * HEAD_DIM:(g + 1) * HEAD_DIM].astype(BF16)
                    for j in range(n_blk) for g in range(n_kv)}
        st["vb"] = {(j, g): vall[j * blk:(j + 2) * blk, g * HEAD_DIM:(g + 1) * HEAD_DIM].astype(BF16)
                    for j in range(n_blk) for g in range(n_kv)}

    def scores():
        st["s"] = {(j, h): jnp.where(st["valid"][j], _mm(st["q"][j, h], st["kb"][j, h // group], NT), NEG_BIG)
                   for j, h in inst}

    def row_max():
        st["m"] = {(j, h): jnp.maximum(jnp.max(st["s"][j, h], axis=-1, keepdims=True), sink_ref[h]) for j, h in inst}

    def probs():
        st["p"] = {(j, h): jnp.exp(st["s"][j, h] - st["m"][j, h]) for j, h in inst}

    def denominators():
        st["d"] = {(j, h): jnp.sum(st["p"][j, h], axis=-1, keepdims=True) + jnp.exp(sink_ref[h] - st["m"][j, h])
                   for j, h in inst}

    def weighted_values():
        st["o"] = {(j, h): _mm(st["p"][j, h].astype(BF16), st["vb"][j, h // group]) for j, h in inst}

    def store():
        for j, h in inst:
            o_ref[j * blk:(j + 1) * blk, h * HEAD_DIM:(h + 1) * HEAD_DIM] = st["o"][j, h] / st["d"][j, h]

    return [prepare, scores, row_max, probs, denominators, weighted_values, store]


def _attn_step_kernel(n_q, group, pos0, sink_ref, q_ref, kn_ref, vn_ref, kc_ref, vc_ref,
                      o_ref, ko_ref, vo_ref):
    bb, wlen, kvw = kc_ref.shape
    rows = bb * n_q
    lane = _iota((rows, kvw), 1)
    rowh = _iota((rows, kvw), 0) % n_q
    mine = (lane // HEAD_DIM) == (rowh // group)
    dupm = (_iota((HEAD_DIM, kvw), 0) == _iota((HEAD_DIM, kvw), 1) % HEAD_DIM).astype(BF16)
    fold = (_iota((kvw, HEAD_DIM), 0) % HEAD_DIM == _iota((kvw, HEAD_DIM), 1)).astype(BF16)
    kidx = _iota((n_q, wlen), 1)
    dist = wlen - kidx
    valid = (dist < WINDOW) & (pos0 - dist >= 0)
    rk = _iota((wlen, kvw), 0)
    sink = sink_ref[...]
    qm_all = jnp.where(mine, _dot_sel_r(q_ref[...] * (HEAD_DIM ** -0.5), dupm), 0.0)
    elems = range(bb)
    qm = [qm_all[t * n_q:(t + 1) * n_q] for t in elems]
    kc = [kc_ref[t] for t in elems]
    vc = [vc_ref[t] for t in elems]
    kn = [kn_ref[t:t + 1, :] for t in elems]
    vn = [vn_ref[t:t + 1, :] for t in elems]
    s = [jnp.where(valid, _dot1(qm[t], kc[t], NT), NEG_BIG) for t in elems]
    s_new = [jnp.sum(qm[t] * kn[t], axis=-1, keepdims=True) for t in elems]
    m = [jnp.maximum(jnp.maximum(jnp.max(s[t], axis=-1, keepdims=True), s_new[t]), sink) for t in elems]
    p = [jnp.exp(s[t] - m[t]) for t in elems]
    p_new = [jnp.exp(s_new[t] - m[t]) for t in elems]
    denom = [jnp.sum(p[t], axis=-1, keepdims=True) + p_new[t] + jnp.exp(sink - m[t]) for t in elems]
    res = [(_dot1(p[t], vc[t]) + p_new[t] * vn[t]) / denom[t] for t in elems]
    o_ref[...] = _dot_sel_r(jnp.where(mine, jnp.concatenate(res, axis=0), 0.0), fold)
    for t in elems:
        ko_ref[t] = jnp.where(rk == wlen - 1, kn[t], pltpu.roll(kc[t], wlen - 1, axis=0))
        vo_ref[t] = jnp.where(rk == wlen - 1, vn[t], pltpu.roll(vc[t], wlen - 1, axis=0))


def _attn_step(q2, k_new, v_new, k_cache, v_cache, sinks_col, n_q, n_kv, pos0):
    B, wlen, kvw = k_cache.shape
    bb = STEP_ATT_BATCH
    return pl.pallas_call(
        functools.partial(_attn_step_kernel, n_q, n_q // n_kv, pos0),
        grid=(B // bb,),
        in_specs=[pl.BlockSpec((n_q, 1), lambda i: (0, 0)),
                  pl.BlockSpec((bb * n_q, HEAD_DIM), lambda i: (i, 0)),
                  pl.BlockSpec((bb, kvw), lambda i: (i, 0)), pl.BlockSpec((bb, kvw), lambda i: (i, 0)),
                  pl.BlockSpec((bb, wlen, kvw), lambda i: (i, 0, 0)),
                  pl.BlockSpec((bb, wlen, kvw), lambda i: (i, 0, 0))],
        out_specs=[pl.BlockSpec((bb * n_q, HEAD_DIM), lambda i: (i, 0)),
                   pl.BlockSpec((bb, wlen, kvw), lambda i: (i, 0, 0)),
                   pl.BlockSpec((bb, wlen, kvw), lambda i: (i, 0, 0))],
        out_shape=[jax.ShapeDtypeStruct((B * n_q, HEAD_DIM), F32),
                   jax.ShapeDtypeStruct((B, wlen, kvw), F32),
                   jax.ShapeDtypeStruct((B, wlen, kvw), F32)],
        compiler_params=_cparams(("parallel",)),
        name="attn_step",
    )(sinks_col, q2, k_new, v_new, k_cache, v_cache)


def _post_stages(counted, x_ref, ya_ref, yb_ref, wa_ref, wb_ref, g2_ref, wr_ref, br_ref,
                 x1_ref, h2_ref, gate_ref, meta_ref, tb_ref, tl_ref, cnt_ref, carry_ref):
    st = {}

    def residual_and_norm():
        mix = _mm(ya_ref[...].astype(BF16), wa_ref[...]) + _mm(yb_ref[...].astype(BF16), wb_ref[...])
        x1 = x_ref[...] + mix
        h2 = x1 * lax.rsqrt(jnp.mean(x1 * x1, axis=-1, keepdims=True) + NORM_EPS) * g2_ref[...]
        x1_ref[...] = x1
        h2_ref[...] = h2
        st["h2"] = h2

    def logits():
        st["l"] = _dot1(st["h2"], wr_ref[...]) + br_ref[...]
        st["vals"], st["idxs"] = [], []

    def next_expert():
        l = st["l"]
        lane = _iota(l.shape, 1)
        m = jnp.max(l, axis=-1, keepdims=True)
        sel = jnp.min(jnp.where(l == m, lane, LANES), axis=-1, keepdims=True)
        st["vals"].append(m)
        st["idxs"].append(sel)
        st["l"] = jnp.where(lane == sel, -jnp.inf, l)

    def gates_and_ranks():
        vals, idxs = st["vals"], st["idxs"]
        tm = vals[0].shape[0]
        lane = _iota((tm, LANES), 1)
        es = [jnp.exp(v - vals[0]) for v in vals]
        tot = es[0] + es[1] + es[2] + es[3]
        onehot = jnp.zeros((tm, LANES), F32)
        for sel in idxs:
            onehot = onehot + (lane == sel).astype(F32)
        strict = (_iota((tm, tm), 1) < _iota((tm, tm), 0)).astype(BF16)
        before = _mm(strict, onehot.astype(BF16))
        for k in range(TOP_K):
            gate_ref[:, k:k + 1] = es[k] / tot
            meta_ref[:, k:k + 1] = idxs[k]
            meta_ref[:, TOP_K + k:TOP_K + k + 1] = jnp.sum(
                jnp.where(lane == idxs[k], before, 0.0), axis=-1, keepdims=True).astype(jnp.int32)
        carry = carry_ref[...]
        cnt_t = jnp.sum(onehot, axis=0, keepdims=True) * counted
        tb_ref[0] = carry.astype(jnp.int32)
        tl_ref[0] = cnt_t.astype(jnp.int32)
        carry_ref[...] = carry + cnt_t
        cnt_ref[...] = carry + cnt_t

    return [residual_and_norm, logits] + [next_expert] * TOP_K + [gates_and_ranks]


def _post_kernel(x_ref, ya_ref, yb_ref, wa_ref, wb_ref, g2_ref, wr_ref, br_ref, cnt0_ref,
                 x1_ref, h2_ref, gate_ref, meta_ref, tb_ref, tl_ref, cnt_ref, carry_ref):
    @pl.when(pl.program_id(0) == 0)
    def _():
        carry_ref[...] = cnt0_ref[...]

    for stage in _post_stages(1.0, x_ref, ya_ref, yb_ref, wa_ref, wb_ref, g2_ref, wr_ref, br_ref,
                              x1_ref, h2_ref, gate_ref, meta_ref, tb_ref, tl_ref, cnt_ref, carry_ref):
        stage()


def _post(x, ya, yb, wp, cnt0, tm):
    rows, d = x.shape
    half = ya.shape[1]
    n_t = rows // tm
    full = lambda i: (0, 0)
    row = lambda i: (i, 0)
    trow = lambda i: (i, 0, 0)
    return pl.pallas_call(
        _post_kernel,
        grid=(n_t,),
        in_specs=[pl.BlockSpec((tm, d), row), pl.BlockSpec((tm, half), row), pl.BlockSpec((tm, half), row),
                  pl.BlockSpec((half, d), full), pl.BlockSpec((half, d), full),
                  pl.BlockSpec((1, d), full), pl.BlockSpec((d, LANES), full), pl.BlockSpec((1, LANES), full),
                  pl.BlockSpec((1, LANES), full)],
        out_specs=[pl.BlockSpec((tm, d), row), pl.BlockSpec((tm, d), row),
                   pl.BlockSpec((tm, TOP_K), row), pl.BlockSpec((tm, 2 * TOP_K), row),
                   pl.BlockSpec((1, 1, LANES), trow), pl.BlockSpec((1, 1, LANES), trow),
                   pl.BlockSpec((1, LANES), full)],
        out_shape=[jax.ShapeDtypeStruct((rows, d), F32), jax.ShapeDtypeStruct((rows, d), F32),
                   jax.ShapeDtypeStruct((rows, TOP_K), F32),
                   jax.ShapeDtypeStruct((rows, 2 * TOP_K), jnp.int32),
                   jax.ShapeDtypeStruct((n_t, 1, LANES), jnp.int32),
                   jax.ShapeDtypeStruct((n_t, 1, LANES), jnp.int32),
                   jax.ShapeDtypeStruct((1, LANES), F32)],
        scratch_shapes=[pltpu.VMEM((1, LANES), F32)],
        compiler_params=_cparams(("arbitrary",)),
        name="post",
    )(x, ya, yb, wp["wa"], wp["wb"], wp["g2"], wp["wr"], wp["br"], cnt0)


def _n_windows(base, length):
    off = base & (SORT_ALIGN - 1)
    n = lax.shift_right_logical(off + length + (MOE_WIN - 1), MOE_WIN_SHIFT)
    return off, jnp.where(length > 0, n, 0)


def _window_targets(meta, tb_vec, tl_vec):
    tm = meta.shape[0]
    off, n_win = _n_windows(tb_vec, tl_vec)
    upper = (_iota((LANES, LANES), 0) < _iota((LANES, LANES), 1)).astype(BF16)
    slot_start = _mm(n_win.astype(F32).astype(BF16), upper)
    pos0 = slot_start * MOE_WIN + off.astype(F32)
    lane = _iota((tm, LANES), 1)
    tgts = []
    for k in range(TOP_K):
        p0 = jnp.sum(jnp.where(lane == meta[:, k:k + 1], pos0, 0.0), axis=-1, keepdims=True)
        tgts.append(p0.astype(jnp.int32) + meta[:, TOP_K + k:TOP_K + k + 1])
    return tgts


def _for_each_window(n_e, pstart_ref, tb_ref, tl_ref, fn, rows_ref, cnt_ref, b, per_expert_fn=None):
    def per_expert(e, slot0):
        base = tb_ref[0, 0, e]
        length = tl_ref[0, 0, e]
        off, n = _n_windows(base, length)
        row0 = pstart_ref[e] + base - off
        if per_expert_fn is not None:
            per_expert_fn(e, slot0, off, length, n)

        def per_window(w, c):
            row = row0 + w * MOE_WIN
            rows_ref[b, slot0 + w] = row
            fn(slot0 + w, pl.multiple_of(row, SORT_ALIGN))
            return c

        lax.fori_loop(0, n, per_window, 0)
        return slot0 + n

    cnt_ref[b] = lax.fori_loop(0, n_e, per_expert, 0)


def _for_recorded_windows(fn, rows_ref, cnt_ref, b):
    def body(slot, c):
        fn(slot, pl.multiple_of(rows_ref[b, slot], SORT_ALIGN))
        return c

    lax.fori_loop(0, cnt_ref[b], body, 0)


def _moe_slots(tm):
    n = -(-(tm * TOP_K + N_EXPERTS * (SORT_ALIGN - 1 + MOE_WIN - 1)) // MOE_WIN)
    per_lane_tile = LANES // MOE_WIN
    return -(-n // per_lane_tile) * per_lane_tile


def _scatter_kernel(n_e, continues, pstart_ref, z1_ref, z2_ref, tb_ref, tl_ref, tbv_ref, tlv_ref,
                    meta_ref, h_ref, *rest):
    if continues:
        cin_ref, _xs_alias, xs_ref, cout_ref, xw_ref, zero_ref, carry_ref, rows_ref, cnt_ref, sem, zsem = rest
    else:
        xs_ref, cout_ref, xw_ref, zero_ref, carry_ref, rows_ref, cnt_ref, sem, zsem = rest
    i = pl.program_id(0)
    n_i = pl.num_programs(0)
    tm = h_ref.shape[0]
    bm = zero_ref.shape[0]
    buf = i % 2
    n_rows_w = xw_ref.shape[1]

    @pl.when(i == 0)
    def _():
        if continues:
            carry_ref[...] = cin_ref[...]
        else:
            zero_ref[...] = jnp.zeros_like(zero_ref)

            def zcopy(row):
                return pltpu.make_async_copy(zero_ref, xs_ref.at[pl.ds(pl.multiple_of(row, SORT_ALIGN), bm)], zsem)

            for e in range(n_e):
                zcopy(z1_ref[e]).start()

                @pl.when(z2_ref[e] != z1_ref[e])
                def _():
                    zcopy(z2_ref[e]).start()
            for e in range(n_e):
                zcopy(z1_ref[e]).wait()

                @pl.when(z2_ref[e] != z1_ref[e])
                def _():
                    zcopy(z2_ref[e]).wait()

            carry_ref[...] = jnp.zeros_like(carry_ref)

    tgts = _window_targets(meta_ref[...], tbv_ref[0], tlv_ref[0])
    hb = h_ref[...].astype(BF16)
    lane_t = _iota((tm, LANES), 1)
    tcols = jnp.zeros((tm, LANES), F32)
    for k, tgt in enumerate(tgts):
        tcols = jnp.where(lane_t == k, tgt.astype(F32), tcols)
    trows = tcols.T.astype(jnp.int32)
    for c0 in range(0, n_rows_w, MOE_SEL_CHUNK):
        w = min(MOE_SEL_CHUNK, n_rows_w - c0)
        row_c = _iota((w, tm), 0) + c0
        sel_t = jnp.zeros((w, tm), F32)
        for k in range(TOP_K):
            sel_t = sel_t + (row_c == trows[k:k + 1, :]).astype(F32)
        xw_ref[buf, c0:c0 + w, :] = _mm(sel_t.astype(BF16), hb).astype(BF16)

    def splice_carry(e, slot0, off, length, n):
        @pl.when(n > 0)
        def _():
            g0 = pl.multiple_of(slot0 * MOE_WIN, MOE_WIN)
            xw_ref[buf, pl.ds(g0, SORT_ALIGN), :] = xw_ref[buf, pl.ds(g0, SORT_ALIGN), :] + carry_ref[e]
            filled = off + length
            gl = pl.multiple_of(g0 + lax.shift_right_logical(filled, SORT_ALIGN_SHIFT) * SORT_ALIGN, SORT_ALIGN)
            last = xw_ref[buf, pl.ds(gl, SORT_ALIGN), :]
            carry_ref[e] = jnp.where((filled & (SORT_ALIGN - 1)) != 0, last, jnp.zeros_like(last))

    def copy(b, slot, row):
        return pltpu.make_async_copy(xw_ref.at[b, pl.ds(pl.multiple_of(slot * MOE_WIN, MOE_WIN), MOE_WIN)],
                                     xs_ref.at[pl.ds(row, MOE_WIN)], sem.at[b])

    @pl.when(i > 0)
    def _():
        _for_recorded_windows(lambda slot, row: copy(1 - buf, slot, row).wait(), rows_ref, cnt_ref, 1 - buf)

    _for_each_window(n_e, pstart_ref, tb_ref, tl_ref, lambda slot, row: copy(buf, slot, row).start(),
                     rows_ref, cnt_ref, buf, splice_carry)

    @pl.when(i == n_i - 1)
    def _():
        _for_recorded_windows(lambda slot, row: copy(buf, slot, row).wait(), rows_ref, cnt_ref, buf)
        cout_ref[...] = carry_ref[...]


def _scatter(pstart, z1, z2, tbase, tlen, meta, h2, prior, n_rows_sorted, bm, tm):
    rows, d = h2.shape
    n_e = pstart.shape[0]
    n_slots = _moe_slots(tm)
    smem = pl.BlockSpec(memory_space=pltpu.SMEM)
    tile3 = lambda i: (i, 0, 0)
    tile_smem = lambda im: pl.BlockSpec((1, 1, LANES), im, memory_space=pltpu.SMEM)
    carry_spec = pl.BlockSpec((n_e, SORT_ALIGN, d), lambda i: (0, 0, 0))
    in_specs = [smem, smem, smem,
                tile_smem(tile3), tile_smem(tile3),
                pl.BlockSpec((1, 1, LANES), tile3), pl.BlockSpec((1, 1, LANES), tile3),
                pl.BlockSpec((tm, 2 * TOP_K), lambda i: (i, 0)),
                pl.BlockSpec((tm, d), lambda i: (i, 0))]
    args = [pstart, z1, z2, tbase, tlen, tbase, tlen, meta, h2]
    aliases = {}
    if prior is not None:
        in_specs += [carry_spec, pl.BlockSpec(memory_space=pl.ANY)]
        aliases = {len(args) + 1: 0}
        args += list(prior)
    return pl.pallas_call(
        functools.partial(_scatter_kernel, n_e, prior is not None),
        grid=(rows // tm,),
        in_specs=in_specs,
        out_specs=[pl.BlockSpec(memory_space=pl.ANY), carry_spec],
        out_shape=[jax.ShapeDtypeStruct((n_rows_sorted, d), BF16),
                   jax.ShapeDtypeStruct((n_e, SORT_ALIGN, d), BF16)],
        input_output_aliases=aliases,
        scratch_shapes=[pltpu.VMEM((2, n_slots * MOE_WIN, d), BF16), pltpu.VMEM((bm, d), BF16),
                        pltpu.VMEM((n_e, SORT_ALIGN, d), BF16),
                        pltpu.SMEM((2, n_slots), jnp.int32), pltpu.SMEM((2,), jnp.int32),
                        pltpu.SemaphoreType.DMA((2,)), pltpu.SemaphoreType.DMA(())],
        compiler_params=_cparams(("arbitrary",)),
        name="moe_scatter",
    )(*args)


def _expert_kernel(d_ff, be_ref, nused_ref, xs_ref, w1_ref, b1_ref, w2_ref, b2_ref, ys_ref, w1b_ref, w2b_ref):
    i = pl.program_id(0)
    new_expert = jnp.logical_or(i == 0, be_ref[i] != be_ref[jnp.maximum(i - 1, 0)])

    @pl.when(jnp.logical_and(i < nused_ref[0], new_expert))
    def _():
        w1b_ref[...] = w1_ref[0].astype(BF16)
        w2b_ref[...] = w2_ref[0].astype(BF16)

    @pl.when(i < nused_ref[0])
    def _():
        h = _mm(xs_ref[...], w1b_ref[...]) + b1_ref[0]
        hg = jnp.minimum(h[:, :d_ff], SWIGLU_LIMIT)
        hu = jnp.clip(h[:, d_ff:], -SWIGLU_LIMIT, SWIGLU_LIMIT)
        act = hg * _sigmoid(SWIGLU_ALPHA * hg) * (hu + 1.0)
        ys_ref[...] = (_mm(act.astype(BF16), w2b_ref[...]) + b2_ref[0]).astype(ys_ref.dtype)

    @pl.when(i >= nused_ref[0])
    def _():
        ys_ref[...] = jnp.zeros_like(ys_ref)


def _experts(block_e, n_used, xs, w1, b1, w2, b2, bm):
    R, d = xs.shape
    d_ff = w2.shape[1]
    nb = R // bm

    def rows(i, be, nu):
        return (jnp.minimum(i, nu[0] - 1), 0)

    def wsel(i, be, nu):
        return (be[i], 0, 0)

    return pl.pallas_call(
        functools.partial(_expert_kernel, d_ff),
        grid_spec=pltpu.PrefetchScalarGridSpec(
            num_scalar_prefetch=2,
            grid=(nb,),
            in_specs=[pl.BlockSpec((bm, d), rows),
                      pl.BlockSpec((1, d, 2 * d_ff), wsel), pl.BlockSpec((1, 1, 2 * d_ff), wsel),
                      pl.BlockSpec((1, d_ff, d), wsel), pl.BlockSpec((1, 1, d), wsel)],
            out_specs=pl.BlockSpec((bm, d), lambda i, be, nu: (i, 0)),
            scratch_shapes=[pltpu.VMEM((d, 2 * d_ff), BF16), pltpu.VMEM((d_ff, d), BF16)]),
        out_shape=jax.ShapeDtypeStruct((R, d), xs.dtype),
        compiler_params=_cparams(("arbitrary",)),
        name="moe_experts",
    )(block_e, n_used, xs, w1, b1, w2, b2)


def _combine_kernel(n_e, pstart_ref, tb_ref, tl_ref, tbn_ref, tln_ref, tbv_ref, tlv_ref, meta_ref, gate_ref,
                    x1_ref, gf_ref, ys_ref, o_ref, win_ref, rows_ref, cnt_ref, sem):
    i = pl.program_id(0)
    n = pl.num_programs(0)
    tm = x1_ref.shape[0]
    buf = i % 2
    n_rows_w = win_ref.shape[1]

    def copy(b, slot, row):
        return pltpu.make_async_copy(
            ys_ref.at[pl.ds(row, MOE_WIN)],
            win_ref.at[b, pl.ds(pl.multiple_of(slot * MOE_WIN, MOE_WIN), MOE_WIN)], sem.at[b])

    @pl.when(i == 0)
    def _():
        win_ref[...] = jnp.zeros_like(win_ref)
        _for_each_window(n_e, pstart_ref, tb_ref, tl_ref, lambda slot, row: copy(buf, slot, row).start(),
                         rows_ref, cnt_ref, buf)

    @pl.when(i + 1 < n)
    def _():
        _for_each_window(n_e, pstart_ref, tbn_ref, tln_ref, lambda slot, row: copy(1 - buf, slot, row).start(),
                         rows_ref, cnt_ref, 1 - buf)

    tgts = _window_targets(meta_ref[...], tbv_ref[0], tlv_ref[0])
    gate = gate_ref[...]
    _for_recorded_windows(lambda slot, row: copy(buf, slot, row).wait(), rows_ref, cnt_ref, buf)
    y = x1_ref[...]
    for c0 in range(0, n_rows_w, MOE_SEL_CHUNK):
        w = min(MOE_SEL_CHUNK, n_rows_w - c0)
        lane_c = _iota((tm, w), 1) + c0
        sel = jnp.zeros((tm, w), F32)
        for k, tgt in enumerate(tgts):
            sel = sel + jnp.where(lane_c == tgt, gate[:, k:k + 1], 0.0)
        sel_hi, sel_lo = _split2(sel)
        wb = win_ref[buf, c0:c0 + w, :]
        y = y + (_mm(sel_hi, wb) + _mm(sel_lo, wb))
    o_ref[...] = y * lax.rsqrt(jnp.mean(y * y, axis=-1, keepdims=True) + NORM_EPS) * gf_ref[...]


def _combine(pstart, tbase, tlen, meta, gate, x1, gf, ys, tm):
    rows, d = x1.shape
    n = rows // tm
    n_e = pstart.shape[0]
    cur3 = lambda i: (i, 0, 0)
    nxt3 = lambda i: (jnp.minimum(i + 1, n - 1), 0, 0)
    tile_smem = lambda im: pl.BlockSpec((1, 1, LANES), im, memory_space=pltpu.SMEM)
    return pl.pallas_call(
        functools.partial(_combine_kernel, n_e),
        grid=(n,),
        in_specs=[pl.BlockSpec(memory_space=pltpu.SMEM),
                  tile_smem(cur3), tile_smem(cur3), tile_smem(nxt3), tile_smem(nxt3),
                  pl.BlockSpec((1, 1, LANES), cur3), pl.BlockSpec((1, 1, LANES), cur3),
                  pl.BlockSpec((tm, 2 * TOP_K), lambda i: (i, 0)),
                  pl.BlockSpec((tm, TOP_K), lambda i: (i, 0)),
                  pl.BlockSpec((tm, d), lambda i: (i, 0)),
                  pl.BlockSpec((1, d), lambda i: (0, 0)),
                  pl.BlockSpec(memory_space=pl.ANY)],
        out_specs=pl.BlockSpec((tm, d), lambda i: (i, 0)),
        out_shape=jax.ShapeDtypeStruct((rows, d), F32),
        scratch_shapes=[pltpu.VMEM((2, _moe_slots(tm) * MOE_WIN, d), ys.dtype),
                        pltpu.SMEM((2, _moe_slots(tm)), jnp.int32), pltpu.SMEM((2,), jnp.int32),
                        pltpu.SemaphoreType.DMA((2,))],
        compiler_params=_cparams(("arbitrary",)),
        name="moe_combine",
    )(pstart, tbase, tlen, tbase, tlen, tbase, tlen, meta, gate, x1, gf, ys)


def _rope_tables(pos):
    half = ROT_DIM // 2
    inv = ROPE_THETA ** (-jnp.arange(0, ROT_DIM, 2, dtype=F32) / ROT_DIM)
    ang = inv[:, None] * pos.astype(F32)[None, :]
    cos, sin = jnp.cos(ang), jnp.sin(ang)
    n = pos.shape[0]
    pad1 = jnp.ones((HEAD_DIM - ROT_DIM, n), F32)
    pad0 = jnp.zeros((HEAD_DIM - ROT_DIM, n), F32)
    cos_h = jnp.concatenate([cos, cos, pad1], axis=0)
    sin_h = jnp.concatenate([-sin, sin, pad0], axis=0)
    reps = (LANES // HEAD_DIM, 1)
    return jnp.tile(cos_h, reps).T, jnp.tile(sin_h, reps).T


def _state_from_pairs(Sp):
    a = Sp[:, :HEAD_DIM, :HEAD_DIM]
    b = Sp[:, HEAD_DIM:, HEAD_DIM:]
    St = jnp.stack([a, b], axis=1).reshape(-1, HEAD_DIM, HEAD_DIM)
    return jnp.swapaxes(St, 1, 2)


def kernel(x_prompt, x_sample, state_rwkv_wkv, state_rwkv_shift, cache_swa_k, cache_swa_v, norm1_g, w_in, mu_shift, decay_w0, decay_w2, aaa_a0, aaa_w2, gate_w2, k_k, k_a, r_k, lnx_g, lnx_b, attn_sinks, w_out, norm2_g, w_router, b_router, w_mlp1, b_mlp1, w_mlp2, b_mlp2, norm_f_g):
    depth = w_in.shape[0]
    assert depth == 1 and x_prompt.shape[0] == 1 and x_sample.shape[1] == 1
    T, d = x_prompt.shape[1], x_prompt.shape[2]
    B = x_sample.shape[0]
    past_len = PAST_LEN
    H = state_rwkv_wkv.shape[2]
    rw_w = H * HEAD_DIM
    n_pairs = H // 2
    n_q = attn_sinks.shape[1]
    n_kv = cache_swa_k.shape[3]
    q_cols = n_q * HEAD_DIM
    kv_cols = n_kv * HEAD_DIM
    rw_cols = state_rwkv_shift.shape[2]
    assert rw_cols == 3 * rw_w + 2 * HEAD_DIM + PAIR and kv_cols == LANES
    assert T % RW_TILE == 0 and B % ROW_TILE == 0 and B % 8 == 0
    wlen = cache_swa_k.shape[2]
    l = 0

    w_in_bf = w_in[l].astype(BF16)
    zero_half = jnp.zeros((HEAD_DIM, rw_w), F32)
    pp = dict(mu=mu_shift[l][None], w0=decay_w0[l][None],
              dw2=jnp.concatenate([decay_w2[l], zero_half], axis=0),
              a0=aaa_a0[l][None], aw2=jnp.concatenate([zero_half, aaa_w2[l]], axis=0),
              gw2=gate_w2[l], kk=k_k[l][None], ka=k_a[l][None], rk=r_k[l].reshape(1, rw_w),
              lng=lnx_g[l][None], lnb=lnx_b[l][None])
    w_out_bf = w_out[l].astype(BF16)
    n_e = w_router.shape[2]
    wr = jnp.pad(w_router[l], ((0, 0), (0, LANES - n_e)))
    br = jnp.concatenate([b_router[l], jnp.full((LANES - n_e,), NEG_BIG, F32)])[None]
    wp = dict(wa=w_out_bf[:rw_w], wb=w_out_bf[rw_w:], g2=norm2_g[l][None], wr=wr, br=br)
    g1 = norm1_g[l][None]

    xp = x_prompt[0]
    cos_p, sin_p = _rope_tables(jnp.arange(T))
    prw_p, q_p, k_p, v_p = _inproj(xp, g1, w_in_bf, cos_p, sin_p, INPROJ_TILE, rw_cols, q_cols, kv_cols)
    s0_p = jnp.zeros((n_pairs, PAIR, PAIR), F32)
    shift0_p = jnp.zeros((1, rw_cols), F32)
    assert RW_TILE == MOE_TILE
    sfin_p, x1_p, h2_p, gate_p, meta_p, tb_p, tl_p, cnt = _layer_prompt(
        xp, prw_p, shift0_p, s0_p, pp, q_p, k_p, v_p, attn_sinks[l], wp, RW_TILE)

    xs_ = x_sample[:, 0]
    cos_s, sin_s = _rope_tables(jnp.full((B,), past_len))
    prw_s, q_s, k_s, v_s = _inproj(xs_, g1, w_in_bf, cos_s, sin_s, ROW_TILE, rw_cols, q_cols, kv_cols)
    s_flat = state_rwkv_wkv[l].reshape(B, H * HEAD_DIM * HEAD_DIM)
    ya_s, snew_flat = _rwkv_step(prw_s, state_rwkv_shift[l], s_flat, pp, n_pairs)
    o2, kc_new, vc_new = _attn_step(q_s.reshape(B * n_q, HEAD_DIM), k_s, v_s,
                                    cache_swa_k[l].reshape(B, wlen, kv_cols),
                                    cache_swa_v[l].reshape(B, wlen, kv_cols),
                                    attn_sinks[l][:, None], n_q, n_kv, past_len)
    yb_s = o2.reshape(B, q_cols)

    rows = T + B
    x1_s, h2_s, gate_s, meta_s, tb_s, tl_s, cnt = _post(xs_, ya_s, yb_s, wp, cnt, ROW_TILE)

    counts = cnt[0, :n_e].astype(jnp.int32)
    padded = (counts + MOE_WIN + MOE_BM - 1) // MOE_BM * MOE_BM
    pend = jnp.cumsum(padded)
    pstart = (pend - padded).astype(jnp.int32)
    n_blocks = -(-(rows * TOP_K) // MOE_BM) + n_e + -(-(n_e * MOE_WIN) // MOE_BM)
    block_start = jnp.arange(n_blocks, dtype=jnp.int32) * MOE_BM
    block_e = jnp.minimum(jnp.sum((pend[None, :] <= block_start[:, None]).astype(jnp.int32), axis=1),
                          n_e - 1).astype(jnp.int32)
    n_used = (pend[-1] // MOE_BM).astype(jnp.int32)[None]
    z1 = (pstart + counts // MOE_BM * MOE_BM).astype(jnp.int32)
    z2 = (pend - MOE_BM).astype(jnp.int32)

    n_sorted = n_blocks * MOE_BM
    xs_sorted, carry = _scatter(pstart, z1, z2, tb_p, tl_p, meta_p, h2_p, None, n_sorted, MOE_BM, MOE_TILE)
    xs_sorted, _ = _scatter(pstart, z1, z2, tb_s, tl_s, meta_s, h2_s, (carry, xs_sorted), n_sorted, MOE_BM,
                            ROW_TILE)
    ys_sorted = _experts(block_e, n_used, xs_sorted, w_mlp1[l], b_mlp1[l][:, None], w_mlp2[l],
                         b_mlp2[l][:, None], MOE_BM)
    gf = norm_f_g[None]
    y_p = _combine(pstart, tb_p, tl_p, meta_p, gate_p, x1_p, gf, ys_sorted, MOE_TILE)
    y_s = _combine(pstart, tb_s, tl_s, meta_s, gate_s, x1_s, gf, ys_sorted, ROW_TILE)

    sdt = state_rwkv_wkv.dtype
    return (y_p[None], y_s[:, None],
            _state_from_pairs(sfin_p)[None, None].astype(sdt), prw_p[T - 1][None, None],
            k_p[T - min(WINDOW, T):].reshape(1, 1, -1, n_kv, HEAD_DIM),
            v_p[T - min(WINDOW, T):].reshape(1, 1, -1, n_kv, HEAD_DIM),
            snew_flat.reshape(1, B, H, HEAD_DIM, HEAD_DIM).astype(sdt), prw_s[None],
            kc_new.reshape(1, B, wlen, n_kv, HEAD_DIM), vc_new.reshape(1, B, wlen, n_kv, HEAD_DIM))
```
